```python
import math
import jax
import jax.numpy as jnp
from jax import lax
import numpy as np

D_MODEL = 1024
BATCH = 2
SEQ = 8192
DEPTH = 2

GRID_W = 64
CTX_LEN = 256
HEAD_DIM = 64
BLOCK = 128
ROPE_BASE = 10000.0
EPS = 1e-6

A_HEADS = 8
A_KV_HEADS = 2
WINDOW = 128

HY_CH = 512
HY_ORDER = 2
HY_EMB = 33
HY_BANDS = (HY_EMB - 1) // 2
HY_HIDDEN = 64
HY_FAST_DECAY = 0.3
HY_SLOW_DECAY = 1.5
HY_TARGET = 1e-2
SHORT_CONV = 3

C_HEADS = 4
C_DK = 128
C_DV = 128
GDN_CHUNK = 64

D_HEADS = 8
D_KV_HEADS = 2

PEER_HEADS = 8
PEER_NKEYS = 128
PEER_EXPERTS = PEER_NKEYS * PEER_NKEYS
PEER_QDIM = 256
PEER_TOPK = 16
PEER_CHUNK = 128

ALPHA = (2 * DEPTH) ** 0.25
BETA_INIT = (8 * DEPTH) ** -0.25
N_EVEN = (DEPTH + 1) // 2
N_ODD = DEPTH // 2

A_Q = A_HEADS * HEAD_DIM
A_KV = A_KV_HEADS * HEAD_DIM
EVEN_IN = A_Q + 2 * A_KV + 3 * HY_CH
C_W = C_HEADS * C_DK
C_GATES = 4 * C_HEADS
D_Q = D_HEADS * HEAD_DIM
D_KV = D_KV_HEADS * HEAD_DIM
ODD_IN = 4 * C_W + C_GATES + D_Q + 2 * D_KV
MIX_W = A_Q + HY_CH

kernel_name = 'hybrid_dit_sinkgqa_hyena_gdn_axialgqa_peer'


def layer_norm(x, g, b):
    xf = x.astype(jnp.float32)
    mu = jnp.mean(xf, -1, keepdims=True)
    var = jnp.mean(jnp.square(xf - mu), -1, keepdims=True)
    return ((xf - mu) * lax.rsqrt(var + EPS) * g + b).astype(x.dtype)


def rms_norm(x, w):
    xf = x.astype(jnp.float32)
    return (xf * lax.rsqrt(jnp.mean(xf * xf, -1, keepdims=True) + EPS) * w).astype(x.dtype)


def l2_normalize(x):
    xf = x.astype(jnp.float32)
    return (xf * lax.rsqrt(jnp.sum(xf * xf, -1, keepdims=True) + EPS)).astype(x.dtype)


def modulate(x, shift, scale):
    return x * (1.0 + scale) + shift


def axial_rope_tables(n_tok):
    rows = n_tok // GRID_W
    row = jnp.repeat(jnp.arange(rows, dtype=jnp.float32), GRID_W)
    col = jnp.tile(jnp.arange(GRID_W, dtype=jnp.float32), rows)
    nf = HEAD_DIM // 4
    inv = ROPE_BASE ** (-jnp.arange(nf, dtype=jnp.float32) / nf)
    ang = jnp.concatenate([row[:, None] * inv, col[:, None] * inv], -1)
    return jnp.cos(ang), jnp.sin(ang)


def apply_axial_rope(x, cos, sin):
    nf = HEAD_DIM // 4
    c = cos[:, None, :]
    s = sin[:, None, :]
    parts = []
    for a in range(2):
        xa = x[..., a * 2 * nf:(a + 1) * 2 * nf]
        x1, x2 = xa[..., :nf], xa[..., nf:]
        ca, sa = c[..., a * nf:(a + 1) * nf], s[..., a * nf:(a + 1) * nf]
        parts += [x1 * ca - x2 * sa, x2 * ca + x1 * sa]
    return jnp.concatenate(parts, -1).astype(x.dtype)


def dwconv(x, w):
    k, ch = w.shape
    return lax.conv_general_dilated(x, w[:, None, :].astype(x.dtype), (1,), [(k // 2, k // 2)],
                                    dimension_numbers=('NWC', 'WIO', 'NWC'), feature_group_count=ch)


def gqa_softmax(q, k, v, sink=None):
    b, lq, h, hd = q.shape
    kvh = k.shape[2]
    g = h // kvh
    lk = k.shape[1]
    s = jnp.einsum('bqkgd,bjkd->bkgqj', q.reshape(b, lq, kvh, g, hd), k).astype(jnp.float32) * hd ** -0.5
    if sink is not None:
        s = jnp.concatenate([s, jnp.broadcast_to(sink.astype(jnp.float32).reshape(kvh, g, 1, 1), s.shape[:-1] + (1,))], -1)
    p = jax.nn.softmax(s, axis=-1)[..., :lk].astype(v.dtype)
    return jnp.einsum('bkgqj,bjkd->bqkgd', p, v).reshape(b, lq, h * hd)


def windowed_sink_attention(q, k, v, kc, vc, sink):
    b, s, h, hd = q.shape
    kvh = k.shape[2]
    g = h // kvh
    nb = s // BLOCK
    lc = kc.shape[1]
    w3 = 3 * BLOCK
    scale = hd ** -0.5
    qb = q.reshape(b, nb, BLOCK, kvh, g, hd).swapaxes(0, 1)

    def band(t):
        tb = jnp.pad(t.reshape(b, nb, BLOCK, kvh, hd), ((0, 0), (1, 1), (0, 0), (0, 0), (0, 0)))
        return jnp.concatenate([tb[:, :-2], tb[:, 1:-1], tb[:, 2:]], axis=2).swapaxes(0, 1)

    kw, vw = band(k), band(v)
    blk = jnp.arange(nb)[:, None, None]
    qpos = blk * BLOCK + jnp.arange(BLOCK)[None, :, None]
    kpos = (blk - 1) * BLOCK + jnp.arange(w3)[None, None, :]
    valid = (jnp.abs(qpos - kpos) <= WINDOW) & (kpos >= 0) & (kpos < s)
    sink_logit = sink.astype(jnp.float32).reshape(kvh, g, 1, 1)

    def one_block(args):
        qk, kk, vk, vm = args
        s_loc = jnp.einsum('bqkgd,bjkd->bkgqj', qk, kk).astype(jnp.float32) * scale
        s_loc = jnp.where(vm, s_loc, -jnp.inf)
        s_ctx = jnp.einsum('bqkgd,bjkd->bkgqj', qk, kc).astype(jnp.float32) * scale
        s_snk = jnp.broadcast_to(sink_logit, s_loc.shape[:-1] + (1,))
        p = jax.nn.softmax(jnp.concatenate([s_loc, s_ctx, s_snk], -1), axis=-1).astype(v.dtype)
        return (jnp.einsum('bkgqj,bjkd->bqkgd', p[..., :w3], vk)
                + jnp.einsum('bkgqj,bjkd->bqkgd', p[..., w3:w3 + lc], vc))

    o = lax.map(one_block, (qb, kw, vw, valid))
    return o.swapaxes(0, 1).reshape(b, s, h * hd)


def global_block_attention(q, k, v, kc, vc):
    b, s, h, hd = q.shape
    nb = s // BLOCK
    k_all = jnp.concatenate([k, kc], 1)
    v_all = jnp.concatenate([v, vc], 1)
    qb = q.reshape(b, nb, BLOCK, h, hd).swapaxes(0, 1)
    o = lax.map(lambda qk: gqa_softmax(qk, k_all, v_all), qb)
    return o.swapaxes(0, 1).reshape(b, s, h * hd)


def hyena_filters(n, w1, b1, freq, w2, b2, w3):
    t = jnp.arange(n, dtype=jnp.float32)
    tn = t / n
    f = jnp.arange(1, HY_BANDS + 1, dtype=jnp.float32)
    ang = 2.0 * math.pi * t[:, None] * f[None, :] / n
    feat = jnp.concatenate([tn[:, None], jnp.sin(ang), jnp.cos(ang)], -1)
    hid = jnp.sin(freq * (feat @ w1 + b1))
    hid = jnp.sin(freq * (hid @ w2 + b2))
    filt = (hid @ w3).astype(jnp.float32).reshape(n, HY_ORDER, 2, HY_CH)
    max_decay = math.log(HY_TARGET) / HY_FAST_DECAY
    min_decay = math.log(HY_TARGET) / HY_SLOW_DECAY
    deltas = jnp.abs(jnp.linspace(min_decay, max_decay, HY_CH, dtype=jnp.float32))
    filt = filt * jnp.exp(-tn[:, None, None, None] * deltas)
    return filt / jnp.sum(jnp.abs(filt), axis=(0, 2), keepdims=True)


def fft_long_conv(x, hf, hb, bias):
    n = x.shape[1]
    kern = jnp.concatenate([hf, jnp.zeros_like(hf[:1]), hb[1:][::-1]], 0)
    kf = jnp.fft.rfft(kern, n=2 * n, axis=0)
    xf = jnp.fft.rfft(x.astype(jnp.float32), n=2 * n, axis=1)
    y = jnp.fft.irfft(xf * kf[None], n=2 * n, axis=1)[:, :n]
    return (y + x.astype(jnp.float32) * bias.astype(jnp.float32)).astype(x.dtype)


def hyena_operator(u, conv_w, conv_b, fw1, fb1, ffreq, fw2, fb2, fw3, hy_bias):
    uc = dwconv(u, conv_w) + conv_b
    x1, x2, v = uc[..., :HY_CH], uc[..., HY_CH:2 * HY_CH], uc[..., 2 * HY_CH:]
    filt = hyena_filters(u.shape[1], fw1, fb1, ffreq, fw2, fb2, fw3)
    v = x1 * fft_long_conv(v, filt[:, 0, 0], filt[:, 0, 1], hy_bias[0])
    v = x2 * fft_long_conv(v, filt[:, 1, 0], filt[:, 1, 1], hy_bias[1])
    return v


def gated_delta_chunked(q, k, v, beta, g, s0):
    b, n_tok, h, dk = q.shape
    dv = v.shape[-1]
    cs = GDN_CHUNK
    n = n_tok // cs

    def blk(t):
        return t.astype(jnp.float32).reshape(b, n, cs, h, -1).transpose(1, 0, 3, 2, 4)

    q = blk(q) * dk ** -0.5
    k = blk(k)
    v = blk(v)
    beta = blk(beta[..., None])[..., 0]
    gcum = jnp.cumsum(blk(g[..., None])[..., 0], -1)
    idx = jnp.arange(cs)
    incl = idx[:, None] >= idx[None, :]
    strict = idx[:, None] > idx[None, :]
    decay = jnp.exp(jnp.where(incl, gcum[..., :, None] - gcum[..., None, :], -jnp.inf))
    kb = k * beta[..., None]
    a = jnp.where(strict, jnp.einsum('nbhid,nbhjd->nbhij', kb, k) * decay, 0.0)
    rhs = jnp.concatenate([v * beta[..., None], kb * jnp.exp(gcum)[..., None]], -1)
    sol = lax.linalg.triangular_solve(jnp.eye(cs, dtype=jnp.float32) + a, rhs, left_side=True, lower=True)
    u, w = sol[..., :dv], sol[..., dv:]
    intra = jnp.einsum('nbhid,nbhjd->nbhij', q, k) * decay
    q_dec = q * jnp.exp(gcum)[..., None]
    k_dec = k * jnp.exp(gcum[..., -1:] - gcum)[..., None]
    g_last = jnp.exp(gcum[..., -1])

    def step(state, xs):
        u_i, w_i, q_i, k_i, intra_i, gl = xs
        v_new = u_i - jnp.einsum('bhcd,bhde->bhce', w_i, state)
        o_i = jnp.einsum('bhcd,bhde->bhce', q_i, state) + jnp.einsum('bhij,bhje->bhie', intra_i, v_new)
        state = state * gl[..., None, None] + jnp.einsum('bhcd,bhce->bhde', k_i, v_new)
        return state, o_i

    s_final, o = lax.scan(step, s0.astype(jnp.float32), (u, w, q_dec, k_dec, intra, g_last))
    return o.transpose(1, 0, 3, 2, 4).reshape(b, n_tok, h, dv), s_final


def gdn_inputs(p, conv_w, a_log, dt_bias):
    b, n, _ = p.shape
    qkv = jax.nn.silu(dwconv(p[..., :3 * C_W], conv_w))
    q = l2_normalize(qkv[..., :C_W].reshape(b, n, C_HEADS, C_DK))
    k = l2_normalize(qkv[..., C_W:2 * C_W].reshape(b, n, C_HEADS, C_DK))
    v = qkv[..., 2 * C_W:].reshape(b, n, C_HEADS, C_DV)
    z = p[..., 3 * C_W:4 * C_W].reshape(b, n, C_HEADS, C_DV)
    gates = p[..., 4 * C_W:4 * C_W + C_GATES].astype(jnp.float32).reshape(b, n, 4, C_HEADS)
    beta = jax.nn.sigmoid(gates[:, :, :2])
    g = -jnp.exp(a_log.astype(jnp.float32)) * jax.nn.softplus(gates[:, :, 2:] + dt_bias.astype(jnp.float32))
    return q, k, v, z, beta, g


def rev(t, flip):
    return t[:, ::-1] if flip else t


def bidirectional_gdn(lat, cx, with_ctx):
    ql, kl, vl, bl, gl = lat
    qc, kc, vc, bc, gcx = cx
    s0 = jnp.zeros((ql.shape[0], C_HEADS, C_DK, C_DV), jnp.float32)
    o_lat = 0.0
    o_ctx = 0.0
    for d in range(2):
        f = d == 1
        oc, sc = gated_delta_chunked(rev(qc, f), rev(kc, f), rev(vc, f), rev(bc[:, :, d], f), rev(gcx[:, :, d], f), s0)
        ol, _ = gated_delta_chunked(rev(ql, f), rev(kl, f), rev(vl, f), rev(bl[:, :, d], f), rev(gl[:, :, d], f), sc)
        o_lat = o_lat + rev(ol, f)
        if with_ctx:
            o_ctx = o_ctx + rev(oc, f)
    return o_lat, o_ctx


def even_mixer(h, hc, cos, sin, w_in, w_out, sink, conv_w, conv_b, fw1, fb1, ffreq, fw2, fb2, fw3, hy_bias, with_ctx):
    p = h @ w_in
    pc = hc @ w_in

    def attn_qkv(t):
        b, n, _ = t.shape
        q = t[..., :A_Q].reshape(b, n, A_HEADS, HEAD_DIM)
        k = t[..., A_Q:A_Q + A_KV].reshape(b, n, A_KV_HEADS, HEAD_DIM)
        v = t[..., A_Q + A_KV:A_Q + 2 * A_KV].reshape(b, n, A_KV_HEADS, HEAD_DIM)
        return q, k, v

    q, k, v = attn_qkv(p)
    qc, kc, vc = attn_qkv(pc)
    o_a = windowed_sink_attention(apply_axial_rope(q, cos, sin), apply_axial_rope(k, cos, sin), v, kc, vc, sink)
    hy_args = (conv_w, conv_b, fw1, fb1, ffreq, fw2, fb2, fw3, hy_bias)
    o_b = hyena_operator(p[..., A_Q + 2 * A_KV:], *hy_args)
    out = jnp.concatenate([o_a, o_b], -1) @ w_out
    out_c = None
    if with_ctx:
        o_ac = gqa_softmax(qc, kc, vc, sink)
        o_bc = hyena_operator(pc[..., A_Q + 2 * A_KV:], *hy_args)
        out_c = jnp.concatenate([o_ac, o_bc], -1) @ w_out
    return out, out_c


def odd_mixer(h, hc, cos, sin, w_in, w_out, conv_w, a_log, dt_bias, gnorm_w, qnorm_w, knorm_w, with_ctx):
    p = h @ w_in
    pc = hc @ w_in
    ql, kl, vl, zl, bl, gl = gdn_inputs(p, conv_w, a_log, dt_bias)
    qc, kc, vc, zc, bc, gcx = gdn_inputs(pc, conv_w, a_log, dt_bias)
    o_l, o_c = bidirectional_gdn((ql, kl, vl, bl, gl), (qc, kc, vc, bc, gcx), with_ctx)

    def gated_out(o, z):
        return (rms_norm(o.astype(z.dtype), gnorm_w) * jax.nn.silu(z)).reshape(z.shape[0], z.shape[1], C_W)

    def attn_qkv(t):
        b, n, _ = t.shape
        base = 4 * C_W + C_GATES
        q = rms_norm(t[..., base:base + D_Q].reshape(b, n, D_HEADS, HEAD_DIM), qnorm_w)
        k = rms_norm(t[..., base + D_Q:base + D_Q + D_KV].reshape(b, n, D_KV_HEADS, HEAD_DIM), knorm_w)
        v = t[..., base + D_Q + D_KV:].reshape(b, n, D_KV_HEADS, HEAD_DIM)
        return q, k, v

    qd, kd, vd = attn_qkv(p)
    qdc, kdc, vdc = attn_qkv(pc)
    o_d = global_block_attention(apply_axial_rope(qd, cos, sin), apply_axial_rope(kd, cos, sin), vd, kdc, vdc)
    out = jnp.concatenate([gated_out(o_l, zl), o_d], -1) @ w_out
    out_c = None
    if with_ctx:
        out_c = jnp.concatenate([gated_out(o_c, zc), gqa_softmax(qdc, kdc, vdc)], -1) @ w_out
    return out, out_c


def peer_ffn(h, wq, k1, k2, u_tab, v_tab):
    b, n, d = h.shape
    tokens = h.reshape(-1, PEER_CHUNK, d)
    half = PEER_QDIM // 2
    n_cand = PEER_TOPK * PEER_TOPK

    def one_chunk(xc):
        t = xc.shape[0]
        q = (xc @ wq).reshape(t, PEER_HEADS, 2, half)
        s1 = jnp.einsum('thd,hnd->thn', q[:, :, 0], k1).astype(jnp.float32)
        s2 = jnp.einsum('thd,hnd->thn', q[:, :, 1], k2).astype(jnp.float32)
        v1, i1 = lax.top_k(s1, PEER_TOPK)
        v2, i2 = lax.top_k(s2, PEER_TOPK)
        cand = (v1[..., :, None] + v2[..., None, :]).reshape(t, PEER_HEADS, n_cand)
        cand_id = (i1[..., :, None] * PEER_NKEYS + i2[..., None, :]).reshape(t, PEER_HEADS, n_cand)
        best, pos = lax.top_k(cand, PEER_TOPK)
        expert = jnp.take_along_axis(cand_id, pos, axis=-1)
        gate = jax.nn.softmax(best, axis=-1)
        u_sel = jnp.take(u_tab, expert, axis=0)
        v_sel = jnp.take(v_tab, expert, axis=0)
        act = jax.nn.gelu(jnp.einsum('td,thkd->thk', xc, u_sel).astype(jnp.float32), approximate=False)
        return jnp.einsum('thk,thkd->td', (gate * act).astype(v_sel.dtype), v_sel)

    return lax.map(one_chunk, tokens).reshape(b, n, d)


def setup_inputs(seed: int = 0) -> dict:
    key = jax.random.key(seed)
    ks = list(jax.random.split(key, 40))
    it = iter(ks)
    D = D_MODEL

    def nrm(shape, scale):
        return jax.random.normal(next(it), shape, jnp.float32) * scale

    def unif(shape, lo, hi):
        return jax.random.uniform(next(it), shape, jnp.float32, lo, hi)

    dt = jnp.exp(unif((N_ODD, 2, C_HEADS), math.log(1e-3), math.log(1e-1)))
    return {
        'x': nrm((BATCH, SEQ, D), 1.0),
        'c': nrm((BATCH, D), 1.0),
        'ctx': nrm((BATCH, CTX_LEN, D), 1.0),
        'c_ctx': nrm((D,), 1.0),
        'ada_w': nrm((DEPTH, D, 6 * D), 0.5 * D ** -0.5),
        'ada_b': nrm((DEPTH, 6 * D), 0.02),
        'ln1_g': 1.0 + nrm((DEPTH, D), 0.05),
        'ln1_b': nrm((DEPTH, D), 0.02),
        'ln2_g': 1.0 + nrm((DEPTH, D), 0.05),
        'ln2_b': nrm((DEPTH, D), 0.02),
        'peer_wq': nrm((DEPTH, D, PEER_HEADS * PEER_QDIM), D ** -0.5),
        'peer_k1': nrm((DEPTH, PEER_HEADS, PEER_NKEYS, PEER_QDIM // 2), (PEER_QDIM // 2) ** -0.5),
        'peer_k2': nrm((DEPTH, PEER_HEADS, PEER_NKEYS, PEER_QDIM // 2), (PEER_QDIM // 2) ** -0.5),
        'peer_u': nrm((DEPTH, PEER_EXPERTS, D), D ** -0.5),
        'peer_v': nrm((DEPTH, PEER_EXPERTS, D), BETA_INIT),
        'ev_w_in': nrm((N_EVEN, D, EVEN_IN), D ** -0.5),
        'ev_w_out': nrm((N_EVEN, MIX_W, D), BETA_INIT * MIX_W ** -0.5),
        'ev_sink': nrm((N_EVEN, A_HEADS), 0.5),
        'ev_conv_w': nrm((N_EVEN, SHORT_CONV, 3 * HY_CH), SHORT_CONV ** -0.5),
        'ev_conv_b': nrm((N_EVEN, 3 * HY_CH), 0.02),
        'ev_filt_w1': nrm((N_EVEN, HY_EMB, HY_HIDDEN), HY_EMB ** -0.5),
        'ev_filt_b1': nrm((N_EVEN, HY_HIDDEN), 0.1),
        'ev_filt_freq': 1.0 + nrm((N_EVEN, HY_HIDDEN), 0.05),
        'ev_filt_w2': nrm((N_EVEN, HY_HIDDEN, HY_HIDDEN), HY_HIDDEN ** -0.5),
        'ev_filt_b2': nrm((N_EVEN, HY_HIDDEN), 0.1),
        'ev_filt_w3': nrm((N_EVEN, HY_HIDDEN, HY_ORDER * 2 * HY_CH), HY_HIDDEN ** -0.5),
        'ev_hy_bias': nrm((N_EVEN, HY_ORDER, HY_CH), 0.5),
        'od_w_in': nrm((N_ODD, D, ODD_IN), D ** -0.5),
        'od_w_out': nrm((N_ODD, MIX_W, D), BETA_INIT * MIX_W ** -0.5),
        'od_conv_w': nrm((N_ODD, SHORT_CONV, 3 * C_W), SHORT_CONV ** -0.5),
        'od_a_log': jnp.log(unif((N_ODD, 2, C_HEADS), 1.0, 16.0)),
        'od_dt_bias': dt + jnp.log(-jnp.expm1(-dt)),
        'od_gnorm_w': 1.0 + nrm((N_ODD, C_DV), 0.05),
        'od_qnorm_w': 1.0 + nrm((N_ODD, HEAD_DIM), 0.05),
        'od_knorm_w': 1.0 + nrm((N_ODD, HEAD_DIM), 0.05),
    }


def reference(x, c, ctx, c_ctx, ada_w, ada_b, ln1_g, ln1_b, ln2_g, ln2_b,
              peer_wq, peer_k1, peer_k2, peer_u, peer_v,
              ev_w_in, ev_w_out, ev_sink, ev_conv_w, ev_conv_b, ev_filt_w1, ev_filt_b1, ev_filt_freq,
              ev_filt_w2, ev_filt_b2, ev_filt_w3, ev_hy_bias,
              od_w_in, od_w_out, od_conv_w, od_a_log, od_dt_bias, od_gnorm_w, od_qnorm_w, od_knorm_w):
    cos, sin = axial_rope_tables(x.shape[1])
    silu_c = jax.nn.silu(c)
    silu_cc = jax.nn.silu(c_ctx)
    for i in range(DEPTH):
        with_ctx = i < DEPTH - 1
        j = i // 2
        mod = (silu_c @ ada_w[i] + ada_b[i])[:, None, :]
        modc = (silu_cc @ ada_w[i] + ada_b[i])[None, None, :]
        sh1, sc1, g1, sh2, sc2, g2 = jnp.split(mod, 6, axis=-1)
        sh1c, sc1c, g1c, sh2c, sc2c, g2c = jnp.split(modc, 6, axis=-1)
        h = modulate(x, sh1, sc1)
        hc = modulate(ctx, sh1c, sc1c)
        if i % 2 == 0:
            out, out_c = even_mixer(h, hc, cos, sin, ev_w_in[j], ev_w_out[j], ev_sink[j], ev_conv_w[j], ev_conv_b[j],
                                    ev_filt_w1[j], ev_filt_b1[j], ev_filt_freq[j], ev_filt_w2[j], ev_filt_b2[j],
                                    ev_filt_w3[j], ev_hy_bias[j], with_ctx)
        else:
            out, out_c = odd_mixer(h, hc, cos, sin, od_w_in[j], od_w_out[j], od_conv_w[j], od_a_log[j], od_dt_bias[j],
                                   od_gnorm_w[j], od_qnorm_w[j], od_knorm_w[j], with_ctx)
        x = layer_norm(ALPHA * x + g1 * out, ln1_g[i], ln1_b[i])
        y = peer_ffn(modulate(x, sh2, sc2), peer_wq[i], peer_k1[i], peer_k2[i], peer_u[i], peer_v[i])
        x = layer_norm(ALPHA * x + g2 * y, ln2_g[i], ln2_b[i])
        if with_ctx:
            ctx = layer_norm(ALPHA * ctx + g1c * out_c, ln1_g[i], ln1_b[i])
            yc = peer_ffn(modulate(ctx, sh2c, sc2c), peer_wq[i], peer_k1[i], peer_k2[i], peer_u[i], peer_v[i])
            ctx = layer_norm(ALPHA * ctx + g2c * yc, ln2_g[i], ln2_b[i])
    return x
```

```python
import functools
import math

import jax
import jax.numpy as jnp
from jax import lax
from jax.experimental import pallas as pl
from jax.experimental.pallas import tpu as pltpu

D_MODEL = 1024
DEPTH = 2
GRID_W = 64
HEAD_DIM = 64
BLOCK = 128
ROPE_BASE = 10000.0
EPS = 1e-6

A_HEADS = 8
A_KV_HEADS = 2
WINDOW = 128

HY_CH = 512
HY_ORDER = 2
HY_EMB = 33
HY_BANDS = (HY_EMB - 1) // 2
HY_FAST_DECAY = 0.3
HY_SLOW_DECAY = 1.5
HY_TARGET = 1e-2

C_HEADS = 4
C_DK = 128
C_DV = 128
GDN_CHUNK = 64

D_HEADS = 8
D_KV_HEADS = 2

PEER_HEADS = 8
PEER_NKEYS = 128
PEER_QDIM = 256
PEER_TOPK = 16
PEER_CHUNK = 128

ALPHA = (2 * DEPTH) ** 0.25

A_Q = A_HEADS * HEAD_DIM
A_KV = A_KV_HEADS * HEAD_DIM
C_W = C_HEADS * C_DK
C_GATES = 4 * C_HEADS
D_Q = D_HEADS * HEAD_DIM
D_KV = D_KV_HEADS * HEAD_DIM

VMEM_LIMIT_BYTES = 48 * 1024 * 1024


def _modmm_kernel(x_ref, sh_ref, sc_ref, w_ref, o_ref):
    h = x_ref[0] * (1.0 + sc_ref[0]) + sh_ref[0]
    o_ref[0] = jnp.dot(h.astype(jnp.bfloat16), w_ref[...], preferred_element_type=jnp.float32)


def mod_matmul(x, shift, scale, w, tm=512, tn=None):
    b, s, k = x.shape
    n = w.shape[1]
    tm = min(tm, s)
    tn = n if tn is None else tn
    wb = w.astype(jnp.bfloat16)
    return pl.pallas_call(
        _modmm_kernel,
        grid=(b, n // tn, s // tm),
        in_specs=[
            pl.BlockSpec((1, tm, k), lambda i, j, m: (i, m, 0)),
            pl.BlockSpec((1, 1, k), lambda i, j, m: (i, 0, 0)),
            pl.BlockSpec((1, 1, k), lambda i, j, m: (i, 0, 0)),
            pl.BlockSpec((k, tn), lambda i, j, m: (0, j)),
        ],
        out_specs=pl.BlockSpec((1, tm, tn), lambda i, j, m: (i, m, j)),
        out_shape=jax.ShapeDtypeStruct((b, s, n), jnp.float32),
        compiler_params=pltpu.CompilerParams(
            dimension_semantics=("arbitrary", "arbitrary", "arbitrary"),
            vmem_limit_bytes=VMEM_LIMIT_BYTES),
        name="mod_matmul",
    )(x, shift, scale, wb)


def _post_kernel(o_ref, w_ref, x_ref, g_ref, lg_ref, lb_ref, y_ref):
    out = jnp.dot(o_ref[0].astype(jnp.bfloat16), w_ref[...], preferred_element_type=jnp.float32)
    r = ALPHA * x_ref[0] + g_ref[0] * out
    mu = jnp.mean(r, -1, keepdims=True)
    d = r - mu
    var = jnp.mean(d * d, -1, keepdims=True)
    y_ref[0] = d * lax.rsqrt(var + EPS) * lg_ref[...] + lb_ref[...]


def proj_residual_ln(o, w, x, gate, ln_g, ln_b, tm=256):
    b, s, k = o.shape
    d = w.shape[1]
    tm = min(tm, s)
    wb = w.astype(jnp.bfloat16)
    return pl.pallas_call(
        _post_kernel,
        grid=(b, s // tm),
        in_specs=[
            pl.BlockSpec((1, tm, k), lambda i, m: (i, m, 0)),
            pl.BlockSpec((k, d), lambda i, m: (0, 0)),
            pl.BlockSpec((1, tm, d), lambda i, m: (i, m, 0)),
            pl.BlockSpec((1, 1, d), lambda i, m: (i, 0, 0)),
            pl.BlockSpec((1, d), lambda i, m: (0, 0)),
            pl.BlockSpec((1, d), lambda i, m: (0, 0)),
        ],
        out_specs=pl.BlockSpec((1, tm, d), lambda i, m: (i, m, 0)),
        out_shape=jax.ShapeDtypeStruct((b, s, d), jnp.float32),
        compiler_params=pltpu.CompilerParams(
            dimension_semantics=("arbitrary", "arbitrary"),
            vmem_limit_bytes=VMEM_LIMIT_BYTES),
        name="proj_residual_ln",
    )(o, wb, x, gate, ln_g.reshape(1, d), ln_b.reshape(1, d))


def _resln_kernel(y_ref, x_ref, g_ref, lg_ref, lb_ref, o_ref):
    r = ALPHA * x_ref[0] + g_ref[0] * y_ref[0]
    mu = jnp.mean(r, -1, keepdims=True)
    d = r - mu
    var = jnp.mean(d * d, -1, keepdims=True)
    o_ref[0] = d * lax.rsqrt(var + EPS) * lg_ref[...] + lb_ref[...]


def residual_ln(y, x, gate, ln_g, ln_b, tm=256):
    b, s, d = x.shape
    tm = min(tm, s)
    return pl.pallas_call(
        _resln_kernel,
        grid=(b, s // tm),
        in_specs=[
            pl.BlockSpec((1, tm, d), lambda i, m: (i, m, 0)),
            pl.BlockSpec((1, tm, d), lambda i, m: (i, m, 0)),
            pl.BlockSpec((1, 1, d), lambda i, m: (i, 0, 0)),
            pl.BlockSpec((1, d), lambda i, m: (0, 0)),
            pl.BlockSpec((1, d), lambda i, m: (0, 0)),
        ],
        out_specs=pl.BlockSpec((1, tm, d), lambda i, m: (i, m, 0)),
        out_shape=jax.ShapeDtypeStruct((b, s, d), jnp.float32),
        compiler_params=pltpu.CompilerParams(
            dimension_semantics=("arbitrary", "arbitrary"),
            vmem_limit_bytes=VMEM_LIMIT_BYTES),
        name="residual_ln",
    )(y, x, gate, ln_g.reshape(1, d), ln_b.reshape(1, d))


def rms_norm(x, w):
    xf = x.astype(jnp.float32)
    return (xf * lax.rsqrt(jnp.mean(xf * xf, -1, keepdims=True) + EPS) * w).astype(x.dtype)


def l2_normalize(x):
    xf = x.astype(jnp.float32)
    return (xf * lax.rsqrt(jnp.sum(xf * xf, -1, keepdims=True) + EPS)).astype(x.dtype)


def axial_rope_tables(n_tok):
    rows = n_tok // GRID_W
    row = jnp.repeat(jnp.arange(rows, dtype=jnp.float32), GRID_W)
    col = jnp.tile(jnp.arange(GRID_W, dtype=jnp.float32), rows)
    nf = HEAD_DIM // 4
    inv = ROPE_BASE ** (-jnp.arange(nf, dtype=jnp.float32) / nf)
    ang = jnp.concatenate([row[:, None] * inv, col[:, None] * inv], -1)
    return jnp.cos(ang), jnp.sin(ang)


def apply_axial_rope(x, cos, sin):
    nf = HEAD_DIM // 4
    c = cos[:, None, :]
    s = sin[:, None, :]
    parts = []
    for a in range(2):
        xa = x[..., a * 2 * nf:(a + 1) * 2 * nf]
        x1, x2 = xa[..., :nf], xa[..., nf:]
        ca, sa = c[..., a * nf:(a + 1) * nf], s[..., a * nf:(a + 1) * nf]
        parts += [x1 * ca - x2 * sa, x2 * ca + x1 * sa]
    return jnp.concatenate(parts, -1).astype(x.dtype)


def dwconv(x, w):
    k, ch = w.shape
    return lax.conv_general_dilated(x, w[:, None, :].astype(x.dtype), (1,), [(k // 2, k // 2)],
                                    dimension_numbers=('NWC', 'WIO', 'NWC'), feature_group_count=ch)


def gqa_softmax(q, k, v, sink=None):
    b, lq, h, hd = q.shape
    kvh = k.shape[2]
    g = h // kvh
    lk = k.shape[1]
    s = jnp.einsum('bqkgd,bjkd->bkgqj', q.reshape(b, lq, kvh, g, hd), k).astype(jnp.float32) * hd ** -0.5
    if sink is not None:
        s = jnp.concatenate([s, jnp.broadcast_to(sink.astype(jnp.float32).reshape(kvh, g, 1, 1), s.shape[:-1] + (1,))], -1)
    p = jax.nn.softmax(s, axis=-1)[..., :lk].astype(v.dtype)
    return jnp.einsum('bkgqj,bjkd->bqkgd', p, v).reshape(b, lq, h * hd)


def windowed_sink_attention(q, k, v, kc, vc, sink):
    b, s, h, hd = q.shape
    kvh = k.shape[2]
    g = h // kvh
    nb = s // BLOCK
    lc = kc.shape[1]
    w3 = 3 * BLOCK
    scale = hd ** -0.5
    qb = q.reshape(b, nb, BLOCK, kvh, g, hd).swapaxes(0, 1)

    def band(t):
        tb = jnp.pad(t.reshape(b, nb, BLOCK, kvh, hd), ((0, 0), (1, 1), (0, 0), (0, 0), (0, 0)))
        return jnp.concatenate([tb[:, :-2], tb[:, 1:-1], tb[:, 2:]], axis=2).swapaxes(0, 1)

    kw, vw = band(k), band(v)
    blk = jnp.arange(nb)[:, None, None]
    qpos = blk * BLOCK + jnp.arange(BLOCK)[None, :, None]
    kpos = (blk - 1) * BLOCK + jnp.arange(w3)[None, None, :]
    valid = (jnp.abs(qpos - kpos) <= WINDOW) & (kpos >= 0) & (kpos < s)
    sink_logit = sink.astype(jnp.float32).reshape(kvh, g, 1, 1)

    def one_block(args):
        qk, kk, vk, vm = args
        s_loc = jnp.einsum('bqkgd,bjkd->bkgqj', qk, kk).astype(jnp.float32) * scale
        s_loc = jnp.where(vm, s_loc, -jnp.inf)
        s_ctx = jnp.einsum('bqkgd,bjkd->bkgqj', qk, kc).astype(jnp.float32) * scale
        s_snk = jnp.broadcast_to(sink_logit, s_loc.shape[:-1] + (1,))
        p = jax.nn.softmax(jnp.concatenate([s_loc, s_ctx, s_snk], -1), axis=-1).astype(v.dtype)
        return (jnp.einsum('bkgqj,bjkd->bqkgd', p[..., :w3], vk)
                + jnp.einsum('bkgqj,bjkd->bqkgd', p[..., w3:w3 + lc], vc))

    o = lax.map(one_block, (qb, kw, vw, valid))
    return o.swapaxes(0, 1).reshape(b, s, h * hd)


def global_block_attention(q, k, v, kc, vc):
    b, s, h, hd = q.shape
    nb = s // BLOCK
    k_all = jnp.concatenate([k, kc], 1)
    v_all = jnp.concatenate([v, vc], 1)
    qb = q.reshape(b, nb, BLOCK, h, hd).swapaxes(0, 1)
    o = lax.map(lambda qk: gqa_softmax(qk, k_all, v_all), qb)
    return o.swapaxes(0, 1).reshape(b, s, h * hd)


def hyena_filters(n, w1, b1, freq, w2, b2, w3):
    t = jnp.arange(n, dtype=jnp.float32)
    tn = t / n
    f = jnp.arange(1, HY_BANDS + 1, dtype=jnp.float32)
    ang = 2.0 * math.pi * t[:, None] * f[None, :] / n
    feat = jnp.concatenate([tn[:, None], jnp.sin(ang), jnp.cos(ang)], -1)
    hid = jnp.sin(freq * (feat @ w1 + b1))
    hid = jnp.sin(freq * (hid @ w2 + b2))
    filt = (hid @ w3).astype(jnp.float32).reshape(n, HY_ORDER, 2, HY_CH)
    max_decay = math.log(HY_TARGET) / HY_FAST_DECAY
    min_decay = math.log(HY_TARGET) / HY_SLOW_DECAY
    deltas = jnp.abs(jnp.linspace(min_decay, max_decay, HY_CH, dtype=jnp.float32))
    filt = filt * jnp.exp(-tn[:, None, None, None] * deltas)
    return filt / jnp.sum(jnp.abs(filt), axis=(0, 2), keepdims=True)


def fft_long_conv(x, hf, hb, bias):
    n = x.shape[1]
    kern = jnp.concatenate([hf, jnp.zeros_like(hf[:1]), hb[1:][::-1]], 0)
    kf = jnp.fft.rfft(kern, n=2 * n, axis=0)
    xf = jnp.fft.rfft(x.astype(jnp.float32), n=2 * n, axis=1)
    y = jnp.fft.irfft(xf * kf[None], n=2 * n, axis=1)[:, :n]
    return (y + x.astype(jnp.float32) * bias.astype(jnp.float32)).astype(x.dtype)


def hyena_operator(u, conv_w, conv_b, fw1, fb1, ffreq, fw2, fb2, fw3, hy_bias):
    uc = dwconv(u, conv_w) + conv_b
    x1, x2, v = uc[..., :HY_CH], uc[..., HY_CH:2 * HY_CH], uc[..., 2 * HY_CH:]
    filt = hyena_filters(u.shape[1], fw1, fb1, ffreq, fw2, fb2, fw3)
    v = x1 * fft_long_conv(v, filt[:, 0, 0], filt[:, 0, 1], hy_bias[0])
    v = x2 * fft_long_conv(v, filt[:, 1, 0], filt[:, 1, 1], hy_bias[1])
    return v


def gated_delta_chunked(q, k, v, beta, g, s0):
    b, n_tok, h, dk = q.shape
    dv = v.shape[-1]
    cs = GDN_CHUNK
    n = n_tok // cs

    def blk(t):
        return t.astype(jnp.float32).reshape(b, n, cs, h, -1).transpose(1, 0, 3, 2, 4)

    q = blk(q) * dk ** -0.5
    k = blk(k)
    v = blk(v)
    beta = blk(beta[..., None])[..., 0]
    gcum = jnp.cumsum(blk(g[..., None])[..., 0], -1)
    idx = jnp.arange(cs)
    incl = idx[:, None] >= idx[None, :]
    strict = idx[:, None] > idx[None, :]
    decay = jnp.exp(jnp.where(incl, gcum[..., :, None] - gcum[..., None, :], -jnp.inf))
    kb = k * beta[..., None]
    a = jnp.where(strict, jnp.einsum('nbhid,nbhjd->nbhij', kb, k) * decay, 0.0)
    rhs = jnp.concatenate([v * beta[..., None], kb * jnp.exp(gcum)[..., None]], -1)
    sol = lax.linalg.triangular_solve(jnp.eye(cs, dtype=jnp.float32) + a, rhs, left_side=True, lower=True)
    u, w = sol[..., :dv], sol[..., dv:]
    intra = jnp.einsum('nbhid,nbhjd->nbhij', q, k) * decay
    q_dec = q * jnp.exp(gcum)[..., None]
    k_dec = k * jnp.exp(gcum[..., -1:] - gcum)[..., None]
    g_last = jnp.exp(gcum[..., -1])

    def step(state, xs):
        u_i, w_i, q_i, k_i, intra_i, gl = xs
        v_new = u_i - jnp.einsum('bhcd,bhde->bhce', w_i, state)
        o_i = jnp.einsum('bhcd,bhde->bhce', q_i, state) + jnp.einsum('bhij,bhje->bhie', intra_i, v_new)
        state = state * gl[..., None, None] + jnp.einsum('bhcd,bhce->bhde', k_i, v_new)
        return state, o_i

    s_final, o = lax.scan(step, s0.astype(jnp.float32), (u, w, q_dec, k_dec, intra, g_last))
    return o.transpose(1, 0, 3, 2, 4).reshape(b, n_tok, h, dv), s_final


def gdn_inputs(p, conv_w, a_log, dt_bias):
    b, n, _ = p.shape
    qkv = jax.nn.silu(dwconv(p[..., :3 * C_W], conv_w))
    q = l2_normalize(qkv[..., :C_W].reshape(b, n, C_HEADS, C_DK))
    k = l2_normalize(qkv[..., C_W:2 * C_W].reshape(b, n, C_HEADS, C_DK))
    v = qkv[..., 2 * C_W:].reshape(b, n, C_HEADS, C_DV)
    z = p[..., 3 * C_W:4 * C_W].reshape(b, n, C_HEADS, C_DV)
    gates = p[..., 4 * C_W:4 * C_W + C_GATES].astype(jnp.float32).reshape(b, n, 4, C_HEADS)
    beta = jax.nn.sigmoid(gates[:, :, :2])
    g = -jnp.exp(a_log.astype(jnp.float32)) * jax.nn.softplus(gates[:, :, 2:] + dt_bias.astype(jnp.float32))
    return q, k, v, z, beta, g


def rev(t, flip):
    return t[:, ::-1] if flip else t


def bidirectional_gdn(lat, cx, with_ctx):
    ql, kl, vl, bl, gl = lat
    qc, kc, vc, bc, gcx = cx
    s0 = jnp.zeros((ql.shape[0], C_HEADS, C_DK, C_DV), jnp.float32)
    o_lat = 0.0
    o_ctx = 0.0
    for d in range(2):
        f = d == 1
        oc, sc = gated_delta_chunked(rev(qc, f), rev(kc, f), rev(vc, f), rev(bc[:, :, d], f), rev(gcx[:, :, d], f), s0)
        ol, _ = gated_delta_chunked(rev(ql, f), rev(kl, f), rev(vl, f), rev(bl[:, :, d], f), rev(gl[:, :, d], f), sc)
        o_lat = o_lat + rev(ol, f)
        if with_ctx:
            o_ctx = o_ctx + rev(oc, f)
    return o_lat, o_ctx


def even_mixer(p, pc, cos, sin, sink, conv_w, conv_b, fw1, fb1, ffreq, fw2, fb2, fw3, hy_bias, with_ctx):
    def attn_qkv(t):
        b, n, _ = t.shape
        q = t[..., :A_Q].reshape(b, n, A_HEADS, HEAD_DIM)
        k = t[..., A_Q:A_Q + A_KV].reshape(b, n, A_KV_HEADS, HEAD_DIM)
        v = t[..., A_Q + A_KV:A_Q + 2 * A_KV].reshape(b, n, A_KV_HEADS, HEAD_DIM)
        return q, k, v

    q, k, v = attn_qkv(p)
    qc, kc, vc = attn_qkv(pc)
    o_a = windowed_sink_attention(apply_axial_rope(q, cos, sin), apply_axial_rope(k, cos, sin), v, kc, vc, sink)
    hy_args = (conv_w, conv_b, fw1, fb1, ffreq, fw2, fb2, fw3, hy_bias)
    o_b = hyena_operator(p[..., A_Q + 2 * A_KV:], *hy_args)
    out = jnp.concatenate([o_a, o_b], -1)
    out_c = None
    if with_ctx:
        o_ac = gqa_softmax(qc, kc, vc, sink)
        o_bc = hyena_operator(pc[..., A_Q + 2 * A_KV:], *hy_args)
        out_c = jnp.concatenate([o_ac, o_bc], -1)
    return out, out_c


def odd_mixer(p, pc, cos, sin, conv_w, a_log, dt_bias, gnorm_w, qnorm_w, knorm_w, with_ctx):
    ql, kl, vl, zl, bl, gl = gdn_inputs(p, conv_w, a_log, dt_bias)
    qc, kc, vc, zc, bc, gcx = gdn_inputs(pc, conv_w, a_log, dt_bias)
    o_l, o_c = bidirectional_gdn((ql, kl, vl, bl, gl), (qc, kc, vc, bc, gcx), with_ctx)

    def gated_out(o, z):
        return (rms_norm(o.astype(z.dtype), gnorm_w) * jax.nn.silu(z)).reshape(z.shape[0], z.shape[1], C_W)

    def attn_qkv(t):
        b, n, _ = t.shape
        base = 4 * C_W + C_GATES
        q = rms_norm(t[..., base:base + D_Q].reshape(b, n, D_HEADS, HEAD_DIM), qnorm_w)
        k = rms_norm(t[..., base + D_Q:base + D_Q + D_KV].reshape(b, n, D_KV_HEADS, HEAD_DIM), knorm_w)
        v = t[..., base + D_Q + D_KV:].reshape(b, n, D_KV_HEADS, HEAD_DIM)
        return q, k, v

    qd, kd, vd = attn_qkv(p)
    qdc, kdc, vdc = attn_qkv(pc)
    o_d = global_block_attention(apply_axial_rope(qd, cos, sin), apply_axial_rope(kd, cos, sin), vd, kdc, vdc)
    out = jnp.concatenate([gated_out(o_l, zl), o_d], -1)
    out_c = None
    if with_ctx:
        out_c = jnp.concatenate([gated_out(o_c, zc), gqa_softmax(qdc, kdc, vdc)], -1)
    return out, out_c


def peer_ffn(h, q_all, k1, k2, u_tab, v_tab):
    b, n, d = h.shape
    tokens = h.reshape(-1, PEER_CHUNK, d)
    qs = q_all.reshape(-1, PEER_CHUNK, q_all.shape[-1])
    half = PEER_QDIM // 2
    n_cand = PEER_TOPK * PEER_TOPK

    def one_chunk(args):
        xc, qc = args
        t = xc.shape[0]
        q = qc.reshape(t, PEER_HEADS, 2, half)
        s1 = jnp.einsum('thd,hnd->thn', q[:, :, 0], k1).astype(jnp.float32)
        s2 = jnp.einsum('thd,hnd->thn', q[:, :, 1], k2).astype(jnp.float32)
        v1, i1 = lax.top_k(s1, PEER_TOPK)
        v2, i2 = lax.top_k(s2, PEER_TOPK)
        cand = (v1[..., :, None] + v2[..., None, :]).reshape(t, PEER_HEADS, n_cand)
        cand_id = (i1[..., :, None] * PEER_NKEYS + i2[..., None, :]).reshape(t, PEER_HEADS, n_cand)
        best, pos = lax.top_k(cand, PEER_TOPK)
        expert = jnp.take_along_axis(cand_id, pos, axis=-1)
        gate = jax.nn.softmax(best, axis=-1)
        u_sel = jnp.take(u_tab, expert, axis=0)
        v_sel = jnp.take(v_tab, expert, axis=0)
        act = jax.nn.gelu(jnp.einsum('td,thkd->thk', xc, u_sel).astype(jnp.float32), approximate=False)
        return jnp.einsum('thk,thkd->td', (gate * act).astype(v_sel.dtype), v_sel)

    return lax.map(one_chunk, (tokens, qs)).reshape(b, n, d)


def kernel(x, c, ctx, c_ctx, ada_w, ada_b, ln1_g, ln1_b, ln2_g, ln2_b, peer_wq, peer_k1, peer_k2, peer_u, peer_v, ev_w_in, ev_w_out, ev_sink, ev_conv_w, ev_conv_b, ev_filt_w1, ev_filt_b1, ev_filt_freq, ev_filt_w2, ev_filt_b2, ev_filt_w3, ev_hy_bias, od_w_in, od_w_out, od_conv_w, od_a_log, od_dt_bias, od_gnorm_w, od_qnorm_w, od_knorm_w):
    cos, sin = axial_rope_tables(x.shape[1])
    bsz = x.shape[0]
    silu_c = jax.nn.silu(c)
    silu_cc = jax.nn.silu(c_ctx)
    for i in range(DEPTH):
        with_ctx = i < DEPTH - 1
        j = i // 2
        mod = (silu_c @ ada_w[i] + ada_b[i])[:, None, :]
        modc = jnp.broadcast_to((silu_cc @ ada_w[i] + ada_b[i])[None, None, :], (bsz, 1, 6 * D_MODEL))
        sh1, sc1, g1, sh2, sc2, g2 = jnp.split(mod, 6, axis=-1)
        sh1c, sc1c, g1c, sh2c, sc2c, g2c = jnp.split(modc, 6, axis=-1)
        if i % 2 == 0:
            p = mod_matmul(x, sh1, sc1, ev_w_in[j])
            pc = mod_matmul(ctx, sh1c, sc1c, ev_w_in[j])
            out, out_c = even_mixer(p, pc, cos, sin, ev_sink[j], ev_conv_w[j], ev_conv_b[j],
                                    ev_filt_w1[j], ev_filt_b1[j], ev_filt_freq[j], ev_filt_w2[j], ev_filt_b2[j],
                                    ev_filt_w3[j], ev_hy_bias[j], with_ctx)
            w_out = ev_w_out[j]
        else:
            p = mod_matmul(x, sh1, sc1, od_w_in[j])
            pc = mod_matmul(ctx, sh1c, sc1c, od_w_in[j])
            out, out_c = odd_mixer(p, pc, cos, sin, od_conv_w[j], od_a_log[j], od_dt_bias[j],
                                   od_gnorm_w[j], od_qnorm_w[j], od_knorm_w[j], with_ctx)
            w_out = od_w_out[j]
        x = proj_residual_ln(out, w_out, x, g1, ln1_g[i], ln1_b[i])
        q_all = mod_matmul(x, sh2, sc2, peer_wq[i])
        y = peer_ffn(x * (1.0 + sc2) + sh2, q_all, peer_k1[i], peer_k2[i], peer_u[i], peer_v[i])
        x = residual_ln(y, x, g2, ln2_g[i], ln2_b[i])
        if with_ctx:
            ctx = proj_residual_ln(out_c, w_out, ctx, g1c, ln1_g[i], ln1_b[i])
            qc_all = mod_matmul(ctx, sh2c, sc2c, peer_wq[i])
            yc = peer_ffn(ctx * (1.0 + sc2c) + sh2c, qc_all, peer_k1[i], peer_k2[i], peer_u[i], peer_v[i])
            ctx = residual_ln(yc, ctx, g2c, ln2_g[i], ln2_b[i])
    return x
```

```python
import functools
import math

import jax
import jax.numpy as jnp
from jax import lax
from jax.experimental import pallas as pl
from jax.experimental.pallas import tpu as pltpu

D_MODEL = 1024
DEPTH = 2
GRID_W = 64
HEAD_DIM = 64
BLOCK = 128
ROPE_BASE = 10000.0
EPS = 1e-6

A_HEADS = 8
A_KV_HEADS = 2
WINDOW = 128

HY_CH = 512
HY_ORDER = 2
HY_EMB = 33
HY_BANDS = (HY_EMB - 1) // 2
HY_FAST_DECAY = 0.3
HY_SLOW_DECAY = 1.5
HY_TARGET = 1e-2

C_HEADS = 4
C_DK = 128
C_DV = 128
GDN_CHUNK = 64

D_HEADS = 8
D_KV_HEADS = 2

PEER_HEADS = 8
PEER_NKEYS = 128
PEER_QDIM = 256
PEER_TOPK = 16
PEER_CHUNK = 128

ALPHA = (2 * DEPTH) ** 0.25

A_Q = A_HEADS * HEAD_DIM
A_KV = A_KV_HEADS * HEAD_DIM
C_W = C_HEADS * C_DK
C_GATES = 4 * C_HEADS
D_Q = D_HEADS * HEAD_DIM
D_KV = D_KV_HEADS * HEAD_DIM

VMEM_LIMIT_BYTES = 48 * 1024 * 1024


def _modmm_kernel(x_ref, sh_ref, sc_ref, w_ref, o_ref):
    h = x_ref[0] * (1.0 + sc_ref[0]) + sh_ref[0]
    o_ref[0] = jnp.dot(h.astype(jnp.bfloat16), w_ref[...], preferred_element_type=jnp.float32)


def mod_matmul(x, shift, scale, w, tm=512, tn=None):
    b, s, k = x.shape
    n = w.shape[1]
    tm = min(tm, s)
    tn = n if tn is None else tn
    wb = w.astype(jnp.bfloat16)
    return pl.pallas_call(
        _modmm_kernel,
        grid=(b, n // tn, s // tm),
        in_specs=[
            pl.BlockSpec((1, tm, k), lambda i, j, m: (i, m, 0)),
            pl.BlockSpec((1, 1, k), lambda i, j, m: (i, 0, 0)),
            pl.BlockSpec((1, 1, k), lambda i, j, m: (i, 0, 0)),
            pl.BlockSpec((k, tn), lambda i, j, m: (0, j)),
        ],
        out_specs=pl.BlockSpec((1, tm, tn), lambda i, j, m: (i, m, j)),
        out_shape=jax.ShapeDtypeStruct((b, s, n), jnp.float32),
        compiler_params=pltpu.CompilerParams(
            dimension_semantics=("arbitrary", "arbitrary", "arbitrary"),
            vmem_limit_bytes=VMEM_LIMIT_BYTES),
        name="mod_matmul",
    )(x, shift, scale, wb)


def _post_kernel(o_ref, w_ref, x_ref, g_ref, lg_ref, lb_ref, y_ref):
    out = jnp.dot(o_ref[0].astype(jnp.bfloat16), w_ref[...], preferred_element_type=jnp.float32)
    r = ALPHA * x_ref[0] + g_ref[0] * out
    mu = jnp.mean(r, -1, keepdims=True)
    d = r - mu
    var = jnp.mean(d * d, -1, keepdims=True)
    y_ref[0] = d * lax.rsqrt(var + EPS) * lg_ref[...] + lb_ref[...]


def proj_residual_ln(o, w, x, gate, ln_g, ln_b, tm=256):
    b, s, k = o.shape
    d = w.shape[1]
    tm = min(tm, s)
    wb = w.astype(jnp.bfloat16)
    return pl.pallas_call(
        _post_kernel,
        grid=(b, s // tm),
        in_specs=[
            pl.BlockSpec((1, tm, k), lambda i, m: (i, m, 0)),
            pl.BlockSpec((k, d), lambda i, m: (0, 0)),
            pl.BlockSpec((1, tm, d), lambda i, m: (i, m, 0)),
            pl.BlockSpec((1, 1, d), lambda i, m: (i, 0, 0)),
            pl.BlockSpec((1, d), lambda i, m: (0, 0)),
            pl.BlockSpec((1, d), lambda i, m: (0, 0)),
        ],
        out_specs=pl.BlockSpec((1, tm, d), lambda i, m: (i, m, 0)),
        out_shape=jax.ShapeDtypeStruct((b, s, d), jnp.float32),
        compiler_params=pltpu.CompilerParams(
            dimension_semantics=("arbitrary", "arbitrary"),
            vmem_limit_bytes=VMEM_LIMIT_BYTES),
        name="proj_residual_ln",
    )(o, wb, x, gate, ln_g.reshape(1, d), ln_b.reshape(1, d))


def _resln_kernel(y_ref, x_ref, g_ref, lg_ref, lb_ref, o_ref):
    r = ALPHA * x_ref[0] + g_ref[0] * y_ref[0]
    mu = jnp.mean(r, -1, keepdims=True)
    d = r - mu
    var = jnp.mean(d * d, -1, keepdims=True)
    o_ref[0] = d * lax.rsqrt(var + EPS) * lg_ref[...] + lb_ref[...]


def residual_ln(y, x, gate, ln_g, ln_b, tm=256):
    b, s, d = x.shape
    tm = min(tm, s)
    return pl.pallas_call(
        _resln_kernel,
        grid=(b, s // tm),
        in_specs=[
            pl.BlockSpec((1, tm, d), lambda i, m: (i, m, 0)),
            pl.BlockSpec((1, tm, d), lambda i, m: (i, m, 0)),
            pl.BlockSpec((1, 1, d), lambda i, m: (i, 0, 0)),
            pl.BlockSpec((1, d), lambda i, m: (0, 0)),
            pl.BlockSpec((1, d), lambda i, m: (0, 0)),
        ],
        out_specs=pl.BlockSpec((1, tm, d), lambda i, m: (i, m, 0)),
        out_shape=jax.ShapeDtypeStruct((b, s, d), jnp.float32),
        compiler_params=pltpu.CompilerParams(
            dimension_semantics=("arbitrary", "arbitrary"),
            vmem_limit_bytes=VMEM_LIMIT_BYTES),
        name="residual_ln",
    )(y, x, gate, ln_g.reshape(1, d), ln_b.reshape(1, d))


_NT = (((1,), (1,)), ((), ()))


def _top16(s, payload=None):
    n = s.shape[0]
    iota = lax.broadcasted_iota(jnp.int32, s.shape, 0)
    vals, ids = [], []
    for _ in range(PEER_TOPK):
        m = jnp.max(s, axis=0, keepdims=True)
        pos = jnp.min(jnp.where(s == m, iota, n), axis=0, keepdims=True)
        hit = iota == pos
        vals.append(m)
        ids.append(pos if payload is None else jnp.max(jnp.where(hit, payload, -1), axis=0, keepdims=True))
        s = jnp.where(hit, -jnp.inf, s)
    return jnp.concatenate(vals, 0), jnp.concatenate(ids, 0)


def _peer_topk_kernel(q_ref, k1_ref, k2_ref, eid_ref, gate_ref, eid_s, gate_s):
    half = PEER_QDIM // 2

    def head(h, carry):
        off = pl.multiple_of(h * PEER_QDIM, PEER_QDIM)
        q1 = q_ref[:, pl.ds(off, half)]
        q2 = q_ref[:, pl.ds(off + half, half)]
        s1 = lax.dot_general(k1_ref[h], q1, _NT, precision=lax.Precision.HIGHEST,
                             preferred_element_type=jnp.float32)
        s2 = lax.dot_general(k2_ref[h], q2, _NT, precision=lax.Precision.HIGHEST,
                             preferred_element_type=jnp.float32)
        v1, i1 = _top16(s1)
        v2, i2 = _top16(s2)
        cand = jnp.concatenate([v1[i:i + 1] + v2 for i in range(PEER_TOPK)], 0)
        cid = jnp.concatenate([i1[i:i + 1] * PEER_NKEYS + i2 for i in range(PEER_TOPK)], 0)
        best, eid = _top16(cand, cid)
        e = jnp.exp(best - best[0:1])
        gate = e / jnp.sum(e, axis=0, keepdims=True)
        row = pl.multiple_of(h * PEER_TOPK, PEER_TOPK)
        eid_s[pl.ds(row, PEER_TOPK), :] = eid
        gate_s[pl.ds(row, PEER_TOPK), :] = gate
        return carry

    lax.fori_loop(0, PEER_HEADS, head, 0)
    eid_ref[...] = eid_s[...].T
    gate_ref[...] = gate_s[...].T


def peer_topk(q, k1, k2, tt=256):
    t = q.shape[0]
    tt = min(tt, t)
    nsel = PEER_HEADS * PEER_TOPK
    return pl.pallas_call(
        _peer_topk_kernel,
        grid=(t // tt,),
        in_specs=[
            pl.BlockSpec((tt, q.shape[1]), lambda i: (i, 0)),
            pl.BlockSpec(k1.shape, lambda i: (0, 0, 0)),
            pl.BlockSpec(k2.shape, lambda i: (0, 0, 0)),
        ],
        out_specs=[pl.BlockSpec((tt, nsel), lambda i: (i, 0)),
                   pl.BlockSpec((tt, nsel), lambda i: (i, 0))],
        out_shape=[jax.ShapeDtypeStruct((t, nsel), jnp.int32),
                   jax.ShapeDtypeStruct((t, nsel), jnp.float32)],
        scratch_shapes=[pltpu.VMEM((nsel, tt), jnp.int32), pltpu.VMEM((nsel, tt), jnp.float32)],
        compiler_params=pltpu.CompilerParams(
            dimension_semantics=("arbitrary",), vmem_limit_bytes=VMEM_LIMIT_BYTES),
        name="peer_topk",
    )(q, k1, k2)


def _peer_w_kernel(e_ref, g_ref, w_ref):
    nk = PEER_NKEYS
    iota = lax.broadcasted_iota(jnp.int32, (nk, e_ref.shape[1]), 0)

    def tok(t, carry):
        e = e_ref[pl.ds(t, 1), :]
        g = g_ref[pl.ds(t, 1), :]
        a_t = jnp.where(iota == (e >> 7), g, 0.0).astype(jnp.bfloat16)
        b_t = jnp.where(iota == (e & (nk - 1)), 1.0, 0.0).astype(jnp.bfloat16)
        w = lax.dot_general(a_t, b_t, _NT, preferred_element_type=jnp.float32)
        w_ref[t] = w.astype(jnp.bfloat16)
        return carry

    lax.fori_loop(0, e_ref.shape[0], tok, 0, unroll=4)


def peer_dense_gates(eid, gate, tt=128):
    t, nsel = eid.shape
    tt = min(tt, t)
    nk = PEER_NKEYS
    w = pl.pallas_call(
        _peer_w_kernel,
        grid=(t // tt,),
        in_specs=[pl.BlockSpec((tt, nsel), lambda i: (i, 0)),
                  pl.BlockSpec((tt, nsel), lambda i: (i, 0))],
        out_specs=pl.BlockSpec((tt, nk, nk), lambda i: (i, 0, 0)),
        out_shape=jax.ShapeDtypeStruct((t, nk, nk), jnp.bfloat16),
        compiler_params=pltpu.CompilerParams(
            dimension_semantics=("arbitrary",), vmem_limit_bytes=VMEM_LIMIT_BYTES),
        name="peer_dense_gates",
    )(eid, gate)
    return w.reshape(t, nk * nk)


def _peer_expert_kernel(x_ref, sh_ref, sc_ref, w_ref, u_ref, v_ref, g_ref, lg_ref, lb_ref, y_ref, xm_s, acc_s):
    e = pl.program_id(2)

    @pl.when(e == 0)
    def _():
        xm_s[...] = (x_ref[0] * (1.0 + sc_ref[0]) + sh_ref[0]).astype(jnp.bfloat16)
        acc_s[...] = jnp.zeros_like(acc_s)

    h = lax.dot_general(xm_s[...], u_ref[...], _NT, preferred_element_type=jnp.float32)
    gelu = 0.5 * h * (1.0 + lax.erf(h * (2.0 ** -0.5)))
    a = gelu * w_ref[0].astype(jnp.float32)
    acc_s[...] += jnp.dot(a.astype(jnp.bfloat16), v_ref[...], preferred_element_type=jnp.float32)

    @pl.when(e == pl.num_programs(2) - 1)
    def _():
        r = ALPHA * x_ref[0] + g_ref[0] * acc_s[...]
        mu = jnp.mean(r, -1, keepdims=True)
        d = r - mu
        var = jnp.mean(d * d, -1, keepdims=True)
        y_ref[0] = d * lax.rsqrt(var + EPS) * lg_ref[...] + lb_ref[...]


def peer_experts_ln(x, shift, scale, w, u_tab, v_tab, gate, ln_g, ln_b, tt=512, te=512):
    b, s, d = x.shape
    tt = min(tt, s)
    ne = u_tab.shape[0]
    w3 = w.reshape(b, s, ne)
    return pl.pallas_call(
        _peer_expert_kernel,
        grid=(b, s // tt, ne // te),
        in_specs=[
            pl.BlockSpec((1, tt, d), lambda i, m, e: (i, m, 0)),
            pl.BlockSpec((1, 1, d), lambda i, m, e: (i, 0, 0)),
            pl.BlockSpec((1, 1, d), lambda i, m, e: (i, 0, 0)),
            pl.BlockSpec((1, tt, te), lambda i, m, e: (i, m, e)),
            pl.BlockSpec((te, d), lambda i, m, e: (e, 0)),
            pl.BlockSpec((te, d), lambda i, m, e: (e, 0)),
            pl.BlockSpec((1, 1, d), lambda i, m, e: (i, 0, 0)),
            pl.BlockSpec((1, d), lambda i, m, e: (0, 0)),
            pl.BlockSpec((1, d), lambda i, m, e: (0, 0)),
        ],
        out_specs=pl.BlockSpec((1, tt, d), lambda i, m, e: (i, m, 0)),
        out_shape=jax.ShapeDtypeStruct((b, s, d), jnp.float32),
        scratch_shapes=[pltpu.VMEM((tt, d), jnp.bfloat16), pltpu.VMEM((tt, d), jnp.float32)],
        compiler_params=pltpu.CompilerParams(
            dimension_semantics=("arbitrary", "arbitrary", "arbitrary"),
            vmem_limit_bytes=VMEM_LIMIT_BYTES),
        name="peer_experts_ln",
    )(x, shift, scale, w3, u_tab, v_tab, gate, ln_g.reshape(1, d), ln_b.reshape(1, d))


def peer_block(x, shift, scale, gate, wq, k1, k2, u_bf, v_bf, ln_g, ln_b):
    b, s, d = x.shape
    q_all = mod_matmul(x, shift, scale, wq).reshape(b * s, -1)
    eid, gsel = peer_topk(q_all, k1, k2)
    w = peer_dense_gates(eid, gsel)
    return peer_experts_ln(x, shift, scale, w, u_bf, v_bf, gate, ln_g, ln_b)


def rms_norm(x, w):
    xf = x.astype(jnp.float32)
    return (xf * lax.rsqrt(jnp.mean(xf * xf, -1, keepdims=True) + EPS) * w).astype(x.dtype)


def l2_normalize(x):
    xf = x.astype(jnp.float32)
    return (xf * lax.rsqrt(jnp.sum(xf * xf, -1, keepdims=True) + EPS)).astype(x.dtype)


def axial_rope_tables(n_tok):
    rows = n_tok // GRID_W
    row = jnp.repeat(jnp.arange(rows, dtype=jnp.float32), GRID_W)
    col = jnp.tile(jnp.arange(GRID_W, dtype=jnp.float32), rows)
    nf = HEAD_DIM // 4
    inv = ROPE_BASE ** (-jnp.arange(nf, dtype=jnp.float32) / nf)
    ang = jnp.concatenate([row[:, None] * inv, col[:, None] * inv], -1)
    return jnp.cos(ang), jnp.sin(ang)


def apply_axial_rope(x, cos, sin):
    nf = HEAD_DIM // 4
    c = cos[:, None, :]
    s = sin[:, None, :]
    parts = []
    for a in range(2):
        xa = x[..., a * 2 * nf:(a + 1) * 2 * nf]
        x1, x2 = xa[..., :nf], xa[..., nf:]
        ca, sa = c[..., a * nf:(a + 1) * nf], s[..., a * nf:(a + 1) * nf]
        parts += [x1 * ca - x2 * sa, x2 * ca + x1 * sa]
    return jnp.concatenate(parts, -1).astype(x.dtype)


def dwconv(x, w):
    k, ch = w.shape
    return lax.conv_general_dilated(x, w[:, None, :].astype(x.dtype), (1,), [(k // 2, k // 2)],
                                    dimension_numbers=('NWC', 'WIO', 'NWC'), feature_group_count=ch)


def gqa_softmax(q, k, v, sink=None):
    b, lq, h, hd = q.shape
    kvh = k.shape[2]
    g = h // kvh
    lk = k.shape[1]
    s = jnp.einsum('bqkgd,bjkd->bkgqj', q.reshape(b, lq, kvh, g, hd), k).astype(jnp.float32) * hd ** -0.5
    if sink is not None:
        s = jnp.concatenate([s, jnp.broadcast_to(sink.astype(jnp.float32).reshape(kvh, g, 1, 1), s.shape[:-1] + (1,))], -1)
    p = jax.nn.softmax(s, axis=-1)[..., :lk].astype(v.dtype)
    return jnp.einsum('bkgqj,bjkd->bqkgd', p, v).reshape(b, lq, h * hd)


def windowed_sink_attention(q, k, v, kc, vc, sink):
    b, s, h, hd = q.shape
    kvh = k.shape[2]
    g = h // kvh
    nb = s // BLOCK
    lc = kc.shape[1]
    w3 = 3 * BLOCK
    scale = hd ** -0.5
    qb = q.reshape(b, nb, BLOCK, kvh, g, hd).swapaxes(0, 1)

    def band(t):
        tb = jnp.pad(t.reshape(b, nb, BLOCK, kvh, hd), ((0, 0), (1, 1), (0, 0), (0, 0), (0, 0)))
        return jnp.concatenate([tb[:, :-2], tb[:, 1:-1], tb[:, 2:]], axis=2).swapaxes(0, 1)

    kw, vw = band(k), band(v)
    blk = jnp.arange(nb)[:, None, None]
    qpos = blk * BLOCK + jnp.arange(BLOCK)[None, :, None]
    kpos = (blk - 1) * BLOCK + jnp.arange(w3)[None, None, :]
    valid = (jnp.abs(qpos - kpos) <= WINDOW) & (kpos >= 0) & (kpos < s)
    sink_logit = sink.astype(jnp.float32).reshape(kvh, g, 1, 1)

    def one_block(args):
        qk, kk, vk, vm = args
        s_loc = jnp.einsum('bqkgd,bjkd->bkgqj', qk, kk).astype(jnp.float32) * scale
        s_loc = jnp.where(vm, s_loc, -jnp.inf)
        s_ctx = jnp.einsum('bqkgd,bjkd->bkgqj', qk, kc).astype(jnp.float32) * scale
        s_snk = jnp.broadcast_to(sink_logit, s_loc.shape[:-1] + (1,))
        p = jax.nn.softmax(jnp.concatenate([s_loc, s_ctx, s_snk], -1), axis=-1).astype(v.dtype)
        return (jnp.einsum('bkgqj,bjkd->bqkgd', p[..., :w3], vk)
                + jnp.einsum('bkgqj,bjkd->bqkgd', p[..., w3:w3 + lc], vc))

    o = lax.map(one_block, (qb, kw, vw, valid))
    return o.swapaxes(0, 1).reshape(b, s, h * hd)


def global_block_attention(q, k, v, kc, vc):
    b, s, h, hd = q.shape
    nb = s // BLOCK
    k_all = jnp.concatenate([k, kc], 1)
    v_all = jnp.concatenate([v, vc], 1)
    qb = q.reshape(b, nb, BLOCK, h, hd).swapaxes(0, 1)
    o = lax.map(lambda qk: gqa_softmax(qk, k_all, v_all), qb)
    return o.swapaxes(0, 1).reshape(b, s, h * hd)


def hyena_filters(n, w1, b1, freq, w2, b2, w3):
    t = jnp.arange(n, dtype=jnp.float32)
    tn = t / n
    f = jnp.arange(1, HY_BANDS + 1, dtype=jnp.float32)
    ang = 2.0 * math.pi * t[:, None] * f[None, :] / n
    feat = jnp.concatenate([tn[:, None], jnp.sin(ang), jnp.cos(ang)], -1)
    hid = jnp.sin(freq * (feat @ w1 + b1))
    hid = jnp.sin(freq * (hid @ w2 + b2))
    filt = (hid @ w3).astype(jnp.float32).reshape(n, HY_ORDER, 2, HY_CH)
    max_decay = math.log(HY_TARGET) / HY_FAST_DECAY
    min_decay = math.log(HY_TARGET) / HY_SLOW_DECAY
    deltas = jnp.abs(jnp.linspace(min_decay, max_decay, HY_CH, dtype=jnp.float32))
    filt = filt * jnp.exp(-tn[:, None, None, None] * deltas)
    return filt / jnp.sum(jnp.abs(filt), axis=(0, 2), keepdims=True)


def fft_long_conv(x, hf, hb, bias):
    n = x.shape[1]
    kern = jnp.concatenate([hf, jnp.zeros_like(hf[:1]), hb[1:][::-1]], 0)
    kf = jnp.fft.rfft(kern, n=2 * n, axis=0)
    xf = jnp.fft.rfft(x.astype(jnp.float32), n=2 * n, axis=1)
    y = jnp.fft.irfft(xf * kf[None], n=2 * n, axis=1)[:, :n]
    return (y + x.astype(jnp.float32) * bias.astype(jnp.float32)).astype(x.dtype)


def hyena_operator(u, conv_w, conv_b, fw1, fb1, ffreq, fw2, fb2, fw3, hy_bias):
    uc = dwconv(u, conv_w) + conv_b
    x1, x2, v = uc[..., :HY_CH], uc[..., HY_CH:2 * HY_CH], uc[..., 2 * HY_CH:]
    filt = hyena_filters(u.shape[1], fw1, fb1, ffreq, fw2, fb2, fw3)
    v = x1 * fft_long_conv(v, filt[:, 0, 0], filt[:, 0, 1], hy_bias[0])
    v = x2 * fft_long_conv(v, filt[:, 1, 0], filt[:, 1, 1], hy_bias[1])
    return v


def gated_delta_chunked(q, k, v, beta, g, s0):
    b, n_tok, h, dk = q.shape
    dv = v.shape[-1]
    cs = GDN_CHUNK
    n = n_tok // cs

    def blk(t):
        return t.astype(jnp.float32).reshape(b, n, cs, h, -1).transpose(1, 0, 3, 2, 4)

    q = blk(q) * dk ** -0.5
    k = blk(k)
    v = blk(v)
    beta = blk(beta[..., None])[..., 0]
    gcum = jnp.cumsum(blk(g[..., None])[..., 0], -1)
    idx = jnp.arange(cs)
    incl = idx[:, None] >= idx[None, :]
    strict = idx[:, None] > idx[None, :]
    decay = jnp.exp(jnp.where(incl, gcum[..., :, None] - gcum[..., None, :], -jnp.inf))
    kb = k * beta[..., None]
    a = jnp.where(strict, jnp.einsum('nbhid,nbhjd->nbhij', kb, k) * decay, 0.0)
    rhs = jnp.concatenate([v * beta[..., None], kb * jnp.exp(gcum)[..., None]], -1)
    sol = lax.linalg.triangular_solve(jnp.eye(cs, dtype=jnp.float32) + a, rhs, left_side=True, lower=True)
    u, w = sol[..., :dv], sol[..., dv:]
    intra = jnp.einsum('nbhid,nbhjd->nbhij', q, k) * decay
    q_dec = q * jnp.exp(gcum)[..., None]
    k_dec = k * jnp.exp(gcum[..., -1:] - gcum)[..., None]
    g_last = jnp.exp(gcum[..., -1])

    def step(state, xs):
        u_i, w_i, q_i, k_i, intra_i, gl = xs
        v_new = u_i - jnp.einsum('bhcd,bhde->bhce', w_i, state)
        o_i = jnp.einsum('bhcd,bhde->bhce', q_i, state) + jnp.einsum('bhij,bhje->bhie', intra_i, v_new)
        state = state * gl[..., None, None] + jnp.einsum('bhcd,bhce->bhde', k_i, v_new)
        return state, o_i

    s_final, o = lax.scan(step, s0.astype(jnp.float32), (u, w, q_dec, k_dec, intra, g_last))
    return o.transpose(1, 0, 3, 2, 4).reshape(b, n_tok, h, dv), s_final


def gdn_inputs(p, conv_w, a_log, dt_bias):
    b, n, _ = p.shape
    qkv = jax.nn.silu(dwconv(p[..., :3 * C_W], conv_w))
    q = l2_normalize(qkv[..., :C_W].reshape(b, n, C_HEADS, C_DK))
    k = l2_normalize(qkv[..., C_W:2 * C_W].reshape(b, n, C_HEADS, C_DK))
    v = qkv[..., 2 * C_W:].reshape(b, n, C_HEADS, C_DV)
    z = p[..., 3 * C_W:4 * C_W].reshape(b, n, C_HEADS, C_DV)
    gates = p[..., 4 * C_W:4 * C_W + C_GATES].astype(jnp.float32).reshape(b, n, 4, C_HEADS)
    beta = jax.nn.sigmoid(gates[:, :, :2])
    g = -jnp.exp(a_log.astype(jnp.float32)) * jax.nn.softplus(gates[:, :, 2:] + dt_bias.astype(jnp.float32))
    return q, k, v, z, beta, g


def rev(t, flip):
    return t[:, ::-1] if flip else t


def bidirectional_gdn(lat, cx, with_ctx):
    ql, kl, vl, bl, gl = lat
    qc, kc, vc, bc, gcx = cx
    s0 = jnp.zeros((ql.shape[0], C_HEADS, C_DK, C_DV), jnp.float32)
    o_lat = 0.0
    o_ctx = 0.0
    for d in range(2):
        f = d == 1
        oc, sc = gated_delta_chunked(rev(qc, f), rev(kc, f), rev(vc, f), rev(bc[:, :, d], f), rev(gcx[:, :, d], f), s0)
        ol, _ = gated_delta_chunked(rev(ql, f), rev(kl, f), rev(vl, f), rev(bl[:, :, d], f), rev(gl[:, :, d], f), sc)
        o_lat = o_lat + rev(ol, f)
        if with_ctx:
            o_ctx = o_ctx + rev(oc, f)
    return o_lat, o_ctx


def even_mixer(p, pc, cos, sin, sink, conv_w, conv_b, fw1, fb1, ffreq, fw2, fb2, fw3, hy_bias, with_ctx):
    def attn_qkv(t):
        b, n, _ = t.shape
        q = t[..., :A_Q].reshape(b, n, A_HEADS, HEAD_DIM)
        k = t[..., A_Q:A_Q + A_KV].reshape(b, n, A_KV_HEADS, HEAD_DIM)
        v = t[..., A_Q + A_KV:A_Q + 2 * A_KV].reshape(b, n, A_KV_HEADS, HEAD_DIM)
        return q, k, v

    q, k, v = attn_qkv(p)
    qc, kc, vc = attn_qkv(pc)
    o_a = windowed_sink_attention(apply_axial_rope(q, cos, sin), apply_axial_rope(k, cos, sin), v, kc, vc, sink)
    hy_args = (conv_w, conv_b, fw1, fb1, ffreq, fw2, fb2, fw3, hy_bias)
    o_b = hyena_operator(p[..., A_Q + 2 * A_KV:], *hy_args)
    out = jnp.concatenate([o_a, o_b], -1)
    out_c = None
    if with_ctx:
        o_ac = gqa_softmax(qc, kc, vc, sink)
        o_bc = hyena_operator(pc[..., A_Q + 2 * A_KV:], *hy_args)
        out_c = jnp.concatenate([o_ac, o_bc], -1)
    return out, out_c


def odd_mixer(p, pc, cos, sin, conv_w, a_log, dt_bias, gnorm_w, qnorm_w, knorm_w, with_ctx):
    ql, kl, vl, zl, bl, gl = gdn_inputs(p, conv_w, a_log, dt_bias)
    qc, kc, vc, zc, bc, gcx = gdn_inputs(pc, conv_w, a_log, dt_bias)
    o_l, o_c = bidirectional_gdn((ql, kl, vl, bl, gl), (qc, kc, vc, bc, gcx), with_ctx)

    def gated_out(o, z):
        return (rms_norm(o.astype(z.dtype), gnorm_w) * jax.nn.silu(z)).reshape(z.shape[0], z.shape[1], C_W)

    def attn_qkv(t):
        b, n, _ = t.shape
        base = 4 * C_W + C_GATES
        q = rms_norm(t[..., base:base + D_Q].reshape(b, n, D_HEADS, HEAD_DIM), qnorm_w)
        k = rms_norm(t[..., base + D_Q:base + D_Q + D_KV].reshape(b, n, D_KV_HEADS, HEAD_DIM), knorm_w)
        v = t[..., base + D_Q + D_KV:].reshape(b, n, D_KV_HEADS, HEAD_DIM)
        return q, k, v

    qd, kd, vd = attn_qkv(p)
    qdc, kdc, vdc = attn_qkv(pc)
    o_d = global_block_attention(apply_axial_rope(qd, cos, sin), apply_axial_rope(kd, cos, sin), vd, kdc, vdc)
    out = jnp.concatenate([gated_out(o_l, zl), o_d], -1)
    out_c = None
    if with_ctx:
        out_c = jnp.concatenate([gated_out(o_c, zc), gqa_softmax(qdc, kdc, vdc)], -1)
    return out, out_c


def kernel(x, c, ctx, c_ctx, ada_w, ada_b, ln1_g, ln1_b, ln2_g, ln2_b, peer_wq, peer_k1, peer_k2, peer_u, peer_v, ev_w_in, ev_w_out, ev_sink, ev_conv_w, ev_conv_b, ev_filt_w1, ev_filt_b1, ev_filt_freq, ev_filt_w2, ev_filt_b2, ev_filt_w3, ev_hy_bias, od_w_in, od_w_out, od_conv_w, od_a_log, od_dt_bias, od_gnorm_w, od_qnorm_w, od_knorm_w):
    cos, sin = axial_rope_tables(x.shape[1])
    bsz = x.shape[0]
    silu_c = jax.nn.silu(c)
    silu_cc = jax.nn.silu(c_ctx)
    for i in range(DEPTH):
        with_ctx = i < DEPTH - 1
        j = i // 2
        mod = (silu_c @ ada_w[i] + ada_b[i])[:, None, :]
        modc = jnp.broadcast_to((silu_cc @ ada_w[i] + ada_b[i])[None, None, :], (bsz, 1, 6 * D_MODEL))
        sh1, sc1, g1, sh2, sc2, g2 = jnp.split(mod, 6, axis=-1)
        sh1c, sc1c, g1c, sh2c, sc2c, g2c = jnp.split(modc, 6, axis=-1)
        if i % 2 == 0:
            p = mod_matmul(x, sh1, sc1, ev_w_in[j])
            pc = mod_matmul(ctx, sh1c, sc1c, ev_w_in[j])
            out, out_c = even_mixer(p, pc, cos, sin, ev_sink[j], ev_conv_w[j], ev_conv_b[j],
                                    ev_filt_w1[j], ev_filt_b1[j], ev_filt_freq[j], ev_filt_w2[j], ev_filt_b2[j],
                                    ev_filt_w3[j], ev_hy_bias[j], with_ctx)
            w_out = ev_w_out[j]
        else:
            p = mod_matmul(x, sh1, sc1, od_w_in[j])
            pc = mod_matmul(ctx, sh1c, sc1c, od_w_in[j])
            out, out_c = odd_mixer(p, pc, cos, sin, od_conv_w[j], od_a_log[j], od_dt_bias[j],
                                   od_gnorm_w[j], od_qnorm_w[j], od_knorm_w[j], with_ctx)
            w_out = od_w_out[j]
        u_bf = peer_u[i].astype(jnp.bfloat16)
        v_bf = peer_v[i].astype(jnp.bfloat16)
        x = proj_residual_ln(out, w_out, x, g1, ln1_g[i], ln1_b[i])
        x = peer_block(x, sh2, sc2, g2, peer_wq[i], peer_k1[i], peer_k2[i], u_bf, v_bf, ln2_g[i], ln2_b[i])
        if with_ctx:
            ctx = proj_residual_ln(out_c, w_out, ctx, g1c, ln1_g[i], ln1_b[i])
            ctx = peer_block(ctx, sh2c, sc2c, g2c, peer_wq[i], peer_k1[i], peer_k2[i], u_bf, v_bf,
                             ln2_g[i], ln2_b[i])
    return x
```

```python
import functools
import math

import jax
import jax.numpy as jnp
from jax import lax
from jax.experimental import pallas as pl
from jax.experimental.pallas import tpu as pltpu

D_MODEL = 1024
DEPTH = 2
GRID_W = 64
HEAD_DIM = 64
BLOCK = 128
ROPE_BASE = 10000.0
EPS = 1e-6

A_HEADS = 8
A_KV_HEADS = 2
WINDOW = 128

HY_CH = 512
HY_ORDER = 2
HY_EMB = 33
HY_BANDS = (HY_EMB - 1) // 2
HY_FAST_DECAY = 0.3
HY_SLOW_DECAY = 1.5
HY_TARGET = 1e-2

C_HEADS = 4
C_DK = 128
C_DV = 128
GDN_CHUNK = 64

D_HEADS = 8
D_KV_HEADS = 2

PEER_HEADS = 8
PEER_NKEYS = 128
PEER_QDIM = 256
PEER_TOPK = 16
PEER_CHUNK = 128

ALPHA = (2 * DEPTH) ** 0.25

A_Q = A_HEADS * HEAD_DIM
A_KV = A_KV_HEADS * HEAD_DIM
C_W = C_HEADS * C_DK
C_GATES = 4 * C_HEADS
D_Q = D_HEADS * HEAD_DIM
D_KV = D_KV_HEADS * HEAD_DIM

VMEM_LIMIT_BYTES = 48 * 1024 * 1024

_NT = (((1,), (1,)), ((), ()))


def _modmm_kernel(x_ref, sh_ref, sc_ref, w_ref, o_ref):
    h = x_ref[0] * (1.0 + sc_ref[0]) + sh_ref[0]
    o_ref[0] = jnp.dot(h.astype(jnp.bfloat16), w_ref[...], preferred_element_type=jnp.float32)


def mod_matmul(x, shift, scale, w, tm=512, tn=None):
    b, s, k = x.shape
    n = w.shape[1]
    tm = min(tm, s)
    tn = n if tn is None else tn
    wb = w.astype(jnp.bfloat16)
    return pl.pallas_call(
        _modmm_kernel,
        grid=(b, n // tn, s // tm),
        in_specs=[
            pl.BlockSpec((1, tm, k), lambda i, j, m: (i, m, 0)),
            pl.BlockSpec((1, 1, k), lambda i, j, m: (i, 0, 0)),
            pl.BlockSpec((1, 1, k), lambda i, j, m: (i, 0, 0)),
            pl.BlockSpec((k, tn), lambda i, j, m: (0, j)),
        ],
        out_specs=pl.BlockSpec((1, tm, tn), lambda i, j, m: (i, m, j)),
        out_shape=jax.ShapeDtypeStruct((b, s, n), jnp.float32),
        compiler_params=pltpu.CompilerParams(
            dimension_semantics=("arbitrary", "arbitrary", "arbitrary"),
            vmem_limit_bytes=VMEM_LIMIT_BYTES),
        name="mod_matmul",
    )(x, shift, scale, wb)


def _qkv_prep_kernel(q_ref, k_ref, v_ref, cs_ref, sn_ref, qw_ref, kw_ref, gm_ref, qo_ref, ko_ref, vo_ref, *,
                     norm, rope, nq, nkv):
    def prep(x, w, nh):
        if norm:
            ms = jnp.dot(x * x, gm_ref[:x.shape[1], :x.shape[1]], precision=lax.Precision.HIGHEST,
                         preferred_element_type=jnp.float32)
            x = x * lax.rsqrt(ms + EPS) * w
        if rope:
            n = x.shape[1]
            reps = n // cs_ref.shape[1]
            cs = jnp.concatenate([cs_ref[...]] * reps, axis=1) if reps > 1 else cs_ref[...]
            sn = jnp.concatenate([sn_ref[...]] * reps, axis=1) if reps > 1 else sn_ref[...]
            lane = lax.broadcasted_iota(jnp.int32, x.shape, 1)
            nf = HEAD_DIM // 4
            partner = jnp.where((lane & nf) == 0, pltpu.roll(x, n - nf, 1), pltpu.roll(x, nf, 1))
            x = x * cs + partner * sn
        return x

    q = prep(q_ref[0], qw_ref[...], nq) * (HEAD_DIM ** -0.5)
    k = prep(k_ref[0], kw_ref[...], nkv)
    v = v_ref[0]
    for h in range(nq):
        qo_ref[0, h] = q[:, h * HEAD_DIM:(h + 1) * HEAD_DIM].astype(jnp.bfloat16)
    for h in range(nkv):
        ko_ref[0, h] = k[:, h * HEAD_DIM:(h + 1) * HEAD_DIM].astype(jnp.bfloat16)
        vo_ref[0, h] = v[:, h * HEAD_DIM:(h + 1) * HEAD_DIM].astype(jnp.bfloat16)


def qkv_prep(p, col0, nq, nkv, rope_tabs, qw=None, kw=None, tq=512):
    b, s, _ = p.shape
    tq = min(tq, s)
    wq_, wk_ = nq * HEAD_DIM, nkv * HEAD_DIM
    norm = qw is not None
    rope = rope_tabs is not None
    if rope:
        cs, sn = rope_tabs
    else:
        cs = sn = jnp.zeros((s, 2 * HEAD_DIM), jnp.float32)
    qw_t = jnp.tile(qw, nq).reshape(1, wq_) if norm else jnp.ones((1, wq_), jnp.float32)
    kw_t = jnp.tile(kw, nkv).reshape(1, wk_) if norm else jnp.ones((1, wk_), jnp.float32)
    grp = jnp.arange(wq_) // HEAD_DIM
    gm = (grp[:, None] == grp[None, :]).astype(jnp.float32) / HEAD_DIM
    kern = functools.partial(_qkv_prep_kernel, norm=norm, rope=rope, nq=nq, nkv=nkv)
    return pl.pallas_call(
        kern,
        grid=(b, s // tq),
        in_specs=[
            pl.BlockSpec((1, tq, wq_), lambda i, m: (i, m, col0 // wq_)),
            pl.BlockSpec((1, tq, wk_), lambda i, m: (i, m, (col0 + wq_) // wk_)),
            pl.BlockSpec((1, tq, wk_), lambda i, m: (i, m, (col0 + wq_) // wk_ + 1)),
            pl.BlockSpec((tq, 2 * HEAD_DIM), lambda i, m: (m, 0)),
            pl.BlockSpec((tq, 2 * HEAD_DIM), lambda i, m: (m, 0)),
            pl.BlockSpec((1, wq_), lambda i, m: (0, 0)),
            pl.BlockSpec((1, wk_), lambda i, m: (0, 0)),
            pl.BlockSpec((wq_, wq_), lambda i, m: (0, 0)),
        ],
        out_specs=[
            pl.BlockSpec((1, nq, tq, HEAD_DIM), lambda i, m: (i, 0, m, 0)),
            pl.BlockSpec((1, nkv, tq, HEAD_DIM), lambda i, m: (i, 0, m, 0)),
            pl.BlockSpec((1, nkv, tq, HEAD_DIM), lambda i, m: (i, 0, m, 0)),
        ],
        out_shape=[
            jax.ShapeDtypeStruct((b, nq, s, HEAD_DIM), jnp.bfloat16),
            jax.ShapeDtypeStruct((b, nkv, s, HEAD_DIM), jnp.bfloat16),
            jax.ShapeDtypeStruct((b, nkv, s, HEAD_DIM), jnp.bfloat16),
        ],
        compiler_params=pltpu.CompilerParams(
            dimension_semantics=("arbitrary", "arbitrary"), vmem_limit_bytes=VMEM_LIMIT_BYTES),
        name="qkv_prep",
    )(p, p, p, cs, sn, qw_t, kw_t, gm)


def rope_tables(n_tok):
    rows = n_tok // GRID_W
    row = jnp.repeat(jnp.arange(rows, dtype=jnp.float32), GRID_W)
    col = jnp.tile(jnp.arange(GRID_W, dtype=jnp.float32), rows)
    nf = HEAD_DIM // 4
    inv = ROPE_BASE ** (-jnp.arange(nf, dtype=jnp.float32) / nf)
    ar, ac = row[:, None] * inv, col[:, None] * inv
    cs = jnp.concatenate([jnp.cos(ar), jnp.cos(ar), jnp.cos(ac), jnp.cos(ac)], -1)
    sn = jnp.concatenate([-jnp.sin(ar), jnp.sin(ar), -jnp.sin(ac), jnp.sin(ac)], -1)
    return jnp.tile(cs, (1, 2)), jnp.tile(sn, (1, 2))


def _flash_kernel(sink_ref, q_ref, k_ref, v_ref, o_ref, m_s, l_s, acc_s, *, use_sink, grp):
    j = pl.program_id(3)
    tq = q_ref.shape[2]

    @pl.when(j == 0)
    def _():
        m_s[...] = jnp.full_like(m_s, -jnp.inf)
        l_s[...] = jnp.zeros_like(l_s)
        acc_s[...] = jnp.zeros_like(acc_s)

    q = q_ref[0].reshape(grp * tq, HEAD_DIM)
    s = lax.dot_general(q, k_ref[0, 0], _NT, preferred_element_type=jnp.float32)
    m_old = m_s[...]
    m_new = jnp.maximum(m_old, jnp.max(s, axis=1, keepdims=True))
    alpha = jnp.exp(m_old - m_new)
    p = jnp.exp(s - m_new)
    l_s[...] = alpha * l_s[...] + jnp.sum(p, axis=1, keepdims=True)
    acc_s[...] = alpha * acc_s[...] + jnp.dot(p.astype(jnp.bfloat16), v_ref[0, 0],
                                              preferred_element_type=jnp.float32)
    m_s[...] = m_new

    @pl.when(j == pl.num_programs(3) - 1)
    def _():
        kvh = pl.program_id(1)
        outs = []
        for g in range(grp):
            rows = slice(g * tq, (g + 1) * tq)
            m = m_s[rows]
            l = l_s[rows]
            acc = acc_s[rows]
            if use_sink:
                sk = sink_ref[kvh * grp + g]
                m2 = jnp.maximum(m, sk)
                a = jnp.exp(m - m2)
                l = a * l + jnp.exp(sk - m2)
                acc = a * acc
            outs.append(acc / l)
        o_ref[0] = jnp.concatenate(outs, axis=1).astype(o_ref.dtype)


def flash_gqa(q, k, v, sink=None, tq=256, tk=768):
    b, h, s, hd = q.shape
    kvh, lk = k.shape[1], k.shape[2]
    grp = h // kvh
    tq = min(tq, s)
    tk = min(tk, lk)
    use_sink = sink is not None
    sink_arr = sink.astype(jnp.float32) if use_sink else jnp.zeros((h,), jnp.float32)
    kern = functools.partial(_flash_kernel, use_sink=use_sink, grp=grp)
    return pl.pallas_call(
        kern,
        grid=(b, kvh, s // tq, lk // tk),
        in_specs=[
            pl.BlockSpec(memory_space=pltpu.SMEM),
            pl.BlockSpec((1, grp, tq, hd), lambda i, c, m, j: (i, c, m, 0)),
            pl.BlockSpec((1, 1, tk, hd), lambda i, c, m, j: (i, c, j, 0)),
            pl.BlockSpec((1, 1, tk, hd), lambda i, c, m, j: (i, c, j, 0)),
        ],
        out_specs=pl.BlockSpec((1, tq, grp * hd), lambda i, c, m, j: (i, m, c)),
        out_shape=jax.ShapeDtypeStruct((b, s, h * hd), jnp.bfloat16),
        scratch_shapes=[pltpu.VMEM((grp * tq, 1), jnp.float32), pltpu.VMEM((grp * tq, 1), jnp.float32),
                        pltpu.VMEM((grp * tq, hd), jnp.float32)],
        compiler_params=pltpu.CompilerParams(
            dimension_semantics=("arbitrary",) * 4, vmem_limit_bytes=VMEM_LIMIT_BYTES),
        name="flash_gqa",
    )(sink_arr, q, k, v)


def _window_kernel(sink_ref, q_ref, kp_ref, kc_ref, kn_ref, vp_ref, vc_ref, vn_ref, kx_ref, vx_ref, o_ref, *, grp):
    kvh = pl.program_id(1)
    i = pl.program_id(2)
    nb = pl.num_programs(2)
    kcat = jnp.concatenate([kp_ref[0, 0], kc_ref[0, 0], kn_ref[0, 0], kx_ref[0, 0]], axis=0)
    vcat = jnp.concatenate([vp_ref[0, 0], vc_ref[0, 0], vn_ref[0, 0], vx_ref[0, 0]], axis=0)
    nk = kcat.shape[0]
    r = lax.broadcasted_iota(jnp.int32, (BLOCK, nk), 0)
    c = lax.broadcasted_iota(jnp.int32, (BLOCK, nk), 1)
    off_prev = jnp.where(i > 0, 0, 2 * nk)
    off_next = jnp.where(i < nb - 1, 0, 2 * nk)
    ok_prev = (c >= BLOCK) | (c >= r + off_prev)
    ok_next = (c < 2 * BLOCK) | (c >= 3 * BLOCK) | (c - 2 * BLOCK <= r - off_next)
    valid = ok_prev & ok_next
    outs = []
    for g in range(grp):
        s = lax.dot_general(q_ref[0, g], kcat, _NT, preferred_element_type=jnp.float32)
        s = jnp.where(valid, s, -jnp.inf)
        sk = sink_ref[kvh * grp + g]
        m = jnp.maximum(jnp.max(s, axis=1, keepdims=True), sk)
        p = jnp.exp(s - m)
        l = jnp.sum(p, axis=1, keepdims=True) + jnp.exp(sk - m)
        o = jnp.dot(p.astype(jnp.bfloat16), vcat, preferred_element_type=jnp.float32)
        outs.append(o / l)
    o_ref[0] = jnp.concatenate(outs, axis=1).astype(o_ref.dtype)


def windowed_sink_gqa(q, k, v, kx, vx, sink):
    b, h, s, hd = q.shape
    kvh = k.shape[1]
    lc = kx.shape[2]
    grp = h // kvh
    nb = s // BLOCK
    kern = functools.partial(_window_kernel, grp=grp)
    blk = lambda f: pl.BlockSpec((1, 1, BLOCK, hd), f)
    prev = lambda i, c, m: (i, c, jnp.maximum(m - 1, 0), 0)
    cur = lambda i, c, m: (i, c, m, 0)
    nxt = lambda i, c, m: (i, c, jnp.minimum(m + 1, nb - 1), 0)
    ctxm = lambda i, c, m: (i, c, 0, 0)
    return pl.pallas_call(
        kern,
        grid=(b, kvh, nb),
        in_specs=[
            pl.BlockSpec(memory_space=pltpu.SMEM),
            pl.BlockSpec((1, grp, BLOCK, hd), cur),
            blk(prev), blk(cur), blk(nxt), blk(prev), blk(cur), blk(nxt),
            pl.BlockSpec((1, 1, lc, hd), ctxm), pl.BlockSpec((1, 1, lc, hd), ctxm),
        ],
        out_specs=pl.BlockSpec((1, BLOCK, grp * hd), lambda i, c, m: (i, m, c)),
        out_shape=jax.ShapeDtypeStruct((b, s, h * hd), jnp.bfloat16),
        compiler_params=pltpu.CompilerParams(
            dimension_semantics=("arbitrary",) * 3, vmem_limit_bytes=VMEM_LIMIT_BYTES),
        name="windowed_sink_gqa",
    )(sink.astype(jnp.float32), q, k, k, k, v, v, v, kx, vx)


def _post_kernel(oa_ref, ob_ref, w_ref, x_ref, g_ref, lg_ref, lb_ref, y_ref):
    ka = oa_ref.shape[2]
    out = jnp.dot(oa_ref[0].astype(jnp.bfloat16), w_ref[:ka], preferred_element_type=jnp.float32)
    out += jnp.dot(ob_ref[0].astype(jnp.bfloat16), w_ref[ka:], preferred_element_type=jnp.float32)
    r = ALPHA * x_ref[0] + g_ref[0] * out
    mu = jnp.mean(r, -1, keepdims=True)
    d = r - mu
    var = jnp.mean(d * d, -1, keepdims=True)
    y_ref[0] = d * lax.rsqrt(var + EPS) * lg_ref[...] + lb_ref[...]


def proj_residual_ln(oa, ob, w, x, gate, ln_g, ln_b, tm=256):
    b, s, ka = oa.shape
    kb = ob.shape[2]
    k = ka + kb
    d = w.shape[1]
    tm = min(tm, s)
    wb = w.astype(jnp.bfloat16)
    return pl.pallas_call(
        _post_kernel,
        grid=(b, s // tm),
        in_specs=[
            pl.BlockSpec((1, tm, ka), lambda i, m: (i, m, 0)),
            pl.BlockSpec((1, tm, kb), lambda i, m: (i, m, 0)),
            pl.BlockSpec((k, d), lambda i, m: (0, 0)),
            pl.BlockSpec((1, tm, d), lambda i, m: (i, m, 0)),
            pl.BlockSpec((1, 1, d), lambda i, m: (i, 0, 0)),
            pl.BlockSpec((1, d), lambda i, m: (0, 0)),
            pl.BlockSpec((1, d), lambda i, m: (0, 0)),
        ],
        out_specs=pl.BlockSpec((1, tm, d), lambda i, m: (i, m, 0)),
        out_shape=jax.ShapeDtypeStruct((b, s, d), jnp.float32),
        compiler_params=pltpu.CompilerParams(
            dimension_semantics=("arbitrary", "arbitrary"),
            vmem_limit_bytes=VMEM_LIMIT_BYTES),
        name="proj_residual_ln",
    )(oa, ob, wb, x, gate, ln_g.reshape(1, d), ln_b.reshape(1, d))


def _top16(s, payload=None):
    n = s.shape[0]
    iota = lax.broadcasted_iota(jnp.int32, s.shape, 0)
    vals, ids = [], []
    for _ in range(PEER_TOPK):
        m = jnp.max(s, axis=0, keepdims=True)
        pos = jnp.min(jnp.where(s == m, iota, n), axis=0, keepdims=True)
        hit = iota == pos
        vals.append(m)
        ids.append(pos if payload is None else jnp.max(jnp.where(hit, payload, -1), axis=0, keepdims=True))
        s = jnp.where(hit, -jnp.inf, s)
    return jnp.concatenate(vals, 0), jnp.concatenate(ids, 0)


def _peer_topk_kernel(q_ref, k1_ref, k2_ref, eid_ref, gate_ref, eid_s, gate_s):
    half = PEER_QDIM // 2

    def head(h, carry):
        off = pl.multiple_of(h * PEER_QDIM, PEER_QDIM)
        q1 = q_ref[:, pl.ds(off, half)]
        q2 = q_ref[:, pl.ds(off + half, half)]
        s1 = lax.dot_general(k1_ref[h], q1, _NT, precision=lax.Precision.HIGHEST,
                             preferred_element_type=jnp.float32)
        s2 = lax.dot_general(k2_ref[h], q2, _NT, precision=lax.Precision.HIGHEST,
                             preferred_element_type=jnp.float32)
        v1, i1 = _top16(s1)
        v2, i2 = _top16(s2)
        cand = jnp.concatenate([v1[i:i + 1] + v2 for i in range(PEER_TOPK)], 0)
        cid = jnp.concatenate([i1[i:i + 1] * PEER_NKEYS + i2 for i in range(PEER_TOPK)], 0)
        best, eid = _top16(cand, cid)
        e = jnp.exp(best - best[0:1])
        gate = e / jnp.sum(e, axis=0, keepdims=True)
        row = pl.multiple_of(h * PEER_TOPK, PEER_TOPK)
        eid_s[pl.ds(row, PEER_TOPK), :] = eid
        gate_s[pl.ds(row, PEER_TOPK), :] = gate
        return carry

    lax.fori_loop(0, PEER_HEADS, head, 0)
    eid_ref[...] = eid_s[...].T
    gate_ref[...] = gate_s[...].T


def peer_topk(q, k1, k2, tt=256):
    t = q.shape[0]
    tt = min(tt, t)
    nsel = PEER_HEADS * PEER_TOPK
    return pl.pallas_call(
        _peer_topk_kernel,
        grid=(t // tt,),
        in_specs=[
            pl.BlockSpec((tt, q.shape[1]), lambda i: (i, 0)),
            pl.BlockSpec(k1.shape, lambda i: (0, 0, 0)),
            pl.BlockSpec(k2.shape, lambda i: (0, 0, 0)),
        ],
        out_specs=[pl.BlockSpec((tt, nsel), lambda i: (i, 0)),
                   pl.BlockSpec((tt, nsel), lambda i: (i, 0))],
        out_shape=[jax.ShapeDtypeStruct((t, nsel), jnp.int32),
                   jax.ShapeDtypeStruct((t, nsel), jnp.float32)],
        scratch_shapes=[pltpu.VMEM((nsel, tt), jnp.int32), pltpu.VMEM((nsel, tt), jnp.float32)],
        compiler_params=pltpu.CompilerParams(
            dimension_semantics=("arbitrary",), vmem_limit_bytes=VMEM_LIMIT_BYTES),
        name="peer_topk",
    )(q, k1, k2)


def _peer_w_kernel(e_ref, g_ref, w_ref):
    nk = PEER_NKEYS
    iota = lax.broadcasted_iota(jnp.int32, (nk, e_ref.shape[1]), 0)

    def tok(t, carry):
        e = e_ref[pl.ds(t, 1), :]
        g = g_ref[pl.ds(t, 1), :]
        a_t = jnp.where(iota == (e >> 7), g, 0.0).astype(jnp.bfloat16)
        b_t = jnp.where(iota == (e & (nk - 1)), 1.0, 0.0).astype(jnp.bfloat16)
        w = lax.dot_general(a_t, b_t, _NT, preferred_element_type=jnp.float32)
        w_ref[t] = w.astype(jnp.bfloat16)
        return carry

    lax.fori_loop(0, e_ref.shape[0], tok, 0, unroll=4)


def peer_dense_gates(eid, gate, tt=128):
    t, nsel = eid.shape
    tt = min(tt, t)
    nk = PEER_NKEYS
    w = pl.pallas_call(
        _peer_w_kernel,
        grid=(t // tt,),
        in_specs=[pl.BlockSpec((tt, nsel), lambda i: (i, 0)),
                  pl.BlockSpec((tt, nsel), lambda i: (i, 0))],
        out_specs=pl.BlockSpec((tt, nk, nk), lambda i: (i, 0, 0)),
        out_shape=jax.ShapeDtypeStruct((t, nk, nk), jnp.bfloat16),
        compiler_params=pltpu.CompilerParams(
            dimension_semantics=("arbitrary",), vmem_limit_bytes=VMEM_LIMIT_BYTES),
        name="peer_dense_gates",
    )(eid, gate)
    return w.reshape(t, nk * nk)


def _peer_expert_kernel(x_ref, sh_ref, sc_ref, w_ref, u_ref, v_ref, g_ref, lg_ref, lb_ref, y_ref, xm_s, acc_s):
    e = pl.program_id(2)

    @pl.when(e == 0)
    def _():
        xm_s[...] = (x_ref[0] * (1.0 + sc_ref[0]) + sh_ref[0]).astype(jnp.bfloat16)
        acc_s[...] = jnp.zeros_like(acc_s)

    h = lax.dot_general(xm_s[...], u_ref[...], _NT, preferred_element_type=jnp.float32)
    gelu = 0.5 * h * (1.0 + lax.erf(h * (2.0 ** -0.5)))
    a = gelu * w_ref[0].astype(jnp.float32)
    acc_s[...] += jnp.dot(a.astype(jnp.bfloat16), v_ref[...], preferred_element_type=jnp.float32)

    @pl.when(e == pl.num_programs(2) - 1)
    def _():
        r = ALPHA * x_ref[0] + g_ref[0] * acc_s[...]
        mu = jnp.mean(r, -1, keepdims=True)
        d = r - mu
        var = jnp.mean(d * d, -1, keepdims=True)
        y_ref[0] = d * lax.rsqrt(var + EPS) * lg_ref[...] + lb_ref[...]


def peer_experts_ln(x, shift, scale, w, u_tab, v_tab, gate, ln_g, ln_b, tt=512, te=512):
    b, s, d = x.shape
    tt = min(tt, s)
    ne = u_tab.shape[0]
    w3 = w.reshape(b, s, ne)
    return pl.pallas_call(
        _peer_expert_kernel,
        grid=(b, s // tt, ne // te),
        in_specs=[
            pl.BlockSpec((1, tt, d), lambda i, m, e: (i, m, 0)),
            pl.BlockSpec((1, 1, d), lambda i, m, e: (i, 0, 0)),
            pl.BlockSpec((1, 1, d), lambda i, m, e: (i, 0, 0)),
            pl.BlockSpec((1, tt, te), lambda i, m, e: (i, m, e)),
            pl.BlockSpec((te, d), lambda i, m, e: (e, 0)),
            pl.BlockSpec((te, d), lambda i, m, e: (e, 0)),
            pl.BlockSpec((1, 1, d), lambda i, m, e: (i, 0, 0)),
            pl.BlockSpec((1, d), lambda i, m, e: (0, 0)),
            pl.BlockSpec((1, d), lambda i, m, e: (0, 0)),
        ],
        out_specs=pl.BlockSpec((1, tt, d), lambda i, m, e: (i, m, 0)),
        out_shape=jax.ShapeDtypeStruct((b, s, d), jnp.float32),
        scratch_shapes=[pltpu.VMEM((tt, d), jnp.bfloat16), pltpu.VMEM((tt, d), jnp.float32)],
        compiler_params=pltpu.CompilerParams(
            dimension_semantics=("arbitrary", "arbitrary", "arbitrary"),
            vmem_limit_bytes=VMEM_LIMIT_BYTES),
        name="peer_experts_ln",
    )(x, shift, scale, w3, u_tab, v_tab, gate, ln_g.reshape(1, d), ln_b.reshape(1, d))


def peer_block(x, shift, scale, gate, wq, k1, k2, u_bf, v_bf, ln_g, ln_b):
    b, s, d = x.shape
    q_all = mod_matmul(x, shift, scale, wq).reshape(b * s, -1)
    eid, gsel = peer_topk(q_all, k1, k2)
    w = peer_dense_gates(eid, gsel)
    return peer_experts_ln(x, shift, scale, w, u_bf, v_bf, gate, ln_g, ln_b)


def rms_norm(x, w):
    xf = x.astype(jnp.float32)
    return (xf * lax.rsqrt(jnp.mean(xf * xf, -1, keepdims=True) + EPS) * w).astype(x.dtype)


def l2_normalize(x):
    xf = x.astype(jnp.float32)
    return (xf * lax.rsqrt(jnp.sum(xf * xf, -1, keepdims=True) + EPS)).astype(x.dtype)


def axial_rope_tables(n_tok):
    rows = n_tok // GRID_W
    row = jnp.repeat(jnp.arange(rows, dtype=jnp.float32), GRID_W)
    col = jnp.tile(jnp.arange(GRID_W, dtype=jnp.float32), rows)
    nf = HEAD_DIM // 4
    inv = ROPE_BASE ** (-jnp.arange(nf, dtype=jnp.float32) / nf)
    ang = jnp.concatenate([row[:, None] * inv, col[:, None] * inv], -1)
    return jnp.cos(ang), jnp.sin(ang)


def apply_axial_rope(x, cos, sin):
    nf = HEAD_DIM // 4
    c = cos[:, None, :]
    s = sin[:, None, :]
    parts = []
    for a in range(2):
        xa = x[..., a * 2 * nf:(a + 1) * 2 * nf]
        x1, x2 = xa[..., :nf], xa[..., nf:]
        ca, sa = c[..., a * nf:(a + 1) * nf], s[..., a * nf:(a + 1) * nf]
        parts += [x1 * ca - x2 * sa, x2 * ca + x1 * sa]
    return jnp.concatenate(parts, -1).astype(x.dtype)


def dwconv(x, w):
    k, ch = w.shape
    return lax.conv_general_dilated(x, w[:, None, :].astype(x.dtype), (1,), [(k // 2, k // 2)],
                                    dimension_numbers=('NWC', 'WIO', 'NWC'), feature_group_count=ch)


def gqa_softmax(q, k, v, sink=None):
    b, lq, h, hd = q.shape
    kvh = k.shape[2]
    g = h // kvh
    lk = k.shape[1]
    s = jnp.einsum('bqkgd,bjkd->bkgqj', q.reshape(b, lq, kvh, g, hd), k).astype(jnp.float32) * hd ** -0.5
    if sink is not None:
        s = jnp.concatenate([s, jnp.broadcast_to(sink.astype(jnp.float32).reshape(kvh, g, 1, 1), s.shape[:-1] + (1,))], -1)
    p = jax.nn.softmax(s, axis=-1)[..., :lk].astype(v.dtype)
    return jnp.einsum('bkgqj,bjkd->bqkgd', p, v).reshape(b, lq, h * hd)


def windowed_sink_attention(q, k, v, kc, vc, sink):
    b, s, h, hd = q.shape
    kvh = k.shape[2]
    g = h // kvh
    nb = s // BLOCK
    lc = kc.shape[1]
    w3 = 3 * BLOCK
    scale = hd ** -0.5
    qb = q.reshape(b, nb, BLOCK, kvh, g, hd).swapaxes(0, 1)

    def band(t):
        tb = jnp.pad(t.reshape(b, nb, BLOCK, kvh, hd), ((0, 0), (1, 1), (0, 0), (0, 0), (0, 0)))
        return jnp.concatenate([tb[:, :-2], tb[:, 1:-1], tb[:, 2:]], axis=2).swapaxes(0, 1)

    kw, vw = band(k), band(v)
    blk = jnp.arange(nb)[:, None, None]
    qpos = blk * BLOCK + jnp.arange(BLOCK)[None, :, None]
    kpos = (blk - 1) * BLOCK + jnp.arange(w3)[None, None, :]
    valid = (jnp.abs(qpos - kpos) <= WINDOW) & (kpos >= 0) & (kpos < s)
    sink_logit = sink.astype(jnp.float32).reshape(kvh, g, 1, 1)

    def one_block(args):
        qk, kk, vk, vm = args
        s_loc = jnp.einsum('bqkgd,bjkd->bkgqj', qk, kk).astype(jnp.float32) * scale
        s_loc = jnp.where(vm, s_loc, -jnp.inf)
        s_ctx = jnp.einsum('bqkgd,bjkd->bkgqj', qk, kc).astype(jnp.float32) * scale
        s_snk = jnp.broadcast_to(sink_logit, s_loc.shape[:-1] + (1,))
        p = jax.nn.softmax(jnp.concatenate([s_loc, s_ctx, s_snk], -1), axis=-1).astype(v.dtype)
        return (jnp.einsum('bkgqj,bjkd->bqkgd', p[..., :w3], vk)
                + jnp.einsum('bkgqj,bjkd->bqkgd', p[..., w3:w3 + lc], vc))

    o = lax.map(one_block, (qb, kw, vw, valid))
    return o.swapaxes(0, 1).reshape(b, s, h * hd)


def global_block_attention(q, k, v, kc, vc):
    b, s, h, hd = q.shape
    nb = s // BLOCK
    k_all = jnp.concatenate([k, kc], 1)
    v_all = jnp.concatenate([v, vc], 1)
    qb = q.reshape(b, nb, BLOCK, h, hd).swapaxes(0, 1)
    o = lax.map(lambda qk: gqa_softmax(qk, k_all, v_all), qb)
    return o.swapaxes(0, 1).reshape(b, s, h * hd)


def hyena_filters(n, w1, b1, freq, w2, b2, w3):
    t = jnp.arange(n, dtype=jnp.float32)
    tn = t / n
    f = jnp.arange(1, HY_BANDS + 1, dtype=jnp.float32)
    ang = 2.0 * math.pi * t[:, None] * f[None, :] / n
    feat = jnp.concatenate([tn[:, None], jnp.sin(ang), jnp.cos(ang)], -1)
    hid = jnp.sin(freq * (feat @ w1 + b1))
    hid = jnp.sin(freq * (hid @ w2 + b2))
    filt = (hid @ w3).astype(jnp.float32).reshape(n, HY_ORDER, 2, HY_CH)
    max_decay = math.log(HY_TARGET) / HY_FAST_DECAY
    min_decay = math.log(HY_TARGET) / HY_SLOW_DECAY
    deltas = jnp.abs(jnp.linspace(min_decay, max_decay, HY_CH, dtype=jnp.float32))
    filt = filt * jnp.exp(-tn[:, None, None, None] * deltas)
    return filt / jnp.sum(jnp.abs(filt), axis=(0, 2), keepdims=True)


def fft_long_conv(x, hf, hb, bias):
    n = x.shape[1]
    kern = jnp.concatenate([hf, jnp.zeros_like(hf[:1]), hb[1:][::-1]], 0)
    kf = jnp.fft.rfft(kern, n=2 * n, axis=0)
    xf = jnp.fft.rfft(x.astype(jnp.float32), n=2 * n, axis=1)
    y = jnp.fft.irfft(xf * kf[None], n=2 * n, axis=1)[:, :n]
    return (y + x.astype(jnp.float32) * bias.astype(jnp.float32)).astype(x.dtype)


def hyena_operator(u, conv_w, conv_b, fw1, fb1, ffreq, fw2, fb2, fw3, hy_bias):
    uc = dwconv(u, conv_w) + conv_b
    x1, x2, v = uc[..., :HY_CH], uc[..., HY_CH:2 * HY_CH], uc[..., 2 * HY_CH:]
    filt = hyena_filters(u.shape[1], fw1, fb1, ffreq, fw2, fb2, fw3)
    v = x1 * fft_long_conv(v, filt[:, 0, 0], filt[:, 0, 1], hy_bias[0])
    v = x2 * fft_long_conv(v, filt[:, 1, 0], filt[:, 1, 1], hy_bias[1])
    return v


def gated_delta_chunked(q, k, v, beta, g, s0):
    b, n_tok, h, dk = q.shape
    dv = v.shape[-1]
    cs = GDN_CHUNK
    n = n_tok // cs

    def blk(t):
        return t.astype(jnp.float32).reshape(b, n, cs, h, -1).transpose(1, 0, 3, 2, 4)

    q = blk(q) * dk ** -0.5
    k = blk(k)
    v = blk(v)
    beta = blk(beta[..., None])[..., 0]
    gcum = jnp.cumsum(blk(g[..., None])[..., 0], -1)
    idx = jnp.arange(cs)
    incl = idx[:, None] >= idx[None, :]
    strict = idx[:, None] > idx[None, :]
    decay = jnp.exp(jnp.where(incl, gcum[..., :, None] - gcum[..., None, :], -jnp.inf))
    kb = k * beta[..., None]
    a = jnp.where(strict, jnp.einsum('nbhid,nbhjd->nbhij', kb, k) * decay, 0.0)
    rhs = jnp.concatenate([v * beta[..., None], kb * jnp.exp(gcum)[..., None]], -1)
    sol = lax.linalg.triangular_solve(jnp.eye(cs, dtype=jnp.float32) + a, rhs, left_side=True, lower=True)
    u, w = sol[..., :dv], sol[..., dv:]
    intra = jnp.einsum('nbhid,nbhjd->nbhij', q, k) * decay
    q_dec = q * jnp.exp(gcum)[..., None]
    k_dec = k * jnp.exp(gcum[..., -1:] - gcum)[..., None]
    g_last = jnp.exp(gcum[..., -1])

    def step(state, xs):
        u_i, w_i, q_i, k_i, intra_i, gl = xs
        v_new = u_i - jnp.einsum('bhcd,bhde->bhce', w_i, state)
        o_i = jnp.einsum('bhcd,bhde->bhce', q_i, state) + jnp.einsum('bhij,bhje->bhie', intra_i, v_new)
        state = state * gl[..., None, None] + jnp.einsum('bhcd,bhce->bhde', k_i, v_new)
        return state, o_i

    s_final, o = lax.scan(step, s0.astype(jnp.float32), (u, w, q_dec, k_dec, intra, g_last))
    return o.transpose(1, 0, 3, 2, 4).reshape(b, n_tok, h, dv), s_final


def gdn_inputs(p, conv_w, a_log, dt_bias):
    b, n, _ = p.shape
    qkv = jax.nn.silu(dwconv(p[..., :3 * C_W], conv_w))
    q = l2_normalize(qkv[..., :C_W].reshape(b, n, C_HEADS, C_DK))
    k = l2_normalize(qkv[..., C_W:2 * C_W].reshape(b, n, C_HEADS, C_DK))
    v = qkv[..., 2 * C_W:].reshape(b, n, C_HEADS, C_DV)
    z = p[..., 3 * C_W:4 * C_W].reshape(b, n, C_HEADS, C_DV)
    gates = p[..., -C_GATES:].astype(jnp.float32).reshape(b, n, 4, C_HEADS)
    beta = jax.nn.sigmoid(gates[:, :, :2])
    g = -jnp.exp(a_log.astype(jnp.float32)) * jax.nn.softplus(gates[:, :, 2:] + dt_bias.astype(jnp.float32))
    return q, k, v, z, beta, g


def rev(t, flip):
    return t[:, ::-1] if flip else t


def bidirectional_gdn(lat, cx, with_ctx):
    ql, kl, vl, bl, gl = lat
    qc, kc, vc, bc, gcx = cx
    s0 = jnp.zeros((ql.shape[0], C_HEADS, C_DK, C_DV), jnp.float32)
    o_lat = 0.0
    o_ctx = 0.0
    for d in range(2):
        f = d == 1
        oc, sc = gated_delta_chunked(rev(qc, f), rev(kc, f), rev(vc, f), rev(bc[:, :, d], f), rev(gcx[:, :, d], f), s0)
        ol, _ = gated_delta_chunked(rev(ql, f), rev(kl, f), rev(vl, f), rev(bl[:, :, d], f), rev(gl[:, :, d], f), sc)
        o_lat = o_lat + rev(ol, f)
        if with_ctx:
            o_ctx = o_ctx + rev(oc, f)
    return o_lat, o_ctx


def even_mixer(p, pc, rope_tabs, sink, conv_w, conv_b, fw1, fb1, ffreq, fw2, fb2, fw3, hy_bias, with_ctx):
    q, k, v = qkv_prep(p, 0, A_HEADS, A_KV_HEADS, rope_tabs)
    qc, kc, vc = qkv_prep(pc, 0, A_HEADS, A_KV_HEADS, None)
    o_a = windowed_sink_gqa(q, k, v, kc, vc, sink)
    hy_args = (conv_w, conv_b, fw1, fb1, ffreq, fw2, fb2, fw3, hy_bias)
    o_b = hyena_operator(p[..., A_Q + 2 * A_KV:], *hy_args)
    out_c = None
    if with_ctx:
        o_ac = flash_gqa(qc, kc, vc, sink)
        o_bc = hyena_operator(pc[..., A_Q + 2 * A_KV:], *hy_args)
        out_c = (o_ac, o_bc)
    return (o_a, o_b), out_c


def odd_mixer(p, pc, rope_tabs, conv_w, a_log, dt_bias, gnorm_w, qnorm_w, knorm_w, with_ctx):
    ql, kl, vl, zl, bl, gl = gdn_inputs(p, conv_w, a_log, dt_bias)
    qc, kc, vc, zc, bc, gcx = gdn_inputs(pc, conv_w, a_log, dt_bias)
    o_l, o_c = bidirectional_gdn((ql, kl, vl, bl, gl), (qc, kc, vc, bc, gcx), with_ctx)

    def gated_out(o, z):
        return (rms_norm(o.astype(z.dtype), gnorm_w) * jax.nn.silu(z)).reshape(z.shape[0], z.shape[1], C_W)

    qd, kd, vd = qkv_prep(p, 4 * C_W, D_HEADS, D_KV_HEADS, rope_tabs, qnorm_w, knorm_w)
    qdc, kdc, vdc = qkv_prep(pc, 4 * C_W, D_HEADS, D_KV_HEADS, None, qnorm_w, knorm_w)
    o_d = flash_gqa(qd, jnp.concatenate([kd, kdc], 2), jnp.concatenate([vd, vdc], 2))
    out_c = None
    if with_ctx:
        out_c = (gated_out(o_c, zc), flash_gqa(qdc, kdc, vdc))
    return (gated_out(o_l, zl), o_d), out_c


def kernel(x, c, ctx, c_ctx, ada_w, ada_b, ln1_g, ln1_b, ln2_g, ln2_b, peer_wq, peer_k1, peer_k2, peer_u, peer_v, ev_w_in, ev_w_out, ev_sink, ev_conv_w, ev_conv_b, ev_filt_w1, ev_filt_b1, ev_filt_freq, ev_filt_w2, ev_filt_b2, ev_filt_w3, ev_hy_bias, od_w_in, od_w_out, od_conv_w, od_a_log, od_dt_bias, od_gnorm_w, od_qnorm_w, od_knorm_w):
    rope_tabs = rope_tables(x.shape[1])
    bsz = x.shape[0]
    silu_c = jax.nn.silu(c)
    silu_cc = jax.nn.silu(c_ctx)
    for i in range(DEPTH):
        with_ctx = i < DEPTH - 1
        j = i // 2
        mod = (silu_c @ ada_w[i] + ada_b[i])[:, None, :]
        modc = jnp.broadcast_to((silu_cc @ ada_w[i] + ada_b[i])[None, None, :], (bsz, 1, 6 * D_MODEL))
        sh1, sc1, g1, sh2, sc2, g2 = jnp.split(mod, 6, axis=-1)
        sh1c, sc1c, g1c, sh2c, sc2c, g2c = jnp.split(modc, 6, axis=-1)
        if i % 2 == 0:
            p = mod_matmul(x, sh1, sc1, ev_w_in[j])
            pc = mod_matmul(ctx, sh1c, sc1c, ev_w_in[j])
            out, out_c = even_mixer(p, pc, rope_tabs, ev_sink[j], ev_conv_w[j], ev_conv_b[j],
                                    ev_filt_w1[j], ev_filt_b1[j], ev_filt_freq[j], ev_filt_w2[j], ev_filt_b2[j],
                                    ev_filt_w3[j], ev_hy_bias[j], with_ctx)
            w_out = ev_w_out[j]
        else:
            w_in = od_w_in[j]
            w_in = jnp.concatenate([w_in[:, :4 * C_W], w_in[:, 4 * C_W + C_GATES:],
                                    w_in[:, 4 * C_W:4 * C_W + C_GATES]], axis=1)
            p = mod_matmul(x, sh1, sc1, w_in)
            pc = mod_matmul(ctx, sh1c, sc1c, w_in)
            out, out_c = odd_mixer(p, pc, rope_tabs, od_conv_w[j], od_a_log[j], od_dt_bias[j],
                                   od_gnorm_w[j], od_qnorm_w[j], od_knorm_w[j], with_ctx)
            w_out = od_w_out[j]
        u_bf = peer_u[i].astype(jnp.bfloat16)
        v_bf = peer_v[i].astype(jnp.bfloat16)
        x = proj_residual_ln(out[0], out[1], w_out, x, g1, ln1_g[i], ln1_b[i])
        x = peer_block(x, sh2, sc2, g2, peer_wq[i], peer_k1[i], peer_k2[i], u_bf, v_bf, ln2_g[i], ln2_b[i])
        if with_ctx:
            ctx = proj_residual_ln(out_c[0], out_c[1], w_out, ctx, g1c, ln1_g[i], ln1_b[i])
            ctx = peer_block(ctx, sh2c, sc2c, g2c, peer_wq[i], peer_k1[i], peer_k2[i], u_bf, v_bf,
                             ln2_g[i], ln2_b[i])
    return x
```

```python
import functools
import math

import jax
import jax.numpy as jnp
from jax import lax
from jax.experimental import pallas as pl
from jax.experimental.pallas import tpu as pltpu

D_MODEL = 1024
DEPTH = 2
GRID_W = 64
HEAD_DIM = 64
BLOCK = 128
ROPE_BASE = 10000.0
EPS = 1e-6

A_HEADS = 8
A_KV_HEADS = 2
WINDOW = 128

HY_CH = 512
HY_ORDER = 2
HY_EMB = 33
HY_BANDS = (HY_EMB - 1) // 2
HY_FAST_DECAY = 0.3
HY_SLOW_DECAY = 1.5
HY_TARGET = 1e-2

C_HEADS = 4
C_DK = 128
C_DV = 128
GDN_CHUNK = 64

D_HEADS = 8
D_KV_HEADS = 2

PEER_HEADS = 8
PEER_NKEYS = 128
PEER_QDIM = 256
PEER_TOPK = 16
PEER_CHUNK = 128

ALPHA = (2 * DEPTH) ** 0.25

A_Q = A_HEADS * HEAD_DIM
A_KV = A_KV_HEADS * HEAD_DIM
C_W = C_HEADS * C_DK
C_GATES = 4 * C_HEADS
D_Q = D_HEADS * HEAD_DIM
D_KV = D_KV_HEADS * HEAD_DIM

VMEM_LIMIT_BYTES = 48 * 1024 * 1024

_NT = (((1,), (1,)), ((), ()))


def _modmm_kernel(x_ref, sh_ref, sc_ref, w_ref, o_ref):
    h = x_ref[0] * (1.0 + sc_ref[0]) + sh_ref[0]
    o_ref[0] = jnp.dot(h.astype(jnp.bfloat16), w_ref[...], preferred_element_type=jnp.float32)


def mod_matmul(x, shift, scale, w, tm=512, tn=None):
    b, s, k = x.shape
    n = w.shape[1]
    tm = min(tm, s)
    tn = n if tn is None else tn
    wb = w.astype(jnp.bfloat16)
    return pl.pallas_call(
        _modmm_kernel,
        grid=(b, n // tn, s // tm),
        in_specs=[
            pl.BlockSpec((1, tm, k), lambda i, j, m: (i, m, 0)),
            pl.BlockSpec((1, 1, k), lambda i, j, m: (i, 0, 0)),
            pl.BlockSpec((1, 1, k), lambda i, j, m: (i, 0, 0)),
            pl.BlockSpec((k, tn), lambda i, j, m: (0, j)),
        ],
        out_specs=pl.BlockSpec((1, tm, tn), lambda i, j, m: (i, m, j)),
        out_shape=jax.ShapeDtypeStruct((b, s, n), jnp.float32),
        compiler_params=pltpu.CompilerParams(
            dimension_semantics=("arbitrary", "arbitrary", "arbitrary"),
            vmem_limit_bytes=VMEM_LIMIT_BYTES),
        name="mod_matmul",
    )(x, shift, scale, wb)


def _qkv_prep_kernel(q_ref, k_ref, v_ref, cs_ref, sn_ref, qw_ref, kw_ref, gm_ref, qo_ref, ko_ref, vo_ref, *,
                     norm, rope, nq, nkv):
    def prep(x, w, nh):
        if norm:
            ms = jnp.dot(x * x, gm_ref[:x.shape[1], :x.shape[1]], precision=lax.Precision.HIGHEST,
                         preferred_element_type=jnp.float32)
            x = x * lax.rsqrt(ms + EPS) * w
        if rope:
            n = x.shape[1]
            reps = n // cs_ref.shape[1]
            cs = jnp.concatenate([cs_ref[...]] * reps, axis=1) if reps > 1 else cs_ref[...]
            sn = jnp.concatenate([sn_ref[...]] * reps, axis=1) if reps > 1 else sn_ref[...]
            lane = lax.broadcasted_iota(jnp.int32, x.shape, 1)
            nf = HEAD_DIM // 4
            partner = jnp.where((lane & nf) == 0, pltpu.roll(x, n - nf, 1), pltpu.roll(x, nf, 1))
            x = x * cs + partner * sn
        return x

    q = prep(q_ref[0], qw_ref[...], nq) * (HEAD_DIM ** -0.5)
    k = prep(k_ref[0], kw_ref[...], nkv)
    v = v_ref[0]
    for h in range(nq):
        qo_ref[0, h] = q[:, h * HEAD_DIM:(h + 1) * HEAD_DIM].astype(jnp.bfloat16)
    for h in range(nkv):
        ko_ref[0, h] = k[:, h * HEAD_DIM:(h + 1) * HEAD_DIM].astype(jnp.bfloat16)
        vo_ref[0, h] = v[:, h * HEAD_DIM:(h + 1) * HEAD_DIM].astype(jnp.bfloat16)


def qkv_prep(p, col0, nq, nkv, rope_tabs, qw=None, kw=None, tq=512):
    b, s, _ = p.shape
    tq = min(tq, s)
    wq_, wk_ = nq * HEAD_DIM, nkv * HEAD_DIM
    norm = qw is not None
    rope = rope_tabs is not None
    if rope:
        cs, sn = rope_tabs
    else:
        cs = sn = jnp.zeros((s, 2 * HEAD_DIM), jnp.float32)
    qw_t = jnp.tile(qw, nq).reshape(1, wq_) if norm else jnp.ones((1, wq_), jnp.float32)
    kw_t = jnp.tile(kw, nkv).reshape(1, wk_) if norm else jnp.ones((1, wk_), jnp.float32)
    grp = jnp.arange(wq_) // HEAD_DIM
    gm = (grp[:, None] == grp[None, :]).astype(jnp.float32) / HEAD_DIM
    kern = functools.partial(_qkv_prep_kernel, norm=norm, rope=rope, nq=nq, nkv=nkv)
    return pl.pallas_call(
        kern,
        grid=(b, s // tq),
        in_specs=[
            pl.BlockSpec((1, tq, wq_), lambda i, m: (i, m, col0 // wq_)),
            pl.BlockSpec((1, tq, wk_), lambda i, m: (i, m, (col0 + wq_) // wk_)),
            pl.BlockSpec((1, tq, wk_), lambda i, m: (i, m, (col0 + wq_) // wk_ + 1)),
            pl.BlockSpec((tq, 2 * HEAD_DIM), lambda i, m: (m, 0)),
            pl.BlockSpec((tq, 2 * HEAD_DIM), lambda i, m: (m, 0)),
            pl.BlockSpec((1, wq_), lambda i, m: (0, 0)),
            pl.BlockSpec((1, wk_), lambda i, m: (0, 0)),
            pl.BlockSpec((wq_, wq_), lambda i, m: (0, 0)),
        ],
        out_specs=[
            pl.BlockSpec((1, nq, tq, HEAD_DIM), lambda i, m: (i, 0, m, 0)),
            pl.BlockSpec((1, nkv, tq, HEAD_DIM), lambda i, m: (i, 0, m, 0)),
            pl.BlockSpec((1, nkv, tq, HEAD_DIM), lambda i, m: (i, 0, m, 0)),
        ],
        out_shape=[
            jax.ShapeDtypeStruct((b, nq, s, HEAD_DIM), jnp.bfloat16),
            jax.ShapeDtypeStruct((b, nkv, s, HEAD_DIM), jnp.bfloat16),
            jax.ShapeDtypeStruct((b, nkv, s, HEAD_DIM), jnp.bfloat16),
        ],
        compiler_params=pltpu.CompilerParams(
            dimension_semantics=("arbitrary", "arbitrary"), vmem_limit_bytes=VMEM_LIMIT_BYTES),
        name="qkv_prep",
    )(p, p, p, cs, sn, qw_t, kw_t, gm)


def rope_tables(n_tok):
    rows = n_tok // GRID_W
    row = jnp.repeat(jnp.arange(rows, dtype=jnp.float32), GRID_W)
    col = jnp.tile(jnp.arange(GRID_W, dtype=jnp.float32), rows)
    nf = HEAD_DIM // 4
    inv = ROPE_BASE ** (-jnp.arange(nf, dtype=jnp.float32) / nf)
    ar, ac = row[:, None] * inv, col[:, None] * inv
    cs = jnp.concatenate([jnp.cos(ar), jnp.cos(ar), jnp.cos(ac), jnp.cos(ac)], -1)
    sn = jnp.concatenate([-jnp.sin(ar), jnp.sin(ar), -jnp.sin(ac), jnp.sin(ac)], -1)
    return jnp.tile(cs, (1, 2)), jnp.tile(sn, (1, 2))


def _flash_kernel(sink_ref, q_ref, k_ref, v_ref, o_ref, m_s, l_s, acc_s, *, use_sink, grp):
    j = pl.program_id(3)
    tq = q_ref.shape[2]

    @pl.when(j == 0)
    def _():
        m_s[...] = jnp.full_like(m_s, -jnp.inf)
        l_s[...] = jnp.zeros_like(l_s)
        acc_s[...] = jnp.zeros_like(acc_s)

    q = q_ref[0].reshape(grp * tq, HEAD_DIM)
    s = lax.dot_general(q, k_ref[0, 0], _NT, preferred_element_type=jnp.float32)
    m_old = m_s[...]
    m_new = jnp.maximum(m_old, jnp.max(s, axis=1, keepdims=True))
    alpha = jnp.exp(m_old - m_new)
    p = jnp.exp(s - m_new)
    l_s[...] = alpha * l_s[...] + jnp.sum(p, axis=1, keepdims=True)
    acc_s[...] = alpha * acc_s[...] + jnp.dot(p.astype(jnp.bfloat16), v_ref[0, 0],
                                              preferred_element_type=jnp.float32)
    m_s[...] = m_new

    @pl.when(j == pl.num_programs(3) - 1)
    def _():
        kvh = pl.program_id(1)
        outs = []
        for g in range(grp):
            rows = slice(g * tq, (g + 1) * tq)
            m = m_s[rows]
            l = l_s[rows]
            acc = acc_s[rows]
            if use_sink:
                sk = sink_ref[kvh * grp + g]
                m2 = jnp.maximum(m, sk)
                a = jnp.exp(m - m2)
                l = a * l + jnp.exp(sk - m2)
                acc = a * acc
            outs.append(acc / l)
        o_ref[0] = jnp.concatenate(outs, axis=1).astype(o_ref.dtype)


def flash_gqa(q, k, v, sink=None, tq=256, tk=768):
    b, h, s, hd = q.shape
    kvh, lk = k.shape[1], k.shape[2]
    grp = h // kvh
    tq = min(tq, s)
    tk = min(tk, lk)
    use_sink = sink is not None
    sink_arr = sink.astype(jnp.float32) if use_sink else jnp.zeros((h,), jnp.float32)
    kern = functools.partial(_flash_kernel, use_sink=use_sink, grp=grp)
    return pl.pallas_call(
        kern,
        grid=(b, kvh, s // tq, lk // tk),
        in_specs=[
            pl.BlockSpec(memory_space=pltpu.SMEM),
            pl.BlockSpec((1, grp, tq, hd), lambda i, c, m, j: (i, c, m, 0)),
            pl.BlockSpec((1, 1, tk, hd), lambda i, c, m, j: (i, c, j, 0)),
            pl.BlockSpec((1, 1, tk, hd), lambda i, c, m, j: (i, c, j, 0)),
        ],
        out_specs=pl.BlockSpec((1, tq, grp * hd), lambda i, c, m, j: (i, m, c)),
        out_shape=jax.ShapeDtypeStruct((b, s, h * hd), jnp.bfloat16),
        scratch_shapes=[pltpu.VMEM((grp * tq, 1), jnp.float32), pltpu.VMEM((grp * tq, 1), jnp.float32),
                        pltpu.VMEM((grp * tq, hd), jnp.float32)],
        compiler_params=pltpu.CompilerParams(
            dimension_semantics=("arbitrary",) * 4, vmem_limit_bytes=VMEM_LIMIT_BYTES),
        name="flash_gqa",
    )(sink_arr, q, k, v)


def _window_kernel(sink_ref, q_ref, kp_ref, kc_ref, kn_ref, vp_ref, vc_ref, vn_ref, kx_ref, vx_ref, o_ref, *, grp):
    kvh = pl.program_id(1)
    i = pl.program_id(2)
    nb = pl.num_programs(2)
    kcat = jnp.concatenate([kp_ref[0, 0], kc_ref[0, 0], kn_ref[0, 0], kx_ref[0, 0]], axis=0)
    vcat = jnp.concatenate([vp_ref[0, 0], vc_ref[0, 0], vn_ref[0, 0], vx_ref[0, 0]], axis=0)
    nk = kcat.shape[0]
    r = lax.broadcasted_iota(jnp.int32, (BLOCK, nk), 0)
    c = lax.broadcasted_iota(jnp.int32, (BLOCK, nk), 1)
    off_prev = jnp.where(i > 0, 0, 2 * nk)
    off_next = jnp.where(i < nb - 1, 0, 2 * nk)
    ok_prev = (c >= BLOCK) | (c >= r + off_prev)
    ok_next = (c < 2 * BLOCK) | (c >= 3 * BLOCK) | (c - 2 * BLOCK <= r - off_next)
    valid = ok_prev & ok_next
    outs = []
    for g in range(grp):
        s = lax.dot_general(q_ref[0, g], kcat, _NT, preferred_element_type=jnp.float32)
        s = jnp.where(valid, s, -jnp.inf)
        sk = sink_ref[kvh * grp + g]
        m = jnp.maximum(jnp.max(s, axis=1, keepdims=True), sk)
        p = jnp.exp(s - m)
        l = jnp.sum(p, axis=1, keepdims=True) + jnp.exp(sk - m)
        o = jnp.dot(p.astype(jnp.bfloat16), vcat, preferred_element_type=jnp.float32)
        outs.append(o / l)
    o_ref[0] = jnp.concatenate(outs, axis=1).astype(o_ref.dtype)


def windowed_sink_gqa(q, k, v, kx, vx, sink):
    b, h, s, hd = q.shape
    kvh = k.shape[1]
    lc = kx.shape[2]
    grp = h // kvh
    nb = s // BLOCK
    kern = functools.partial(_window_kernel, grp=grp)
    blk = lambda f: pl.BlockSpec((1, 1, BLOCK, hd), f)
    prev = lambda i, c, m: (i, c, jnp.maximum(m - 1, 0), 0)
    cur = lambda i, c, m: (i, c, m, 0)
    nxt = lambda i, c, m: (i, c, jnp.minimum(m + 1, nb - 1), 0)
    ctxm = lambda i, c, m: (i, c, 0, 0)
    return pl.pallas_call(
        kern,
        grid=(b, kvh, nb),
        in_specs=[
            pl.BlockSpec(memory_space=pltpu.SMEM),
            pl.BlockSpec((1, grp, BLOCK, hd), cur),
            blk(prev), blk(cur), blk(nxt), blk(prev), blk(cur), blk(nxt),
            pl.BlockSpec((1, 1, lc, hd), ctxm), pl.BlockSpec((1, 1, lc, hd), ctxm),
        ],
        out_specs=pl.BlockSpec((1, BLOCK, grp * hd), lambda i, c, m: (i, m, c)),
        out_shape=jax.ShapeDtypeStruct((b, s, h * hd), jnp.bfloat16),
        compiler_params=pltpu.CompilerParams(
            dimension_semantics=("arbitrary",) * 3, vmem_limit_bytes=VMEM_LIMIT_BYTES),
        name="windowed_sink_gqa",
    )(sink.astype(jnp.float32), q, k, k, k, v, v, v, kx, vx)


def _post_kernel(oa_ref, ob_ref, w_ref, x_ref, g_ref, lg_ref, lb_ref, y_ref):
    ka = oa_ref.shape[2]
    out = jnp.dot(oa_ref[0].astype(jnp.bfloat16), w_ref[:ka], preferred_element_type=jnp.float32)
    out += jnp.dot(ob_ref[0].astype(jnp.bfloat16), w_ref[ka:], preferred_element_type=jnp.float32)
    r = ALPHA * x_ref[0] + g_ref[0] * out
    mu = jnp.mean(r, -1, keepdims=True)
    d = r - mu
    var = jnp.mean(d * d, -1, keepdims=True)
    y_ref[0] = d * lax.rsqrt(var + EPS) * lg_ref[...] + lb_ref[...]


def proj_residual_ln(oa, ob, w, x, gate, ln_g, ln_b, tm=256):
    b, s, ka = oa.shape
    kb = ob.shape[2]
    k = ka + kb
    d = w.shape[1]
    tm = min(tm, s)
    wb = w.astype(jnp.bfloat16)
    return pl.pallas_call(
        _post_kernel,
        grid=(b, s // tm),
        in_specs=[
            pl.BlockSpec((1, tm, ka), lambda i, m: (i, m, 0)),
            pl.BlockSpec((1, tm, kb), lambda i, m: (i, m, 0)),
            pl.BlockSpec((k, d), lambda i, m: (0, 0)),
            pl.BlockSpec((1, tm, d), lambda i, m: (i, m, 0)),
            pl.BlockSpec((1, 1, d), lambda i, m: (i, 0, 0)),
            pl.BlockSpec((1, d), lambda i, m: (0, 0)),
            pl.BlockSpec((1, d), lambda i, m: (0, 0)),
        ],
        out_specs=pl.BlockSpec((1, tm, d), lambda i, m: (i, m, 0)),
        out_shape=jax.ShapeDtypeStruct((b, s, d), jnp.float32),
        compiler_params=pltpu.CompilerParams(
            dimension_semantics=("arbitrary", "arbitrary"),
            vmem_limit_bytes=VMEM_LIMIT_BYTES),
        name="proj_residual_ln",
    )(oa, ob, wb, x, gate, ln_g.reshape(1, d), ln_b.reshape(1, d))


def _top16(s, payload=None):
    n = s.shape[0]
    iota = lax.broadcasted_iota(jnp.int32, s.shape, 0)
    vals, ids = [], []
    for _ in range(PEER_TOPK):
        m = jnp.max(s, axis=0, keepdims=True)
        pos = jnp.min(jnp.where(s == m, iota, n), axis=0, keepdims=True)
        hit = iota == pos
        vals.append(m)
        ids.append(pos if payload is None else jnp.max(jnp.where(hit, payload, -1), axis=0, keepdims=True))
        s = jnp.where(hit, -jnp.inf, s)
    return jnp.concatenate(vals, 0), jnp.concatenate(ids, 0)


def _peer_topk_kernel(q_ref, k1_ref, k2_ref, eid_ref, gate_ref, eid_s, gate_s):
    half = PEER_QDIM // 2

    def head(h, carry):
        off = pl.multiple_of(h * PEER_QDIM, PEER_QDIM)
        q1 = q_ref[:, pl.ds(off, half)]
        q2 = q_ref[:, pl.ds(off + half, half)]
        s1 = lax.dot_general(k1_ref[h], q1, _NT, precision=lax.Precision.HIGHEST,
                             preferred_element_type=jnp.float32)
        s2 = lax.dot_general(k2_ref[h], q2, _NT, precision=lax.Precision.HIGHEST,
                             preferred_element_type=jnp.float32)
        v1, i1 = _top16(s1)
        v2, i2 = _top16(s2)
        k8 = PEER_TOPK // 2
        cand = jnp.concatenate([v1[0:1] + v2] + [v1[i:i + 1] + v2[:k8] for i in range(1, k8)]
                               + [v1[k8:] + v2[0:1]], 0)
        cid = jnp.concatenate([i1[0:1] * PEER_NKEYS + i2]
                              + [i1[i:i + 1] * PEER_NKEYS + i2[:k8] for i in range(1, k8)]
                              + [i1[k8:] * PEER_NKEYS + i2[0:1]], 0)
        best, eid = _top16(cand, cid)
        e = jnp.exp(best - best[0:1])
        gate = e / jnp.sum(e, axis=0, keepdims=True)
        row = pl.multiple_of(h * PEER_TOPK, PEER_TOPK)
        eid_s[pl.ds(row, PEER_TOPK), :] = eid
        gate_s[pl.ds(row, PEER_TOPK), :] = gate
        return carry

    lax.fori_loop(0, PEER_HEADS, head, 0)
    eid_ref[...] = eid_s[...].T
    gate_ref[...] = gate_s[...].T


def peer_topk(q, k1, k2, tt=512):
    t = q.shape[0]
    tt = min(tt, t)
    nsel = PEER_HEADS * PEER_TOPK
    return pl.pallas_call(
        _peer_topk_kernel,
        grid=(t // tt,),
        in_specs=[
            pl.BlockSpec((tt, q.shape[1]), lambda i: (i, 0)),
            pl.BlockSpec(k1.shape, lambda i: (0, 0, 0)),
            pl.BlockSpec(k2.shape, lambda i: (0, 0, 0)),
        ],
        out_specs=[pl.BlockSpec((tt, nsel), lambda i: (i, 0)),
                   pl.BlockSpec((tt, nsel), lambda i: (i, 0))],
        out_shape=[jax.ShapeDtypeStruct((t, nsel), jnp.int32),
                   jax.ShapeDtypeStruct((t, nsel), jnp.float32)],
        scratch_shapes=[pltpu.VMEM((nsel, tt), jnp.int32), pltpu.VMEM((nsel, tt), jnp.float32)],
        compiler_params=pltpu.CompilerParams(
            dimension_semantics=("arbitrary",), vmem_limit_bytes=VMEM_LIMIT_BYTES),
        name="peer_topk",
    )(q, k1, k2)


def _peer_w_kernel(e_ref, g_ref, w_ref):
    nk = PEER_NKEYS
    iota = lax.broadcasted_iota(jnp.int32, (nk, e_ref.shape[1]), 0)

    def tok(t, carry):
        e = e_ref[pl.ds(t, 1), :]
        g = g_ref[pl.ds(t, 1), :]
        a_t = jnp.where(iota == (e >> 7), g, 0.0).astype(jnp.bfloat16)
        b_t = jnp.where(iota == (e & (nk - 1)), 1.0, 0.0).astype(jnp.bfloat16)
        w = lax.dot_general(a_t, b_t, _NT, preferred_element_type=jnp.float32)
        w_ref[t] = w.astype(jnp.bfloat16)
        return carry

    lax.fori_loop(0, e_ref.shape[0], tok, 0, unroll=4)


def peer_dense_gates(eid, gate, tt=128):
    t, nsel = eid.shape
    tt = min(tt, t)
    nk = PEER_NKEYS
    w = pl.pallas_call(
        _peer_w_kernel,
        grid=(t // tt,),
        in_specs=[pl.BlockSpec((tt, nsel), lambda i: (i, 0)),
                  pl.BlockSpec((tt, nsel), lambda i: (i, 0))],
        out_specs=pl.BlockSpec((tt, nk, nk), lambda i: (i, 0, 0)),
        out_shape=jax.ShapeDtypeStruct((t, nk, nk), jnp.bfloat16),
        compiler_params=pltpu.CompilerParams(
            dimension_semantics=("arbitrary",), vmem_limit_bytes=VMEM_LIMIT_BYTES),
        name="peer_dense_gates",
    )(eid, gate)
    return w


def _peer_expert_kernel(x_ref, sh_ref, sc_ref, w_ref, u_ref, v_ref, g_ref, lg_ref, lb_ref, y_ref, xm_s, acc_s):
    e = pl.program_id(2)

    @pl.when(e == 0)
    def _():
        xm_s[...] = (x_ref[0] * (1.0 + sc_ref[0]) + sh_ref[0]).astype(jnp.bfloat16)
        acc_s[...] = jnp.zeros_like(acc_s)

    h = lax.dot_general(xm_s[...], u_ref[...], _NT, preferred_element_type=jnp.float32)
    gelu = 0.5 * h * (1.0 + lax.erf(h * (2.0 ** -0.5)))
    w = w_ref[0].reshape(h.shape)
    a = gelu * w.astype(jnp.float32)
    acc_s[...] += jnp.dot(a.astype(jnp.bfloat16), v_ref[...], preferred_element_type=jnp.float32)

    @pl.when(e == pl.num_programs(2) - 1)
    def _():
        r = ALPHA * x_ref[0] + g_ref[0] * acc_s[...]
        mu = jnp.mean(r, -1, keepdims=True)
        d = r - mu
        var = jnp.mean(d * d, -1, keepdims=True)
        y_ref[0] = d * lax.rsqrt(var + EPS) * lg_ref[...] + lb_ref[...]


def peer_experts_ln(x, shift, scale, w, u_tab, v_tab, gate, ln_g, ln_b, tt=512, te=2048):
    b, s, d = x.shape
    tt = min(tt, s)
    ne = u_tab.shape[0]
    nk = PEER_NKEYS
    w3 = w.reshape(b, s, nk, nk)
    return pl.pallas_call(
        _peer_expert_kernel,
        grid=(b, s // tt, ne // te),
        in_specs=[
            pl.BlockSpec((1, tt, d), lambda i, m, e: (i, m, 0)),
            pl.BlockSpec((1, 1, d), lambda i, m, e: (i, 0, 0)),
            pl.BlockSpec((1, 1, d), lambda i, m, e: (i, 0, 0)),
            pl.BlockSpec((1, tt, te // nk, nk), lambda i, m, e: (i, m, e, 0)),
            pl.BlockSpec((te, d), lambda i, m, e: (e, 0)),
            pl.BlockSpec((te, d), lambda i, m, e: (e, 0)),
            pl.BlockSpec((1, 1, d), lambda i, m, e: (i, 0, 0)),
            pl.BlockSpec((1, d), lambda i, m, e: (0, 0)),
            pl.BlockSpec((1, d), lambda i, m, e: (0, 0)),
        ],
        out_specs=pl.BlockSpec((1, tt, d), lambda i, m, e: (i, m, 0)),
        out_shape=jax.ShapeDtypeStruct((b, s, d), jnp.float32),
        scratch_shapes=[pltpu.VMEM((tt, d), jnp.bfloat16), pltpu.VMEM((tt, d), jnp.float32)],
        compiler_params=pltpu.CompilerParams(
            dimension_semantics=("arbitrary", "arbitrary", "arbitrary"),
            vmem_limit_bytes=VMEM_LIMIT_BYTES),
        name="peer_experts_ln",
    )(x, shift, scale, w3, u_tab, v_tab, gate, ln_g.reshape(1, d), ln_b.reshape(1, d))


def peer_block(x, shift, scale, gate, wq, k1, k2, u_bf, v_bf, ln_g, ln_b):
    b, s, d = x.shape
    q_all = mod_matmul(x, shift, scale, wq).reshape(b * s, -1)
    eid, gsel = peer_topk(q_all, k1, k2)
    w = peer_dense_gates(eid, gsel)
    return peer_experts_ln(x, shift, scale, w, u_bf, v_bf, gate, ln_g, ln_b)


HEAD_LANES = 128


def _short_conv_kernel(x_ref, xp_ref, xn_ref, w_ref, b_ref, o_ref, *, silu, n_l2, n_scaled):
    cb = pl.program_id(1)
    m = pl.program_id(2)
    x = x_ref[0]
    tq, wb = x.shape
    prev_row = jnp.where(m > 0, xp_ref[0][7:8], 0.0)
    next_row = jnp.where(m < pl.num_programs(2) - 1, xn_ref[0][0:1], 0.0)
    row = lax.broadcasted_iota(jnp.int32, x.shape, 0)
    x_m1 = jnp.where(row == 0, prev_row, pltpu.roll(x, 1, 0))
    x_p1 = jnp.where(row == tq - 1, next_row, pltpu.roll(x, tq - 1, 0))
    y = w_ref[0:1] * x_m1 + w_ref[1:2] * x + w_ref[2:3] * x_p1 + b_ref[...]
    if silu:
        y = y * jax.nn.sigmoid(y)
    if n_l2 == 0:
        o_ref[0] = y
        return
    hpb = wb // HEAD_LANES
    for hh in range(hpb):
        gh = cb * hpb + hh
        seg = y[:, hh * HEAD_LANES:(hh + 1) * HEAD_LANES]
        inv = lax.rsqrt(jnp.sum(seg * seg, axis=-1, keepdims=True) + EPS)
        f = jnp.where(gh < n_l2, inv, 1.0) * jnp.where(gh < n_scaled, C_DK ** -0.5, 1.0)
        o_ref[0, :, hh * HEAD_LANES:(hh + 1) * HEAD_LANES] = seg * f


def short_conv(p, col0, width, w, bias=None, silu=False, n_l2=0, n_scaled=0, wb=768, tq=512):
    b, l, _ = p.shape
    tq = min(tq, l)
    bias2 = (jnp.zeros((width,), jnp.float32) if bias is None else bias).reshape(1, width)
    c0 = col0 // wb
    kern = functools.partial(_short_conv_kernel, silu=silu, n_l2=n_l2, n_scaled=n_scaled)
    r8 = tq // 8
    return pl.pallas_call(
        kern,
        grid=(b, width // wb, l // tq),
        in_specs=[
            pl.BlockSpec((1, tq, wb), lambda i, c, m: (i, m, c0 + c)),
            pl.BlockSpec((1, 8, wb), lambda i, c, m: (i, jnp.maximum(m * r8 - 1, 0), c0 + c)),
            pl.BlockSpec((1, 8, wb), lambda i, c, m: (i, jnp.minimum((m + 1) * r8, l // 8 - 1), c0 + c)),
            pl.BlockSpec((3, wb), lambda i, c, m: (0, c)),
            pl.BlockSpec((1, wb), lambda i, c, m: (0, c)),
        ],
        out_specs=pl.BlockSpec((1, tq, wb), lambda i, c, m: (i, m, c)),
        out_shape=jax.ShapeDtypeStruct((b, l, width), jnp.float32),
        compiler_params=pltpu.CompilerParams(
            dimension_semantics=("arbitrary",) * 3, vmem_limit_bytes=VMEM_LIMIT_BYTES),
        name="short_conv",
    )(p, p, p, w, bias2)


def _dot3(a, b):
    ah = a.astype(jnp.bfloat16)
    bh = b.astype(jnp.bfloat16)
    al = (a - ah.astype(jnp.float32)).astype(jnp.bfloat16)
    bl = (b - bh.astype(jnp.float32)).astype(jnp.bfloat16)
    d = functools.partial(jnp.dot, preferred_element_type=jnp.float32)
    return d(ah, bh) + (d(ah, bl) + d(al, bh))


def _gdn_chunk_kernel(qkv_ref, beta_ref, g_ref, u_ref, w_ref, qd_ref, kd_ref, in_ref, gl_ref, *, nc):
    d = pl.program_id(0)
    cs = GDN_CHUNK
    ii = lax.broadcasted_iota(jnp.int32, (cs, cs), 0)
    jj = lax.broadcasted_iota(jnp.int32, (cs, cs), 1)
    lo = (ii - jj) * (1 - 2 * d)
    incl = lo >= 0
    strict = lo > 0
    tri = jnp.where(incl, 1.0, 0.0).astype(jnp.bfloat16)
    tri3 = jnp.concatenate([tri, tri, tri], axis=1)
    eye = jnp.where(ii == jj, 1.0, 0.0)

    def chunk_pair(cp, carry):
        probs = []
        for c in (2 * cp, 2 * cp + 1):
            rows = pl.ds(pl.multiple_of(c * cs, cs), cs)
            g_c = g_ref[0, 0, rows, :]
            b_c = beta_ref[0, 0, rows, :]
            g_hi = g_c.astype(jnp.bfloat16)
            r1 = g_c - g_hi.astype(jnp.float32)
            g_mid = r1.astype(jnp.bfloat16)
            g_lo = (r1 - g_mid.astype(jnp.float32)).astype(jnp.bfloat16)
            gc = jnp.dot(tri3, jnp.concatenate([g_hi, g_mid, g_lo], axis=0),
                         preferred_element_type=jnp.float32)
            tot = jnp.sum(g_c, axis=0, keepdims=True)
            for h in range(C_HEADS):
                probs.append(dict(c=c, h=h, rows=rows, gc=gc[:, h:h + 1], bt=b_c[:, h:h + 1], tot=tot[:, h:h + 1]))
        for pr in probs:
            h, rows = pr["h"], pr["rows"]
            q = qkv_ref[0, rows, h * HEAD_LANES:(h + 1) * HEAD_LANES]
            k = qkv_ref[0, rows, C_W + h * HEAD_LANES:C_W + (h + 1) * HEAD_LANES]
            kb = k * pr["bt"]
            kq = lax.dot_general(jnp.concatenate([kb, q], axis=0).astype(jnp.bfloat16), k.astype(jnp.bfloat16),
                                 _NT, preferred_element_type=jnp.float32)
            gc_row = jnp.broadcast_to(pr["gc"], (cs, HEAD_LANES)).T[:cs, :]
            dm = jnp.where(incl, jnp.exp(pr["gc"] - gc_row), 0.0)
            x = jnp.where(strict, -(kq[:cs] * dm), 0.0)
            in_ref[0, 0, pr["c"], h] = (kq[cs:] * dm).astype(in_ref.dtype)
            pr.update(t=eye + x, pw=x)
        for _ in range(5):
            for pr in probs:
                pr["pw"] = _dot3(pr["pw"], pr["pw"])
            for pr in probs:
                pr["t"] = pr["t"] + _dot3(pr["t"], pr["pw"])
        for pr in probs:
            h, rows = pr["h"], pr["rows"]
            lanes = slice(h * HEAD_LANES, (h + 1) * HEAD_LANES)
            q = qkv_ref[0, rows, h * HEAD_LANES:(h + 1) * HEAD_LANES]
            k = qkv_ref[0, rows, C_W + h * HEAD_LANES:C_W + (h + 1) * HEAD_LANES]
            v = qkv_ref[0, rows, 2 * C_W + h * HEAD_LANES:2 * C_W + (h + 1) * HEAD_LANES]
            eg = jnp.exp(pr["gc"])
            uw = _dot3(pr["t"], jnp.concatenate([v * pr["bt"], k * (pr["bt"] * eg)], axis=1))
            u_ref[0, 0, rows, lanes] = uw[:, :HEAD_LANES]
            w_ref[0, 0, rows, lanes] = uw[:, HEAD_LANES:].astype(w_ref.dtype)
            qd_ref[0, 0, rows, lanes] = (q * eg).astype(qd_ref.dtype)
            kd_ref[0, 0, rows, lanes] = (k * jnp.exp(pr["tot"] - pr["gc"])).astype(kd_ref.dtype)
            gl_ref[0, 0, pr["c"], h:h + 1, :] = jnp.broadcast_to(jnp.exp(pr["tot"]), (1, HEAD_LANES))
        return carry

    lax.fori_loop(0, nc // 2, chunk_pair, 0)


def gdn_chunk_prep(qkv, beta, g, nc=4):
    b, l, _ = qkv.shape
    cs = GDN_CHUNK
    tq = nc * cs
    nchunks = l // cs
    bf = jnp.bfloat16
    big = lambda dt: jax.ShapeDtypeStruct((2, b, l, C_W), dt)
    bspec = pl.BlockSpec((1, 1, tq, C_W), lambda d, i, m: (d, i, m, 0))
    gspec = pl.BlockSpec((1, 1, tq, C_HEADS), lambda d, i, m: (d, i, m, 0))
    return pl.pallas_call(
        functools.partial(_gdn_chunk_kernel, nc=nc),
        grid=(2, b, l // tq),
        in_specs=[pl.BlockSpec((1, tq, 3 * C_W), lambda d, i, m: (i, m, 0)), gspec, gspec],
        out_specs=[bspec, bspec, bspec, bspec,
                   pl.BlockSpec((1, 1, nc, C_HEADS, cs, cs), lambda d, i, m: (d, i, m, 0, 0, 0)),
                   pl.BlockSpec((1, 1, nc, C_HEADS, HEAD_LANES), lambda d, i, m: (d, i, m, 0, 0))],
        out_shape=[big(jnp.float32), big(bf), big(bf), big(bf),
                   jax.ShapeDtypeStruct((2, b, nchunks, C_HEADS, cs, cs), bf),
                   jax.ShapeDtypeStruct((2, b, nchunks, C_HEADS, HEAD_LANES), jnp.float32)],
        compiler_params=pltpu.CompilerParams(
            dimension_semantics=("arbitrary",) * 3, vmem_limit_bytes=VMEM_LIMIT_BYTES),
        name="gdn_chunk_prep",
    )(qkv, beta, g)


def _gdn_scan_kernel(*refs):
    ins, (of_ref, ob_ref, s_ref) = refs[:12], refs[12:]
    step = pl.program_id(1)

    @pl.when(step == 0)
    def _():
        s_ref[...] = jnp.zeros_like(s_ref)

    for d, o_ref in enumerate((of_ref, ob_ref)):
        u_ref, w_ref, qd_ref, kd_ref, in_ref, gl_ref = ins[6 * d:6 * d + 6]
        for h in range(C_HEADS):
            lanes = slice(h * HEAD_LANES, (h + 1) * HEAD_LANES)
            s = s_ref[d * C_HEADS + h]
            sb = s.astype(jnp.bfloat16)
            v_new = u_ref[0, 0, :, lanes] - jnp.dot(w_ref[0, 0, :, lanes], sb, preferred_element_type=jnp.float32)
            vb = v_new.astype(jnp.bfloat16)
            o_ref[0, :, lanes] = (jnp.dot(qd_ref[0, 0, :, lanes], sb, preferred_element_type=jnp.float32)
                                  + jnp.dot(in_ref[0, 0, 0, h], vb, preferred_element_type=jnp.float32))
            s_ref[d * C_HEADS + h] = s * gl_ref[0, 0, 0, h:h + 1, :] + lax.dot_general(
                kd_ref[0, 0, :, lanes], vb, (((0,), (0,)), ((), ())), preferred_element_type=jnp.float32)


def gdn_scan(u, w, qd, kd, intra, gl, n_ctx_chunks):
    _, b, l, _ = u.shape
    cs = GDN_CHUNK
    nchunks = l // cs

    def chunk_of(d, s):
        if d == 0:
            return s
        return jnp.where(s < n_ctx_chunks, n_ctx_chunks - 1 - s, nchunks - 1 + n_ctx_chunks - s)

    in_specs, args = [], []
    for d in range(2):
        big = pl.BlockSpec((1, 1, cs, C_W), lambda i, s, d=d: (d, i, chunk_of(d, s), 0))
        in_specs += [big, big, big, big,
                     pl.BlockSpec((1, 1, 1, C_HEADS, cs, cs), lambda i, s, d=d: (d, i, chunk_of(d, s), 0, 0, 0)),
                     pl.BlockSpec((1, 1, 1, C_HEADS, HEAD_LANES), lambda i, s, d=d: (d, i, chunk_of(d, s), 0, 0))]
        args += [u, w, qd, kd, intra, gl]
    out_specs = [pl.BlockSpec((1, cs, C_W), lambda i, s, d=d: (i, chunk_of(d, s), 0)) for d in range(2)]
    return pl.pallas_call(
        _gdn_scan_kernel,
        grid=(b, nchunks),
        in_specs=in_specs,
        out_specs=out_specs,
        out_shape=[jax.ShapeDtypeStruct((b, l, C_W), jnp.float32)] * 2,
        scratch_shapes=[pltpu.VMEM((2 * C_HEADS, C_DK, C_DV), jnp.float32)],
        compiler_params=pltpu.CompilerParams(
            dimension_semantics=("arbitrary",) * 2, vmem_limit_bytes=VMEM_LIMIT_BYTES),
        name="gdn_scan",
    )(*args)


def _gdn_gate_kernel(of_ref, ob_ref, z_ref, gw_ref, y_ref):
    o = of_ref[0] + ob_ref[0]
    z = z_ref[0]
    for h in range(C_HEADS):
        lanes = slice(h * HEAD_LANES, (h + 1) * HEAD_LANES)
        oh = o[:, lanes]
        zh = z[:, lanes]
        n = oh * lax.rsqrt(jnp.mean(oh * oh, axis=-1, keepdims=True) + EPS) * gw_ref[...]
        y_ref[0, :, lanes] = (n * (zh * jax.nn.sigmoid(zh))).astype(y_ref.dtype)


def gdn_gate(o_f, o_b, row0, p, gnorm_w, tq=256):
    b, l, _ = p.shape
    tq = min(tq, l)
    r0 = row0 // tq
    ospec = pl.BlockSpec((1, tq, C_W), lambda i, m: (i, r0 + m, 0))
    return pl.pallas_call(
        _gdn_gate_kernel,
        grid=(b, l // tq),
        in_specs=[ospec, ospec,
                  pl.BlockSpec((1, tq, C_W), lambda i, m: (i, m, 3)),
                  pl.BlockSpec((1, HEAD_LANES), lambda i, m: (0, 0))],
        out_specs=pl.BlockSpec((1, tq, C_W), lambda i, m: (i, m, 0)),
        out_shape=jax.ShapeDtypeStruct((b, l, C_W), jnp.bfloat16),
        compiler_params=pltpu.CompilerParams(
            dimension_semantics=("arbitrary",) * 2, vmem_limit_bytes=VMEM_LIMIT_BYTES),
        name="gdn_gate",
    )(o_f, o_b, p, gnorm_w.reshape(1, HEAD_LANES))


def gdn_mixer(p, pc, conv_w, a_log, dt_bias, gnorm_w, with_ctx):
    lc = pc.shape[1]
    conv = functools.partial(short_conv, col0=0, width=3 * C_W, w=conv_w, silu=True,
                             n_l2=2 * C_HEADS, n_scaled=C_HEADS)
    qkv = jnp.concatenate([conv(pc), conv(p)], axis=1)
    gates = jnp.concatenate([pc[..., -C_GATES:], p[..., -C_GATES:]], axis=1)
    gates = gates.reshape(gates.shape[0], gates.shape[1], 4, C_HEADS)
    beta = jax.nn.sigmoid(gates[:, :, :2])
    g = -jnp.exp(a_log) * jax.nn.softplus(gates[:, :, 2:] + dt_bias)
    beta = jnp.moveaxis(beta, 2, 0)
    g = jnp.moveaxis(g, 2, 0)
    u, w, qd, kd, intra, gl = gdn_chunk_prep(qkv, beta, g)
    o_f, o_b = gdn_scan(u, w, qd, kd, intra, gl, lc // GDN_CHUNK)
    out = gdn_gate(o_f, o_b, lc, p, gnorm_w)
    out_c = gdn_gate(o_f, o_b, 0, pc, gnorm_w) if with_ctx else None
    return out, out_c


def rms_norm(x, w):
    xf = x.astype(jnp.float32)
    return (xf * lax.rsqrt(jnp.mean(xf * xf, -1, keepdims=True) + EPS) * w).astype(x.dtype)


def l2_normalize(x):
    xf = x.astype(jnp.float32)
    return (xf * lax.rsqrt(jnp.sum(xf * xf, -1, keepdims=True) + EPS)).astype(x.dtype)


def axial_rope_tables(n_tok):
    rows = n_tok // GRID_W
    row = jnp.repeat(jnp.arange(rows, dtype=jnp.float32), GRID_W)
    col = jnp.tile(jnp.arange(GRID_W, dtype=jnp.float32), rows)
    nf = HEAD_DIM // 4
    inv = ROPE_BASE ** (-jnp.arange(nf, dtype=jnp.float32) / nf)
    ang = jnp.concatenate([row[:, None] * inv, col[:, None] * inv], -1)
    return jnp.cos(ang), jnp.sin(ang)


def apply_axial_rope(x, cos, sin):
    nf = HEAD_DIM // 4
    c = cos[:, None, :]
    s = sin[:, None, :]
    parts = []
    for a in range(2):
        xa = x[..., a * 2 * nf:(a + 1) * 2 * nf]
        x1, x2 = xa[..., :nf], xa[..., nf:]
        ca, sa = c[..., a * nf:(a + 1) * nf], s[..., a * nf:(a + 1) * nf]
        parts += [x1 * ca - x2 * sa, x2 * ca + x1 * sa]
    return jnp.concatenate(parts, -1).astype(x.dtype)


def dwconv(x, w):
    k, ch = w.shape
    return lax.conv_general_dilated(x, w[:, None, :].astype(x.dtype), (1,), [(k // 2, k // 2)],
                                    dimension_numbers=('NWC', 'WIO', 'NWC'), feature_group_count=ch)


def gqa_softmax(q, k, v, sink=None):
    b, lq, h, hd = q.shape
    kvh = k.shape[2]
    g = h // kvh
    lk = k.shape[1]
    s = jnp.einsum('bqkgd,bjkd->bkgqj', q.reshape(b, lq, kvh, g, hd), k).astype(jnp.float32) * hd ** -0.5
    if sink is not None:
        s = jnp.concatenate([s, jnp.broadcast_to(sink.astype(jnp.float32).reshape(kvh, g, 1, 1), s.shape[:-1] + (1,))], -1)
    p = jax.nn.softmax(s, axis=-1)[..., :lk].astype(v.dtype)
    return jnp.einsum('bkgqj,bjkd->bqkgd', p, v).reshape(b, lq, h * hd)


def windowed_sink_attention(q, k, v, kc, vc, sink):
    b, s, h, hd = q.shape
    kvh = k.shape[2]
    g = h // kvh
    nb = s // BLOCK
    lc = kc.shape[1]
    w3 = 3 * BLOCK
    scale = hd ** -0.5
    qb = q.reshape(b, nb, BLOCK, kvh, g, hd).swapaxes(0, 1)

    def band(t):
        tb = jnp.pad(t.reshape(b, nb, BLOCK, kvh, hd), ((0, 0), (1, 1), (0, 0), (0, 0), (0, 0)))
        return jnp.concatenate([tb[:, :-2], tb[:, 1:-1], tb[:, 2:]], axis=2).swapaxes(0, 1)

    kw, vw = band(k), band(v)
    blk = jnp.arange(nb)[:, None, None]
    qpos = blk * BLOCK + jnp.arange(BLOCK)[None, :, None]
    kpos = (blk - 1) * BLOCK + jnp.arange(w3)[None, None, :]
    valid = (jnp.abs(qpos - kpos) <= WINDOW) & (kpos >= 0) & (kpos < s)
    sink_logit = sink.astype(jnp.float32).reshape(kvh, g, 1, 1)

    def one_block(args):
        qk, kk, vk, vm = args
        s_loc = jnp.einsum('bqkgd,bjkd->bkgqj', qk, kk).astype(jnp.float32) * scale
        s_loc = jnp.where(vm, s_loc, -jnp.inf)
        s_ctx = jnp.einsum('bqkgd,bjkd->bkgqj', qk, kc).astype(jnp.float32) * scale
        s_snk = jnp.broadcast_to(sink_logit, s_loc.shape[:-1] + (1,))
        p = jax.nn.softmax(jnp.concatenate([s_loc, s_ctx, s_snk], -1), axis=-1).astype(v.dtype)
        return (jnp.einsum('bkgqj,bjkd->bqkgd', p[..., :w3], vk)
                + jnp.einsum('bkgqj,bjkd->bqkgd', p[..., w3:w3 + lc], vc))

    o = lax.map(one_block, (qb, kw, vw, valid))
    return o.swapaxes(0, 1).reshape(b, s, h * hd)


def global_block_attention(q, k, v, kc, vc):
    b, s, h, hd = q.shape
    nb = s // BLOCK
    k_all = jnp.concatenate([k, kc], 1)
    v_all = jnp.concatenate([v, vc], 1)
    qb = q.reshape(b, nb, BLOCK, h, hd).swapaxes(0, 1)
    o = lax.map(lambda qk: gqa_softmax(qk, k_all, v_all), qb)
    return o.swapaxes(0, 1).reshape(b, s, h * hd)


def hyena_filters(n, w1, b1, freq, w2, b2, w3):
    t = jnp.arange(n, dtype=jnp.float32)
    tn = t / n
    f = jnp.arange(1, HY_BANDS + 1, dtype=jnp.float32)
    ang = 2.0 * math.pi * t[:, None] * f[None, :] / n
    feat = jnp.concatenate([tn[:, None], jnp.sin(ang), jnp.cos(ang)], -1)
    hid = jnp.sin(freq * (feat @ w1 + b1))
    hid = jnp.sin(freq * (hid @ w2 + b2))
    filt = (hid @ w3).astype(jnp.float32).reshape(n, HY_ORDER, 2, HY_CH)
    max_decay = math.log(HY_TARGET) / HY_FAST_DECAY
    min_decay = math.log(HY_TARGET) / HY_SLOW_DECAY
    deltas = jnp.abs(jnp.linspace(min_decay, max_decay, HY_CH, dtype=jnp.float32))
    filt = filt * jnp.exp(-tn[:, None, None, None] * deltas)
    return filt / jnp.sum(jnp.abs(filt), axis=(0, 2), keepdims=True)


def fft_long_conv(x, hf, hb, bias):
    n = x.shape[1]
    kern = jnp.concatenate([hf, jnp.zeros_like(hf[:1]), hb[1:][::-1]], 0)
    kf = jnp.fft.rfft(kern, n=2 * n, axis=0)
    xf = jnp.fft.rfft(x.astype(jnp.float32), n=2 * n, axis=1)
    y = jnp.fft.irfft(xf * kf[None], n=2 * n, axis=1)[:, :n]
    return (y + x.astype(jnp.float32) * bias.astype(jnp.float32)).astype(x.dtype)


def hyena_operator(u, conv_w, conv_b, fw1, fb1, ffreq, fw2, fb2, fw3, hy_bias):
    uc = dwconv(u, conv_w) + conv_b
    x1, x2, v = uc[..., :HY_CH], uc[..., HY_CH:2 * HY_CH], uc[..., 2 * HY_CH:]
    filt = hyena_filters(u.shape[1], fw1, fb1, ffreq, fw2, fb2, fw3)
    v = x1 * fft_long_conv(v, filt[:, 0, 0], filt[:, 0, 1], hy_bias[0])
    v = x2 * fft_long_conv(v, filt[:, 1, 0], filt[:, 1, 1], hy_bias[1])
    return v


def gated_delta_chunked(q, k, v, beta, g, s0):
    b, n_tok, h, dk = q.shape
    dv = v.shape[-1]
    cs = GDN_CHUNK
    n = n_tok // cs

    def blk(t):
        return t.astype(jnp.float32).reshape(b, n, cs, h, -1).transpose(1, 0, 3, 2, 4)

    q = blk(q) * dk ** -0.5
    k = blk(k)
    v = blk(v)
    beta = blk(beta[..., None])[..., 0]
    gcum = jnp.cumsum(blk(g[..., None])[..., 0], -1)
    idx = jnp.arange(cs)
    incl = idx[:, None] >= idx[None, :]
    strict = idx[:, None] > idx[None, :]
    decay = jnp.exp(jnp.where(incl, gcum[..., :, None] - gcum[..., None, :], -jnp.inf))
    kb = k * beta[..., None]
    a = jnp.where(strict, jnp.einsum('nbhid,nbhjd->nbhij', kb, k) * decay, 0.0)
    rhs = jnp.concatenate([v * beta[..., None], kb * jnp.exp(gcum)[..., None]], -1)
    sol = lax.linalg.triangular_solve(jnp.eye(cs, dtype=jnp.float32) + a, rhs, left_side=True, lower=True)
    u, w = sol[..., :dv], sol[..., dv:]
    intra = jnp.einsum('nbhid,nbhjd->nbhij', q, k) * decay
    q_dec = q * jnp.exp(gcum)[..., None]
    k_dec = k * jnp.exp(gcum[..., -1:] - gcum)[..., None]
    g_last = jnp.exp(gcum[..., -1])

    def step(state, xs):
        u_i, w_i, q_i, k_i, intra_i, gl = xs
        v_new = u_i - jnp.einsum('bhcd,bhde->bhce', w_i, state)
        o_i = jnp.einsum('bhcd,bhde->bhce', q_i, state) + jnp.einsum('bhij,bhje->bhie', intra_i, v_new)
        state = state * gl[..., None, None] + jnp.einsum('bhcd,bhce->bhde', k_i, v_new)
        return state, o_i

    s_final, o = lax.scan(step, s0.astype(jnp.float32), (u, w, q_dec, k_dec, intra, g_last))
    return o.transpose(1, 0, 3, 2, 4).reshape(b, n_tok, h, dv), s_final


def gdn_inputs(p, conv_w, a_log, dt_bias):
    b, n, _ = p.shape
    qkv = jax.nn.silu(dwconv(p[..., :3 * C_W], conv_w))
    q = l2_normalize(qkv[..., :C_W].reshape(b, n, C_HEADS, C_DK))
    k = l2_normalize(qkv[..., C_W:2 * C_W].reshape(b, n, C_HEADS, C_DK))
    v = qkv[..., 2 * C_W:].reshape(b, n, C_HEADS, C_DV)
    z = p[..., 3 * C_W:4 * C_W].reshape(b, n, C_HEADS, C_DV)
    gates = p[..., -C_GATES:].astype(jnp.float32).reshape(b, n, 4, C_HEADS)
    beta = jax.nn.sigmoid(gates[:, :, :2])
    g = -jnp.exp(a_log.astype(jnp.float32)) * jax.nn.softplus(gates[:, :, 2:] + dt_bias.astype(jnp.float32))
    return q, k, v, z, beta, g


def rev(t, flip):
    return t[:, ::-1] if flip else t


def bidirectional_gdn(lat, cx, with_ctx):
    ql, kl, vl, bl, gl = lat
    qc, kc, vc, bc, gcx = cx
    s0 = jnp.zeros((ql.shape[0], C_HEADS, C_DK, C_DV), jnp.float32)
    o_lat = 0.0
    o_ctx = 0.0
    for d in range(2):
        f = d == 1
        oc, sc = gated_delta_chunked(rev(qc, f), rev(kc, f), rev(vc, f), rev(bc[:, :, d], f), rev(gcx[:, :, d], f), s0)
        ol, _ = gated_delta_chunked(rev(ql, f), rev(kl, f), rev(vl, f), rev(bl[:, :, d], f), rev(gl[:, :, d], f), sc)
        o_lat = o_lat + rev(ol, f)
        if with_ctx:
            o_ctx = o_ctx + rev(oc, f)
    return o_lat, o_ctx


def even_mixer(p, pc, rope_tabs, sink, conv_w, conv_b, fw1, fb1, ffreq, fw2, fb2, fw3, hy_bias, with_ctx):
    q, k, v = qkv_prep(p, 0, A_HEADS, A_KV_HEADS, rope_tabs)
    qc, kc, vc = qkv_prep(pc, 0, A_HEADS, A_KV_HEADS, None)
    o_a = windowed_sink_gqa(q, k, v, kc, vc, sink)
    hy_args = (conv_w, conv_b, fw1, fb1, ffreq, fw2, fb2, fw3, hy_bias)
    o_b = hyena_operator(p[..., A_Q + 2 * A_KV:], *hy_args)
    out_c = None
    if with_ctx:
        o_ac = flash_gqa(qc, kc, vc, sink)
        o_bc = hyena_operator(pc[..., A_Q + 2 * A_KV:], *hy_args)
        out_c = (o_ac, o_bc)
    return (o_a, o_b), out_c


def odd_mixer(p, pc, rope_tabs, conv_w, a_log, dt_bias, gnorm_w, qnorm_w, knorm_w, with_ctx):
    o_l, o_c = gdn_mixer(p, pc, conv_w, a_log, dt_bias, gnorm_w, with_ctx)
    qd, kd, vd = qkv_prep(p, 4 * C_W, D_HEADS, D_KV_HEADS, rope_tabs, qnorm_w, knorm_w)
    qdc, kdc, vdc = qkv_prep(pc, 4 * C_W, D_HEADS, D_KV_HEADS, None, qnorm_w, knorm_w)
    o_d = flash_gqa(qd, jnp.concatenate([kd, kdc], 2), jnp.concatenate([vd, vdc], 2))
    out_c = None
    if with_ctx:
        out_c = (o_c, flash_gqa(qdc, kdc, vdc))
    return (o_l, o_d), out_c


def kernel(x, c, ctx, c_ctx, ada_w, ada_b, ln1_g, ln1_b, ln2_g, ln2_b, peer_wq, peer_k1, peer_k2, peer_u, peer_v, ev_w_in, ev_w_out, ev_sink, ev_conv_w, ev_conv_b, ev_filt_w1, ev_filt_b1, ev_filt_freq, ev_filt_w2, ev_filt_b2, ev_filt_w3, ev_hy_bias, od_w_in, od_w_out, od_conv_w, od_a_log, od_dt_bias, od_gnorm_w, od_qnorm_w, od_knorm_w):
    rope_tabs = rope_tables(x.shape[1])
    bsz = x.shape[0]
    silu_c = jax.nn.silu(c)
    silu_cc = jax.nn.silu(c_ctx)
    for i in range(DEPTH):
        with_ctx = i < DEPTH - 1
        j = i // 2
        mod = (silu_c @ ada_w[i] + ada_b[i])[:, None, :]
        modc = jnp.broadcast_to((silu_cc @ ada_w[i] + ada_b[i])[None, None, :], (bsz, 1, 6 * D_MODEL))
        sh1, sc1, g1, sh2, sc2, g2 = jnp.split(mod, 6, axis=-1)
        sh1c, sc1c, g1c, sh2c, sc2c, g2c = jnp.split(modc, 6, axis=-1)
        if i % 2 == 0:
            p = mod_matmul(x, sh1, sc1, ev_w_in[j])
            pc = mod_matmul(ctx, sh1c, sc1c, ev_w_in[j])
            out, out_c = even_mixer(p, pc, rope_tabs, ev_sink[j], ev_conv_w[j], ev_conv_b[j],
                                    ev_filt_w1[j], ev_filt_b1[j], ev_filt_freq[j], ev_filt_w2[j], ev_filt_b2[j],
                                    ev_filt_w3[j], ev_hy_bias[j], with_ctx)
            w_out = ev_w_out[j]
        else:
            w_in = od_w_in[j]
            w_in = jnp.concatenate([w_in[:, :4 * C_W], w_in[:, 4 * C_W + C_GATES:],
                                    w_in[:, 4 * C_W:4 * C_W + C_GATES]], axis=1)
            p = mod_matmul(x, sh1, sc1, w_in)
            pc = mod_matmul(ctx, sh1c, sc1c, w_in)
            out, out_c = odd_mixer(p, pc, rope_tabs, od_conv_w[j], od_a_log[j], od_dt_bias[j],
                                   od_gnorm_w[j], od_qnorm_w[j], od_knorm_w[j], with_ctx)
            w_out = od_w_out[j]
        u_bf = peer_u[i].astype(jnp.bfloat16)
        v_bf = peer_v[i].astype(jnp.bfloat16)
        x = proj_residual_ln(out[0], out[1], w_out, x, g1, ln1_g[i], ln1_b[i])
        x = peer_block(x, sh2, sc2, g2, peer_wq[i], peer_k1[i], peer_k2[i], u_bf, v_bf, ln2_g[i], ln2_b[i])
        if with_ctx:
            ctx = proj_residual_ln(out_c[0], out_c[1], w_out, ctx, g1c, ln1_g[i], ln1_b[i])
            ctx = peer_block(ctx, sh2c, sc2c, g2c, peer_wq[i], peer_k1[i], peer_k2[i], u_bf, v_bf,
                             ln2_g[i], ln2_b[i])
    return x
```

```python
import functools
import math

import numpy as np

import jax
import jax.numpy as jnp
from jax import lax
from jax.experimental import pallas as pl
from jax.experimental.pallas import tpu as pltpu

D_MODEL = 1024
DEPTH = 2
GRID_W = 64
HEAD_DIM = 64
BLOCK = 128
ROPE_BASE = 10000.0
EPS = 1e-6

A_HEADS = 8
A_KV_HEADS = 2
WINDOW = 128

HY_CH = 512
HY_ORDER = 2
HY_EMB = 33
HY_BANDS = (HY_EMB - 1) // 2
HY_FAST_DECAY = 0.3
HY_SLOW_DECAY = 1.5
HY_TARGET = 1e-2

C_HEADS = 4
C_DK = 128
C_DV = 128
GDN_CHUNK = 64

D_HEADS = 8
D_KV_HEADS = 2

PEER_HEADS = 8
PEER_NKEYS = 128
PEER_QDIM = 256
PEER_TOPK = 16
PEER_CHUNK = 128

ALPHA = (2 * DEPTH) ** 0.25

A_Q = A_HEADS * HEAD_DIM
A_KV = A_KV_HEADS * HEAD_DIM
C_W = C_HEADS * C_DK
C_GATES = 4 * C_HEADS
D_Q = D_HEADS * HEAD_DIM
D_KV = D_KV_HEADS * HEAD_DIM

VMEM_LIMIT_BYTES = 48 * 1024 * 1024

_NT = (((1,), (1,)), ((), ()))


def _modmm_kernel(x_ref, sh_ref, sc_ref, w_ref, o_ref):
    h = x_ref[0] * (1.0 + sc_ref[0]) + sh_ref[0]
    o_ref[0] = jnp.dot(h.astype(jnp.bfloat16), w_ref[...], preferred_element_type=jnp.float32)


def mod_matmul(x, shift, scale, w, tm=512, tn=None):
    b, s, k = x.shape
    n = w.shape[1]
    tm = min(tm, s)
    tn = n if tn is None else tn
    wb = w.astype(jnp.bfloat16)
    return pl.pallas_call(
        _modmm_kernel,
        grid=(b, n // tn, s // tm),
        in_specs=[
            pl.BlockSpec((1, tm, k), lambda i, j, m: (i, m, 0)),
            pl.BlockSpec((1, 1, k), lambda i, j, m: (i, 0, 0)),
            pl.BlockSpec((1, 1, k), lambda i, j, m: (i, 0, 0)),
            pl.BlockSpec((k, tn), lambda i, j, m: (0, j)),
        ],
        out_specs=pl.BlockSpec((1, tm, tn), lambda i, j, m: (i, m, j)),
        out_shape=jax.ShapeDtypeStruct((b, s, n), jnp.float32),
        compiler_params=pltpu.CompilerParams(
            dimension_semantics=("arbitrary", "arbitrary", "arbitrary"),
            vmem_limit_bytes=VMEM_LIMIT_BYTES),
        name="mod_matmul",
    )(x, shift, scale, wb)


def _qkv_prep_kernel(q_ref, k_ref, v_ref, cs_ref, sn_ref, qw_ref, kw_ref, gm_ref, qo_ref, ko_ref, vo_ref, *,
                     norm, rope, nq, nkv):
    def prep(x, w, nh):
        if norm:
            ms = jnp.dot(x * x, gm_ref[:x.shape[1], :x.shape[1]], precision=lax.Precision.HIGHEST,
                         preferred_element_type=jnp.float32)
            x = x * lax.rsqrt(ms + EPS) * w
        if rope:
            n = x.shape[1]
            reps = n // cs_ref.shape[1]
            cs = jnp.concatenate([cs_ref[...]] * reps, axis=1) if reps > 1 else cs_ref[...]
            sn = jnp.concatenate([sn_ref[...]] * reps, axis=1) if reps > 1 else sn_ref[...]
            lane = lax.broadcasted_iota(jnp.int32, x.shape, 1)
            nf = HEAD_DIM // 4
            partner = jnp.where((lane & nf) == 0, pltpu.roll(x, n - nf, 1), pltpu.roll(x, nf, 1))
            x = x * cs + partner * sn
        return x

    q = prep(q_ref[0], qw_ref[...], nq) * (HEAD_DIM ** -0.5)
    k = prep(k_ref[0], kw_ref[...], nkv)
    v = v_ref[0]
    for h in range(nq):
        qo_ref[0, h] = q[:, h * HEAD_DIM:(h + 1) * HEAD_DIM].astype(jnp.bfloat16)
    for h in range(nkv):
        ko_ref[0, h] = k[:, h * HEAD_DIM:(h + 1) * HEAD_DIM].astype(jnp.bfloat16)
        vo_ref[0, h] = v[:, h * HEAD_DIM:(h + 1) * HEAD_DIM].astype(jnp.bfloat16)


def qkv_prep(p, col0, nq, nkv, rope_tabs, qw=None, kw=None, tq=512):
    b, s, _ = p.shape
    tq = min(tq, s)
    wq_, wk_ = nq * HEAD_DIM, nkv * HEAD_DIM
    norm = qw is not None
    rope = rope_tabs is not None
    if rope:
        cs, sn = rope_tabs
    else:
        cs = sn = jnp.zeros((s, 2 * HEAD_DIM), jnp.float32)
    qw_t = jnp.tile(qw, nq).reshape(1, wq_) if norm else jnp.ones((1, wq_), jnp.float32)
    kw_t = jnp.tile(kw, nkv).reshape(1, wk_) if norm else jnp.ones((1, wk_), jnp.float32)
    grp = jnp.arange(wq_) // HEAD_DIM
    gm = (grp[:, None] == grp[None, :]).astype(jnp.float32) / HEAD_DIM
    kern = functools.partial(_qkv_prep_kernel, norm=norm, rope=rope, nq=nq, nkv=nkv)
    return pl.pallas_call(
        kern,
        grid=(b, s // tq),
        in_specs=[
            pl.BlockSpec((1, tq, wq_), lambda i, m: (i, m, col0 // wq_)),
            pl.BlockSpec((1, tq, wk_), lambda i, m: (i, m, (col0 + wq_) // wk_)),
            pl.BlockSpec((1, tq, wk_), lambda i, m: (i, m, (col0 + wq_) // wk_ + 1)),
            pl.BlockSpec((tq, 2 * HEAD_DIM), lambda i, m: (m, 0)),
            pl.BlockSpec((tq, 2 * HEAD_DIM), lambda i, m: (m, 0)),
            pl.BlockSpec((1, wq_), lambda i, m: (0, 0)),
            pl.BlockSpec((1, wk_), lambda i, m: (0, 0)),
            pl.BlockSpec((wq_, wq_), lambda i, m: (0, 0)),
        ],
        out_specs=[
            pl.BlockSpec((1, nq, tq, HEAD_DIM), lambda i, m: (i, 0, m, 0)),
            pl.BlockSpec((1, nkv, tq, HEAD_DIM), lambda i, m: (i, 0, m, 0)),
            pl.BlockSpec((1, nkv, tq, HEAD_DIM), lambda i, m: (i, 0, m, 0)),
        ],
        out_shape=[
            jax.ShapeDtypeStruct((b, nq, s, HEAD_DIM), jnp.bfloat16),
            jax.ShapeDtypeStruct((b, nkv, s, HEAD_DIM), jnp.bfloat16),
            jax.ShapeDtypeStruct((b, nkv, s, HEAD_DIM), jnp.bfloat16),
        ],
        compiler_params=pltpu.CompilerParams(
            dimension_semantics=("arbitrary", "arbitrary"), vmem_limit_bytes=VMEM_LIMIT_BYTES),
        name="qkv_prep",
    )(p, p, p, cs, sn, qw_t, kw_t, gm)


def rope_tables(n_tok):
    rows = n_tok // GRID_W
    row = jnp.repeat(jnp.arange(rows, dtype=jnp.float32), GRID_W)
    col = jnp.tile(jnp.arange(GRID_W, dtype=jnp.float32), rows)
    nf = HEAD_DIM // 4
    inv = ROPE_BASE ** (-jnp.arange(nf, dtype=jnp.float32) / nf)
    ar, ac = row[:, None] * inv, col[:, None] * inv
    cs = jnp.concatenate([jnp.cos(ar), jnp.cos(ar), jnp.cos(ac), jnp.cos(ac)], -1)
    sn = jnp.concatenate([-jnp.sin(ar), jnp.sin(ar), -jnp.sin(ac), jnp.sin(ac)], -1)
    return jnp.tile(cs, (1, 2)), jnp.tile(sn, (1, 2))


def _flash_kernel(sink_ref, q_ref, k_ref, v_ref, o_ref, m_s, l_s, acc_s, *, use_sink, grp):
    j = pl.program_id(3)
    tq = q_ref.shape[2]

    @pl.when(j == 0)
    def _():
        m_s[...] = jnp.full_like(m_s, -jnp.inf)
        l_s[...] = jnp.zeros_like(l_s)
        acc_s[...] = jnp.zeros_like(acc_s)

    q = q_ref[0].reshape(grp * tq, HEAD_DIM)
    s = lax.dot_general(q, k_ref[0, 0], _NT, preferred_element_type=jnp.float32)
    m_old = m_s[...]
    m_new = jnp.maximum(m_old, jnp.max(s, axis=1, keepdims=True))
    alpha = jnp.exp(m_old - m_new)
    p = jnp.exp(s - m_new)
    l_s[...] = alpha * l_s[...] + jnp.sum(p, axis=1, keepdims=True)
    acc_s[...] = alpha * acc_s[...] + jnp.dot(p.astype(jnp.bfloat16), v_ref[0, 0],
                                              preferred_element_type=jnp.float32)
    m_s[...] = m_new

    @pl.when(j == pl.num_programs(3) - 1)
    def _():
        kvh = pl.program_id(1)
        outs = []
        for g in range(grp):
            rows = slice(g * tq, (g + 1) * tq)
            m = m_s[rows]
            l = l_s[rows]
            acc = acc_s[rows]
            if use_sink:
                sk = sink_ref[kvh * grp + g]
                m2 = jnp.maximum(m, sk)
                a = jnp.exp(m - m2)
                l = a * l + jnp.exp(sk - m2)
                acc = a * acc
            outs.append(acc / l)
        o_ref[0] = jnp.concatenate(outs, axis=1).astype(o_ref.dtype)


def flash_gqa(q, k, v, sink=None, tq=256, tk=768):
    b, h, s, hd = q.shape
    kvh, lk = k.shape[1], k.shape[2]
    grp = h // kvh
    tq = min(tq, s)
    tk = min(tk, lk)
    use_sink = sink is not None
    sink_arr = sink.astype(jnp.float32) if use_sink else jnp.zeros((h,), jnp.float32)
    kern = functools.partial(_flash_kernel, use_sink=use_sink, grp=grp)
    return pl.pallas_call(
        kern,
        grid=(b, kvh, s // tq, lk // tk),
        in_specs=[
            pl.BlockSpec(memory_space=pltpu.SMEM),
            pl.BlockSpec((1, grp, tq, hd), lambda i, c, m, j: (i, c, m, 0)),
            pl.BlockSpec((1, 1, tk, hd), lambda i, c, m, j: (i, c, j, 0)),
            pl.BlockSpec((1, 1, tk, hd), lambda i, c, m, j: (i, c, j, 0)),
        ],
        out_specs=pl.BlockSpec((1, tq, grp * hd), lambda i, c, m, j: (i, m, c)),
        out_shape=jax.ShapeDtypeStruct((b, s, h * hd), jnp.bfloat16),
        scratch_shapes=[pltpu.VMEM((grp * tq, 1), jnp.float32), pltpu.VMEM((grp * tq, 1), jnp.float32),
                        pltpu.VMEM((grp * tq, hd), jnp.float32)],
        compiler_params=pltpu.CompilerParams(
            dimension_semantics=("arbitrary",) * 4, vmem_limit_bytes=VMEM_LIMIT_BYTES),
        name="flash_gqa",
    )(sink_arr, q, k, v)


def _window_kernel(sink_ref, q_ref, kp_ref, kc_ref, kn_ref, vp_ref, vc_ref, vn_ref, kx_ref, vx_ref, o_ref, *, grp):
    kvh = pl.program_id(1)
    i = pl.program_id(2)
    nb = pl.num_programs(2)
    kcat = jnp.concatenate([kp_ref[0, 0], kc_ref[0, 0], kn_ref[0, 0], kx_ref[0, 0]], axis=0)
    vcat = jnp.concatenate([vp_ref[0, 0], vc_ref[0, 0], vn_ref[0, 0], vx_ref[0, 0]], axis=0)
    nk = kcat.shape[0]
    r = lax.broadcasted_iota(jnp.int32, (BLOCK, nk), 0)
    c = lax.broadcasted_iota(jnp.int32, (BLOCK, nk), 1)
    off_prev = jnp.where(i > 0, 0, 2 * nk)
    off_next = jnp.where(i < nb - 1, 0, 2 * nk)
    ok_prev = (c >= BLOCK) | (c >= r + off_prev)
    ok_next = (c < 2 * BLOCK) | (c >= 3 * BLOCK) | (c - 2 * BLOCK <= r - off_next)
    valid = ok_prev & ok_next
    outs = []
    for g in range(grp):
        s = lax.dot_general(q_ref[0, g], kcat, _NT, preferred_element_type=jnp.float32)
        s = jnp.where(valid, s, -jnp.inf)
        sk = sink_ref[kvh * grp + g]
        m = jnp.maximum(jnp.max(s, axis=1, keepdims=True), sk)
        p = jnp.exp(s - m)
        l = jnp.sum(p, axis=1, keepdims=True) + jnp.exp(sk - m)
        o = jnp.dot(p.astype(jnp.bfloat16), vcat, preferred_element_type=jnp.float32)
        outs.append(o / l)
    o_ref[0] = jnp.concatenate(outs, axis=1).astype(o_ref.dtype)


def windowed_sink_gqa(q, k, v, kx, vx, sink):
    b, h, s, hd = q.shape
    kvh = k.shape[1]
    lc = kx.shape[2]
    grp = h // kvh
    nb = s // BLOCK
    kern = functools.partial(_window_kernel, grp=grp)
    blk = lambda f: pl.BlockSpec((1, 1, BLOCK, hd), f)
    prev = lambda i, c, m: (i, c, jnp.maximum(m - 1, 0), 0)
    cur = lambda i, c, m: (i, c, m, 0)
    nxt = lambda i, c, m: (i, c, jnp.minimum(m + 1, nb - 1), 0)
    ctxm = lambda i, c, m: (i, c, 0, 0)
    return pl.pallas_call(
        kern,
        grid=(b, kvh, nb),
        in_specs=[
            pl.BlockSpec(memory_space=pltpu.SMEM),
            pl.BlockSpec((1, grp, BLOCK, hd), cur),
            blk(prev), blk(cur), blk(nxt), blk(prev), blk(cur), blk(nxt),
            pl.BlockSpec((1, 1, lc, hd), ctxm), pl.BlockSpec((1, 1, lc, hd), ctxm),
        ],
        out_specs=pl.BlockSpec((1, BLOCK, grp * hd), lambda i, c, m: (i, m, c)),
        out_shape=jax.ShapeDtypeStruct((b, s, h * hd), jnp.bfloat16),
        compiler_params=pltpu.CompilerParams(
            dimension_semantics=("arbitrary",) * 3, vmem_limit_bytes=VMEM_LIMIT_BYTES),
        name="windowed_sink_gqa",
    )(sink.astype(jnp.float32), q, k, k, k, v, v, v, kx, vx)


def _post_kernel(oa_ref, ob_ref, w_ref, x_ref, g_ref, lg_ref, lb_ref, y_ref):
    ka = oa_ref.shape[2]
    out = jnp.dot(oa_ref[0].astype(jnp.bfloat16), w_ref[:ka], preferred_element_type=jnp.float32)
    out += jnp.dot(ob_ref[0].astype(jnp.bfloat16), w_ref[ka:], preferred_element_type=jnp.float32)
    r = ALPHA * x_ref[0] + g_ref[0] * out
    mu = jnp.mean(r, -1, keepdims=True)
    d = r - mu
    var = jnp.mean(d * d, -1, keepdims=True)
    y_ref[0] = d * lax.rsqrt(var + EPS) * lg_ref[...] + lb_ref[...]


def proj_residual_ln(oa, ob, w, x, gate, ln_g, ln_b, tm=256):
    b, s, ka = oa.shape
    kb = ob.shape[2]
    k = ka + kb
    d = w.shape[1]
    tm = min(tm, s)
    wb = w.astype(jnp.bfloat16)
    return pl.pallas_call(
        _post_kernel,
        grid=(b, s // tm),
        in_specs=[
            pl.BlockSpec((1, tm, ka), lambda i, m: (i, m, 0)),
            pl.BlockSpec((1, tm, kb), lambda i, m: (i, m, 0)),
            pl.BlockSpec((k, d), lambda i, m: (0, 0)),
            pl.BlockSpec((1, tm, d), lambda i, m: (i, m, 0)),
            pl.BlockSpec((1, 1, d), lambda i, m: (i, 0, 0)),
            pl.BlockSpec((1, d), lambda i, m: (0, 0)),
            pl.BlockSpec((1, d), lambda i, m: (0, 0)),
        ],
        out_specs=pl.BlockSpec((1, tm, d), lambda i, m: (i, m, 0)),
        out_shape=jax.ShapeDtypeStruct((b, s, d), jnp.float32),
        compiler_params=pltpu.CompilerParams(
            dimension_semantics=("arbitrary", "arbitrary"),
            vmem_limit_bytes=VMEM_LIMIT_BYTES),
        name="proj_residual_ln",
    )(oa, ob, wb, x, gate, ln_g.reshape(1, d), ln_b.reshape(1, d))


def _top16(s, payload=None):
    n = s.shape[0]
    iota = lax.broadcasted_iota(jnp.int32, s.shape, 0)
    vals, ids = [], []
    for _ in range(PEER_TOPK):
        m = jnp.max(s, axis=0, keepdims=True)
        pos = jnp.min(jnp.where(s == m, iota, n), axis=0, keepdims=True)
        hit = iota == pos
        vals.append(m)
        ids.append(pos if payload is None else jnp.max(jnp.where(hit, payload, -1), axis=0, keepdims=True))
        s = jnp.where(hit, -jnp.inf, s)
    return jnp.concatenate(vals, 0), jnp.concatenate(ids, 0)


def _peer_topk_kernel(q_ref, k1_ref, k2_ref, eid_ref, gate_ref, eid_s, gate_s):
    half = PEER_QDIM // 2

    def head(h, carry):
        off = pl.multiple_of(h * PEER_QDIM, PEER_QDIM)
        q1 = q_ref[:, pl.ds(off, half)]
        q2 = q_ref[:, pl.ds(off + half, half)]
        s1 = lax.dot_general(k1_ref[h], q1, _NT, precision=lax.Precision.HIGHEST,
                             preferred_element_type=jnp.float32)
        s2 = lax.dot_general(k2_ref[h], q2, _NT, precision=lax.Precision.HIGHEST,
                             preferred_element_type=jnp.float32)
        v1, i1 = _top16(s1)
        v2, i2 = _top16(s2)
        k8 = PEER_TOPK // 2
        cand = jnp.concatenate([v1[0:1] + v2] + [v1[i:i + 1] + v2[:k8] for i in range(1, k8)]
                               + [v1[k8:] + v2[0:1]], 0)
        cid = jnp.concatenate([i1[0:1] * PEER_NKEYS + i2]
                              + [i1[i:i + 1] * PEER_NKEYS + i2[:k8] for i in range(1, k8)]
                              + [i1[k8:] * PEER_NKEYS + i2[0:1]], 0)
        best, eid = _top16(cand, cid)
        e = jnp.exp(best - best[0:1])
        gate = e / jnp.sum(e, axis=0, keepdims=True)
        row = pl.multiple_of(h * PEER_TOPK, PEER_TOPK)
        eid_s[pl.ds(row, PEER_TOPK), :] = eid
        gate_s[pl.ds(row, PEER_TOPK), :] = gate
        return carry

    lax.fori_loop(0, PEER_HEADS, head, 0)
    eid_ref[...] = eid_s[...].T
    gate_ref[...] = gate_s[...].T


def peer_topk(q, k1, k2, tt=512):
    t = q.shape[0]
    tt = min(tt, t)
    nsel = PEER_HEADS * PEER_TOPK
    return pl.pallas_call(
        _peer_topk_kernel,
        grid=(t // tt,),
        in_specs=[
            pl.BlockSpec((tt, q.shape[1]), lambda i: (i, 0)),
            pl.BlockSpec(k1.shape, lambda i: (0, 0, 0)),
            pl.BlockSpec(k2.shape, lambda i: (0, 0, 0)),
        ],
        out_specs=[pl.BlockSpec((tt, nsel), lambda i: (i, 0)),
                   pl.BlockSpec((tt, nsel), lambda i: (i, 0))],
        out_shape=[jax.ShapeDtypeStruct((t, nsel), jnp.int32),
                   jax.ShapeDtypeStruct((t, nsel), jnp.float32)],
        scratch_shapes=[pltpu.VMEM((nsel, tt), jnp.int32), pltpu.VMEM((nsel, tt), jnp.float32)],
        compiler_params=pltpu.CompilerParams(
            dimension_semantics=("arbitrary",), vmem_limit_bytes=VMEM_LIMIT_BYTES),
        name="peer_topk",
    )(q, k1, k2)


def _peer_w_kernel(e_ref, g_ref, w_ref):
    nk = PEER_NKEYS
    iota = lax.broadcasted_iota(jnp.int32, (nk, e_ref.shape[1]), 0)

    def tok(t, carry):
        e = e_ref[pl.ds(t, 1), :]
        g = g_ref[pl.ds(t, 1), :]
        a_t = jnp.where(iota == (e >> 7), g, 0.0).astype(jnp.bfloat16)
        b_t = jnp.where(iota == (e & (nk - 1)), 1.0, 0.0).astype(jnp.bfloat16)
        w = lax.dot_general(a_t, b_t, _NT, preferred_element_type=jnp.float32)
        w_ref[t] = w.astype(jnp.bfloat16)
        return carry

    lax.fori_loop(0, e_ref.shape[0], tok, 0, unroll=4)


def peer_dense_gates(eid, gate, tt=128):
    t, nsel = eid.shape
    tt = min(tt, t)
    nk = PEER_NKEYS
    w = pl.pallas_call(
        _peer_w_kernel,
        grid=(t // tt,),
        in_specs=[pl.BlockSpec((tt, nsel), lambda i: (i, 0)),
                  pl.BlockSpec((tt, nsel), lambda i: (i, 0))],
        out_specs=pl.BlockSpec((tt, nk, nk), lambda i: (i, 0, 0)),
        out_shape=jax.ShapeDtypeStruct((t, nk, nk), jnp.bfloat16),
        compiler_params=pltpu.CompilerParams(
            dimension_semantics=("arbitrary",), vmem_limit_bytes=VMEM_LIMIT_BYTES),
        name="peer_dense_gates",
    )(eid, gate)
    return w


def _peer_expert_kernel(x_ref, sh_ref, sc_ref, w_ref, u_ref, v_ref, g_ref, lg_ref, lb_ref, y_ref, xm_s, acc_s):
    e = pl.program_id(2)

    @pl.when(e == 0)
    def _():
        xm_s[...] = (x_ref[0] * (1.0 + sc_ref[0]) + sh_ref[0]).astype(jnp.bfloat16)
        acc_s[...] = jnp.zeros_like(acc_s)

    h = lax.dot_general(xm_s[...], u_ref[...], _NT, preferred_element_type=jnp.float32)
    gelu = 0.5 * h * (1.0 + lax.erf(h * (2.0 ** -0.5)))
    w = w_ref[0].reshape(h.shape)
    a = gelu * w.astype(jnp.float32)
    acc_s[...] += jnp.dot(a.astype(jnp.bfloat16), v_ref[...], preferred_element_type=jnp.float32)

    @pl.when(e == pl.num_programs(2) - 1)
    def _():
        r = ALPHA * x_ref[0] + g_ref[0] * acc_s[...]
        mu = jnp.mean(r, -1, keepdims=True)
        d = r - mu
        var = jnp.mean(d * d, -1, keepdims=True)
        y_ref[0] = d * lax.rsqrt(var + EPS) * lg_ref[...] + lb_ref[...]


def peer_experts_ln(x, shift, scale, w, u_tab, v_tab, gate, ln_g, ln_b, tt=512, te=2048):
    b, s, d = x.shape
    tt = min(tt, s)
    ne = u_tab.shape[0]
    nk = PEER_NKEYS
    w3 = w.reshape(b, s, nk, nk)
    return pl.pallas_call(
        _peer_expert_kernel,
        grid=(b, s // tt, ne // te),
        in_specs=[
            pl.BlockSpec((1, tt, d), lambda i, m, e: (i, m, 0)),
            pl.BlockSpec((1, 1, d), lambda i, m, e: (i, 0, 0)),
            pl.BlockSpec((1, 1, d), lambda i, m, e: (i, 0, 0)),
            pl.BlockSpec((1, tt, te // nk, nk), lambda i, m, e: (i, m, e, 0)),
            pl.BlockSpec((te, d), lambda i, m, e: (e, 0)),
            pl.BlockSpec((te, d), lambda i, m, e: (e, 0)),
            pl.BlockSpec((1, 1, d), lambda i, m, e: (i, 0, 0)),
            pl.BlockSpec((1, d), lambda i, m, e: (0, 0)),
            pl.BlockSpec((1, d), lambda i, m, e: (0, 0)),
        ],
        out_specs=pl.BlockSpec((1, tt, d), lambda i, m, e: (i, m, 0)),
        out_shape=jax.ShapeDtypeStruct((b, s, d), jnp.float32),
        scratch_shapes=[pltpu.VMEM((tt, d), jnp.bfloat16), pltpu.VMEM((tt, d), jnp.float32)],
        compiler_params=pltpu.CompilerParams(
            dimension_semantics=("arbitrary", "arbitrary", "arbitrary"),
            vmem_limit_bytes=VMEM_LIMIT_BYTES),
        name="peer_experts_ln",
    )(x, shift, scale, w3, u_tab, v_tab, gate, ln_g.reshape(1, d), ln_b.reshape(1, d))


def peer_block(x, shift, scale, gate, wq, k1, k2, u_bf, v_bf, ln_g, ln_b):
    b, s, d = x.shape
    q_all = mod_matmul(x, shift, scale, wq).reshape(b * s, -1)
    eid, gsel = peer_topk(q_all, k1, k2)
    w = peer_dense_gates(eid, gsel)
    return peer_experts_ln(x, shift, scale, w, u_bf, v_bf, gate, ln_g, ln_b)


HEAD_LANES = 128


def _short_conv_kernel(x_ref, xp_ref, xn_ref, w_ref, b_ref, o_ref, *, silu, n_l2, n_scaled):
    cb = pl.program_id(1)
    m = pl.program_id(2)
    x = x_ref[0]
    tq, wb = x.shape
    prev_row = jnp.where(m > 0, xp_ref[0][7:8], 0.0)
    next_row = jnp.where(m < pl.num_programs(2) - 1, xn_ref[0][0:1], 0.0)
    row = lax.broadcasted_iota(jnp.int32, x.shape, 0)
    x_m1 = jnp.where(row == 0, prev_row, pltpu.roll(x, 1, 0))
    x_p1 = jnp.where(row == tq - 1, next_row, pltpu.roll(x, tq - 1, 0))
    y = w_ref[0:1] * x_m1 + w_ref[1:2] * x + w_ref[2:3] * x_p1 + b_ref[...]
    if silu:
        y = y * jax.nn.sigmoid(y)
    if n_l2 == 0:
        o_ref[0] = y
        return
    hpb = wb // HEAD_LANES
    for hh in range(hpb):
        gh = cb * hpb + hh
        seg = y[:, hh * HEAD_LANES:(hh + 1) * HEAD_LANES]
        inv = lax.rsqrt(jnp.sum(seg * seg, axis=-1, keepdims=True) + EPS)
        f = jnp.where(gh < n_l2, inv, 1.0) * jnp.where(gh < n_scaled, C_DK ** -0.5, 1.0)
        o_ref[0, :, hh * HEAD_LANES:(hh + 1) * HEAD_LANES] = seg * f


def short_conv(p, col0, width, w, bias=None, silu=False, n_l2=0, n_scaled=0, wb=768, tq=512):
    b, l, _ = p.shape
    tq = min(tq, l)
    bias2 = (jnp.zeros((width,), jnp.float32) if bias is None else bias).reshape(1, width)
    c0 = col0 // wb
    kern = functools.partial(_short_conv_kernel, silu=silu, n_l2=n_l2, n_scaled=n_scaled)
    r8 = tq // 8
    return pl.pallas_call(
        kern,
        grid=(b, width // wb, l // tq),
        in_specs=[
            pl.BlockSpec((1, tq, wb), lambda i, c, m: (i, m, c0 + c)),
            pl.BlockSpec((1, 8, wb), lambda i, c, m: (i, jnp.maximum(m * r8 - 1, 0), c0 + c)),
            pl.BlockSpec((1, 8, wb), lambda i, c, m: (i, jnp.minimum((m + 1) * r8, l // 8 - 1), c0 + c)),
            pl.BlockSpec((3, wb), lambda i, c, m: (0, c)),
            pl.BlockSpec((1, wb), lambda i, c, m: (0, c)),
        ],
        out_specs=pl.BlockSpec((1, tq, wb), lambda i, c, m: (i, m, c)),
        out_shape=jax.ShapeDtypeStruct((b, l, width), jnp.float32),
        compiler_params=pltpu.CompilerParams(
            dimension_semantics=("arbitrary",) * 3, vmem_limit_bytes=VMEM_LIMIT_BYTES),
        name="short_conv",
    )(p, p, p, w, bias2)


def _dot3(a, b):
    ah = a.astype(jnp.bfloat16)
    bh = b.astype(jnp.bfloat16)
    al = (a - ah.astype(jnp.float32)).astype(jnp.bfloat16)
    bl = (b - bh.astype(jnp.float32)).astype(jnp.bfloat16)
    d = functools.partial(jnp.dot, preferred_element_type=jnp.float32)
    return d(ah, bh) + (d(ah, bl) + d(al, bh))


def _gdn_chunk_kernel(qkv_ref, beta_ref, g_ref, u_ref, w_ref, qd_ref, kd_ref, in_ref, gl_ref, *, nc):
    d = pl.program_id(0)
    cs = GDN_CHUNK
    ii = lax.broadcasted_iota(jnp.int32, (cs, cs), 0)
    jj = lax.broadcasted_iota(jnp.int32, (cs, cs), 1)
    lo = (ii - jj) * (1 - 2 * d)
    incl = lo >= 0
    strict = lo > 0
    tri = jnp.where(incl, 1.0, 0.0).astype(jnp.bfloat16)
    tri3 = jnp.concatenate([tri, tri, tri], axis=1)
    eye = jnp.where(ii == jj, 1.0, 0.0)

    def chunk_pair(cp, carry):
        probs = []
        for c in (2 * cp, 2 * cp + 1):
            rows = pl.ds(pl.multiple_of(c * cs, cs), cs)
            g_c = g_ref[0, 0, rows, :]
            b_c = beta_ref[0, 0, rows, :]
            g_hi = g_c.astype(jnp.bfloat16)
            r1 = g_c - g_hi.astype(jnp.float32)
            g_mid = r1.astype(jnp.bfloat16)
            g_lo = (r1 - g_mid.astype(jnp.float32)).astype(jnp.bfloat16)
            gc = jnp.dot(tri3, jnp.concatenate([g_hi, g_mid, g_lo], axis=0),
                         preferred_element_type=jnp.float32)
            tot = jnp.sum(g_c, axis=0, keepdims=True)
            for h in range(C_HEADS):
                probs.append(dict(c=c, h=h, rows=rows, gc=gc[:, h:h + 1], bt=b_c[:, h:h + 1], tot=tot[:, h:h + 1]))
        for pr in probs:
            h, rows = pr["h"], pr["rows"]
            q = qkv_ref[0, rows, h * HEAD_LANES:(h + 1) * HEAD_LANES]
            k = qkv_ref[0, rows, C_W + h * HEAD_LANES:C_W + (h + 1) * HEAD_LANES]
            kb = k * pr["bt"]
            kq = lax.dot_general(jnp.concatenate([kb, q], axis=0).astype(jnp.bfloat16), k.astype(jnp.bfloat16),
                                 _NT, preferred_element_type=jnp.float32)
            gc_row = jnp.broadcast_to(pr["gc"], (cs, HEAD_LANES)).T[:cs, :]
            dm = jnp.where(incl, jnp.exp(pr["gc"] - gc_row), 0.0)
            x = jnp.where(strict, -(kq[:cs] * dm), 0.0)
            in_ref[0, 0, pr["c"], h] = (kq[cs:] * dm).astype(in_ref.dtype)
            pr.update(t=eye + x, pw=x)
        for _ in range(5):
            for pr in probs:
                pr["pw"] = _dot3(pr["pw"], pr["pw"])
            for pr in probs:
                pr["t"] = pr["t"] + _dot3(pr["t"], pr["pw"])
        for pr in probs:
            h, rows = pr["h"], pr["rows"]
            lanes = slice(h * HEAD_LANES, (h + 1) * HEAD_LANES)
            q = qkv_ref[0, rows, h * HEAD_LANES:(h + 1) * HEAD_LANES]
            k = qkv_ref[0, rows, C_W + h * HEAD_LANES:C_W + (h + 1) * HEAD_LANES]
            v = qkv_ref[0, rows, 2 * C_W + h * HEAD_LANES:2 * C_W + (h + 1) * HEAD_LANES]
            eg = jnp.exp(pr["gc"])
            uw = _dot3(pr["t"], jnp.concatenate([v * pr["bt"], k * (pr["bt"] * eg)], axis=1))
            u_ref[0, 0, rows, lanes] = uw[:, :HEAD_LANES]
            w_ref[0, 0, rows, lanes] = uw[:, HEAD_LANES:].astype(w_ref.dtype)
            qd_ref[0, 0, rows, lanes] = (q * eg).astype(qd_ref.dtype)
            kd_ref[0, 0, rows, lanes] = (k * jnp.exp(pr["tot"] - pr["gc"])).astype(kd_ref.dtype)
            gl_ref[0, 0, pr["c"], h:h + 1, :] = jnp.broadcast_to(jnp.exp(pr["tot"]), (1, HEAD_LANES))
        return carry

    lax.fori_loop(0, nc // 2, chunk_pair, 0)


def gdn_chunk_prep(qkv, beta, g, nc=4):
    b, l, _ = qkv.shape
    cs = GDN_CHUNK
    tq = nc * cs
    nchunks = l // cs
    bf = jnp.bfloat16
    big = lambda dt: jax.ShapeDtypeStruct((2, b, l, C_W), dt)
    bspec = pl.BlockSpec((1, 1, tq, C_W), lambda d, i, m: (d, i, m, 0))
    gspec = pl.BlockSpec((1, 1, tq, C_HEADS), lambda d, i, m: (d, i, m, 0))
    return pl.pallas_call(
        functools.partial(_gdn_chunk_kernel, nc=nc),
        grid=(2, b, l // tq),
        in_specs=[pl.BlockSpec((1, tq, 3 * C_W), lambda d, i, m: (i, m, 0)), gspec, gspec],
        out_specs=[bspec, bspec, bspec, bspec,
                   pl.BlockSpec((1, 1, nc, C_HEADS, cs, cs), lambda d, i, m: (d, i, m, 0, 0, 0)),
                   pl.BlockSpec((1, 1, nc, C_HEADS, HEAD_LANES), lambda d, i, m: (d, i, m, 0, 0))],
        out_shape=[big(jnp.float32), big(bf), big(bf), big(bf),
                   jax.ShapeDtypeStruct((2, b, nchunks, C_HEADS, cs, cs), bf),
                   jax.ShapeDtypeStruct((2, b, nchunks, C_HEADS, HEAD_LANES), jnp.float32)],
        compiler_params=pltpu.CompilerParams(
            dimension_semantics=("arbitrary",) * 3, vmem_limit_bytes=VMEM_LIMIT_BYTES),
        name="gdn_chunk_prep",
    )(qkv, beta, g)


def _gdn_scan_kernel(*refs):
    ins, (of_ref, ob_ref, s_ref) = refs[:12], refs[12:]
    step = pl.program_id(1)

    @pl.when(step == 0)
    def _():
        s_ref[...] = jnp.zeros_like(s_ref)

    for d, o_ref in enumerate((of_ref, ob_ref)):
        u_ref, w_ref, qd_ref, kd_ref, in_ref, gl_ref = ins[6 * d:6 * d + 6]
        for h in range(C_HEADS):
            lanes = slice(h * HEAD_LANES, (h + 1) * HEAD_LANES)
            s = s_ref[d * C_HEADS + h]
            sb = s.astype(jnp.bfloat16)
            v_new = u_ref[0, 0, :, lanes] - jnp.dot(w_ref[0, 0, :, lanes], sb, preferred_element_type=jnp.float32)
            vb = v_new.astype(jnp.bfloat16)
            o_ref[0, :, lanes] = (jnp.dot(qd_ref[0, 0, :, lanes], sb, preferred_element_type=jnp.float32)
                                  + jnp.dot(in_ref[0, 0, 0, h], vb, preferred_element_type=jnp.float32))
            s_ref[d * C_HEADS + h] = s * gl_ref[0, 0, 0, h:h + 1, :] + lax.dot_general(
                kd_ref[0, 0, :, lanes], vb, (((0,), (0,)), ((), ())), preferred_element_type=jnp.float32)


def gdn_scan(u, w, qd, kd, intra, gl, n_ctx_chunks):
    _, b, l, _ = u.shape
    cs = GDN_CHUNK
    nchunks = l // cs

    def chunk_of(d, s):
        if d == 0:
            return s
        return jnp.where(s < n_ctx_chunks, n_ctx_chunks - 1 - s, nchunks - 1 + n_ctx_chunks - s)

    in_specs, args = [], []
    for d in range(2):
        big = pl.BlockSpec((1, 1, cs, C_W), lambda i, s, d=d: (d, i, chunk_of(d, s), 0))
        in_specs += [big, big, big, big,
                     pl.BlockSpec((1, 1, 1, C_HEADS, cs, cs), lambda i, s, d=d: (d, i, chunk_of(d, s), 0, 0, 0)),
                     pl.BlockSpec((1, 1, 1, C_HEADS, HEAD_LANES), lambda i, s, d=d: (d, i, chunk_of(d, s), 0, 0))]
        args += [u, w, qd, kd, intra, gl]
    out_specs = [pl.BlockSpec((1, cs, C_W), lambda i, s, d=d: (i, chunk_of(d, s), 0)) for d in range(2)]
    return pl.pallas_call(
        _gdn_scan_kernel,
        grid=(b, nchunks),
        in_specs=in_specs,
        out_specs=out_specs,
        out_shape=[jax.ShapeDtypeStruct((b, l, C_W), jnp.float32)] * 2,
        scratch_shapes=[pltpu.VMEM((2 * C_HEADS, C_DK, C_DV), jnp.float32)],
        compiler_params=pltpu.CompilerParams(
            dimension_semantics=("arbitrary",) * 2, vmem_limit_bytes=VMEM_LIMIT_BYTES),
        name="gdn_scan",
    )(*args)


def _gdn_gate_kernel(of_ref, ob_ref, z_ref, gw_ref, y_ref):
    o = of_ref[0] + ob_ref[0]
    z = z_ref[0]
    for h in range(C_HEADS):
        lanes = slice(h * HEAD_LANES, (h + 1) * HEAD_LANES)
        oh = o[:, lanes]
        zh = z[:, lanes]
        n = oh * lax.rsqrt(jnp.mean(oh * oh, axis=-1, keepdims=True) + EPS) * gw_ref[...]
        y_ref[0, :, lanes] = (n * (zh * jax.nn.sigmoid(zh))).astype(y_ref.dtype)


def gdn_gate(o_f, o_b, row0, p, gnorm_w, tq=256):
    b, l, _ = p.shape
    tq = min(tq, l)
    r0 = row0 // tq
    ospec = pl.BlockSpec((1, tq, C_W), lambda i, m: (i, r0 + m, 0))
    return pl.pallas_call(
        _gdn_gate_kernel,
        grid=(b, l // tq),
        in_specs=[ospec, ospec,
                  pl.BlockSpec((1, tq, C_W), lambda i, m: (i, m, 3)),
                  pl.BlockSpec((1, HEAD_LANES), lambda i, m: (0, 0))],
        out_specs=pl.BlockSpec((1, tq, C_W), lambda i, m: (i, m, 0)),
        out_shape=jax.ShapeDtypeStruct((b, l, C_W), jnp.bfloat16),
        compiler_params=pltpu.CompilerParams(
            dimension_semantics=("arbitrary",) * 2, vmem_limit_bytes=VMEM_LIMIT_BYTES),
        name="gdn_gate",
    )(o_f, o_b, p, gnorm_w.reshape(1, HEAD_LANES))


def gdn_mixer(p, pc, conv_w, a_log, dt_bias, gnorm_w, with_ctx):
    lc = pc.shape[1]
    conv = functools.partial(short_conv, col0=0, width=3 * C_W, w=conv_w, silu=True,
                             n_l2=2 * C_HEADS, n_scaled=C_HEADS)
    qkv = jnp.concatenate([conv(pc), conv(p)], axis=1)
    gates = jnp.concatenate([pc[..., -C_GATES:], p[..., -C_GATES:]], axis=1)
    gates = gates.reshape(gates.shape[0], gates.shape[1], 4, C_HEADS)
    beta = jax.nn.sigmoid(gates[:, :, :2])
    g = -jnp.exp(a_log) * jax.nn.softplus(gates[:, :, 2:] + dt_bias)
    beta = jnp.moveaxis(beta, 2, 0)
    g = jnp.moveaxis(g, 2, 0)
    u, w, qd, kd, intra, gl = gdn_chunk_prep(qkv, beta, g)
    o_f, o_b = gdn_scan(u, w, qd, kd, intra, gl, lc // GDN_CHUNK)
    out = gdn_gate(o_f, o_b, lc, p, gnorm_w)
    out_c = gdn_gate(o_f, o_b, 0, pc, gnorm_w) if with_ctx else None
    return out, out_c


FFT_R = 128
FFT_N = FFT_R * FFT_R
SUB = 8


def _dft_tables():
    idx = np.arange(FFT_R)
    ang = 2.0 * np.pi * np.outer(idx, idx) / FFT_R
    c, s = np.cos(ang), np.sin(ang)
    stage_a = np.stack([c, -s], axis=1).reshape(2 * FFT_R, FFT_R)
    stage_b = np.block([[c, s], [-s, c]])
    return (jnp.asarray(stage_a, jnp.float32), jnp.asarray(stage_b, jnp.float32))


def _fft_stage_a_kernel(x_ref, l_ref, y_ref):
    l = l_ref[...]
    c = x_ref.shape[-1]
    for j in range(SUB):
        y_ref[0, :, :, j, :] = _dot3(l, x_ref[0, :, j, :]).reshape(FFT_R, 2, c)


def fft_stage_a(x, col_blk, width, stage_a):
    b, l, wtot = x.shape
    n1cnt = l // FFT_R
    x4 = x.reshape(b, n1cnt, FFT_R, wtot)
    return pl.pallas_call(
        _fft_stage_a_kernel,
        grid=(b, FFT_R // SUB),
        in_specs=[pl.BlockSpec((1, n1cnt, SUB, width), lambda i, j: (i, 0, j, col_blk)),
                  pl.BlockSpec((2 * FFT_R, n1cnt), lambda i, j: (0, 0))],
        out_specs=pl.BlockSpec((1, FFT_R, 2, SUB, width), lambda i, j: (i, 0, 0, j, 0)),
        out_shape=jax.ShapeDtypeStruct((b, FFT_R, 2, FFT_R, width), jnp.float32),
        compiler_params=pltpu.CompilerParams(
            dimension_semantics=("arbitrary",) * 2, vmem_limit_bytes=VMEM_LIMIT_BYTES),
        name="fft_stage_a",
    )(x4, stage_a[:, :n1cnt])


def _fft_mid_kernel(y_ref, h_ref, m_ref, mi_ref, o_ref, *, conv):
    k1 = pl.program_id(0)
    r, c = FFT_R, y_ref.shape[-1]
    n2 = lax.broadcasted_iota(jnp.int32, (r, 1), 0)
    ang = (n2 * k1).astype(jnp.float32) * (2.0 * math.pi / FFT_N)
    tc, ts = jnp.cos(ang), jnp.sin(ang)
    yr, yi = y_ref[0, 0, 0], y_ref[0, 0, 1]
    z = jnp.concatenate([yr * tc + yi * ts, yi * tc - yr * ts], axis=0)
    m = m_ref[...]
    x = _dot3(m, z)
    xr, xi = x[:r], x[r:]
    if not conv:
        o_ref[0, 0] = (x * (1.0 / FFT_N)).reshape(2, r, c)
        return
    hr, hi = h_ref[0, 0], h_ref[0, 1]
    p = jnp.concatenate([xr * hr - xi * hi, xr * hi + xi * hr], axis=0)
    a = _dot3(mi_ref[...], p)
    ar, ai = a[:r], a[r:]
    o_ref[0, 0, 0] = ar * tc - ai * ts
    o_ref[0, 0, 1] = ai * tc + ar * ts


def fft_mid(y, h, stage_b, conv):
    b, r, _, _, c = y.shape
    hh = h if conv else jnp.zeros((1, 2, 8, c), jnp.float32)
    hspec = (pl.BlockSpec((1, 2, r, c), lambda k, i: (k, 0, 0, 0)) if conv
             else pl.BlockSpec((1, 2, 8, c), lambda k, i: (0, 0, 0, 0)))
    blk = pl.BlockSpec((1, 1, 2, r, c), lambda k, i: (i, k, 0, 0, 0))
    return pl.pallas_call(
        functools.partial(_fft_mid_kernel, conv=conv),
        grid=(r, b),
        in_specs=[blk, hspec, pl.BlockSpec((2 * r, 2 * r), lambda k, i: (0, 0)),
                  pl.BlockSpec((2 * r, 2 * r), lambda k, i: (0, 0))],
        out_specs=blk,
        out_shape=jax.ShapeDtypeStruct(y.shape, jnp.float32),
        compiler_params=pltpu.CompilerParams(
            dimension_semantics=("arbitrary",) * 2, vmem_limit_bytes=VMEM_LIMIT_BYTES),
        name="fft_mid",
    )(y, hh, stage_b, stage_b.T)


def _fft_out_kernel(b_ref, l_ref, xg_ref, xin_ref, bias_ref, o_ref):
    l = l_ref[...]
    c = o_ref.shape[-1]
    for j in range(SUB):
        y = _dot3(l, b_ref[0, :, :, j, :].reshape(2 * FFT_R, c))
        o_ref[0, :, j, :] = xg_ref[0, :, j, :] * (y + bias_ref[...] * xin_ref[0, :, j, :])


def fft_out_gate(bm, stage_a, xg, xg_blk, xin, xin_blk, bias):
    b, r, _, _, c = bm.shape
    l = xg.shape[1]
    n1cnt = l // r
    view = lambda t: t.reshape(b, n1cnt, r, t.shape[-1])
    lhs = stage_a.T[:n1cnt]
    return pl.pallas_call(
        _fft_out_kernel,
        grid=(b, r // SUB),
        in_specs=[pl.BlockSpec((1, r, 2, SUB, c), lambda i, j: (i, 0, 0, j, 0)),
                  pl.BlockSpec((n1cnt, 2 * r), lambda i, j: (0, 0)),
                  pl.BlockSpec((1, n1cnt, SUB, c), lambda i, j: (i, 0, j, xg_blk)),
                  pl.BlockSpec((1, n1cnt, SUB, c), lambda i, j: (i, 0, j, xin_blk)),
                  pl.BlockSpec((1, c), lambda i, j: (0, 0))],
        out_specs=pl.BlockSpec((1, n1cnt, SUB, c), lambda i, j: (i, 0, j, 0)),
        out_shape=jax.ShapeDtypeStruct((b, n1cnt, r, c), jnp.float32),
        compiler_params=pltpu.CompilerParams(
            dimension_semantics=("arbitrary",) * 2, vmem_limit_bytes=VMEM_LIMIT_BYTES),
        name="fft_out_gate",
    )(bm, lhs, view(xg), view(xin), bias.reshape(1, c)).reshape(b, l, c)


def _direct_conv_kernel(xin_ref, xg_ref, kern_ref, d1_ref, d2_ref, bias_ref, o_ref):
    n = xin_ref.shape[1]
    d1 = d1_ref[...]
    x = xin_ref[0]
    xs = _dot3(d1[:, :n], x)
    hs = _dot3(d1, kern_ref[...])
    xr, xi, hr, hi = xs[:2 * n], xs[2 * n:], hs[:2 * n], hs[2 * n:]
    p = jnp.concatenate([xr * hr - xi * hi, xr * hi + xi * hr], axis=0)
    y = _dot3(d2_ref[...], p)
    o_ref[0] = xg_ref[0] * (y + bias_ref[...] * x)


def direct_long_conv(xin, xin_blk, xg, xg_blk, kern, bias):
    b, n, _ = xin.shape
    c = kern.shape[1]
    idx = np.arange(2 * n)
    ang = 2.0 * np.pi * np.outer(idx, idx) / (2 * n)
    d1 = jnp.asarray(np.concatenate([np.cos(ang), -np.sin(ang)], axis=0), jnp.float32)
    d2 = jnp.asarray(np.concatenate([np.cos(ang[:n]), -np.sin(ang[:n])], axis=1) / (2 * n), jnp.float32)
    return pl.pallas_call(
        _direct_conv_kernel,
        grid=(b,),
        in_specs=[pl.BlockSpec((1, n, c), lambda i: (i, 0, xin_blk)),
                  pl.BlockSpec((1, n, c), lambda i: (i, 0, xg_blk)),
                  pl.BlockSpec((2 * n, c), lambda i: (0, 0)),
                  pl.BlockSpec((4 * n, 2 * n), lambda i: (0, 0)),
                  pl.BlockSpec((n, 4 * n), lambda i: (0, 0)),
                  pl.BlockSpec((1, c), lambda i: (0, 0))],
        out_specs=pl.BlockSpec((1, n, c), lambda i: (i, 0, 0)),
        out_shape=jax.ShapeDtypeStruct((b, n, c), jnp.float32),
        compiler_params=pltpu.CompilerParams(
            dimension_semantics=("arbitrary",), vmem_limit_bytes=VMEM_LIMIT_BYTES),
        name="direct_long_conv",
    )(xin, xg, kern, d1, d2, bias.reshape(1, c))


def hyena_mixer(p, col0, conv_w, conv_b, filt_args, hy_bias):
    n = p.shape[1]
    uc = short_conv(p, col0, 3 * HY_CH, conv_w, conv_b)
    filt = hyena_filters(n, *filt_args)
    kerns = [jnp.concatenate([filt[:, o, 0], jnp.zeros_like(filt[:1, o, 0]), filt[1:, o, 1][::-1]], 0)
             for o in range(HY_ORDER)]
    if 2 * n != FFT_N:
        v = direct_long_conv(uc, 2, uc, 0, kerns[0], hy_bias[0])
        return direct_long_conv(v, 0, uc, 1, kerns[1], hy_bias[1])
    stage_a, stage_b = _dft_tables()
    spec = fft_mid(fft_stage_a(jnp.stack(kerns), 0, HY_CH, stage_a), None, stage_b, conv=False)
    v = fft_out_gate(fft_mid(fft_stage_a(uc, 2, HY_CH, stage_a), spec[0], stage_b, conv=True),
                     stage_a, uc, 0, uc, 2, hy_bias[0])
    return fft_out_gate(fft_mid(fft_stage_a(v, 0, HY_CH, stage_a), spec[1], stage_b, conv=True),
                        stage_a, uc, 1, v, 0, hy_bias[1])


def rms_norm(x, w):
    xf = x.astype(jnp.float32)
    return (xf * lax.rsqrt(jnp.mean(xf * xf, -1, keepdims=True) + EPS) * w).astype(x.dtype)


def l2_normalize(x):
    xf = x.astype(jnp.float32)
    return (xf * lax.rsqrt(jnp.sum(xf * xf, -1, keepdims=True) + EPS)).astype(x.dtype)


def axial_rope_tables(n_tok):
    rows = n_tok // GRID_W
    row = jnp.repeat(jnp.arange(rows, dtype=jnp.float32), GRID_W)
    col = jnp.tile(jnp.arange(GRID_W, dtype=jnp.float32), rows)
    nf = HEAD_DIM // 4
    inv = ROPE_BASE ** (-jnp.arange(nf, dtype=jnp.float32) / nf)
    ang = jnp.concatenate([row[:, None] * inv, col[:, None] * inv], -1)
    return jnp.cos(ang), jnp.sin(ang)


def apply_axial_rope(x, cos, sin):
    nf = HEAD_DIM // 4
    c = cos[:, None, :]
    s = sin[:, None, :]
    parts = []
    for a in range(2):
        xa = x[..., a * 2 * nf:(a + 1) * 2 * nf]
        x1, x2 = xa[..., :nf], xa[..., nf:]
        ca, sa = c[..., a * nf:(a + 1) * nf], s[..., a * nf:(a + 1) * nf]
        parts += [x1 * ca - x2 * sa, x2 * ca + x1 * sa]
    return jnp.concatenate(parts, -1).astype(x.dtype)


def dwconv(x, w):
    k, ch = w.shape
    return lax.conv_general_dilated(x, w[:, None, :].astype(x.dtype), (1,), [(k // 2, k // 2)],
                                    dimension_numbers=('NWC', 'WIO', 'NWC'), feature_group_count=ch)


def gqa_softmax(q, k, v, sink=None):
    b, lq, h, hd = q.shape
    kvh = k.shape[2]
    g = h // kvh
    lk = k.shape[1]
    s = jnp.einsum('bqkgd,bjkd->bkgqj', q.reshape(b, lq, kvh, g, hd), k).astype(jnp.float32) * hd ** -0.5
    if sink is not None:
        s = jnp.concatenate([s, jnp.broadcast_to(sink.astype(jnp.float32).reshape(kvh, g, 1, 1), s.shape[:-1] + (1,))], -1)
    p = jax.nn.softmax(s, axis=-1)[..., :lk].astype(v.dtype)
    return jnp.einsum('bkgqj,bjkd->bqkgd', p, v).reshape(b, lq, h * hd)


def windowed_sink_attention(q, k, v, kc, vc, sink):
    b, s, h, hd = q.shape
    kvh = k.shape[2]
    g = h // kvh
    nb = s // BLOCK
    lc = kc.shape[1]
    w3 = 3 * BLOCK
    scale = hd ** -0.5
    qb = q.reshape(b, nb, BLOCK, kvh, g, hd).swapaxes(0, 1)

    def band(t):
        tb = jnp.pad(t.reshape(b, nb, BLOCK, kvh, hd), ((0, 0), (1, 1), (0, 0), (0, 0), (0, 0)))
        return jnp.concatenate([tb[:, :-2], tb[:, 1:-1], tb[:, 2:]], axis=2).swapaxes(0, 1)

    kw, vw = band(k), band(v)
    blk = jnp.arange(nb)[:, None, None]
    qpos = blk * BLOCK + jnp.arange(BLOCK)[None, :, None]
    kpos = (blk - 1) * BLOCK + jnp.arange(w3)[None, None, :]
    valid = (jnp.abs(qpos - kpos) <= WINDOW) & (kpos >= 0) & (kpos < s)
    sink_logit = sink.astype(jnp.float32).reshape(kvh, g, 1, 1)

    def one_block(args):
        qk, kk, vk, vm = args
        s_loc = jnp.einsum('bqkgd,bjkd->bkgqj', qk, kk).astype(jnp.float32) * scale
        s_loc = jnp.where(vm, s_loc, -jnp.inf)
        s_ctx = jnp.einsum('bqkgd,bjkd->bkgqj', qk, kc).astype(jnp.float32) * scale
        s_snk = jnp.broadcast_to(sink_logit, s_loc.shape[:-1] + (1,))
        p = jax.nn.softmax(jnp.concatenate([s_loc, s_ctx, s_snk], -1), axis=-1).astype(v.dtype)
        return (jnp.einsum('bkgqj,bjkd->bqkgd', p[..., :w3], vk)
                + jnp.einsum('bkgqj,bjkd->bqkgd', p[..., w3:w3 + lc], vc))

    o = lax.map(one_block, (qb, kw, vw, valid))
    return o.swapaxes(0, 1).reshape(b, s, h * hd)


def global_block_attention(q, k, v, kc, vc):
    b, s, h, hd = q.shape
    nb = s // BLOCK
    k_all = jnp.concatenate([k, kc], 1)
    v_all = jnp.concatenate([v, vc], 1)
    qb = q.reshape(b, nb, BLOCK, h, hd).swapaxes(0, 1)
    o = lax.map(lambda qk: gqa_softmax(qk, k_all, v_all), qb)
    return o.swapaxes(0, 1).reshape(b, s, h * hd)


def hyena_filters(n, w1, b1, freq, w2, b2, w3):
    t = jnp.arange(n, dtype=jnp.float32)
    tn = t / n
    f = jnp.arange(1, HY_BANDS + 1, dtype=jnp.float32)
    ang = 2.0 * math.pi * t[:, None] * f[None, :] / n
    feat = jnp.concatenate([tn[:, None], jnp.sin(ang), jnp.cos(ang)], -1)
    hid = jnp.sin(freq * (feat @ w1 + b1))
    hid = jnp.sin(freq * (hid @ w2 + b2))
    filt = (hid @ w3).astype(jnp.float32).reshape(n, HY_ORDER, 2, HY_CH)
    max_decay = math.log(HY_TARGET) / HY_FAST_DECAY
    min_decay = math.log(HY_TARGET) / HY_SLOW_DECAY
    deltas = jnp.abs(jnp.linspace(min_decay, max_decay, HY_CH, dtype=jnp.float32))
    filt = filt * jnp.exp(-tn[:, None, None, None] * deltas)
    return filt / jnp.sum(jnp.abs(filt), axis=(0, 2), keepdims=True)


def fft_long_conv(x, hf, hb, bias):
    n = x.shape[1]
    kern = jnp.concatenate([hf, jnp.zeros_like(hf[:1]), hb[1:][::-1]], 0)
    kf = jnp.fft.rfft(kern, n=2 * n, axis=0)
    xf = jnp.fft.rfft(x.astype(jnp.float32), n=2 * n, axis=1)
    y = jnp.fft.irfft(xf * kf[None], n=2 * n, axis=1)[:, :n]
    return (y + x.astype(jnp.float32) * bias.astype(jnp.float32)).astype(x.dtype)


def hyena_operator(u, conv_w, conv_b, fw1, fb1, ffreq, fw2, fb2, fw3, hy_bias):
    uc = dwconv(u, conv_w) + conv_b
    x1, x2, v = uc[..., :HY_CH], uc[..., HY_CH:2 * HY_CH], uc[..., 2 * HY_CH:]
    filt = hyena_filters(u.shape[1], fw1, fb1, ffreq, fw2, fb2, fw3)
    v = x1 * fft_long_conv(v, filt[:, 0, 0], filt[:, 0, 1], hy_bias[0])
    v = x2 * fft_long_conv(v, filt[:, 1, 0], filt[:, 1, 1], hy_bias[1])
    return v


def gated_delta_chunked(q, k, v, beta, g, s0):
    b, n_tok, h, dk = q.shape
    dv = v.shape[-1]
    cs = GDN_CHUNK
    n = n_tok // cs

    def blk(t):
        return t.astype(jnp.float32).reshape(b, n, cs, h, -1).transpose(1, 0, 3, 2, 4)

    q = blk(q) * dk ** -0.5
    k = blk(k)
    v = blk(v)
    beta = blk(beta[..., None])[..., 0]
    gcum = jnp.cumsum(blk(g[..., None])[..., 0], -1)
    idx = jnp.arange(cs)
    incl = idx[:, None] >= idx[None, :]
    strict = idx[:, None] > idx[None, :]
    decay = jnp.exp(jnp.where(incl, gcum[..., :, None] - gcum[..., None, :], -jnp.inf))
    kb = k * beta[..., None]
    a = jnp.where(strict, jnp.einsum('nbhid,nbhjd->nbhij', kb, k) * decay, 0.0)
    rhs = jnp.concatenate([v * beta[..., None], kb * jnp.exp(gcum)[..., None]], -1)
    sol = lax.linalg.triangular_solve(jnp.eye(cs, dtype=jnp.float32) + a, rhs, left_side=True, lower=True)
    u, w = sol[..., :dv], sol[..., dv:]
    intra = jnp.einsum('nbhid,nbhjd->nbhij', q, k) * decay
    q_dec = q * jnp.exp(gcum)[..., None]
    k_dec = k * jnp.exp(gcum[..., -1:] - gcum)[..., None]
    g_last = jnp.exp(gcum[..., -1])

    def step(state, xs):
        u_i, w_i, q_i, k_i, intra_i, gl = xs
        v_new = u_i - jnp.einsum('bhcd,bhde->bhce', w_i, state)
        o_i = jnp.einsum('bhcd,bhde->bhce', q_i, state) + jnp.einsum('bhij,bhje->bhie', intra_i, v_new)
        state = state * gl[..., None, None] + jnp.einsum('bhcd,bhce->bhde', k_i, v_new)
        return state, o_i

    s_final, o = lax.scan(step, s0.astype(jnp.float32), (u, w, q_dec, k_dec, intra, g_last))
    return o.transpose(1, 0, 3, 2, 4).reshape(b, n_tok, h, dv), s_final


def gdn_inputs(p, conv_w, a_log, dt_bias):
    b, n, _ = p.shape
    qkv = jax.nn.silu(dwconv(p[..., :3 * C_W], conv_w))
    q = l2_normalize(qkv[..., :C_W].reshape(b, n, C_HEADS, C_DK))
    k = l2_normalize(qkv[..., C_W:2 * C_W].reshape(b, n, C_HEADS, C_DK))
    v = qkv[..., 2 * C_W:].reshape(b, n, C_HEADS, C_DV)
    z = p[..., 3 * C_W:4 * C_W].reshape(b, n, C_HEADS, C_DV)
    gates = p[..., -C_GATES:].astype(jnp.float32).reshape(b, n, 4, C_HEADS)
    beta = jax.nn.sigmoid(gates[:, :, :2])
    g = -jnp.exp(a_log.astype(jnp.float32)) * jax.nn.softplus(gates[:, :, 2:] + dt_bias.astype(jnp.float32))
    return q, k, v, z, beta, g


def rev(t, flip):
    return t[:, ::-1] if flip else t


def bidirectional_gdn(lat, cx, with_ctx):
    ql, kl, vl, bl, gl = lat
    qc, kc, vc, bc, gcx = cx
    s0 = jnp.zeros((ql.shape[0], C_HEADS, C_DK, C_DV), jnp.float32)
    o_lat = 0.0
    o_ctx = 0.0
    for d in range(2):
        f = d == 1
        oc, sc = gated_delta_chunked(rev(qc, f), rev(kc, f), rev(vc, f), rev(bc[:, :, d], f), rev(gcx[:, :, d], f), s0)
        ol, _ = gated_delta_chunked(rev(ql, f), rev(kl, f), rev(vl, f), rev(bl[:, :, d], f), rev(gl[:, :, d], f), sc)
        o_lat = o_lat + rev(ol, f)
        if with_ctx:
            o_ctx = o_ctx + rev(oc, f)
    return o_lat, o_ctx


def even_mixer(p, pc, rope_tabs, sink, conv_w, conv_b, fw1, fb1, ffreq, fw2, fb2, fw3, hy_bias, with_ctx):
    q, k, v = qkv_prep(p, 0, A_HEADS, A_KV_HEADS, rope_tabs)
    qc, kc, vc = qkv_prep(pc, 0, A_HEADS, A_KV_HEADS, None)
    o_a = windowed_sink_gqa(q, k, v, kc, vc, sink)
    filt_args = (fw1, fb1, ffreq, fw2, fb2, fw3)
    o_b = hyena_mixer(p, A_Q + 2 * A_KV, conv_w, conv_b, filt_args, hy_bias)
    out_c = None
    if with_ctx:
        o_ac = flash_gqa(qc, kc, vc, sink)
        o_bc = hyena_mixer(pc, A_Q + 2 * A_KV, conv_w, conv_b, filt_args, hy_bias)
        out_c = (o_ac, o_bc)
    return (o_a, o_b), out_c


def odd_mixer(p, pc, rope_tabs, conv_w, a_log, dt_bias, gnorm_w, qnorm_w, knorm_w, with_ctx):
    o_l, o_c = gdn_mixer(p, pc, conv_w, a_log, dt_bias, gnorm_w, with_ctx)
    qd, kd, vd = qkv_prep(p, 4 * C_W, D_HEADS, D_KV_HEADS, rope_tabs, qnorm_w, knorm_w)
    qdc, kdc, vdc = qkv_prep(pc, 4 * C_W, D_HEADS, D_KV_HEADS, None, qnorm_w, knorm_w)
    o_d = flash_gqa(qd, jnp.concatenate([kd, kdc], 2), jnp.concatenate([vd, vdc], 2))
    out_c = None
    if with_ctx:
        out_c = (o_c, flash_gqa(qdc, kdc, vdc))
    return (o_l, o_d), out_c


def kernel(x, c, ctx, c_ctx, ada_w, ada_b, ln1_g, ln1_b, ln2_g, ln2_b, peer_wq, peer_k1, peer_k2, peer_u, peer_v, ev_w_in, ev_w_out, ev_sink, ev_conv_w, ev_conv_b, ev_filt_w1, ev_filt_b1, ev_filt_freq, ev_filt_w2, ev_filt_b2, ev_filt_w3, ev_hy_bias, od_w_in, od_w_out, od_conv_w, od_a_log, od_dt_bias, od_gnorm_w, od_qnorm_w, od_knorm_w):
    rope_tabs = rope_tables(x.shape[1])
    bsz = x.shape[0]
    silu_c = jax.nn.silu(c)
    silu_cc = jax.nn.silu(c_ctx)
    for i in range(DEPTH):
        with_ctx = i < DEPTH - 1
        j = i // 2
        mod = (silu_c @ ada_w[i] + ada_b[i])[:, None, :]
        modc = jnp.broadcast_to((silu_cc @ ada_w[i] + ada_b[i])[None, None, :], (bsz, 1, 6 * D_MODEL))
        sh1, sc1, g1, sh2, sc2, g2 = jnp.split(mod, 6, axis=-1)
        sh1c, sc1c, g1c, sh2c, sc2c, g2c = jnp.split(modc, 6, axis=-1)
        if i % 2 == 0:
            p = mod_matmul(x, sh1, sc1, ev_w_in[j])
            pc = mod_matmul(ctx, sh1c, sc1c, ev_w_in[j])
            out, out_c = even_mixer(p, pc, rope_tabs, ev_sink[j], ev_conv_w[j], ev_conv_b[j],
                                    ev_filt_w1[j], ev_filt_b1[j], ev_filt_freq[j], ev_filt_w2[j], ev_filt_b2[j],
                                    ev_filt_w3[j], ev_hy_bias[j], with_ctx)
            w_out = ev_w_out[j]
        else:
            w_in = od_w_in[j]
            w_in = jnp.concatenate([w_in[:, :4 * C_W], w_in[:, 4 * C_W + C_GATES:],
                                    w_in[:, 4 * C_W:4 * C_W + C_GATES]], axis=1)
            p = mod_matmul(x, sh1, sc1, w_in)
            pc = mod_matmul(ctx, sh1c, sc1c, w_in)
            out, out_c = odd_mixer(p, pc, rope_tabs, od_conv_w[j], od_a_log[j], od_dt_bias[j],
                                   od_gnorm_w[j], od_qnorm_w[j], od_knorm_w[j], with_ctx)
            w_out = od_w_out[j]
        u_bf = peer_u[i].astype(jnp.bfloat16)
        v_bf = peer_v[i].astype(jnp.bfloat16)
        x = proj_residual_ln(out[0], out[1], w_out, x, g1, ln1_g[i], ln1_b[i])
        x = peer_block(x, sh2, sc2, g2, peer_wq[i], peer_k1[i], peer_k2[i], u_bf, v_bf, ln2_g[i], ln2_b[i])
        if with_ctx:
            ctx = proj_residual_ln(out_c[0], out_c[1], w_out, ctx, g1c, ln1_g[i], ln1_b[i])
            ctx = peer_block(ctx, sh2c, sc2c, g2c, peer_wq[i], peer_k1[i], peer_k2[i], u_bf, v_bf,
                             ln2_g[i], ln2_b[i])
    return x
```

```python
import functools
import math

import numpy as np

import jax
import jax.numpy as jnp
from jax import lax
from jax.experimental import pallas as pl
from jax.experimental.pallas import tpu as pltpu

D_MODEL = 1024
DEPTH = 2
GRID_W = 64
HEAD_DIM = 64
BLOCK = 128
ROPE_BASE = 10000.0
EPS = 1e-6

A_HEADS = 8
A_KV_HEADS = 2
WINDOW = 128

HY_CH = 512
HY_ORDER = 2
HY_EMB = 33
HY_BANDS = (HY_EMB - 1) // 2
HY_FAST_DECAY = 0.3
HY_SLOW_DECAY = 1.5
HY_TARGET = 1e-2

C_HEADS = 4
C_DK = 128
C_DV = 128
GDN_CHUNK = 64

D_HEADS = 8
D_KV_HEADS = 2

PEER_HEADS = 8
PEER_NKEYS = 128
PEER_QDIM = 256
PEER_TOPK = 16
PEER_CHUNK = 128

ALPHA = (2 * DEPTH) ** 0.25

A_Q = A_HEADS * HEAD_DIM
A_KV = A_KV_HEADS * HEAD_DIM
C_W = C_HEADS * C_DK
C_GATES = 4 * C_HEADS
D_Q = D_HEADS * HEAD_DIM
D_KV = D_KV_HEADS * HEAD_DIM

VMEM_LIMIT_BYTES = 48 * 1024 * 1024

LANES = 128
_NT = (((1,), (1,)), ((), ()))


def _modmm_kernel(x_ref, sh_ref, sc_ref, w_ref, o_ref):
    h = x_ref[0] * (1.0 + sc_ref[0]) + sh_ref[0]
    o_ref[0] = jnp.dot(h.astype(jnp.bfloat16), w_ref[...], preferred_element_type=jnp.float32)


def mod_matmul(x, shift, scale, w, tm=512, tn=None):
    b, s, k = x.shape
    n = w.shape[1]
    tm = min(tm, s)
    tn = n if tn is None else tn
    wb = w.astype(jnp.bfloat16)
    return pl.pallas_call(
        _modmm_kernel,
        grid=(b, n // tn, s // tm),
        in_specs=[
            pl.BlockSpec((1, tm, k), lambda i, j, m: (i, m, 0)),
            pl.BlockSpec((1, 1, k), lambda i, j, m: (i, 0, 0)),
            pl.BlockSpec((1, 1, k), lambda i, j, m: (i, 0, 0)),
            pl.BlockSpec((k, tn), lambda i, j, m: (0, j)),
        ],
        out_specs=pl.BlockSpec((1, tm, tn), lambda i, j, m: (i, m, j)),
        out_shape=jax.ShapeDtypeStruct((b, s, n), jnp.float32),
        compiler_params=pltpu.CompilerParams(
            dimension_semantics=("arbitrary", "arbitrary", "arbitrary"),
            vmem_limit_bytes=VMEM_LIMIT_BYTES),
        name="mod_matmul",
    )(x, shift, scale, wb)


def _qkv_prep_kernel(q_ref, k_ref, v_ref, cs_ref, sn_ref, qw_ref, kw_ref, gm_ref, qo_ref, ko_ref, vo_ref, *,
                     norm, rope, nq, nkv):
    def prep(x, w, nh):
        if norm:
            ms = jnp.dot(x * x, gm_ref[:x.shape[1], :x.shape[1]], precision=lax.Precision.HIGHEST,
                         preferred_element_type=jnp.float32)
            x = x * lax.rsqrt(ms + EPS) * w
        if rope:
            n = x.shape[1]
            reps = n // cs_ref.shape[1]
            cs = jnp.concatenate([cs_ref[...]] * reps, axis=1) if reps > 1 else cs_ref[...]
            sn = jnp.concatenate([sn_ref[...]] * reps, axis=1) if reps > 1 else sn_ref[...]
            lane = lax.broadcasted_iota(jnp.int32, x.shape, 1)
            nf = HEAD_DIM // 4
            partner = jnp.where((lane & nf) == 0, pltpu.roll(x, n - nf, 1), pltpu.roll(x, nf, 1))
            x = x * cs + partner * sn
        return x

    q = prep(q_ref[0], qw_ref[...], nq) * (HEAD_DIM ** -0.5)
    k = prep(k_ref[0], kw_ref[...], nkv)
    v = v_ref[0]
    for h in range(nq):
        qo_ref[0, h] = q[:, h * HEAD_DIM:(h + 1) * HEAD_DIM].astype(jnp.bfloat16)
    for h in range(nkv):
        ko_ref[0, h] = k[:, h * HEAD_DIM:(h + 1) * HEAD_DIM].astype(jnp.bfloat16)
        vo_ref[0, h] = v[:, h * HEAD_DIM:(h + 1) * HEAD_DIM].astype(jnp.bfloat16)


def qkv_prep(p, col0, nq, nkv, rope_tabs, qw=None, kw=None, tq=512):
    b, s, _ = p.shape
    tq = min(tq, s)
    wq_, wk_ = nq * HEAD_DIM, nkv * HEAD_DIM
    norm = qw is not None
    rope = rope_tabs is not None
    if rope:
        cs, sn = rope_tabs
    else:
        cs = sn = jnp.zeros((s, 2 * HEAD_DIM), jnp.float32)
    qw_t = jnp.tile(qw, nq).reshape(1, wq_) if norm else jnp.ones((1, wq_), jnp.float32)
    kw_t = jnp.tile(kw, nkv).reshape(1, wk_) if norm else jnp.ones((1, wk_), jnp.float32)
    grp = jnp.arange(wq_) // HEAD_DIM
    gm = (grp[:, None] == grp[None, :]).astype(jnp.float32) / HEAD_DIM
    kern = functools.partial(_qkv_prep_kernel, norm=norm, rope=rope, nq=nq, nkv=nkv)
    return pl.pallas_call(
        kern,
        grid=(b, s // tq),
        in_specs=[
            pl.BlockSpec((1, tq, wq_), lambda i, m: (i, m, col0 // wq_)),
            pl.BlockSpec((1, tq, wk_), lambda i, m: (i, m, (col0 + wq_) // wk_)),
            pl.BlockSpec((1, tq, wk_), lambda i, m: (i, m, (col0 + wq_) // wk_ + 1)),
            pl.BlockSpec((tq, 2 * HEAD_DIM), lambda i, m: (m, 0)),
            pl.BlockSpec((tq, 2 * HEAD_DIM), lambda i, m: (m, 0)),
            pl.BlockSpec((1, wq_), lambda i, m: (0, 0)),
            pl.BlockSpec((1, wk_), lambda i, m: (0, 0)),
            pl.BlockSpec((wq_, wq_), lambda i, m: (0, 0)),
        ],
        out_specs=[
            pl.BlockSpec((1, nq, tq, HEAD_DIM), lambda i, m: (i, 0, m, 0)),
            pl.BlockSpec((1, nkv, tq, HEAD_DIM), lambda i, m: (i, 0, m, 0)),
            pl.BlockSpec((1, nkv, tq, HEAD_DIM), lambda i, m: (i, 0, m, 0)),
        ],
        out_shape=[
            jax.ShapeDtypeStruct((b, nq, s, HEAD_DIM), jnp.bfloat16),
            jax.ShapeDtypeStruct((b, nkv, s, HEAD_DIM), jnp.bfloat16),
            jax.ShapeDtypeStruct((b, nkv, s, HEAD_DIM), jnp.bfloat16),
        ],
        compiler_params=pltpu.CompilerParams(
            dimension_semantics=("arbitrary", "arbitrary"), vmem_limit_bytes=VMEM_LIMIT_BYTES),
        name="qkv_prep",
    )(p, p, p, cs, sn, qw_t, kw_t, gm)


def rope_tables(n_tok):
    rows = n_tok // GRID_W
    row = jnp.repeat(jnp.arange(rows, dtype=jnp.float32), GRID_W)
    col = jnp.tile(jnp.arange(GRID_W, dtype=jnp.float32), rows)
    nf = HEAD_DIM // 4
    inv = ROPE_BASE ** (-jnp.arange(nf, dtype=jnp.float32) / nf)
    ar, ac = row[:, None] * inv, col[:, None] * inv
    cs = jnp.concatenate([jnp.cos(ar), jnp.cos(ar), jnp.cos(ac), jnp.cos(ac)], -1)
    sn = jnp.concatenate([-jnp.sin(ar), jnp.sin(ar), -jnp.sin(ac), jnp.sin(ac)], -1)
    return jnp.tile(cs, (1, 2)), jnp.tile(sn, (1, 2))


def _flash_kernel(sink_ref, q_ref, k_ref, v_ref, o_ref, m_s, l_s, acc_s, *, use_sink, grp):
    j = pl.program_id(3)
    tq = q_ref.shape[2]

    @pl.when(j == 0)
    def _():
        m_s[...] = jnp.full_like(m_s, -jnp.inf)
        l_s[...] = jnp.zeros_like(l_s)
        acc_s[...] = jnp.zeros_like(acc_s)

    tk = k_ref.shape[2]
    nt = tk // LANES
    q = q_ref[0].reshape(grp * tq, HEAD_DIM)
    s = lax.dot_general(q, k_ref[0, 0], _NT, preferred_element_type=jnp.float32)
    tiles = [s[:, c * LANES:(c + 1) * LANES] for c in range(nt)]
    m_tile = functools.reduce(jnp.maximum, tiles)
    m_old = m_s[...]
    m_new = jnp.maximum(m_old, jnp.broadcast_to(jnp.max(m_tile, axis=1, keepdims=True), m_old.shape))
    alpha = jnp.exp(m_old - m_new)
    p_tiles = [jnp.exp(t - m_new) for t in tiles]
    l_s[...] = alpha * l_s[...] + functools.reduce(jnp.add, p_tiles)
    p = jnp.concatenate([t.astype(jnp.bfloat16) for t in p_tiles], axis=1)
    acc_s[...] = alpha[:, :HEAD_DIM] * acc_s[...] + jnp.dot(p, v_ref[0, 0], preferred_element_type=jnp.float32)
    m_s[...] = m_new

    @pl.when(j == pl.num_programs(3) - 1)
    def _():
        kvh = pl.program_id(1)
        outs = []
        for g in range(grp):
            rows = slice(g * tq, (g + 1) * tq)
            m = m_s[rows][:, :1]
            l = jnp.sum(l_s[rows], axis=1, keepdims=True)
            acc = acc_s[rows]
            if use_sink:
                sk = sink_ref[kvh * grp + g]
                m2 = jnp.maximum(m, sk)
                a = jnp.exp(m - m2)
                l = a * l + jnp.exp(sk - m2)
                acc = a * acc
            outs.append(acc / l)
        o_ref[0] = jnp.concatenate(outs, axis=1).astype(o_ref.dtype)


def flash_gqa(q, k, v, sink=None, tq=256, tk=768):
    b, h, s, hd = q.shape
    kvh, lk = k.shape[1], k.shape[2]
    grp = h // kvh
    tq = min(tq, s)
    tk = min(tk, lk)
    use_sink = sink is not None
    sink_arr = sink.astype(jnp.float32) if use_sink else jnp.zeros((h,), jnp.float32)
    kern = functools.partial(_flash_kernel, use_sink=use_sink, grp=grp)
    return pl.pallas_call(
        kern,
        grid=(b, kvh, s // tq, lk // tk),
        in_specs=[
            pl.BlockSpec(memory_space=pltpu.SMEM),
            pl.BlockSpec((1, grp, tq, hd), lambda i, c, m, j: (i, c, m, 0)),
            pl.BlockSpec((1, 1, tk, hd), lambda i, c, m, j: (i, c, j, 0)),
            pl.BlockSpec((1, 1, tk, hd), lambda i, c, m, j: (i, c, j, 0)),
        ],
        out_specs=pl.BlockSpec((1, tq, grp * hd), lambda i, c, m, j: (i, m, c)),
        out_shape=jax.ShapeDtypeStruct((b, s, h * hd), jnp.bfloat16),
        scratch_shapes=[pltpu.VMEM((grp * tq, LANES), jnp.float32), pltpu.VMEM((grp * tq, LANES), jnp.float32),
                        pltpu.VMEM((grp * tq, hd), jnp.float32)],
        compiler_params=pltpu.CompilerParams(
            dimension_semantics=("arbitrary",) * 4, vmem_limit_bytes=VMEM_LIMIT_BYTES),
        name="flash_gqa",
    )(sink_arr, q, k, v)


def _window_kernel(sink_ref, q_ref, kp_ref, kc_ref, kn_ref, vp_ref, vc_ref, vn_ref, kx_ref, vx_ref, o_ref, *, grp):
    kvh = pl.program_id(1)
    i = pl.program_id(2)
    nb = pl.num_programs(2)
    kcat = jnp.concatenate([kp_ref[0, 0], kc_ref[0, 0], kn_ref[0, 0], kx_ref[0, 0]], axis=0)
    vcat = jnp.concatenate([vp_ref[0, 0], vc_ref[0, 0], vn_ref[0, 0], vx_ref[0, 0]], axis=0)
    nk = kcat.shape[0]
    r = lax.broadcasted_iota(jnp.int32, (BLOCK, nk), 0)
    c = lax.broadcasted_iota(jnp.int32, (BLOCK, nk), 1)
    off_prev = jnp.where(i > 0, 0, 2 * nk)
    off_next = jnp.where(i < nb - 1, 0, 2 * nk)
    ok_prev = (c >= BLOCK) | (c >= r + off_prev)
    ok_next = (c < 2 * BLOCK) | (c >= 3 * BLOCK) | (c - 2 * BLOCK <= r - off_next)
    valid = ok_prev & ok_next
    outs = []
    for g in range(grp):
        s = lax.dot_general(q_ref[0, g], kcat, _NT, preferred_element_type=jnp.float32)
        s = jnp.where(valid, s, -jnp.inf)
        sk = sink_ref[kvh * grp + g]
        m = jnp.maximum(jnp.max(s, axis=1, keepdims=True), sk)
        p = jnp.exp(s - m)
        l = jnp.sum(p, axis=1, keepdims=True) + jnp.exp(sk - m)
        o = jnp.dot(p.astype(jnp.bfloat16), vcat, preferred_element_type=jnp.float32)
        outs.append(o / l)
    o_ref[0] = jnp.concatenate(outs, axis=1).astype(o_ref.dtype)


def windowed_sink_gqa(q, k, v, kx, vx, sink):
    b, h, s, hd = q.shape
    kvh = k.shape[1]
    lc = kx.shape[2]
    grp = h // kvh
    nb = s // BLOCK
    kern = functools.partial(_window_kernel, grp=grp)
    blk = lambda f: pl.BlockSpec((1, 1, BLOCK, hd), f)
    prev = lambda i, c, m: (i, c, jnp.maximum(m - 1, 0), 0)
    cur = lambda i, c, m: (i, c, m, 0)
    nxt = lambda i, c, m: (i, c, jnp.minimum(m + 1, nb - 1), 0)
    ctxm = lambda i, c, m: (i, c, 0, 0)
    return pl.pallas_call(
        kern,
        grid=(b, kvh, nb),
        in_specs=[
            pl.BlockSpec(memory_space=pltpu.SMEM),
            pl.BlockSpec((1, grp, BLOCK, hd), cur),
            blk(prev), blk(cur), blk(nxt), blk(prev), blk(cur), blk(nxt),
            pl.BlockSpec((1, 1, lc, hd), ctxm), pl.BlockSpec((1, 1, lc, hd), ctxm),
        ],
        out_specs=pl.BlockSpec((1, BLOCK, grp * hd), lambda i, c, m: (i, m, c)),
        out_shape=jax.ShapeDtypeStruct((b, s, h * hd), jnp.bfloat16),
        compiler_params=pltpu.CompilerParams(
            dimension_semantics=("arbitrary",) * 3, vmem_limit_bytes=VMEM_LIMIT_BYTES),
        name="windowed_sink_gqa",
    )(sink.astype(jnp.float32), q, k, k, k, v, v, v, kx, vx)


def _post_kernel(oa_ref, ob_ref, w_ref, x_ref, g_ref, lg_ref, lb_ref, y_ref):
    ka = oa_ref.shape[2]
    out = jnp.dot(oa_ref[0].astype(jnp.bfloat16), w_ref[:ka], preferred_element_type=jnp.float32)
    out += jnp.dot(ob_ref[0].astype(jnp.bfloat16), w_ref[ka:], preferred_element_type=jnp.float32)
    r = ALPHA * x_ref[0] + g_ref[0] * out
    mu = jnp.mean(r, -1, keepdims=True)
    d = r - mu
    var = jnp.mean(d * d, -1, keepdims=True)
    y_ref[0] = d * lax.rsqrt(var + EPS) * lg_ref[...] + lb_ref[...]


def proj_residual_ln(oa, ob, w, x, gate, ln_g, ln_b, tm=256):
    b, s, ka = oa.shape
    kb = ob.shape[2]
    k = ka + kb
    d = w.shape[1]
    tm = min(tm, s)
    wb = w.astype(jnp.bfloat16)
    return pl.pallas_call(
        _post_kernel,
        grid=(b, s // tm),
        in_specs=[
            pl.BlockSpec((1, tm, ka), lambda i, m: (i, m, 0)),
            pl.BlockSpec((1, tm, kb), lambda i, m: (i, m, 0)),
            pl.BlockSpec((k, d), lambda i, m: (0, 0)),
            pl.BlockSpec((1, tm, d), lambda i, m: (i, m, 0)),
            pl.BlockSpec((1, 1, d), lambda i, m: (i, 0, 0)),
            pl.BlockSpec((1, d), lambda i, m: (0, 0)),
            pl.BlockSpec((1, d), lambda i, m: (0, 0)),
        ],
        out_specs=pl.BlockSpec((1, tm, d), lambda i, m: (i, m, 0)),
        out_shape=jax.ShapeDtypeStruct((b, s, d), jnp.float32),
        compiler_params=pltpu.CompilerParams(
            dimension_semantics=("arbitrary", "arbitrary"),
            vmem_limit_bytes=VMEM_LIMIT_BYTES),
        name="proj_residual_ln",
    )(oa, ob, wb, x, gate, ln_g.reshape(1, d), ln_b.reshape(1, d))


def _top16(s, payload=None):
    n = s.shape[0]
    iota = lax.broadcasted_iota(jnp.int32, s.shape, 0)
    vals, ids = [], []
    for _ in range(PEER_TOPK):
        m = jnp.max(s, axis=0, keepdims=True)
        pos = jnp.min(jnp.where(s == m, iota, n), axis=0, keepdims=True)
        hit = iota == pos
        vals.append(m)
        ids.append(pos if payload is None else jnp.max(jnp.where(hit, payload, -1), axis=0, keepdims=True))
        s = jnp.where(hit, -jnp.inf, s)
    return jnp.concatenate(vals, 0), jnp.concatenate(ids, 0)


def _peer_topk_kernel(q_ref, k1_ref, k2_ref, eid_ref, gate_ref, eid_s, gate_s):
    half = PEER_QDIM // 2

    def head(h, carry):
        off = pl.multiple_of(h * PEER_QDIM, PEER_QDIM)
        q1 = q_ref[:, pl.ds(off, half)]
        q2 = q_ref[:, pl.ds(off + half, half)]
        s1 = lax.dot_general(k1_ref[h], q1, _NT, precision=lax.Precision.HIGHEST,
                             preferred_element_type=jnp.float32)
        s2 = lax.dot_general(k2_ref[h], q2, _NT, precision=lax.Precision.HIGHEST,
                             preferred_element_type=jnp.float32)
        v1, i1 = _top16(s1)
        v2, i2 = _top16(s2)
        k8 = PEER_TOPK // 2
        cand = jnp.concatenate([v1[0:1] + v2] + [v1[i:i + 1] + v2[:k8] for i in range(1, k8)]
                               + [v1[k8:] + v2[0:1]], 0)
        cid = jnp.concatenate([i1[0:1] * PEER_NKEYS + i2]
                              + [i1[i:i + 1] * PEER_NKEYS + i2[:k8] for i in range(1, k8)]
                              + [i1[k8:] * PEER_NKEYS + i2[0:1]], 0)
        best, eid = _top16(cand, cid)
        e = jnp.exp(best - best[0:1])
        gate = e / jnp.sum(e, axis=0, keepdims=True)
        row = pl.multiple_of(h * PEER_TOPK, PEER_TOPK)
        eid_s[pl.ds(row, PEER_TOPK), :] = eid
        gate_s[pl.ds(row, PEER_TOPK), :] = gate
        return carry

    lax.fori_loop(0, PEER_HEADS, head, 0)
    eid_ref[...] = eid_s[...].T
    gate_ref[...] = gate_s[...].T


def peer_topk(q, k1, k2, tt=512):
    t = q.shape[0]
    tt = min(tt, t)
    nsel = PEER_HEADS * PEER_TOPK
    return pl.pallas_call(
        _peer_topk_kernel,
        grid=(t // tt,),
        in_specs=[
            pl.BlockSpec((tt, q.shape[1]), lambda i: (i, 0)),
            pl.BlockSpec(k1.shape, lambda i: (0, 0, 0)),
            pl.BlockSpec(k2.shape, lambda i: (0, 0, 0)),
        ],
        out_specs=[pl.BlockSpec((tt, nsel), lambda i: (i, 0)),
                   pl.BlockSpec((tt, nsel), lambda i: (i, 0))],
        out_shape=[jax.ShapeDtypeStruct((t, nsel), jnp.int32),
                   jax.ShapeDtypeStruct((t, nsel), jnp.float32)],
        scratch_shapes=[pltpu.VMEM((nsel, tt), jnp.int32), pltpu.VMEM((nsel, tt), jnp.float32)],
        compiler_params=pltpu.CompilerParams(
            dimension_semantics=("arbitrary",), vmem_limit_bytes=VMEM_LIMIT_BYTES),
        name="peer_topk",
    )(q, k1, k2)


def _peer_w_kernel(e_ref, g_ref, w_ref):
    nk = PEER_NKEYS
    iota = lax.broadcasted_iota(jnp.int32, (nk, e_ref.shape[1]), 0)

    def tok(t, carry):
        e = e_ref[pl.ds(t, 1), :]
        g = g_ref[pl.ds(t, 1), :]
        a_t = jnp.where(iota == (e >> 7), g, 0.0).astype(jnp.bfloat16)
        b_t = jnp.where(iota == (e & (nk - 1)), 1.0, 0.0).astype(jnp.bfloat16)
        w = lax.dot_general(a_t, b_t, _NT, preferred_element_type=jnp.float32)
        w_ref[t] = w.astype(jnp.bfloat16)
        return carry

    lax.fori_loop(0, e_ref.shape[0], tok, 0, unroll=32)


def peer_dense_gates(eid, gate, tt=128):
    t, nsel = eid.shape
    tt = min(tt, t)
    nk = PEER_NKEYS
    w = pl.pallas_call(
        _peer_w_kernel,
        grid=(t // tt,),
        in_specs=[pl.BlockSpec((tt, nsel), lambda i: (i, 0)),
                  pl.BlockSpec((tt, nsel), lambda i: (i, 0))],
        out_specs=pl.BlockSpec((tt, nk, nk), lambda i: (i, 0, 0)),
        out_shape=jax.ShapeDtypeStruct((t, nk, nk), jnp.bfloat16),
        compiler_params=pltpu.CompilerParams(
            dimension_semantics=("arbitrary",), vmem_limit_bytes=VMEM_LIMIT_BYTES),
        name="peer_dense_gates",
    )(eid, gate)
    return w


def _peer_expert_kernel(x_ref, sh_ref, sc_ref, w_ref, u_ref, v_ref, g_ref, lg_ref, lb_ref, y_ref, xm_s, acc_s):
    e = pl.program_id(2)

    @pl.when(e == 0)
    def _():
        xm_s[...] = (x_ref[0] * (1.0 + sc_ref[0]) + sh_ref[0]).astype(jnp.bfloat16)
        acc_s[...] = jnp.zeros_like(acc_s)

    h = lax.dot_general(xm_s[...], u_ref[...], _NT, preferred_element_type=jnp.float32)
    gelu = 0.5 * h * (1.0 + lax.erf(h * (2.0 ** -0.5)))
    w = w_ref[0].reshape(h.shape)
    a = gelu * w.astype(jnp.float32)
    acc_s[...] += jnp.dot(a.astype(jnp.bfloat16), v_ref[...], preferred_element_type=jnp.float32)

    @pl.when(e == pl.num_programs(2) - 1)
    def _():
        r = ALPHA * x_ref[0] + g_ref[0] * acc_s[...]
        mu = jnp.mean(r, -1, keepdims=True)
        d = r - mu
        var = jnp.mean(d * d, -1, keepdims=True)
        y_ref[0] = d * lax.rsqrt(var + EPS) * lg_ref[...] + lb_ref[...]


def peer_experts_ln(x, shift, scale, w, u_tab, v_tab, gate, ln_g, ln_b, tt=512, te=2048):
    b, s, d = x.shape
    tt = min(tt, s)
    ne = u_tab.shape[0]
    nk = PEER_NKEYS
    w3 = w.reshape(b, s, nk, nk)
    return pl.pallas_call(
        _peer_expert_kernel,
        grid=(b, s // tt, ne // te),
        in_specs=[
            pl.BlockSpec((1, tt, d), lambda i, m, e: (i, m, 0)),
            pl.BlockSpec((1, 1, d), lambda i, m, e: (i, 0, 0)),
            pl.BlockSpec((1, 1, d), lambda i, m, e: (i, 0, 0)),
            pl.BlockSpec((1, tt, te // nk, nk), lambda i, m, e: (i, m, e, 0)),
            pl.BlockSpec((te, d), lambda i, m, e: (e, 0)),
            pl.BlockSpec((te, d), lambda i, m, e: (e, 0)),
            pl.BlockSpec((1, 1, d), lambda i, m, e: (i, 0, 0)),
            pl.BlockSpec((1, d), lambda i, m, e: (0, 0)),
            pl.BlockSpec((1, d), lambda i, m, e: (0, 0)),
        ],
        out_specs=pl.BlockSpec((1, tt, d), lambda i, m, e: (i, m, 0)),
        out_shape=jax.ShapeDtypeStruct((b, s, d), jnp.float32),
        scratch_shapes=[pltpu.VMEM((tt, d), jnp.bfloat16), pltpu.VMEM((tt, d), jnp.float32)],
        compiler_params=pltpu.CompilerParams(
            dimension_semantics=("arbitrary", "arbitrary", "arbitrary"),
            vmem_limit_bytes=VMEM_LIMIT_BYTES),
        name="peer_experts_ln",
    )(x, shift, scale, w3, u_tab, v_tab, gate, ln_g.reshape(1, d), ln_b.reshape(1, d))


def peer_block(x, shift, scale, gate, wq, k1, k2, u_bf, v_bf, ln_g, ln_b):
    b, s, d = x.shape
    q_all = mod_matmul(x, shift, scale, wq).reshape(b * s, -1)
    eid, gsel = peer_topk(q_all, k1, k2)
    w = peer_dense_gates(eid, gsel)
    return peer_experts_ln(x, shift, scale, w, u_bf, v_bf, gate, ln_g, ln_b)


HEAD_LANES = 128


def _short_conv_kernel(x_ref, xp_ref, xn_ref, w_ref, b_ref, o_ref, *, silu, n_l2, n_scaled):
    cb = pl.program_id(1)
    m = pl.program_id(2)
    x = x_ref[0]
    tq, wb = x.shape
    prev_row = jnp.where(m > 0, xp_ref[0][7:8], 0.0)
    next_row = jnp.where(m < pl.num_programs(2) - 1, xn_ref[0][0:1], 0.0)
    row = lax.broadcasted_iota(jnp.int32, x.shape, 0)
    x_m1 = jnp.where(row == 0, prev_row, pltpu.roll(x, 1, 0))
    x_p1 = jnp.where(row == tq - 1, next_row, pltpu.roll(x, tq - 1, 0))
    y = w_ref[0:1] * x_m1 + w_ref[1:2] * x + w_ref[2:3] * x_p1 + b_ref[...]
    if silu:
        y = y * jax.nn.sigmoid(y)
    if n_l2 == 0:
        o_ref[0] = y
        return
    hpb = wb // HEAD_LANES
    for hh in range(hpb):
        gh = cb * hpb + hh
        seg = y[:, hh * HEAD_LANES:(hh + 1) * HEAD_LANES]
        inv = lax.rsqrt(jnp.sum(seg * seg, axis=-1, keepdims=True) + EPS)
        f = jnp.where(gh < n_l2, inv, 1.0) * jnp.where(gh < n_scaled, C_DK ** -0.5, 1.0)
        o_ref[0, :, hh * HEAD_LANES:(hh + 1) * HEAD_LANES] = seg * f


def short_conv(p, col0, width, w, bias=None, silu=False, n_l2=0, n_scaled=0, wb=768, tq=512):
    b, l, _ = p.shape
    tq = min(tq, l)
    bias2 = (jnp.zeros((width,), jnp.float32) if bias is None else bias).reshape(1, width)
    c0 = col0 // wb
    kern = functools.partial(_short_conv_kernel, silu=silu, n_l2=n_l2, n_scaled=n_scaled)
    r8 = tq // 8
    return pl.pallas_call(
        kern,
        grid=(b, width // wb, l // tq),
        in_specs=[
            pl.BlockSpec((1, tq, wb), lambda i, c, m: (i, m, c0 + c)),
            pl.BlockSpec((1, 8, wb), lambda i, c, m: (i, jnp.maximum(m * r8 - 1, 0), c0 + c)),
            pl.BlockSpec((1, 8, wb), lambda i, c, m: (i, jnp.minimum((m + 1) * r8, l // 8 - 1), c0 + c)),
            pl.BlockSpec((3, wb), lambda i, c, m: (0, c)),
            pl.BlockSpec((1, wb), lambda i, c, m: (0, c)),
        ],
        out_specs=pl.BlockSpec((1, tq, wb), lambda i, c, m: (i, m, c)),
        out_shape=jax.ShapeDtypeStruct((b, l, width), jnp.float32),
        compiler_params=pltpu.CompilerParams(
            dimension_semantics=("arbitrary",) * 3, vmem_limit_bytes=VMEM_LIMIT_BYTES),
        name="short_conv",
    )(p, p, p, w, bias2)


def _dot3(a, b):
    ah = a.astype(jnp.bfloat16)
    bh = b.astype(jnp.bfloat16)
    al = (a - ah.astype(jnp.float32)).astype(jnp.bfloat16)
    bl = (b - bh.astype(jnp.float32)).astype(jnp.bfloat16)
    d = functools.partial(jnp.dot, preferred_element_type=jnp.float32)
    return d(ah, bh) + (d(ah, bl) + d(al, bh))


def _gdn_chunk_kernel(qkv_ref, beta_ref, g_ref, u_ref, w_ref, qd_ref, kd_ref, in_ref, gl_ref, *, nc):
    d = pl.program_id(0)
    cs = GDN_CHUNK
    ii = lax.broadcasted_iota(jnp.int32, (cs, cs), 0)
    jj = lax.broadcasted_iota(jnp.int32, (cs, cs), 1)
    lo = (ii - jj) * (1 - 2 * d)
    incl = lo >= 0
    strict = lo > 0
    tri = jnp.where(incl, 1.0, 0.0).astype(jnp.bfloat16)
    tri3 = jnp.concatenate([tri, tri, tri], axis=1)
    eye = jnp.where(ii == jj, 1.0, 0.0)

    def chunk_pair(cp, carry):
        probs = []
        for c in (2 * cp, 2 * cp + 1):
            rows = pl.ds(pl.multiple_of(c * cs, cs), cs)
            g_c = g_ref[0, 0, rows, :]
            b_c = beta_ref[0, 0, rows, :]
            g_hi = g_c.astype(jnp.bfloat16)
            r1 = g_c - g_hi.astype(jnp.float32)
            g_mid = r1.astype(jnp.bfloat16)
            g_lo = (r1 - g_mid.astype(jnp.float32)).astype(jnp.bfloat16)
            gc = jnp.dot(tri3, jnp.concatenate([g_hi, g_mid, g_lo], axis=0),
                         preferred_element_type=jnp.float32)
            tot = jnp.sum(g_c, axis=0, keepdims=True)
            for h in range(C_HEADS):
                probs.append(dict(c=c, h=h, rows=rows, gc=gc[:, h:h + 1], bt=b_c[:, h:h + 1], tot=tot[:, h:h + 1]))
        for pr in probs:
            h, rows = pr["h"], pr["rows"]
            q = qkv_ref[0, rows, h * HEAD_LANES:(h + 1) * HEAD_LANES]
            k = qkv_ref[0, rows, C_W + h * HEAD_LANES:C_W + (h + 1) * HEAD_LANES]
            kb = k * pr["bt"]
            kq = lax.dot_general(jnp.concatenate([kb, q], axis=0).astype(jnp.bfloat16), k.astype(jnp.bfloat16),
                                 _NT, preferred_element_type=jnp.float32)
            gc_row = jnp.broadcast_to(pr["gc"], (cs, HEAD_LANES)).T[:cs, :]
            dm = jnp.where(incl, jnp.exp(pr["gc"] - gc_row), 0.0)
            x = jnp.where(strict, -(kq[:cs] * dm), 0.0)
            in_ref[0, 0, pr["c"], h] = (kq[cs:] * dm).astype(in_ref.dtype)
            pr.update(t=eye + x, pw=x)
        for _ in range(5):
            for pr in probs:
                pr["pw"] = _dot3(pr["pw"], pr["pw"])
            for pr in probs:
                pr["t"] = pr["t"] + _dot3(pr["t"], pr["pw"])
        for pr in probs:
            h, rows = pr["h"], pr["rows"]
            lanes = slice(h * HEAD_LANES, (h + 1) * HEAD_LANES)
            q = qkv_ref[0, rows, h * HEAD_LANES:(h + 1) * HEAD_LANES]
            k = qkv_ref[0, rows, C_W + h * HEAD_LANES:C_W + (h + 1) * HEAD_LANES]
            v = qkv_ref[0, rows, 2 * C_W + h * HEAD_LANES:2 * C_W + (h + 1) * HEAD_LANES]
            eg = jnp.exp(pr["gc"])
            uw = _dot3(pr["t"], jnp.concatenate([v * pr["bt"], k * (pr["bt"] * eg)], axis=1))
            u_ref[0, 0, rows, lanes] = uw[:, :HEAD_LANES]
            w_ref[0, 0, rows, lanes] = uw[:, HEAD_LANES:].astype(w_ref.dtype)
            qd_ref[0, 0, rows, lanes] = (q * eg).astype(qd_ref.dtype)
            kd_ref[0, 0, rows, lanes] = (k * jnp.exp(pr["tot"] - pr["gc"])).astype(kd_ref.dtype)
            gl_ref[0, 0, pr["c"], h:h + 1, :] = jnp.broadcast_to(jnp.exp(pr["tot"]), (1, HEAD_LANES))
        return carry

    lax.fori_loop(0, nc // 2, chunk_pair, 0)


def gdn_chunk_prep(qkv, beta, g, nc=4):
    b, l, _ = qkv.shape
    cs = GDN_CHUNK
    tq = nc * cs
    nchunks = l // cs
    bf = jnp.bfloat16
    big = lambda dt: jax.ShapeDtypeStruct((2, b, l, C_W), dt)
    bspec = pl.BlockSpec((1, 1, tq, C_W), lambda d, i, m: (d, i, m, 0))
    gspec = pl.BlockSpec((1, 1, tq, C_HEADS), lambda d, i, m: (d, i, m, 0))
    return pl.pallas_call(
        functools.partial(_gdn_chunk_kernel, nc=nc),
        grid=(2, b, l // tq),
        in_specs=[pl.BlockSpec((1, tq, 3 * C_W), lambda d, i, m: (i, m, 0)), gspec, gspec],
        out_specs=[bspec, bspec, bspec, bspec,
                   pl.BlockSpec((1, 1, nc, C_HEADS, cs, cs), lambda d, i, m: (d, i, m, 0, 0, 0)),
                   pl.BlockSpec((1, 1, nc, C_HEADS, HEAD_LANES), lambda d, i, m: (d, i, m, 0, 0))],
        out_shape=[big(jnp.float32), big(bf), big(bf), big(bf),
                   jax.ShapeDtypeStruct((2, b, nchunks, C_HEADS, cs, cs), bf),
                   jax.ShapeDtypeStruct((2, b, nchunks, C_HEADS, HEAD_LANES), jnp.float32)],
        compiler_params=pltpu.CompilerParams(
            dimension_semantics=("arbitrary",) * 3, vmem_limit_bytes=VMEM_LIMIT_BYTES),
        name="gdn_chunk_prep",
    )(qkv, beta, g)


def _gdn_scan_kernel(*refs):
    ins, (of_ref, ob_ref, s_ref) = refs[:12], refs[12:]
    step = pl.program_id(1)

    @pl.when(step == 0)
    def _():
        s_ref[...] = jnp.zeros_like(s_ref)

    for d, o_ref in enumerate((of_ref, ob_ref)):
        u_ref, w_ref, qd_ref, kd_ref, in_ref, gl_ref = ins[6 * d:6 * d + 6]
        for h in range(C_HEADS):
            lanes = slice(h * HEAD_LANES, (h + 1) * HEAD_LANES)
            s = s_ref[d * C_HEADS + h]
            sb = s.astype(jnp.bfloat16)
            v_new = u_ref[0, 0, :, lanes] - jnp.dot(w_ref[0, 0, :, lanes], sb, preferred_element_type=jnp.float32)
            vb = v_new.astype(jnp.bfloat16)
            o_ref[0, :, lanes] = (jnp.dot(qd_ref[0, 0, :, lanes], sb, preferred_element_type=jnp.float32)
                                  + jnp.dot(in_ref[0, 0, 0, h], vb, preferred_element_type=jnp.float32))
            s_ref[d * C_HEADS + h] = s * gl_ref[0, 0, 0, h:h + 1, :] + lax.dot_general(
                kd_ref[0, 0, :, lanes], vb, (((0,), (0,)), ((), ())), preferred_element_type=jnp.float32)


def gdn_scan(u, w, qd, kd, intra, gl, n_ctx_chunks):
    _, b, l, _ = u.shape
    cs = GDN_CHUNK
    nchunks = l // cs

    def chunk_of(d, s):
        if d == 0:
            return s
        return jnp.where(s < n_ctx_chunks, n_ctx_chunks - 1 - s, nchunks - 1 + n_ctx_chunks - s)

    in_specs, args = [], []
    for d in range(2):
        big = pl.BlockSpec((1, 1, cs, C_W), lambda i, s, d=d: (d, i, chunk_of(d, s), 0))
        in_specs += [big, big, big, big,
                     pl.BlockSpec((1, 1, 1, C_HEADS, cs, cs), lambda i, s, d=d: (d, i, chunk_of(d, s), 0, 0, 0)),
                     pl.BlockSpec((1, 1, 1, C_HEADS, HEAD_LANES), lambda i, s, d=d: (d, i, chunk_of(d, s), 0, 0))]
        args += [u, w, qd, kd, intra, gl]
    out_specs = [pl.BlockSpec((1, cs, C_W), lambda i, s, d=d: (i, chunk_of(d, s), 0)) for d in range(2)]
    return pl.pallas_call(
        _gdn_scan_kernel,
        grid=(b, nchunks),
        in_specs=in_specs,
        out_specs=out_specs,
        out_shape=[jax.ShapeDtypeStruct((b, l, C_W), jnp.float32)] * 2,
        scratch_shapes=[pltpu.VMEM((2 * C_HEADS, C_DK, C_DV), jnp.float32)],
        compiler_params=pltpu.CompilerParams(
            dimension_semantics=("arbitrary",) * 2, vmem_limit_bytes=VMEM_LIMIT_BYTES),
        name="gdn_scan",
    )(*args)


def _gdn_gate_kernel(of_ref, ob_ref, z_ref, gw_ref, y_ref):
    o = of_ref[0] + ob_ref[0]
    z = z_ref[0]
    for h in range(C_HEADS):
        lanes = slice(h * HEAD_LANES, (h + 1) * HEAD_LANES)
        oh = o[:, lanes]
        zh = z[:, lanes]
        n = oh * lax.rsqrt(jnp.mean(oh * oh, axis=-1, keepdims=True) + EPS) * gw_ref[...]
        y_ref[0, :, lanes] = (n * (zh * jax.nn.sigmoid(zh))).astype(y_ref.dtype)


def gdn_gate(o_f, o_b, row0, p, gnorm_w, tq=256):
    b, l, _ = p.shape
    tq = min(tq, l)
    r0 = row0 // tq
    ospec = pl.BlockSpec((1, tq, C_W), lambda i, m: (i, r0 + m, 0))
    return pl.pallas_call(
        _gdn_gate_kernel,
        grid=(b, l // tq),
        in_specs=[ospec, ospec,
                  pl.BlockSpec((1, tq, C_W), lambda i, m: (i, m, 3)),
                  pl.BlockSpec((1, HEAD_LANES), lambda i, m: (0, 0))],
        out_specs=pl.BlockSpec((1, tq, C_W), lambda i, m: (i, m, 0)),
        out_shape=jax.ShapeDtypeStruct((b, l, C_W), jnp.bfloat16),
        compiler_params=pltpu.CompilerParams(
            dimension_semantics=("arbitrary",) * 2, vmem_limit_bytes=VMEM_LIMIT_BYTES),
        name="gdn_gate",
    )(o_f, o_b, p, gnorm_w.reshape(1, HEAD_LANES))


def gdn_mixer(p, pc, conv_w, a_log, dt_bias, gnorm_w, with_ctx):
    lc = pc.shape[1]
    conv = functools.partial(short_conv, col0=0, width=3 * C_W, w=conv_w, silu=True,
                             n_l2=2 * C_HEADS, n_scaled=C_HEADS)
    qkv = jnp.concatenate([conv(pc), conv(p)], axis=1)
    gates = jnp.concatenate([pc[..., -C_GATES:], p[..., -C_GATES:]], axis=1)
    gates = gates.reshape(gates.shape[0], gates.shape[1], 4, C_HEADS)
    beta = jax.nn.sigmoid(gates[:, :, :2])
    g = -jnp.exp(a_log) * jax.nn.softplus(gates[:, :, 2:] + dt_bias)
    beta = jnp.moveaxis(beta, 2, 0)
    g = jnp.moveaxis(g, 2, 0)
    u, w, qd, kd, intra, gl = gdn_chunk_prep(qkv, beta, g)
    o_f, o_b = gdn_scan(u, w, qd, kd, intra, gl, lc // GDN_CHUNK)
    out = gdn_gate(o_f, o_b, lc, p, gnorm_w)
    out_c = gdn_gate(o_f, o_b, 0, pc, gnorm_w) if with_ctx else None
    return out, out_c


FFT_R = 128
FFT_N = FFT_R * FFT_R
SUB = 8


def _dft_tables():
    idx = np.arange(FFT_R)
    ang = 2.0 * np.pi * np.outer(idx, idx) / FFT_R
    c, s = np.cos(ang), np.sin(ang)
    stage_a = np.stack([c, -s], axis=1).reshape(2 * FFT_R, FFT_R)
    stage_b = np.block([[c, s], [-s, c]])
    return (jnp.asarray(stage_a, jnp.float32), jnp.asarray(stage_b, jnp.float32))


def _fft_stage_a_kernel(x_ref, l_ref, y_ref):
    l = l_ref[...]
    c = x_ref.shape[-1]
    xs = jnp.swapaxes(x_ref[0], 0, 1)
    ys = jnp.stack([_dot3(l, xs[j]) for j in range(SUB)], axis=0)
    y_ref[0] = jnp.swapaxes(ys, 0, 1).reshape(FFT_R, 2, SUB, c)


def fft_stage_a(x, col_blk, width, stage_a):
    b, l, wtot = x.shape
    n1cnt = l // FFT_R
    x4 = x.reshape(b, n1cnt, FFT_R, wtot)
    return pl.pallas_call(
        _fft_stage_a_kernel,
        grid=(b, FFT_R // SUB),
        in_specs=[pl.BlockSpec((1, n1cnt, SUB, width), lambda i, j: (i, 0, j, col_blk)),
                  pl.BlockSpec((2 * FFT_R, n1cnt), lambda i, j: (0, 0))],
        out_specs=pl.BlockSpec((1, FFT_R, 2, SUB, width), lambda i, j: (i, 0, 0, j, 0)),
        out_shape=jax.ShapeDtypeStruct((b, FFT_R, 2, FFT_R, width), jnp.float32),
        compiler_params=pltpu.CompilerParams(
            dimension_semantics=("arbitrary",) * 2, vmem_limit_bytes=VMEM_LIMIT_BYTES),
        name="fft_stage_a",
    )(x4, stage_a[:, :n1cnt])


def _fft_mid_kernel(y_ref, h_ref, m_ref, mi_ref, o_ref, *, conv):
    k1 = pl.program_id(0)
    r, c = FFT_R, y_ref.shape[-1]
    n2 = lax.broadcasted_iota(jnp.int32, (r, 1), 0)
    ang = (n2 * k1).astype(jnp.float32) * (2.0 * math.pi / FFT_N)
    tc, ts = jnp.cos(ang), jnp.sin(ang)
    yr, yi = y_ref[0, 0, 0], y_ref[0, 0, 1]
    z = jnp.concatenate([yr * tc + yi * ts, yi * tc - yr * ts], axis=0)
    m = m_ref[...]
    x = _dot3(m, z)
    xr, xi = x[:r], x[r:]
    if not conv:
        o_ref[0, 0] = (x * ((1.0 / FFT_N) / h_ref[0])).reshape(2, r, c)
        return
    hr, hi = h_ref[0, 0], h_ref[0, 1]
    p = jnp.concatenate([xr * hr - xi * hi, xr * hi + xi * hr], axis=0)
    a = _dot3(mi_ref[...], p)
    ar, ai = a[:r], a[r:]
    o_ref[0, 0, 0] = ar * tc - ai * ts
    o_ref[0, 0, 1] = ai * tc + ar * ts


def fft_mid(y, h, stage_b, conv):
    b, r, _, _, c = y.shape
    hh = h if conv else h.reshape(b, 1, c)
    hspec = (pl.BlockSpec((1, 2, r, c), lambda k, i: (k, 0, 0, 0)) if conv
             else pl.BlockSpec((1, 1, c), lambda k, i: (i, 0, 0)))
    blk = pl.BlockSpec((1, 1, 2, r, c), lambda k, i: (i, k, 0, 0, 0))
    return pl.pallas_call(
        functools.partial(_fft_mid_kernel, conv=conv),
        grid=(r, b),
        in_specs=[blk, hspec, pl.BlockSpec((2 * r, 2 * r), lambda k, i: (0, 0)),
                  pl.BlockSpec((2 * r, 2 * r), lambda k, i: (0, 0))],
        out_specs=blk,
        out_shape=jax.ShapeDtypeStruct(y.shape, jnp.float32),
        compiler_params=pltpu.CompilerParams(
            dimension_semantics=("arbitrary",) * 2, vmem_limit_bytes=VMEM_LIMIT_BYTES),
        name="fft_mid",
    )(y, hh, stage_b, stage_b.T)


def _fft_out_kernel(b_ref, l_ref, xg_ref, xin_ref, bias_ref, o_ref):
    l = l_ref[...]
    c = o_ref.shape[-1]
    bs = jnp.swapaxes(b_ref[0].reshape(2 * FFT_R, SUB, c), 0, 1)
    ys = jnp.stack([_dot3(l, bs[j]) for j in range(SUB)], axis=0)
    y = jnp.swapaxes(ys, 0, 1)
    o_ref[0] = xg_ref[0] * (y + bias_ref[...] * xin_ref[0])


def fft_out_gate(bm, stage_a, xg, xg_blk, xin, xin_blk, bias):
    b, r, _, _, c = bm.shape
    l = xg.shape[1]
    n1cnt = l // r
    view = lambda t: t.reshape(b, n1cnt, r, t.shape[-1])
    lhs = stage_a.T[:n1cnt]
    return pl.pallas_call(
        _fft_out_kernel,
        grid=(b, r // SUB),
        in_specs=[pl.BlockSpec((1, r, 2, SUB, c), lambda i, j: (i, 0, 0, j, 0)),
                  pl.BlockSpec((n1cnt, 2 * r), lambda i, j: (0, 0)),
                  pl.BlockSpec((1, n1cnt, SUB, c), lambda i, j: (i, 0, j, xg_blk)),
                  pl.BlockSpec((1, n1cnt, SUB, c), lambda i, j: (i, 0, j, xin_blk)),
                  pl.BlockSpec((1, c), lambda i, j: (0, 0))],
        out_specs=pl.BlockSpec((1, n1cnt, SUB, c), lambda i, j: (i, 0, j, 0)),
        out_shape=jax.ShapeDtypeStruct((b, n1cnt, r, c), jnp.float32),
        compiler_params=pltpu.CompilerParams(
            dimension_semantics=("arbitrary",) * 2, vmem_limit_bytes=VMEM_LIMIT_BYTES),
        name="fft_out_gate",
    )(bm, lhs, view(xg), view(xin), bias.reshape(1, c)).reshape(b, l, c)


def _direct_conv_kernel(xin_ref, xg_ref, kern_ref, d1_ref, d2_ref, norm_ref, bias_ref, o_ref):
    n = xin_ref.shape[1]
    d1 = d1_ref[...]
    x = xin_ref[0]
    xs = _dot3(d1[:, :n], x)
    hs = _dot3(d1, kern_ref[...]) / norm_ref[...]
    xr, xi, hr, hi = xs[:2 * n], xs[2 * n:], hs[:2 * n], hs[2 * n:]
    p = jnp.concatenate([xr * hr - xi * hi, xr * hi + xi * hr], axis=0)
    y = _dot3(d2_ref[...], p)
    o_ref[0] = xg_ref[0] * (y + bias_ref[...] * x)


def direct_long_conv(xin, xin_blk, xg, xg_blk, kern, norm, bias):
    b, n, _ = xin.shape
    c = kern.shape[1]
    idx = np.arange(2 * n)
    ang = 2.0 * np.pi * np.outer(idx, idx) / (2 * n)
    d1 = jnp.asarray(np.concatenate([np.cos(ang), -np.sin(ang)], axis=0), jnp.float32)
    d2 = jnp.asarray(np.concatenate([np.cos(ang[:n]), -np.sin(ang[:n])], axis=1) / (2 * n), jnp.float32)
    return pl.pallas_call(
        _direct_conv_kernel,
        grid=(b,),
        in_specs=[pl.BlockSpec((1, n, c), lambda i: (i, 0, xin_blk)),
                  pl.BlockSpec((1, n, c), lambda i: (i, 0, xg_blk)),
                  pl.BlockSpec((2 * n, c), lambda i: (0, 0)),
                  pl.BlockSpec((4 * n, 2 * n), lambda i: (0, 0)),
                  pl.BlockSpec((n, 4 * n), lambda i: (0, 0)),
                  pl.BlockSpec((1, c), lambda i: (0, 0)),
                  pl.BlockSpec((1, c), lambda i: (0, 0))],
        out_specs=pl.BlockSpec((1, n, c), lambda i: (i, 0, 0)),
        out_shape=jax.ShapeDtypeStruct((b, n, c), jnp.float32),
        compiler_params=pltpu.CompilerParams(
            dimension_semantics=("arbitrary",), vmem_limit_bytes=VMEM_LIMIT_BYTES),
        name="direct_long_conv",
    )(xin, xg, kern, d1, d2, norm.reshape(1, c), bias.reshape(1, c))


def _hy_filter_kernel(w1_ref, b1_ref, fr_ref, w2_ref, b2_ref, w3_ref, dl_ref, k_ref, s_ref, *, n):
    i = pl.program_id(0)
    tp, c = k_ref.shape[1], k_ref.shape[2]

    @pl.when(i == 0)
    def _():
        s_ref[...] = jnp.zeros_like(s_ref)

    def pos(shape):
        idx = i * tp + lax.broadcasted_iota(jnp.int32, shape, 0)
        t = jnp.where(idx < n, idx, jnp.where(idx == n, 0, 2 * n - idx))
        return idx, t.astype(jnp.float32)

    _, t = pos((tp, LANES))
    lane = lax.broadcasted_iota(jnp.int32, (tp, LANES), 1)
    band = jnp.where(lane <= HY_BANDS, lane, lane - HY_BANDS).astype(jnp.float32)
    ang = 2.0 * math.pi * t * band / n
    feat = jnp.where(lane == 0, t / n,
                     jnp.where(lane <= HY_BANDS, jnp.sin(ang), jnp.where(lane < HY_EMB, jnp.cos(ang), 0.0)))
    hid = jnp.sin(fr_ref[...] * (_dot3(feat, w1_ref[...]) + b1_ref[...]))
    hid = jnp.sin(fr_ref[...] * (_dot3(hid, w2_ref[...]) + b2_ref[...]))
    f = _dot3(hid, w3_ref[...])
    idx, t = pos((tp, c))
    decay = jnp.exp(-(t / n) * dl_ref[...])
    for o in range(HY_ORDER):
        fwd = f[:, (2 * o) * c:(2 * o + 1) * c]
        bwd = f[:, (2 * o + 1) * c:(2 * o + 2) * c]
        val = jnp.where(idx < n, fwd, bwd) * decay
        s_ref[o:o + 1, :] += jnp.sum(jnp.abs(val), axis=0, keepdims=True)
        k_ref[o] = jnp.where(idx == n, 0.0, val)


def hyena_kernels(n, w1, b1, freq, w2, b2, w3):
    c = HY_CH
    tp = min(512, n)
    max_decay = math.log(HY_TARGET) / HY_FAST_DECAY
    min_decay = math.log(HY_TARGET) / HY_SLOW_DECAY
    deltas = jnp.abs(jnp.linspace(min_decay, max_decay, c, dtype=jnp.float32)).reshape(1, c)
    w1p = jnp.zeros((LANES, w1.shape[1]), jnp.float32).at[:w1.shape[0]].set(w1)
    hd = w1.shape[1]
    full = lambda shape: pl.BlockSpec(shape, lambda i: (0,) * len(shape))
    return pl.pallas_call(
        functools.partial(_hy_filter_kernel, n=n),
        grid=(2 * n // tp,),
        in_specs=[full((LANES, hd)), full((1, hd)), full((1, hd)), full((hd, hd)), full((1, hd)),
                  full((hd, HY_ORDER * 2 * c)), full((1, c))],
        out_specs=[pl.BlockSpec((HY_ORDER, tp, c), lambda i: (0, i, 0)), full((HY_ORDER, c))],
        out_shape=[jax.ShapeDtypeStruct((HY_ORDER, 2 * n, c), jnp.float32),
                   jax.ShapeDtypeStruct((HY_ORDER, c), jnp.float32)],
        compiler_params=pltpu.CompilerParams(
            dimension_semantics=("arbitrary",), vmem_limit_bytes=VMEM_LIMIT_BYTES),
        name="hyena_kernels",
    )(w1p, b1.reshape(1, hd), freq.reshape(1, hd), w2, b2.reshape(1, hd), w3, deltas)


def hyena_mixer(p, col0, conv_w, conv_b, filt_args, hy_bias):
    n = p.shape[1]
    uc = short_conv(p, col0, 3 * HY_CH, conv_w, conv_b)
    kerns, norm = hyena_kernels(n, *filt_args)
    if 2 * n != FFT_N:
        v = direct_long_conv(uc, 2, uc, 0, kerns[0], norm[0], hy_bias[0])
        return direct_long_conv(v, 0, uc, 1, kerns[1], norm[1], hy_bias[1])
    stage_a, stage_b = _dft_tables()
    spec = fft_mid(fft_stage_a(kerns, 0, HY_CH, stage_a), norm, stage_b, conv=False)
    v = fft_out_gate(fft_mid(fft_stage_a(uc, 2, HY_CH, stage_a), spec[0], stage_b, conv=True),
                     stage_a, uc, 0, uc, 2, hy_bias[0])
    return fft_out_gate(fft_mid(fft_stage_a(v, 0, HY_CH, stage_a), spec[1], stage_b, conv=True),
                        stage_a, uc, 1, v, 0, hy_bias[1])


def rms_norm(x, w):
    xf = x.astype(jnp.float32)
    return (xf * lax.rsqrt(jnp.mean(xf * xf, -1, keepdims=True) + EPS) * w).astype(x.dtype)


def l2_normalize(x):
    xf = x.astype(jnp.float32)
    return (xf * lax.rsqrt(jnp.sum(xf * xf, -1, keepdims=True) + EPS)).astype(x.dtype)


def axial_rope_tables(n_tok):
    rows = n_tok // GRID_W
    row = jnp.repeat(jnp.arange(rows, dtype=jnp.float32), GRID_W)
    col = jnp.tile(jnp.arange(GRID_W, dtype=jnp.float32), rows)
    nf = HEAD_DIM // 4
    inv = ROPE_BASE ** (-jnp.arange(nf, dtype=jnp.float32) / nf)
    ang = jnp.concatenate([row[:, None] * inv, col[:, None] * inv], -1)
    return jnp.cos(ang), jnp.sin(ang)


def apply_axial_rope(x, cos, sin):
    nf = HEAD_DIM // 4
    c = cos[:, None, :]
    s = sin[:, None, :]
    parts = []
    for a in range(2):
        xa = x[..., a * 2 * nf:(a + 1) * 2 * nf]
        x1, x2 = xa[..., :nf], xa[..., nf:]
        ca, sa = c[..., a * nf:(a + 1) * nf], s[..., a * nf:(a + 1) * nf]
        parts += [x1 * ca - x2 * sa, x2 * ca + x1 * sa]
    return jnp.concatenate(parts, -1).astype(x.dtype)


def dwconv(x, w):
    k, ch = w.shape
    return lax.conv_general_dilated(x, w[:, None, :].astype(x.dtype), (1,), [(k // 2, k // 2)],
                                    dimension_numbers=('NWC', 'WIO', 'NWC'), feature_group_count=ch)


def gqa_softmax(q, k, v, sink=None):
    b, lq, h, hd = q.shape
    kvh = k.shape[2]
    g = h // kvh
    lk = k.shape[1]
    s = jnp.einsum('bqkgd,bjkd->bkgqj', q.reshape(b, lq, kvh, g, hd), k).astype(jnp.float32) * hd ** -0.5
    if sink is not None:
        s = jnp.concatenate([s, jnp.broadcast_to(sink.astype(jnp.float32).reshape(kvh, g, 1, 1), s.shape[:-1] + (1,))], -1)
    p = jax.nn.softmax(s, axis=-1)[..., :lk].astype(v.dtype)
    return jnp.einsum('bkgqj,bjkd->bqkgd', p, v).reshape(b, lq, h * hd)


def windowed_sink_attention(q, k, v, kc, vc, sink):
    b, s, h, hd = q.shape
    kvh = k.shape[2]
    g = h // kvh
    nb = s // BLOCK
    lc = kc.shape[1]
    w3 = 3 * BLOCK
    scale = hd ** -0.5
    qb = q.reshape(b, nb, BLOCK, kvh, g, hd).swapaxes(0, 1)

    def band(t):
        tb = jnp.pad(t.reshape(b, nb, BLOCK, kvh, hd), ((0, 0), (1, 1), (0, 0), (0, 0), (0, 0)))
        return jnp.concatenate([tb[:, :-2], tb[:, 1:-1], tb[:, 2:]], axis=2).swapaxes(0, 1)

    kw, vw = band(k), band(v)
    blk = jnp.arange(nb)[:, None, None]
    qpos = blk * BLOCK + jnp.arange(BLOCK)[None, :, None]
    kpos = (blk - 1) * BLOCK + jnp.arange(w3)[None, None, :]
    valid = (jnp.abs(qpos - kpos) <= WINDOW) & (kpos >= 0) & (kpos < s)
    sink_logit = sink.astype(jnp.float32).reshape(kvh, g, 1, 1)

    def one_block(args):
        qk, kk, vk, vm = args
        s_loc = jnp.einsum('bqkgd,bjkd->bkgqj', qk, kk).astype(jnp.float32) * scale
        s_loc = jnp.where(vm, s_loc, -jnp.inf)
        s_ctx = jnp.einsum('bqkgd,bjkd->bkgqj', qk, kc).astype(jnp.float32) * scale
        s_snk = jnp.broadcast_to(sink_logit, s_loc.shape[:-1] + (1,))
        p = jax.nn.softmax(jnp.concatenate([s_loc, s_ctx, s_snk], -1), axis=-1).astype(v.dtype)
        return (jnp.einsum('bkgqj,bjkd->bqkgd', p[..., :w3], vk)
                + jnp.einsum('bkgqj,bjkd->bqkgd', p[..., w3:w3 + lc], vc))

    o = lax.map(one_block, (qb, kw, vw, valid))
    return o.swapaxes(0, 1).reshape(b, s, h * hd)


def global_block_attention(q, k, v, kc, vc):
    b, s, h, hd = q.shape
    nb = s // BLOCK
    k_all = jnp.concatenate([k, kc], 1)
    v_all = jnp.concatenate([v, vc], 1)
    qb = q.reshape(b, nb, BLOCK, h, hd).swapaxes(0, 1)
    o = lax.map(lambda qk: gqa_softmax(qk, k_all, v_all), qb)
    return o.swapaxes(0, 1).reshape(b, s, h * hd)


def hyena_filters(n, w1, b1, freq, w2, b2, w3):
    t = jnp.arange(n, dtype=jnp.float32)
    tn = t / n
    f = jnp.arange(1, HY_BANDS + 1, dtype=jnp.float32)
    ang = 2.0 * math.pi * t[:, None] * f[None, :] / n
    feat = jnp.concatenate([tn[:, None], jnp.sin(ang), jnp.cos(ang)], -1)
    hid = jnp.sin(freq * (feat @ w1 + b1))
    hid = jnp.sin(freq * (hid @ w2 + b2))
    filt = (hid @ w3).astype(jnp.float32).reshape(n, HY_ORDER, 2, HY_CH)
    max_decay = math.log(HY_TARGET) / HY_FAST_DECAY
    min_decay = math.log(HY_TARGET) / HY_SLOW_DECAY
    deltas = jnp.abs(jnp.linspace(min_decay, max_decay, HY_CH, dtype=jnp.float32))
    filt = filt * jnp.exp(-tn[:, None, None, None] * deltas)
    return filt / jnp.sum(jnp.abs(filt), axis=(0, 2), keepdims=True)


def fft_long_conv(x, hf, hb, bias):
    n = x.shape[1]
    kern = jnp.concatenate([hf, jnp.zeros_like(hf[:1]), hb[1:][::-1]], 0)
    kf = jnp.fft.rfft(kern, n=2 * n, axis=0)
    xf = jnp.fft.rfft(x.astype(jnp.float32), n=2 * n, axis=1)
    y = jnp.fft.irfft(xf * kf[None], n=2 * n, axis=1)[:, :n]
    return (y + x.astype(jnp.float32) * bias.astype(jnp.float32)).astype(x.dtype)


def hyena_operator(u, conv_w, conv_b, fw1, fb1, ffreq, fw2, fb2, fw3, hy_bias):
    uc = dwconv(u, conv_w) + conv_b
    x1, x2, v = uc[..., :HY_CH], uc[..., HY_CH:2 * HY_CH], uc[..., 2 * HY_CH:]
    filt = hyena_filters(u.shape[1], fw1, fb1, ffreq, fw2, fb2, fw3)
    v = x1 * fft_long_conv(v, filt[:, 0, 0], filt[:, 0, 1], hy_bias[0])
    v = x2 * fft_long_conv(v, filt[:, 1, 0], filt[:, 1, 1], hy_bias[1])
    return v


def gated_delta_chunked(q, k, v, beta, g, s0):
    b, n_tok, h, dk = q.shape
    dv = v.shape[-1]
    cs = GDN_CHUNK
    n = n_tok // cs

    def blk(t):
        return t.astype(jnp.float32).reshape(b, n, cs, h, -1).transpose(1, 0, 3, 2, 4)

    q = blk(q) * dk ** -0.5
    k = blk(k)
    v = blk(v)
    beta = blk(beta[..., None])[..., 0]
    gcum = jnp.cumsum(blk(g[..., None])[..., 0], -1)
    idx = jnp.arange(cs)
    incl = idx[:, None] >= idx[None, :]
    strict = idx[:, None] > idx[None, :]
    decay = jnp.exp(jnp.where(incl, gcum[..., :, None] - gcum[..., None, :], -jnp.inf))
    kb = k * beta[..., None]
    a = jnp.where(strict, jnp.einsum('nbhid,nbhjd->nbhij', kb, k) * decay, 0.0)
    rhs = jnp.concatenate([v * beta[..., None], kb * jnp.exp(gcum)[..., None]], -1)
    sol = lax.linalg.triangular_solve(jnp.eye(cs, dtype=jnp.float32) + a, rhs, left_side=True, lower=True)
    u, w = sol[..., :dv], sol[..., dv:]
    intra = jnp.einsum('nbhid,nbhjd->nbhij', q, k) * decay
    q_dec = q * jnp.exp(gcum)[..., None]
    k_dec = k * jnp.exp(gcum[..., -1:] - gcum)[..., None]
    g_last = jnp.exp(gcum[..., -1])

    def step(state, xs):
        u_i, w_i, q_i, k_i, intra_i, gl = xs
        v_new = u_i - jnp.einsum('bhcd,bhde->bhce', w_i, state)
        o_i = jnp.einsum('bhcd,bhde->bhce', q_i, state) + jnp.einsum('bhij,bhje->bhie', intra_i, v_new)
        state = state * gl[..., None, None] + jnp.einsum('bhcd,bhce->bhde', k_i, v_new)
        return state, o_i

    s_final, o = lax.scan(step, s0.astype(jnp.float32), (u, w, q_dec, k_dec, intra, g_last))
    return o.transpose(1, 0, 3, 2, 4).reshape(b, n_tok, h, dv), s_final


def gdn_inputs(p, conv_w, a_log, dt_bias):
    b, n, _ = p.shape
    qkv = jax.nn.silu(dwconv(p[..., :3 * C_W], conv_w))
    q = l2_normalize(qkv[..., :C_W].reshape(b, n, C_HEADS, C_DK))
    k = l2_normalize(qkv[..., C_W:2 * C_W].reshape(b, n, C_HEADS, C_DK))
    v = qkv[..., 2 * C_W:].reshape(b, n, C_HEADS, C_DV)
    z = p[..., 3 * C_W:4 * C_W].reshape(b, n, C_HEADS, C_DV)
    gates = p[..., -C_GATES:].astype(jnp.float32).reshape(b, n, 4, C_HEADS)
    beta = jax.nn.sigmoid(gates[:, :, :2])
    g = -jnp.exp(a_log.astype(jnp.float32)) * jax.nn.softplus(gates[:, :, 2:] + dt_bias.astype(jnp.float32))
    return q, k, v, z, beta, g


def rev(t, flip):
    return t[:, ::-1] if flip else t


def bidirectional_gdn(lat, cx, with_ctx):
    ql, kl, vl, bl, gl = lat
    qc, kc, vc, bc, gcx = cx
    s0 = jnp.zeros((ql.shape[0], C_HEADS, C_DK, C_DV), jnp.float32)
    o_lat = 0.0
    o_ctx = 0.0
    for d in range(2):
        f = d == 1
        oc, sc = gated_delta_chunked(rev(qc, f), rev(kc, f), rev(vc, f), rev(bc[:, :, d], f), rev(gcx[:, :, d], f), s0)
        ol, _ = gated_delta_chunked(rev(ql, f), rev(kl, f), rev(vl, f), rev(bl[:, :, d], f), rev(gl[:, :, d], f), sc)
        o_lat = o_lat + rev(ol, f)
        if with_ctx:
            o_ctx = o_ctx + rev(oc, f)
    return o_lat, o_ctx


def even_mixer(p, pc, rope_tabs, sink, conv_w, conv_b, fw1, fb1, ffreq, fw2, fb2, fw3, hy_bias, with_ctx):
    q, k, v = qkv_prep(p, 0, A_HEADS, A_KV_HEADS, rope_tabs)
    qc, kc, vc = qkv_prep(pc, 0, A_HEADS, A_KV_HEADS, None)
    o_a = windowed_sink_gqa(q, k, v, kc, vc, sink)
    filt_args = (fw1, fb1, ffreq, fw2, fb2, fw3)
    o_b = hyena_mixer(p, A_Q + 2 * A_KV, conv_w, conv_b, filt_args, hy_bias)
    out_c = None
    if with_ctx:
        o_ac = flash_gqa(qc, kc, vc, sink)
        o_bc = hyena_mixer(pc, A_Q + 2 * A_KV, conv_w, conv_b, filt_args, hy_bias)
        out_c = (o_ac, o_bc)
    return (o_a, o_b), out_c


def odd_mixer(p, pc, rope_tabs, conv_w, a_log, dt_bias, gnorm_w, qnorm_w, knorm_w, with_ctx):
    o_l, o_c = gdn_mixer(p, pc, conv_w, a_log, dt_bias, gnorm_w, with_ctx)
    qd, kd, vd = qkv_prep(p, 4 * C_W, D_HEADS, D_KV_HEADS, rope_tabs, qnorm_w, knorm_w)
    qdc, kdc, vdc = qkv_prep(pc, 4 * C_W, D_HEADS, D_KV_HEADS, None, qnorm_w, knorm_w)
    o_d = flash_gqa(qd, jnp.concatenate([kd, kdc], 2), jnp.concatenate([vd, vdc], 2))
    out_c = None
    if with_ctx:
        out_c = (o_c, flash_gqa(qdc, kdc, vdc))
    return (o_l, o_d), out_c


def kernel(x, c, ctx, c_ctx, ada_w, ada_b, ln1_g, ln1_b, ln2_g, ln2_b, peer_wq, peer_k1, peer_k2, peer_u, peer_v, ev_w_in, ev_w_out, ev_sink, ev_conv_w, ev_conv_b, ev_filt_w1, ev_filt_b1, ev_filt_freq, ev_filt_w2, ev_filt_b2, ev_filt_w3, ev_hy_bias, od_w_in, od_w_out, od_conv_w, od_a_log, od_dt_bias, od_gnorm_w, od_qnorm_w, od_knorm_w):
    rope_tabs = rope_tables(x.shape[1])
    bsz = x.shape[0]
    silu_c = jax.nn.silu(c)
    silu_cc = jax.nn.silu(c_ctx)
    for i in range(DEPTH):
        with_ctx = i < DEPTH - 1
        j = i // 2
        mod = (silu_c @ ada_w[i] + ada_b[i])[:, None, :]
        modc = jnp.broadcast_to((silu_cc @ ada_w[i] + ada_b[i])[None, None, :], (bsz, 1, 6 * D_MODEL))
        sh1, sc1, g1, sh2, sc2, g2 = jnp.split(mod, 6, axis=-1)
        sh1c, sc1c, g1c, sh2c, sc2c, g2c = jnp.split(modc, 6, axis=-1)
        if i % 2 == 0:
            p = mod_matmul(x, sh1, sc1, ev_w_in[j])
            pc = mod_matmul(ctx, sh1c, sc1c, ev_w_in[j])
            out, out_c = even_mixer(p, pc, rope_tabs, ev_sink[j], ev_conv_w[j], ev_conv_b[j],
                                    ev_filt_w1[j], ev_filt_b1[j], ev_filt_freq[j], ev_filt_w2[j], ev_filt_b2[j],
                                    ev_filt_w3[j], ev_hy_bias[j], with_ctx)
            w_out = ev_w_out[j]
        else:
            w_in = od_w_in[j]
            w_in = jnp.concatenate([w_in[:, :4 * C_W], w_in[:, 4 * C_W + C_GATES:],
                                    w_in[:, 4 * C_W:4 * C_W + C_GATES]], axis=1)
            p = mod_matmul(x, sh1, sc1, w_in)
            pc = mod_matmul(ctx, sh1c, sc1c, w_in)
            out, out_c = odd_mixer(p, pc, rope_tabs, od_conv_w[j], od_a_log[j], od_dt_bias[j],
                                   od_gnorm_w[j], od_qnorm_w[j], od_knorm_w[j], with_ctx)
            w_out = od_w_out[j]
        u_bf = peer_u[i].astype(jnp.bfloat16)
        v_bf = peer_v[i].astype(jnp.bfloat16)
        x = proj_residual_ln(out[0], out[1], w_out, x, g1, ln1_g[i], ln1_b[i])
        x = peer_block(x, sh2, sc2, g2, peer_wq[i], peer_k1[i], peer_k2[i], u_bf, v_bf, ln2_g[i], ln2_b[i])
        if with_ctx:
            ctx = proj_residual_ln(out_c[0], out_c[1], w_out, ctx, g1c, ln1_g[i], ln1_b[i])
            ctx = peer_block(ctx, sh2c, sc2c, g2c, peer_wq[i], peer_k1[i], peer_k2[i], u_bf, v_bf,
                             ln2_g[i], ln2_b[i])
    return x
```

```python
import functools
import math

import numpy as np

import jax
import jax.numpy as jnp
from jax import lax
from jax.experimental import pallas as pl
from jax.experimental.pallas import tpu as pltpu

D_MODEL = 1024
DEPTH = 2
GRID_W = 64
HEAD_DIM = 64
BLOCK = 128
ROPE_BASE = 10000.0
EPS = 1e-6

A_HEADS = 8
A_KV_HEADS = 2
WINDOW = 128

HY_CH = 512
HY_ORDER = 2
HY_EMB = 33
HY_BANDS = (HY_EMB - 1) // 2
HY_FAST_DECAY = 0.3
HY_SLOW_DECAY = 1.5
HY_TARGET = 1e-2

C_HEADS = 4
C_DK = 128
C_DV = 128
GDN_CHUNK = 64

D_HEADS = 8
D_KV_HEADS = 2

PEER_HEADS = 8
PEER_NKEYS = 128
PEER_QDIM = 256
PEER_TOPK = 16
PEER_CHUNK = 128

ALPHA = (2 * DEPTH) ** 0.25

A_Q = A_HEADS * HEAD_DIM
A_KV = A_KV_HEADS * HEAD_DIM
C_W = C_HEADS * C_DK
C_GATES = 4 * C_HEADS
D_Q = D_HEADS * HEAD_DIM
D_KV = D_KV_HEADS * HEAD_DIM

VMEM_LIMIT_BYTES = 48 * 1024 * 1024

LANES = 128
_NT = (((1,), (1,)), ((), ()))


def _modmm_kernel(x_ref, sh_ref, sc_ref, w_ref, o_ref):
    h = x_ref[0] * (1.0 + sc_ref[0]) + sh_ref[0]
    o_ref[0] = jnp.dot(h.astype(jnp.bfloat16), w_ref[...], preferred_element_type=jnp.float32)


def mod_matmul(x, shift, scale, w, tm=512, tn=None):
    b, s, k = x.shape
    n = w.shape[1]
    tm = min(tm, s)
    tn = n if tn is None else tn
    wb = w.astype(jnp.bfloat16)
    return pl.pallas_call(
        _modmm_kernel,
        grid=(b, n // tn, s // tm),
        in_specs=[
            pl.BlockSpec((1, tm, k), lambda i, j, m: (i, m, 0)),
            pl.BlockSpec((1, 1, k), lambda i, j, m: (i, 0, 0)),
            pl.BlockSpec((1, 1, k), lambda i, j, m: (i, 0, 0)),
            pl.BlockSpec((k, tn), lambda i, j, m: (0, j)),
        ],
        out_specs=pl.BlockSpec((1, tm, tn), lambda i, j, m: (i, m, j)),
        out_shape=jax.ShapeDtypeStruct((b, s, n), jnp.float32),
        compiler_params=pltpu.CompilerParams(
            dimension_semantics=("arbitrary", "arbitrary", "arbitrary"),
            vmem_limit_bytes=VMEM_LIMIT_BYTES),
        name="mod_matmul",
    )(x, shift, scale, wb)


def _qkv_prep_kernel(q_ref, k_ref, v_ref, cs_ref, sn_ref, qw_ref, kw_ref, gm_ref, qo_ref, ko_ref, vo_ref, *,
                     norm, rope, nq, nkv):
    def prep(x, w, nh):
        if norm:
            ms = jnp.dot(x * x, gm_ref[:x.shape[1], :x.shape[1]], precision=lax.Precision.HIGHEST,
                         preferred_element_type=jnp.float32)
            x = x * lax.rsqrt(ms + EPS) * w
        if rope:
            n = x.shape[1]
            reps = n // cs_ref.shape[1]
            cs = jnp.concatenate([cs_ref[...]] * reps, axis=1) if reps > 1 else cs_ref[...]
            sn = jnp.concatenate([sn_ref[...]] * reps, axis=1) if reps > 1 else sn_ref[...]
            lane = lax.broadcasted_iota(jnp.int32, x.shape, 1)
            nf = HEAD_DIM // 4
            partner = jnp.where((lane & nf) == 0, pltpu.roll(x, n - nf, 1), pltpu.roll(x, nf, 1))
            x = x * cs + partner * sn
        return x

    q = prep(q_ref[0], qw_ref[...], nq) * (HEAD_DIM ** -0.5)
    k = prep(k_ref[0], kw_ref[...], nkv)
    v = v_ref[0]
    for h in range(nq):
        qo_ref[0, h] = q[:, h * HEAD_DIM:(h + 1) * HEAD_DIM].astype(jnp.bfloat16)
    for h in range(nkv):
        ko_ref[0, h] = k[:, h * HEAD_DIM:(h + 1) * HEAD_DIM].astype(jnp.bfloat16)
        vo_ref[0, h] = v[:, h * HEAD_DIM:(h + 1) * HEAD_DIM].astype(jnp.bfloat16)


def qkv_prep(p, col0, nq, nkv, rope_tabs, qw=None, kw=None, tq=512):
    b, s, _ = p.shape
    tq = min(tq, s)
    wq_, wk_ = nq * HEAD_DIM, nkv * HEAD_DIM
    norm = qw is not None
    rope = rope_tabs is not None
    if rope:
        cs, sn = rope_tabs
    else:
        cs = sn = jnp.zeros((s, 2 * HEAD_DIM), jnp.float32)
    qw_t = jnp.tile(qw, nq).reshape(1, wq_) if norm else jnp.ones((1, wq_), jnp.float32)
    kw_t = jnp.tile(kw, nkv).reshape(1, wk_) if norm else jnp.ones((1, wk_), jnp.float32)
    grp = jnp.arange(wq_) // HEAD_DIM
    gm = (grp[:, None] == grp[None, :]).astype(jnp.float32) / HEAD_DIM
    kern = functools.partial(_qkv_prep_kernel, norm=norm, rope=rope, nq=nq, nkv=nkv)
    return pl.pallas_call(
        kern,
        grid=(b, s // tq),
        in_specs=[
            pl.BlockSpec((1, tq, wq_), lambda i, m: (i, m, col0 // wq_)),
            pl.BlockSpec((1, tq, wk_), lambda i, m: (i, m, (col0 + wq_) // wk_)),
            pl.BlockSpec((1, tq, wk_), lambda i, m: (i, m, (col0 + wq_) // wk_ + 1)),
            pl.BlockSpec((tq, 2 * HEAD_DIM), lambda i, m: (m, 0)),
            pl.BlockSpec((tq, 2 * HEAD_DIM), lambda i, m: (m, 0)),
            pl.BlockSpec((1, wq_), lambda i, m: (0, 0)),
            pl.BlockSpec((1, wk_), lambda i, m: (0, 0)),
            pl.BlockSpec((wq_, wq_), lambda i, m: (0, 0)),
        ],
        out_specs=[
            pl.BlockSpec((1, nq, tq, HEAD_DIM), lambda i, m: (i, 0, m, 0)),
            pl.BlockSpec((1, nkv, tq, HEAD_DIM), lambda i, m: (i, 0, m, 0)),
            pl.BlockSpec((1, nkv, tq, HEAD_DIM), lambda i, m: (i, 0, m, 0)),
        ],
        out_shape=[
            jax.ShapeDtypeStruct((b, nq, s, HEAD_DIM), jnp.bfloat16),
            jax.ShapeDtypeStruct((b, nkv, s, HEAD_DIM), jnp.bfloat16),
            jax.ShapeDtypeStruct((b, nkv, s, HEAD_DIM), jnp.bfloat16),
        ],
        compiler_params=pltpu.CompilerParams(
            dimension_semantics=("arbitrary", "arbitrary"), vmem_limit_bytes=VMEM_LIMIT_BYTES),
        name="qkv_prep",
    )(p, p, p, cs, sn, qw_t, kw_t, gm)


def rope_tables(n_tok):
    rows = n_tok // GRID_W
    row = jnp.repeat(jnp.arange(rows, dtype=jnp.float32), GRID_W)
    col = jnp.tile(jnp.arange(GRID_W, dtype=jnp.float32), rows)
    nf = HEAD_DIM // 4
    inv = ROPE_BASE ** (-jnp.arange(nf, dtype=jnp.float32) / nf)
    ar, ac = row[:, None] * inv, col[:, None] * inv
    cs = jnp.concatenate([jnp.cos(ar), jnp.cos(ar), jnp.cos(ac), jnp.cos(ac)], -1)
    sn = jnp.concatenate([-jnp.sin(ar), jnp.sin(ar), -jnp.sin(ac), jnp.sin(ac)], -1)
    return jnp.tile(cs, (1, 2)), jnp.tile(sn, (1, 2))


def _flash_kernel(sink_ref, q_ref, k_ref, v_ref, o_ref, m_s, l_s, acc_s, *, use_sink, grp):
    j = pl.program_id(3)
    tq = q_ref.shape[2]

    @pl.when(j == 0)
    def _():
        m_s[...] = jnp.full_like(m_s, -jnp.inf)
        l_s[...] = jnp.zeros_like(l_s)
        acc_s[...] = jnp.zeros_like(acc_s)

    tk = k_ref.shape[2]
    nt = tk // LANES
    kt = k_ref[0, 0]
    vt = v_ref[0, 0]
    scores = [lax.dot_general(q_ref[0, g], kt, _NT, preferred_element_type=jnp.float32) for g in range(grp)]
    for g, s in enumerate(scores):
        rows = slice(g * tq, (g + 1) * tq)
        tiles = [s[:, c * LANES:(c + 1) * LANES] for c in range(nt)]
        m_tile = functools.reduce(jnp.maximum, tiles)
        m_old = m_s[rows]
        m_new = jnp.maximum(m_old, jnp.broadcast_to(jnp.max(m_tile, axis=1, keepdims=True), m_old.shape))
        alpha = jnp.exp(m_old - m_new)
        p_tiles = [jnp.exp(t - m_new) for t in tiles]
        l_s[rows] = alpha * l_s[rows] + functools.reduce(jnp.add, p_tiles)
        p = jnp.concatenate([t.astype(jnp.bfloat16) for t in p_tiles], axis=1)
        acc_s[rows] = alpha[:, :HEAD_DIM] * acc_s[rows] + jnp.dot(p, vt, preferred_element_type=jnp.float32)
        m_s[rows] = m_new

    @pl.when(j == pl.num_programs(3) - 1)
    def _():
        kvh = pl.program_id(1)
        outs = []
        for g in range(grp):
            rows = slice(g * tq, (g + 1) * tq)
            m = m_s[rows][:, :1]
            l = jnp.sum(l_s[rows], axis=1, keepdims=True)
            acc = acc_s[rows]
            if use_sink:
                sk = sink_ref[kvh * grp + g]
                m2 = jnp.maximum(m, sk)
                a = jnp.exp(m - m2)
                l = a * l + jnp.exp(sk - m2)
                acc = a * acc
            outs.append(acc / l)
        o_ref[0] = jnp.concatenate(outs, axis=1).astype(o_ref.dtype)


def flash_gqa(q, k, v, sink=None, tq=256, tk=768):
    b, h, s, hd = q.shape
    kvh, lk = k.shape[1], k.shape[2]
    grp = h // kvh
    tq = min(tq, s)
    tk = min(tk, lk)
    use_sink = sink is not None
    sink_arr = sink.astype(jnp.float32) if use_sink else jnp.zeros((h,), jnp.float32)
    kern = functools.partial(_flash_kernel, use_sink=use_sink, grp=grp)
    return pl.pallas_call(
        kern,
        grid=(b, kvh, s // tq, lk // tk),
        in_specs=[
            pl.BlockSpec(memory_space=pltpu.SMEM),
            pl.BlockSpec((1, grp, tq, hd), lambda i, c, m, j: (i, c, m, 0)),
            pl.BlockSpec((1, 1, tk, hd), lambda i, c, m, j: (i, c, j, 0)),
            pl.BlockSpec((1, 1, tk, hd), lambda i, c, m, j: (i, c, j, 0)),
        ],
        out_specs=pl.BlockSpec((1, tq, grp * hd), lambda i, c, m, j: (i, m, c)),
        out_shape=jax.ShapeDtypeStruct((b, s, h * hd), jnp.bfloat16),
        scratch_shapes=[pltpu.VMEM((grp * tq, LANES), jnp.float32), pltpu.VMEM((grp * tq, LANES), jnp.float32),
                        pltpu.VMEM((grp * tq, hd), jnp.float32)],
        compiler_params=pltpu.CompilerParams(
            dimension_semantics=("arbitrary",) * 4, vmem_limit_bytes=VMEM_LIMIT_BYTES),
        name="flash_gqa",
    )(sink_arr, q, k, v)


def _window_kernel(sink_ref, q_ref, kp_ref, kc_ref, kn_ref, vp_ref, vc_ref, vn_ref, kx_ref, vx_ref, o_ref, *, grp):
    kvh = pl.program_id(1)
    i = pl.program_id(2)
    nb = pl.num_programs(2)
    kcat = jnp.concatenate([kp_ref[0, 0], kc_ref[0, 0], kn_ref[0, 0], kx_ref[0, 0]], axis=0)
    vcat = jnp.concatenate([vp_ref[0, 0], vc_ref[0, 0], vn_ref[0, 0], vx_ref[0, 0]], axis=0)
    nk = kcat.shape[0]
    r = lax.broadcasted_iota(jnp.int32, (BLOCK, nk), 0)
    c = lax.broadcasted_iota(jnp.int32, (BLOCK, nk), 1)
    off_prev = jnp.where(i > 0, 0, 2 * nk)
    off_next = jnp.where(i < nb - 1, 0, 2 * nk)
    ok_prev = (c >= BLOCK) | (c >= r + off_prev)
    ok_next = (c < 2 * BLOCK) | (c >= 3 * BLOCK) | (c - 2 * BLOCK <= r - off_next)
    valid = ok_prev & ok_next
    outs = []
    for g in range(grp):
        s = lax.dot_general(q_ref[0, g], kcat, _NT, preferred_element_type=jnp.float32)
        s = jnp.where(valid, s, -jnp.inf)
        sk = sink_ref[kvh * grp + g]
        m = jnp.maximum(jnp.max(s, axis=1, keepdims=True), sk)
        p = jnp.exp(s - m)
        l = jnp.sum(p, axis=1, keepdims=True) + jnp.exp(sk - m)
        o = jnp.dot(p.astype(jnp.bfloat16), vcat, preferred_element_type=jnp.float32)
        outs.append(o / l)
    o_ref[0] = jnp.concatenate(outs, axis=1).astype(o_ref.dtype)


def windowed_sink_gqa(q, k, v, kx, vx, sink):
    b, h, s, hd = q.shape
    kvh = k.shape[1]
    lc = kx.shape[2]
    grp = h // kvh
    nb = s // BLOCK
    kern = functools.partial(_window_kernel, grp=grp)
    blk = lambda f: pl.BlockSpec((1, 1, BLOCK, hd), f)
    prev = lambda i, c, m: (i, c, jnp.maximum(m - 1, 0), 0)
    cur = lambda i, c, m: (i, c, m, 0)
    nxt = lambda i, c, m: (i, c, jnp.minimum(m + 1, nb - 1), 0)
    ctxm = lambda i, c, m: (i, c, 0, 0)
    return pl.pallas_call(
        kern,
        grid=(b, kvh, nb),
        in_specs=[
            pl.BlockSpec(memory_space=pltpu.SMEM),
            pl.BlockSpec((1, grp, BLOCK, hd), cur),
            blk(prev), blk(cur), blk(nxt), blk(prev), blk(cur), blk(nxt),
            pl.BlockSpec((1, 1, lc, hd), ctxm), pl.BlockSpec((1, 1, lc, hd), ctxm),
        ],
        out_specs=pl.BlockSpec((1, BLOCK, grp * hd), lambda i, c, m: (i, m, c)),
        out_shape=jax.ShapeDtypeStruct((b, s, h * hd), jnp.bfloat16),
        compiler_params=pltpu.CompilerParams(
            dimension_semantics=("arbitrary",) * 3, vmem_limit_bytes=VMEM_LIMIT_BYTES),
        name="windowed_sink_gqa",
    )(sink.astype(jnp.float32), q, k, k, k, v, v, v, kx, vx)


def _post_kernel(oa_ref, ob_ref, w_ref, x_ref, g_ref, lg_ref, lb_ref, y_ref):
    ka = oa_ref.shape[2]
    out = jnp.dot(oa_ref[0].astype(jnp.bfloat16), w_ref[:ka], preferred_element_type=jnp.float32)
    out += jnp.dot(ob_ref[0].astype(jnp.bfloat16), w_ref[ka:], preferred_element_type=jnp.float32)
    r = ALPHA * x_ref[0] + g_ref[0] * out
    mu = jnp.mean(r, -1, keepdims=True)
    d = r - mu
    var = jnp.mean(d * d, -1, keepdims=True)
    y_ref[0] = d * lax.rsqrt(var + EPS) * lg_ref[...] + lb_ref[...]


def proj_residual_ln(oa, ob, w, x, gate, ln_g, ln_b, tm=256):
    b, s, ka = oa.shape
    kb = ob.shape[2]
    k = ka + kb
    d = w.shape[1]
    tm = min(tm, s)
    wb = w.astype(jnp.bfloat16)
    return pl.pallas_call(
        _post_kernel,
        grid=(b, s // tm),
        in_specs=[
            pl.BlockSpec((1, tm, ka), lambda i, m: (i, m, 0)),
            pl.BlockSpec((1, tm, kb), lambda i, m: (i, m, 0)),
            pl.BlockSpec((k, d), lambda i, m: (0, 0)),
            pl.BlockSpec((1, tm, d), lambda i, m: (i, m, 0)),
            pl.BlockSpec((1, 1, d), lambda i, m: (i, 0, 0)),
            pl.BlockSpec((1, d), lambda i, m: (0, 0)),
            pl.BlockSpec((1, d), lambda i, m: (0, 0)),
        ],
        out_specs=pl.BlockSpec((1, tm, d), lambda i, m: (i, m, 0)),
        out_shape=jax.ShapeDtypeStruct((b, s, d), jnp.float32),
        compiler_params=pltpu.CompilerParams(
            dimension_semantics=("arbitrary", "arbitrary"),
            vmem_limit_bytes=VMEM_LIMIT_BYTES),
        name="proj_residual_ln",
    )(oa, ob, wb, x, gate, ln_g.reshape(1, d), ln_b.reshape(1, d))


def _top16(s, payload=None):
    n = s.shape[0]
    iota = lax.broadcasted_iota(jnp.int32, s.shape, 0)
    vals, ids = [], []
    for _ in range(PEER_TOPK):
        m = jnp.max(s, axis=0, keepdims=True)
        pos = jnp.min(jnp.where(s == m, iota, n), axis=0, keepdims=True)
        hit = iota == pos
        vals.append(m)
        ids.append(pos if payload is None else jnp.max(jnp.where(hit, payload, -1), axis=0, keepdims=True))
        s = jnp.where(hit, -jnp.inf, s)
    return jnp.concatenate(vals, 0), jnp.concatenate(ids, 0)


def _peer_topk_kernel(q_ref, k1_ref, k2_ref, eid_ref, gate_ref, eid_s, gate_s):
    half = PEER_QDIM // 2

    def head(h, carry):
        off = pl.multiple_of(h * PEER_QDIM, PEER_QDIM)
        q1 = q_ref[:, pl.ds(off, half)]
        q2 = q_ref[:, pl.ds(off + half, half)]
        s1 = lax.dot_general(k1_ref[h], q1, _NT, precision=lax.Precision.HIGHEST,
                             preferred_element_type=jnp.float32)
        s2 = lax.dot_general(k2_ref[h], q2, _NT, precision=lax.Precision.HIGHEST,
                             preferred_element_type=jnp.float32)
        v1, i1 = _top16(s1)
        v2, i2 = _top16(s2)
        k8 = PEER_TOPK // 2
        cand = jnp.concatenate([v1[0:1] + v2] + [v1[i:i + 1] + v2[:k8] for i in range(1, k8)]
                               + [v1[k8:] + v2[0:1]], 0)
        cid = jnp.concatenate([i1[0:1] * PEER_NKEYS + i2]
                              + [i1[i:i + 1] * PEER_NKEYS + i2[:k8] for i in range(1, k8)]
                              + [i1[k8:] * PEER_NKEYS + i2[0:1]], 0)
        best, eid = _top16(cand, cid)
        e = jnp.exp(best - best[0:1])
        gate = e / jnp.sum(e, axis=0, keepdims=True)
        row = pl.multiple_of(h * PEER_TOPK, PEER_TOPK)
        eid_s[pl.ds(row, PEER_TOPK), :] = eid
        gate_s[pl.ds(row, PEER_TOPK), :] = gate
        return carry

    lax.fori_loop(0, PEER_HEADS, head, 0)
    eid_ref[...] = eid_s[...].T
    gate_ref[...] = gate_s[...].T


def peer_topk(q, k1, k2, tt=512):
    t = q.shape[0]
    tt = min(tt, t)
    nsel = PEER_HEADS * PEER_TOPK
    return pl.pallas_call(
        _peer_topk_kernel,
        grid=(t // tt,),
        in_specs=[
            pl.BlockSpec((tt, q.shape[1]), lambda i: (i, 0)),
            pl.BlockSpec(k1.shape, lambda i: (0, 0, 0)),
            pl.BlockSpec(k2.shape, lambda i: (0, 0, 0)),
        ],
        out_specs=[pl.BlockSpec((tt, nsel), lambda i: (i, 0)),
                   pl.BlockSpec((tt, nsel), lambda i: (i, 0))],
        out_shape=[jax.ShapeDtypeStruct((t, nsel), jnp.int32),
                   jax.ShapeDtypeStruct((t, nsel), jnp.float32)],
        scratch_shapes=[pltpu.VMEM((nsel, tt), jnp.int32), pltpu.VMEM((nsel, tt), jnp.float32)],
        compiler_params=pltpu.CompilerParams(
            dimension_semantics=("arbitrary",), vmem_limit_bytes=VMEM_LIMIT_BYTES),
        name="peer_topk",
    )(q, k1, k2)


def _peer_w_kernel(e_ref, g_ref, w_ref):
    nk = PEER_NKEYS
    iota = lax.broadcasted_iota(jnp.int32, (nk, e_ref.shape[1]), 0)

    def tok(t, carry):
        e = e_ref[pl.ds(t, 1), :]
        g = g_ref[pl.ds(t, 1), :]
        a_t = jnp.where(iota == (e >> 7), g, 0.0).astype(jnp.bfloat16)
        b_t = jnp.where(iota == (e & (nk - 1)), 1.0, 0.0).astype(jnp.bfloat16)
        w = lax.dot_general(a_t, b_t, _NT, preferred_element_type=jnp.float32)
        w_ref[t] = w.astype(jnp.bfloat16)
        return carry

    lax.fori_loop(0, e_ref.shape[0], tok, 0, unroll=32)


def peer_dense_gates(eid, gate, tt=128):
    t, nsel = eid.shape
    tt = min(tt, t)
    nk = PEER_NKEYS
    w = pl.pallas_call(
        _peer_w_kernel,
        grid=(t // tt,),
        in_specs=[pl.BlockSpec((tt, nsel), lambda i: (i, 0)),
                  pl.BlockSpec((tt, nsel), lambda i: (i, 0))],
        out_specs=pl.BlockSpec((tt, nk, nk), lambda i: (i, 0, 0)),
        out_shape=jax.ShapeDtypeStruct((t, nk, nk), jnp.bfloat16),
        compiler_params=pltpu.CompilerParams(
            dimension_semantics=("arbitrary",), vmem_limit_bytes=VMEM_LIMIT_BYTES),
        name="peer_dense_gates",
    )(eid, gate)
    return w


def _peer_expert_kernel(x_ref, sh_ref, sc_ref, w_ref, u_ref, v_ref, g_ref, lg_ref, lb_ref, y_ref, xm_s, acc_s):
    e = pl.program_id(2)

    @pl.when(e == 0)
    def _():
        xm_s[...] = (x_ref[0] * (1.0 + sc_ref[0]) + sh_ref[0]).astype(jnp.bfloat16)
        acc_s[...] = jnp.zeros_like(acc_s)

    h = lax.dot_general(xm_s[...], u_ref[...], _NT, preferred_element_type=jnp.float32)
    gelu = 0.5 * h * (1.0 + lax.erf(h * (2.0 ** -0.5)))
    w = w_ref[0].reshape(h.shape)
    a = gelu * w.astype(jnp.float32)
    acc_s[...] += jnp.dot(a.astype(jnp.bfloat16), v_ref[...], preferred_element_type=jnp.float32)

    @pl.when(e == pl.num_programs(2) - 1)
    def _():
        r = ALPHA * x_ref[0] + g_ref[0] * acc_s[...]
        mu = jnp.mean(r, -1, keepdims=True)
        d = r - mu
        var = jnp.mean(d * d, -1, keepdims=True)
        y_ref[0] = d * lax.rsqrt(var + EPS) * lg_ref[...] + lb_ref[...]


def peer_experts_ln(x, shift, scale, w, u_tab, v_tab, gate, ln_g, ln_b, tt=512, te=2048):
    b, s, d = x.shape
    tt = min(tt, s)
    ne = u_tab.shape[0]
    nk = PEER_NKEYS
    w3 = w.reshape(b, s, nk, nk)
    return pl.pallas_call(
        _peer_expert_kernel,
        grid=(b, s // tt, ne // te),
        in_specs=[
            pl.BlockSpec((1, tt, d), lambda i, m, e: (i, m, 0)),
            pl.BlockSpec((1, 1, d), lambda i, m, e: (i, 0, 0)),
            pl.BlockSpec((1, 1, d), lambda i, m, e: (i, 0, 0)),
            pl.BlockSpec((1, tt, te // nk, nk), lambda i, m, e: (i, m, e, 0)),
            pl.BlockSpec((te, d), lambda i, m, e: (e, 0)),
            pl.BlockSpec((te, d), lambda i, m, e: (e, 0)),
            pl.BlockSpec((1, 1, d), lambda i, m, e: (i, 0, 0)),
            pl.BlockSpec((1, d), lambda i, m, e: (0, 0)),
            pl.BlockSpec((1, d), lambda i, m, e: (0, 0)),
        ],
        out_specs=pl.BlockSpec((1, tt, d), lambda i, m, e: (i, m, 0)),
        out_shape=jax.ShapeDtypeStruct((b, s, d), jnp.float32),
        scratch_shapes=[pltpu.VMEM((tt, d), jnp.bfloat16), pltpu.VMEM((tt, d), jnp.float32)],
        compiler_params=pltpu.CompilerParams(
            dimension_semantics=("arbitrary", "arbitrary", "arbitrary"),
            vmem_limit_bytes=VMEM_LIMIT_BYTES),
        name="peer_experts_ln",
    )(x, shift, scale, w3, u_tab, v_tab, gate, ln_g.reshape(1, d), ln_b.reshape(1, d))


def peer_block(x, shift, scale, gate, wq, k1, k2, u_bf, v_bf, ln_g, ln_b):
    b, s, d = x.shape
    q_all = mod_matmul(x, shift, scale, wq).reshape(b * s, -1)
    eid, gsel = peer_topk(q_all, k1, k2)
    w = peer_dense_gates(eid, gsel)
    return peer_experts_ln(x, shift, scale, w, u_bf, v_bf, gate, ln_g, ln_b)


HEAD_LANES = 128


def _short_conv_kernel(x_ref, xp_ref, xn_ref, w_ref, b_ref, o_ref, *, silu, n_l2, n_scaled):
    cb = pl.program_id(1)
    m = pl.program_id(2)
    x = x_ref[0]
    tq, wb = x.shape
    prev_row = jnp.where(m > 0, xp_ref[0][7:8], 0.0)
    next_row = jnp.where(m < pl.num_programs(2) - 1, xn_ref[0][0:1], 0.0)
    row = lax.broadcasted_iota(jnp.int32, x.shape, 0)
    x_m1 = jnp.where(row == 0, prev_row, pltpu.roll(x, 1, 0))
    x_p1 = jnp.where(row == tq - 1, next_row, pltpu.roll(x, tq - 1, 0))
    y = w_ref[0:1] * x_m1 + w_ref[1:2] * x + w_ref[2:3] * x_p1 + b_ref[...]
    if silu:
        y = y * jax.nn.sigmoid(y)
    if n_l2 == 0:
        o_ref[0] = y
        return
    hpb = wb // HEAD_LANES
    for hh in range(hpb):
        gh = cb * hpb + hh
        seg = y[:, hh * HEAD_LANES:(hh + 1) * HEAD_LANES]
        inv = lax.rsqrt(jnp.sum(seg * seg, axis=-1, keepdims=True) + EPS)
        f = jnp.where(gh < n_l2, inv, 1.0) * jnp.where(gh < n_scaled, C_DK ** -0.5, 1.0)
        o_ref[0, :, hh * HEAD_LANES:(hh + 1) * HEAD_LANES] = seg * f


def short_conv(p, col0, width, w, bias=None, silu=False, n_l2=0, n_scaled=0, wb=768, tq=512):
    b, l, _ = p.shape
    tq = min(tq, l)
    bias2 = (jnp.zeros((width,), jnp.float32) if bias is None else bias).reshape(1, width)
    c0 = col0 // wb
    kern = functools.partial(_short_conv_kernel, silu=silu, n_l2=n_l2, n_scaled=n_scaled)
    r8 = tq // 8
    return pl.pallas_call(
        kern,
        grid=(b, width // wb, l // tq),
        in_specs=[
            pl.BlockSpec((1, tq, wb), lambda i, c, m: (i, m, c0 + c)),
            pl.BlockSpec((1, 8, wb), lambda i, c, m: (i, jnp.maximum(m * r8 - 1, 0), c0 + c)),
            pl.BlockSpec((1, 8, wb), lambda i, c, m: (i, jnp.minimum((m + 1) * r8, l // 8 - 1), c0 + c)),
            pl.BlockSpec((3, wb), lambda i, c, m: (0, c)),
            pl.BlockSpec((1, wb), lambda i, c, m: (0, c)),
        ],
        out_specs=pl.BlockSpec((1, tq, wb), lambda i, c, m: (i, m, c)),
        out_shape=jax.ShapeDtypeStruct((b, l, width), jnp.float32),
        compiler_params=pltpu.CompilerParams(
            dimension_semantics=("arbitrary",) * 3, vmem_limit_bytes=VMEM_LIMIT_BYTES),
        name="short_conv",
    )(p, p, p, w, bias2)


def _dot3(a, b):
    ah = a.astype(jnp.bfloat16)
    bh = b.astype(jnp.bfloat16)
    al = (a - ah.astype(jnp.float32)).astype(jnp.bfloat16)
    bl = (b - bh.astype(jnp.float32)).astype(jnp.bfloat16)
    d = functools.partial(jnp.dot, preferred_element_type=jnp.float32)
    return d(ah, bh) + (d(ah, bl) + d(al, bh))


def _gdn_chunk_kernel(qkv_ref, beta_ref, g_ref, u_ref, w_ref, qd_ref, kd_ref, in_ref, gl_ref, *, nc):
    d = pl.program_id(0)
    cs = GDN_CHUNK
    ii = lax.broadcasted_iota(jnp.int32, (cs, cs), 0)
    jj = lax.broadcasted_iota(jnp.int32, (cs, cs), 1)
    lo = (ii - jj) * (1 - 2 * d)
    incl = lo >= 0
    strict = lo > 0
    tri = jnp.where(incl, 1.0, 0.0).astype(jnp.bfloat16)
    tri3 = jnp.concatenate([tri, tri, tri], axis=1)
    eye = jnp.where(ii == jj, 1.0, 0.0)

    def chunk_pair(cp, carry):
        probs = []
        for c in (2 * cp, 2 * cp + 1):
            rows = pl.ds(pl.multiple_of(c * cs, cs), cs)
            g_c = g_ref[0, 0, rows, :]
            b_c = beta_ref[0, 0, rows, :]
            g_hi = g_c.astype(jnp.bfloat16)
            r1 = g_c - g_hi.astype(jnp.float32)
            g_mid = r1.astype(jnp.bfloat16)
            g_lo = (r1 - g_mid.astype(jnp.float32)).astype(jnp.bfloat16)
            gc = jnp.dot(tri3, jnp.concatenate([g_hi, g_mid, g_lo], axis=0),
                         preferred_element_type=jnp.float32)
            tot = jnp.sum(g_c, axis=0, keepdims=True)
            for h in range(C_HEADS):
                probs.append(dict(c=c, h=h, rows=rows, gc=gc[:, h:h + 1], bt=b_c[:, h:h + 1], tot=tot[:, h:h + 1]))
        for pr in probs:
            h, rows = pr["h"], pr["rows"]
            q = qkv_ref[0, rows, h * HEAD_LANES:(h + 1) * HEAD_LANES]
            k = qkv_ref[0, rows, C_W + h * HEAD_LANES:C_W + (h + 1) * HEAD_LANES]
            kb = k * pr["bt"]
            kq = lax.dot_general(jnp.concatenate([kb, q], axis=0).astype(jnp.bfloat16), k.astype(jnp.bfloat16),
                                 _NT, preferred_element_type=jnp.float32)
            gc_row = jnp.broadcast_to(pr["gc"], (cs, HEAD_LANES)).T[:cs, :]
            dm = jnp.where(incl, jnp.exp(pr["gc"] - gc_row), 0.0)
            x = jnp.where(strict, -(kq[:cs] * dm), 0.0)
            in_ref[0, 0, pr["c"], h] = (kq[cs:] * dm).astype(in_ref.dtype)
            pr.update(t=eye + x, pw=x)
        for _ in range(5):
            for pr in probs:
                pr["pw"] = _dot3(pr["pw"], pr["pw"])
            for pr in probs:
                pr["t"] = pr["t"] + _dot3(pr["t"], pr["pw"])
        for pr in probs:
            h, rows = pr["h"], pr["rows"]
            lanes = slice(h * HEAD_LANES, (h + 1) * HEAD_LANES)
            q = qkv_ref[0, rows, h * HEAD_LANES:(h + 1) * HEAD_LANES]
            k = qkv_ref[0, rows, C_W + h * HEAD_LANES:C_W + (h + 1) * HEAD_LANES]
            v = qkv_ref[0, rows, 2 * C_W + h * HEAD_LANES:2 * C_W + (h + 1) * HEAD_LANES]
            eg = jnp.exp(pr["gc"])
            uw = _dot3(pr["t"], jnp.concatenate([v * pr["bt"], k * (pr["bt"] * eg)], axis=1))
            u_ref[0, 0, rows, lanes] = uw[:, :HEAD_LANES]
            w_ref[0, 0, rows, lanes] = uw[:, HEAD_LANES:].astype(w_ref.dtype)
            qd_ref[0, 0, rows, lanes] = (q * eg).astype(qd_ref.dtype)
            kd_ref[0, 0, rows, lanes] = (k * jnp.exp(pr["tot"] - pr["gc"])).astype(kd_ref.dtype)
            gl_ref[0, 0, pr["c"], h:h + 1, :] = jnp.broadcast_to(jnp.exp(pr["tot"]), (1, HEAD_LANES))
        return carry

    lax.fori_loop(0, nc // 2, chunk_pair, 0)


def gdn_chunk_prep(qkv, beta, g, nc=4):
    b, l, _ = qkv.shape
    cs = GDN_CHUNK
    tq = nc * cs
    nchunks = l // cs
    bf = jnp.bfloat16
    big = lambda dt: jax.ShapeDtypeStruct((2, b, l, C_W), dt)
    bspec = pl.BlockSpec((1, 1, tq, C_W), lambda d, i, m: (d, i, m, 0))
    gspec = pl.BlockSpec((1, 1, tq, C_HEADS), lambda d, i, m: (d, i, m, 0))
    return pl.pallas_call(
        functools.partial(_gdn_chunk_kernel, nc=nc),
        grid=(2, b, l // tq),
        in_specs=[pl.BlockSpec((1, tq, 3 * C_W), lambda d, i, m: (i, m, 0)), gspec, gspec],
        out_specs=[bspec, bspec, bspec, bspec,
                   pl.BlockSpec((1, 1, nc, C_HEADS, cs, cs), lambda d, i, m: (d, i, m, 0, 0, 0)),
                   pl.BlockSpec((1, 1, nc, C_HEADS, HEAD_LANES), lambda d, i, m: (d, i, m, 0, 0))],
        out_shape=[big(jnp.float32), big(bf), big(bf), big(bf),
                   jax.ShapeDtypeStruct((2, b, nchunks, C_HEADS, cs, cs), bf),
                   jax.ShapeDtypeStruct((2, b, nchunks, C_HEADS, HEAD_LANES), jnp.float32)],
        compiler_params=pltpu.CompilerParams(
            dimension_semantics=("arbitrary",) * 3, vmem_limit_bytes=VMEM_LIMIT_BYTES),
        name="gdn_chunk_prep",
    )(qkv, beta, g)


def _gdn_scan_kernel(*refs):
    ins, (of_ref, ob_ref, s_ref) = refs[:12], refs[12:]
    step = pl.program_id(1)

    @pl.when(step == 0)
    def _():
        s_ref[...] = jnp.zeros_like(s_ref)

    for d, o_ref in enumerate((of_ref, ob_ref)):
        u_ref, w_ref, qd_ref, kd_ref, in_ref, gl_ref = ins[6 * d:6 * d + 6]
        for h in range(C_HEADS):
            lanes = slice(h * HEAD_LANES, (h + 1) * HEAD_LANES)
            s = s_ref[d * C_HEADS + h]
            sb = s.astype(jnp.bfloat16)
            v_new = u_ref[0, 0, :, lanes] - jnp.dot(w_ref[0, 0, :, lanes], sb, preferred_element_type=jnp.float32)
            vb = v_new.astype(jnp.bfloat16)
            o_ref[0, :, lanes] = (jnp.dot(qd_ref[0, 0, :, lanes], sb, preferred_element_type=jnp.float32)
                                  + jnp.dot(in_ref[0, 0, 0, h], vb, preferred_element_type=jnp.float32))
            s_ref[d * C_HEADS + h] = s * gl_ref[0, 0, 0, h:h + 1, :] + lax.dot_general(
                kd_ref[0, 0, :, lanes], vb, (((0,), (0,)), ((), ())), preferred_element_type=jnp.float32)


def gdn_scan(u, w, qd, kd, intra, gl, n_ctx_chunks):
    _, b, l, _ = u.shape
    cs = GDN_CHUNK
    nchunks = l // cs

    def chunk_of(d, s):
        if d == 0:
            return s
        return jnp.where(s < n_ctx_chunks, n_ctx_chunks - 1 - s, nchunks - 1 + n_ctx_chunks - s)

    in_specs, args = [], []
    for d in range(2):
        big = pl.BlockSpec((1, 1, cs, C_W), lambda i, s, d=d: (d, i, chunk_of(d, s), 0))
        in_specs += [big, big, big, big,
                     pl.BlockSpec((1, 1, 1, C_HEADS, cs, cs), lambda i, s, d=d: (d, i, chunk_of(d, s), 0, 0, 0)),
                     pl.BlockSpec((1, 1, 1, C_HEADS, HEAD_LANES), lambda i, s, d=d: (d, i, chunk_of(d, s), 0, 0))]
        args += [u, w, qd, kd, intra, gl]
    out_specs = [pl.BlockSpec((1, cs, C_W), lambda i, s, d=d: (i, chunk_of(d, s), 0)) for d in range(2)]
    return pl.pallas_call(
        _gdn_scan_kernel,
        grid=(b, nchunks),
        in_specs=in_specs,
        out_specs=out_specs,
        out_shape=[jax.ShapeDtypeStruct((b, l, C_W), jnp.float32)] * 2,
        scratch_shapes=[pltpu.VMEM((2 * C_HEADS, C_DK, C_DV), jnp.float32)],
        compiler_params=pltpu.CompilerParams(
            dimension_semantics=("arbitrary",) * 2, vmem_limit_bytes=VMEM_LIMIT_BYTES),
        name="gdn_scan",
    )(*args)


def _gdn_gate_kernel(of_ref, ob_ref, z_ref, gw_ref, y_ref):
    o = of_ref[0] + ob_ref[0]
    z = z_ref[0]
    for h in range(C_HEADS):
        lanes = slice(h * HEAD_LANES, (h + 1) * HEAD_LANES)
        oh = o[:, lanes]
        zh = z[:, lanes]
        n = oh * lax.rsqrt(jnp.mean(oh * oh, axis=-1, keepdims=True) + EPS) * gw_ref[...]
        y_ref[0, :, lanes] = (n * (zh * jax.nn.sigmoid(zh))).astype(y_ref.dtype)


def gdn_gate(o_f, o_b, row0, p, gnorm_w, tq=256):
    b, l, _ = p.shape
    tq = min(tq, l)
    r0 = row0 // tq
    ospec = pl.BlockSpec((1, tq, C_W), lambda i, m: (i, r0 + m, 0))
    return pl.pallas_call(
        _gdn_gate_kernel,
        grid=(b, l // tq),
        in_specs=[ospec, ospec,
                  pl.BlockSpec((1, tq, C_W), lambda i, m: (i, m, 3)),
                  pl.BlockSpec((1, HEAD_LANES), lambda i, m: (0, 0))],
        out_specs=pl.BlockSpec((1, tq, C_W), lambda i, m: (i, m, 0)),
        out_shape=jax.ShapeDtypeStruct((b, l, C_W), jnp.bfloat16),
        compiler_params=pltpu.CompilerParams(
            dimension_semantics=("arbitrary",) * 2, vmem_limit_bytes=VMEM_LIMIT_BYTES),
        name="gdn_gate",
    )(o_f, o_b, p, gnorm_w.reshape(1, HEAD_LANES))


def gdn_mixer(p, pc, conv_w, a_log, dt_bias, gnorm_w, with_ctx):
    lc = pc.shape[1]
    conv = functools.partial(short_conv, col0=0, width=3 * C_W, w=conv_w, silu=True,
                             n_l2=2 * C_HEADS, n_scaled=C_HEADS)
    qkv = jnp.concatenate([conv(pc), conv(p)], axis=1)
    gates = jnp.concatenate([pc[..., -C_GATES:], p[..., -C_GATES:]], axis=1)
    gates = gates.reshape(gates.shape[0], gates.shape[1], 4, C_HEADS)
    beta = jax.nn.sigmoid(gates[:, :, :2])
    g = -jnp.exp(a_log) * jax.nn.softplus(gates[:, :, 2:] + dt_bias)
    beta = jnp.moveaxis(beta, 2, 0)
    g = jnp.moveaxis(g, 2, 0)
    u, w, qd, kd, intra, gl = gdn_chunk_prep(qkv, beta, g)
    o_f, o_b = gdn_scan(u, w, qd, kd, intra, gl, lc // GDN_CHUNK)
    out = gdn_gate(o_f, o_b, lc, p, gnorm_w)
    out_c = gdn_gate(o_f, o_b, 0, pc, gnorm_w) if with_ctx else None
    return out, out_c


FFT_R = 128
FFT_N = FFT_R * FFT_R
SUB = 8


def stage_a_table():
    idx = np.arange(FFT_R)
    ang = 2.0 * np.pi * np.outer(idx, idx) / FFT_R
    return jnp.asarray(np.stack([np.cos(ang), -np.sin(ang)], axis=1).reshape(2 * FFT_R, FFT_R), jnp.float32)


def _fft_stage_a_kernel(x_ref, l_ref, y_ref):
    l = l_ref[...]
    c = x_ref.shape[-1]
    xs = jnp.swapaxes(x_ref[0], 0, 1)
    ys = jnp.stack([_dot3(l, xs[j]) for j in range(SUB)], axis=0)
    y_ref[0] = jnp.swapaxes(ys, 0, 1).reshape(FFT_R, 2, SUB, c)


def fft_stage_a(x, col_blk, width, stage_a):
    b, l, wtot = x.shape
    n1cnt = l // FFT_R
    x4 = x.reshape(b, n1cnt, FFT_R, wtot)
    return pl.pallas_call(
        _fft_stage_a_kernel,
        grid=(b, FFT_R // SUB),
        in_specs=[pl.BlockSpec((1, n1cnt, SUB, width), lambda i, j: (i, 0, j, col_blk)),
                  pl.BlockSpec((2 * FFT_R, n1cnt), lambda i, j: (0, 0))],
        out_specs=pl.BlockSpec((1, FFT_R, 2, SUB, width), lambda i, j: (i, 0, 0, j, 0)),
        out_shape=jax.ShapeDtypeStruct((b, FFT_R, 2, FFT_R, width), jnp.float32),
        compiler_params=pltpu.CompilerParams(
            dimension_semantics=("arbitrary",) * 2, vmem_limit_bytes=VMEM_LIMIT_BYTES),
        name="fft_stage_a",
    )(x4, stage_a[:, :n1cnt])


def _dot3_presplit(ah, al, b):
    bh = b.astype(jnp.bfloat16)
    bl = (b - bh.astype(jnp.float32)).astype(jnp.bfloat16)
    d = functools.partial(jnp.dot, preferred_element_type=jnp.float32)
    return d(ah, bh) + (d(ah, bl) + d(al, bh))


def stage_b_tables():
    r = FFT_R
    k1 = jnp.arange(r, dtype=jnp.int32)[:, None, None]
    k2 = jnp.arange(r, dtype=jnp.int32)[None, :, None]
    n2 = jnp.arange(r, dtype=jnp.int32)[None, None, :]
    th = ((n2 * (r * k2 + k1)) % FFT_N).astype(jnp.float32) * (2.0 * math.pi / FFT_N)
    c, s = jnp.cos(th), jnp.sin(th)
    t = jnp.concatenate([jnp.concatenate([c, s], 2), jnp.concatenate([-s, c], 2)], 1)

    def split(m):
        hi = m.astype(jnp.bfloat16)
        return hi, (m - hi.astype(jnp.float32)).astype(jnp.bfloat16)

    return split(t) + split(jnp.swapaxes(t, 1, 2))


def _fft_mid_kernel(y_ref, h_ref, th_ref, tl_ref, ih_ref, il_ref, o_ref, *, conv):
    r, c = FFT_R, y_ref.shape[-1]
    x = _dot3_presplit(th_ref[0], tl_ref[0], y_ref[0, 0].reshape(2 * r, c))
    if not conv:
        o_ref[0, 0] = (x * ((1.0 / FFT_N) / h_ref[0])).reshape(2, r, c)
        return
    xr, xi = x[:r], x[r:]
    hr, hi = h_ref[0, 0, 0], h_ref[0, 0, 1]
    p = jnp.concatenate([xr * hr - xi * hi, xr * hi + xi * hr], axis=0)
    o_ref[0, 0] = _dot3_presplit(ih_ref[0], il_ref[0], p).reshape(2, r, c)


def fft_mid(y, h, tables, conv, order=0):
    b, r, _, _, c = y.shape
    hh = h if conv else h.reshape(b, 1, c)
    hspec = (pl.BlockSpec((1, 1, 2, r, c), lambda k, i: (order, k, 0, 0, 0)) if conv
             else pl.BlockSpec((1, 1, c), lambda k, i: (i, 0, 0)))
    blk = pl.BlockSpec((1, 1, 2, r, c), lambda k, i: (i, k, 0, 0, 0))
    tspec = pl.BlockSpec((1, 2 * r, 2 * r), lambda k, i: (k, 0, 0))
    return pl.pallas_call(
        functools.partial(_fft_mid_kernel, conv=conv),
        grid=(r, b),
        in_specs=[blk, hspec, tspec, tspec, tspec, tspec],
        out_specs=blk,
        out_shape=jax.ShapeDtypeStruct(y.shape, jnp.float32),
        compiler_params=pltpu.CompilerParams(
            dimension_semantics=("arbitrary",) * 2, vmem_limit_bytes=VMEM_LIMIT_BYTES),
        name="fft_mid",
    )(y, hh, *tables)


def _fft_out_kernel(b_ref, l_ref, xg_ref, xin_ref, bias_ref, o_ref):
    l = l_ref[...]
    c = o_ref.shape[-1]
    bs = jnp.swapaxes(b_ref[0].reshape(2 * FFT_R, SUB, c), 0, 1)
    ys = jnp.stack([_dot3(l, bs[j]) for j in range(SUB)], axis=0)
    y = jnp.swapaxes(ys, 0, 1)
    o_ref[0] = xg_ref[0] * (y + bias_ref[...] * xin_ref[0])


def fft_out_gate(bm, stage_a, xg, xg_blk, xin, xin_blk, bias):
    b, r, _, _, c = bm.shape
    l = xg.shape[1]
    n1cnt = l // r
    view = lambda t: t.reshape(b, n1cnt, r, t.shape[-1])
    lhs = stage_a.T[:n1cnt]
    return pl.pallas_call(
        _fft_out_kernel,
        grid=(b, r // SUB),
        in_specs=[pl.BlockSpec((1, r, 2, SUB, c), lambda i, j: (i, 0, 0, j, 0)),
                  pl.BlockSpec((n1cnt, 2 * r), lambda i, j: (0, 0)),
                  pl.BlockSpec((1, n1cnt, SUB, c), lambda i, j: (i, 0, j, xg_blk)),
                  pl.BlockSpec((1, n1cnt, SUB, c), lambda i, j: (i, 0, j, xin_blk)),
                  pl.BlockSpec((1, c), lambda i, j: (0, 0))],
        out_specs=pl.BlockSpec((1, n1cnt, SUB, c), lambda i, j: (i, 0, j, 0)),
        out_shape=jax.ShapeDtypeStruct((b, n1cnt, r, c), jnp.float32),
        compiler_params=pltpu.CompilerParams(
            dimension_semantics=("arbitrary",) * 2, vmem_limit_bytes=VMEM_LIMIT_BYTES),
        name="fft_out_gate",
    )(bm, lhs, view(xg), view(xin), bias.reshape(1, c)).reshape(b, l, c)


def _direct_conv_kernel(xin_ref, xg_ref, kern_ref, d1_ref, d2_ref, norm_ref, bias_ref, o_ref):
    n = xin_ref.shape[1]
    d1 = d1_ref[...]
    x = xin_ref[0]
    xs = _dot3(d1[:, :n], x)
    hs = _dot3(d1, kern_ref[...]) / norm_ref[...]
    xr, xi, hr, hi = xs[:2 * n], xs[2 * n:], hs[:2 * n], hs[2 * n:]
    p = jnp.concatenate([xr * hr - xi * hi, xr * hi + xi * hr], axis=0)
    y = _dot3(d2_ref[...], p)
    o_ref[0] = xg_ref[0] * (y + bias_ref[...] * x)


def direct_long_conv(xin, xin_blk, xg, xg_blk, kern, norm, bias):
    b, n, _ = xin.shape
    c = kern.shape[1]
    idx = np.arange(2 * n)
    ang = 2.0 * np.pi * np.outer(idx, idx) / (2 * n)
    d1 = jnp.asarray(np.concatenate([np.cos(ang), -np.sin(ang)], axis=0), jnp.float32)
    d2 = jnp.asarray(np.concatenate([np.cos(ang[:n]), -np.sin(ang[:n])], axis=1) / (2 * n), jnp.float32)
    return pl.pallas_call(
        _direct_conv_kernel,
        grid=(b,),
        in_specs=[pl.BlockSpec((1, n, c), lambda i: (i, 0, xin_blk)),
                  pl.BlockSpec((1, n, c), lambda i: (i, 0, xg_blk)),
                  pl.BlockSpec((2 * n, c), lambda i: (0, 0)),
                  pl.BlockSpec((4 * n, 2 * n), lambda i: (0, 0)),
                  pl.BlockSpec((n, 4 * n), lambda i: (0, 0)),
                  pl.BlockSpec((1, c), lambda i: (0, 0)),
                  pl.BlockSpec((1, c), lambda i: (0, 0))],
        out_specs=pl.BlockSpec((1, n, c), lambda i: (i, 0, 0)),
        out_shape=jax.ShapeDtypeStruct((b, n, c), jnp.float32),
        compiler_params=pltpu.CompilerParams(
            dimension_semantics=("arbitrary",), vmem_limit_bytes=VMEM_LIMIT_BYTES),
        name="direct_long_conv",
    )(xin, xg, kern, d1, d2, norm.reshape(1, c), bias.reshape(1, c))


def _hy_filter_kernel(w1_ref, b1_ref, fr_ref, w2_ref, b2_ref, w3_ref, dl_ref, k_ref, s_ref, *, n):
    i = pl.program_id(0)
    tp, c = k_ref.shape[1], k_ref.shape[2]

    @pl.when(i == 0)
    def _():
        s_ref[...] = jnp.zeros_like(s_ref)

    def pos(shape):
        idx = i * tp + lax.broadcasted_iota(jnp.int32, shape, 0)
        t = jnp.where(idx < n, idx, jnp.where(idx == n, 0, 2 * n - idx))
        return idx, t.astype(jnp.float32)

    _, t = pos((tp, LANES))
    lane = lax.broadcasted_iota(jnp.int32, (tp, LANES), 1)
    band = jnp.where(lane <= HY_BANDS, lane, lane - HY_BANDS).astype(jnp.float32)
    ang = 2.0 * math.pi * t * band / n
    feat = jnp.where(lane == 0, t / n,
                     jnp.where(lane <= HY_BANDS, jnp.sin(ang), jnp.where(lane < HY_EMB, jnp.cos(ang), 0.0)))
    hid = jnp.sin(fr_ref[...] * (_dot3(feat, w1_ref[...]) + b1_ref[...]))
    hid = jnp.sin(fr_ref[...] * (_dot3(hid, w2_ref[...]) + b2_ref[...]))
    f = _dot3(hid, w3_ref[...])
    idx, t = pos((tp, c))
    decay = jnp.exp(-(t / n) * dl_ref[...])
    for o in range(HY_ORDER):
        fwd = f[:, (2 * o) * c:(2 * o + 1) * c]
        bwd = f[:, (2 * o + 1) * c:(2 * o + 2) * c]
        val = jnp.where(idx < n, fwd, bwd) * decay
        s_ref[o:o + 1, :] += jnp.sum(jnp.abs(val), axis=0, keepdims=True)
        k_ref[o] = jnp.where(idx == n, 0.0, val)


def hyena_kernels(n, w1, b1, freq, w2, b2, w3):
    c = HY_CH
    tp = min(512, n)
    max_decay = math.log(HY_TARGET) / HY_FAST_DECAY
    min_decay = math.log(HY_TARGET) / HY_SLOW_DECAY
    deltas = jnp.abs(jnp.linspace(min_decay, max_decay, c, dtype=jnp.float32)).reshape(1, c)
    w1p = jnp.zeros((LANES, w1.shape[1]), jnp.float32).at[:w1.shape[0]].set(w1)
    hd = w1.shape[1]
    full = lambda shape: pl.BlockSpec(shape, lambda i: (0,) * len(shape))
    return pl.pallas_call(
        functools.partial(_hy_filter_kernel, n=n),
        grid=(2 * n // tp,),
        in_specs=[full((LANES, hd)), full((1, hd)), full((1, hd)), full((hd, hd)), full((1, hd)),
                  full((hd, HY_ORDER * 2 * c)), full((1, c))],
        out_specs=[pl.BlockSpec((HY_ORDER, tp, c), lambda i: (0, i, 0)), full((HY_ORDER, c))],
        out_shape=[jax.ShapeDtypeStruct((HY_ORDER, 2 * n, c), jnp.float32),
                   jax.ShapeDtypeStruct((HY_ORDER, c), jnp.float32)],
        compiler_params=pltpu.CompilerParams(
            dimension_semantics=("arbitrary",), vmem_limit_bytes=VMEM_LIMIT_BYTES),
        name="hyena_kernels",
    )(w1p, b1.reshape(1, hd), freq.reshape(1, hd), w2, b2.reshape(1, hd), w3, deltas)


def hyena_mixer(p, col0, conv_w, conv_b, filt_args, hy_bias):
    n = p.shape[1]
    uc = short_conv(p, col0, 3 * HY_CH, conv_w, conv_b)
    kerns, norm = hyena_kernels(n, *filt_args)
    if 2 * n != FFT_N:
        v = direct_long_conv(uc, 2, uc, 0, kerns[0], norm[0], hy_bias[0])
        return direct_long_conv(v, 0, uc, 1, kerns[1], norm[1], hy_bias[1])
    stage_a, stage_b = stage_a_table(), stage_b_tables()
    spec = fft_mid(fft_stage_a(kerns, 0, HY_CH, stage_a), norm, stage_b, conv=False)
    v = fft_out_gate(fft_mid(fft_stage_a(uc, 2, HY_CH, stage_a), spec, stage_b, conv=True, order=0),
                     stage_a, uc, 0, uc, 2, hy_bias[0])
    return fft_out_gate(fft_mid(fft_stage_a(v, 0, HY_CH, stage_a), spec, stage_b, conv=True, order=1),
                        stage_a, uc, 1, v, 0, hy_bias[1])


def rms_norm(x, w):
    xf = x.astype(jnp.float32)
    return (xf * lax.rsqrt(jnp.mean(xf * xf, -1, keepdims=True) + EPS) * w).astype(x.dtype)


def l2_normalize(x):
    xf = x.astype(jnp.float32)
    return (xf * lax.rsqrt(jnp.sum(xf * xf, -1, keepdims=True) + EPS)).astype(x.dtype)


def axial_rope_tables(n_tok):
    rows = n_tok // GRID_W
    row = jnp.repeat(jnp.arange(rows, dtype=jnp.float32), GRID_W)
    col = jnp.tile(jnp.arange(GRID_W, dtype=jnp.float32), rows)
    nf = HEAD_DIM // 4
    inv = ROPE_BASE ** (-jnp.arange(nf, dtype=jnp.float32) / nf)
    ang = jnp.concatenate([row[:, None] * inv, col[:, None] * inv], -1)
    return jnp.cos(ang), jnp.sin(ang)


def apply_axial_rope(x, cos, sin):
    nf = HEAD_DIM // 4
    c = cos[:, None, :]
    s = sin[:, None, :]
    parts = []
    for a in range(2):
        xa = x[..., a * 2 * nf:(a + 1) * 2 * nf]
        x1, x2 = xa[..., :nf], xa[..., nf:]
        ca, sa = c[..., a * nf:(a + 1) * nf], s[..., a * nf:(a + 1) * nf]
        parts += [x1 * ca - x2 * sa, x2 * ca + x1 * sa]
    return jnp.concatenate(parts, -1).astype(x.dtype)


def dwconv(x, w):
    k, ch = w.shape
    return lax.conv_general_dilated(x, w[:, None, :].astype(x.dtype), (1,), [(k // 2, k // 2)],
                                    dimension_numbers=('NWC', 'WIO', 'NWC'), feature_group_count=ch)


def gqa_softmax(q, k, v, sink=None):
    b, lq, h, hd = q.shape
    kvh = k.shape[2]
    g = h // kvh
    lk = k.shape[1]
    s = jnp.einsum('bqkgd,bjkd->bkgqj', q.reshape(b, lq, kvh, g, hd), k).astype(jnp.float32) * hd ** -0.5
    if sink is not None:
        s = jnp.concatenate([s, jnp.broadcast_to(sink.astype(jnp.float32).reshape(kvh, g, 1, 1), s.shape[:-1] + (1,))], -1)
    p = jax.nn.softmax(s, axis=-1)[..., :lk].astype(v.dtype)
    return jnp.einsum('bkgqj,bjkd->bqkgd', p, v).reshape(b, lq, h * hd)


def windowed_sink_attention(q, k, v, kc, vc, sink):
    b, s, h, hd = q.shape
    kvh = k.shape[2]
    g = h // kvh
    nb = s // BLOCK
    lc = kc.shape[1]
    w3 = 3 * BLOCK
    scale = hd ** -0.5
    qb = q.reshape(b, nb, BLOCK, kvh, g, hd).swapaxes(0, 1)

    def band(t):
        tb = jnp.pad(t.reshape(b, nb, BLOCK, kvh, hd), ((0, 0), (1, 1), (0, 0), (0, 0), (0, 0)))
        return jnp.concatenate([tb[:, :-2], tb[:, 1:-1], tb[:, 2:]], axis=2).swapaxes(0, 1)

    kw, vw = band(k), band(v)
    blk = jnp.arange(nb)[:, None, None]
    qpos = blk * BLOCK + jnp.arange(BLOCK)[None, :, None]
    kpos = (blk - 1) * BLOCK + jnp.arange(w3)[None, None, :]
    valid = (jnp.abs(qpos - kpos) <= WINDOW) & (kpos >= 0) & (kpos < s)
    sink_logit = sink.astype(jnp.float32).reshape(kvh, g, 1, 1)

    def one_block(args):
        qk, kk, vk, vm = args
        s_loc = jnp.einsum('bqkgd,bjkd->bkgqj', qk, kk).astype(jnp.float32) * scale
        s_loc = jnp.where(vm, s_loc, -jnp.inf)
        s_ctx = jnp.einsum('bqkgd,bjkd->bkgqj', qk, kc).astype(jnp.float32) * scale
        s_snk = jnp.broadcast_to(sink_logit, s_loc.shape[:-1] + (1,))
        p = jax.nn.softmax(jnp.concatenate([s_loc, s_ctx, s_snk], -1), axis=-1).astype(v.dtype)
        return (jnp.einsum('bkgqj,bjkd->bqkgd', p[..., :w3], vk)
                + jnp.einsum('bkgqj,bjkd->bqkgd', p[..., w3:w3 + lc], vc))

    o = lax.map(one_block, (qb, kw, vw, valid))
    return o.swapaxes(0, 1).reshape(b, s, h * hd)


def global_block_attention(q, k, v, kc, vc):
    b, s, h, hd = q.shape
    nb = s // BLOCK
    k_all = jnp.concatenate([k, kc], 1)
    v_all = jnp.concatenate([v, vc], 1)
    qb = q.reshape(b, nb, BLOCK, h, hd).swapaxes(0, 1)
    o = lax.map(lambda qk: gqa_softmax(qk, k_all, v_all), qb)
    return o.swapaxes(0, 1).reshape(b, s, h * hd)


def hyena_filters(n, w1, b1, freq, w2, b2, w3):
    t = jnp.arange(n, dtype=jnp.float32)
    tn = t / n
    f = jnp.arange(1, HY_BANDS + 1, dtype=jnp.float32)
    ang = 2.0 * math.pi * t[:, None] * f[None, :] / n
    feat = jnp.concatenate([tn[:, None], jnp.sin(ang), jnp.cos(ang)], -1)
    hid = jnp.sin(freq * (feat @ w1 + b1))
    hid = jnp.sin(freq * (hid @ w2 + b2))
    filt = (hid @ w3).astype(jnp.float32).reshape(n, HY_ORDER, 2, HY_CH)
    max_decay = math.log(HY_TARGET) / HY_FAST_DECAY
    min_decay = math.log(HY_TARGET) / HY_SLOW_DECAY
    deltas = jnp.abs(jnp.linspace(min_decay, max_decay, HY_CH, dtype=jnp.float32))
    filt = filt * jnp.exp(-tn[:, None, None, None] * deltas)
    return filt / jnp.sum(jnp.abs(filt), axis=(0, 2), keepdims=True)


def fft_long_conv(x, hf, hb, bias):
    n = x.shape[1]
    kern = jnp.concatenate([hf, jnp.zeros_like(hf[:1]), hb[1:][::-1]], 0)
    kf = jnp.fft.rfft(kern, n=2 * n, axis=0)
    xf = jnp.fft.rfft(x.astype(jnp.float32), n=2 * n, axis=1)
    y = jnp.fft.irfft(xf * kf[None], n=2 * n, axis=1)[:, :n]
    return (y + x.astype(jnp.float32) * bias.astype(jnp.float32)).astype(x.dtype)


def hyena_operator(u, conv_w, conv_b, fw1, fb1, ffreq, fw2, fb2, fw3, hy_bias):
    uc = dwconv(u, conv_w) + conv_b
    x1, x2, v = uc[..., :HY_CH], uc[..., HY_CH:2 * HY_CH], uc[..., 2 * HY_CH:]
    filt = hyena_filters(u.shape[1], fw1, fb1, ffreq, fw2, fb2, fw3)
    v = x1 * fft_long_conv(v, filt[:, 0, 0], filt[:, 0, 1], hy_bias[0])
    v = x2 * fft_long_conv(v, filt[:, 1, 0], filt[:, 1, 1], hy_bias[1])
    return v


def gated_delta_chunked(q, k, v, beta, g, s0):
    b, n_tok, h, dk = q.shape
    dv = v.shape[-1]
    cs = GDN_CHUNK
    n = n_tok // cs

    def blk(t):
        return t.astype(jnp.float32).reshape(b, n, cs, h, -1).transpose(1, 0, 3, 2, 4)

    q = blk(q) * dk ** -0.5
    k = blk(k)
    v = blk(v)
    beta = blk(beta[..., None])[..., 0]
    gcum = jnp.cumsum(blk(g[..., None])[..., 0], -1)
    idx = jnp.arange(cs)
    incl = idx[:, None] >= idx[None, :]
    strict = idx[:, None] > idx[None, :]
    decay = jnp.exp(jnp.where(incl, gcum[..., :, None] - gcum[..., None, :], -jnp.inf))
    kb = k * beta[..., None]
    a = jnp.where(strict, jnp.einsum('nbhid,nbhjd->nbhij', kb, k) * decay, 0.0)
    rhs = jnp.concatenate([v * beta[..., None], kb * jnp.exp(gcum)[..., None]], -1)
    sol = lax.linalg.triangular_solve(jnp.eye(cs, dtype=jnp.float32) + a, rhs, left_side=True, lower=True)
    u, w = sol[..., :dv], sol[..., dv:]
    intra = jnp.einsum('nbhid,nbhjd->nbhij', q, k) * decay
    q_dec = q * jnp.exp(gcum)[..., None]
    k_dec = k * jnp.exp(gcum[..., -1:] - gcum)[..., None]
    g_last = jnp.exp(gcum[..., -1])

    def step(state, xs):
        u_i, w_i, q_i, k_i, intra_i, gl = xs
        v_new = u_i - jnp.einsum('bhcd,bhde->bhce', w_i, state)
        o_i = jnp.einsum('bhcd,bhde->bhce', q_i, state) + jnp.einsum('bhij,bhje->bhie', intra_i, v_new)
        state = state * gl[..., None, None] + jnp.einsum('bhcd,bhce->bhde', k_i, v_new)
        return state, o_i

    s_final, o = lax.scan(step, s0.astype(jnp.float32), (u, w, q_dec, k_dec, intra, g_last))
    return o.transpose(1, 0, 3, 2, 4).reshape(b, n_tok, h, dv), s_final


def gdn_inputs(p, conv_w, a_log, dt_bias):
    b, n, _ = p.shape
    qkv = jax.nn.silu(dwconv(p[..., :3 * C_W], conv_w))
    q = l2_normalize(qkv[..., :C_W].reshape(b, n, C_HEADS, C_DK))
    k = l2_normalize(qkv[..., C_W:2 * C_W].reshape(b, n, C_HEADS, C_DK))
    v = qkv[..., 2 * C_W:].reshape(b, n, C_HEADS, C_DV)
    z = p[..., 3 * C_W:4 * C_W].reshape(b, n, C_HEADS, C_DV)
    gates = p[..., -C_GATES:].astype(jnp.float32).reshape(b, n, 4, C_HEADS)
    beta = jax.nn.sigmoid(gates[:, :, :2])
    g = -jnp.exp(a_log.astype(jnp.float32)) * jax.nn.softplus(gates[:, :, 2:] + dt_bias.astype(jnp.float32))
    return q, k, v, z, beta, g


def rev(t, flip):
    return t[:, ::-1] if flip else t


def bidirectional_gdn(lat, cx, with_ctx):
    ql, kl, vl, bl, gl = lat
    qc, kc, vc, bc, gcx = cx
    s0 = jnp.zeros((ql.shape[0], C_HEADS, C_DK, C_DV), jnp.float32)
    o_lat = 0.0
    o_ctx = 0.0
    for d in range(2):
        f = d == 1
        oc, sc = gated_delta_chunked(rev(qc, f), rev(kc, f), rev(vc, f), rev(bc[:, :, d], f), rev(gcx[:, :, d], f), s0)
        ol, _ = gated_delta_chunked(rev(ql, f), rev(kl, f), rev(vl, f), rev(bl[:, :, d], f), rev(gl[:, :, d], f), sc)
        o_lat = o_lat + rev(ol, f)
        if with_ctx:
            o_ctx = o_ctx + rev(oc, f)
    return o_lat, o_ctx


def even_mixer(p, pc, rope_tabs, sink, conv_w, conv_b, fw1, fb1, ffreq, fw2, fb2, fw3, hy_bias, with_ctx):
    q, k, v = qkv_prep(p, 0, A_HEADS, A_KV_HEADS, rope_tabs)
    qc, kc, vc = qkv_prep(pc, 0, A_HEADS, A_KV_HEADS, None)
    o_a = windowed_sink_gqa(q, k, v, kc, vc, sink)
    filt_args = (fw1, fb1, ffreq, fw2, fb2, fw3)
    o_b = hyena_mixer(p, A_Q + 2 * A_KV, conv_w, conv_b, filt_args, hy_bias)
    out_c = None
    if with_ctx:
        o_ac = flash_gqa(qc, kc, vc, sink)
        o_bc = hyena_mixer(pc, A_Q + 2 * A_KV, conv_w, conv_b, filt_args, hy_bias)
        out_c = (o_ac, o_bc)
    return (o_a, o_b), out_c


def odd_mixer(p, pc, rope_tabs, conv_w, a_log, dt_bias, gnorm_w, qnorm_w, knorm_w, with_ctx):
    o_l, o_c = gdn_mixer(p, pc, conv_w, a_log, dt_bias, gnorm_w, with_ctx)
    qd, kd, vd = qkv_prep(p, 4 * C_W, D_HEADS, D_KV_HEADS, rope_tabs, qnorm_w, knorm_w)
    qdc, kdc, vdc = qkv_prep(pc, 4 * C_W, D_HEADS, D_KV_HEADS, None, qnorm_w, knorm_w)
    o_d = flash_gqa(qd, jnp.concatenate([kd, kdc], 2), jnp.concatenate([vd, vdc], 2))
    out_c = None
    if with_ctx:
        out_c = (o_c, flash_gqa(qdc, kdc, vdc))
    return (o_l, o_d), out_c


def kernel(x, c, ctx, c_ctx, ada_w, ada_b, ln1_g, ln1_b, ln2_g, ln2_b, peer_wq, peer_k1, peer_k2, peer_u, peer_v, ev_w_in, ev_w_out, ev_sink, ev_conv_w, ev_conv_b, ev_filt_w1, ev_filt_b1, ev_filt_freq, ev_filt_w2, ev_filt_b2, ev_filt_w3, ev_hy_bias, od_w_in, od_w_out, od_conv_w, od_a_log, od_dt_bias, od_gnorm_w, od_qnorm_w, od_knorm_w):
    rope_tabs = rope_tables(x.shape[1])
    bsz = x.shape[0]
    silu_c = jax.nn.silu(c)
    silu_cc = jax.nn.silu(c_ctx)
    for i in range(DEPTH):
        with_ctx = i < DEPTH - 1
        j = i // 2
        mod = (silu_c @ ada_w[i] + ada_b[i])[:, None, :]
        modc = jnp.broadcast_to((silu_cc @ ada_w[i] + ada_b[i])[None, None, :], (bsz, 1, 6 * D_MODEL))
        sh1, sc1, g1, sh2, sc2, g2 = jnp.split(mod, 6, axis=-1)
        sh1c, sc1c, g1c, sh2c, sc2c, g2c = jnp.split(modc, 6, axis=-1)
        if i % 2 == 0:
            p = mod_matmul(x, sh1, sc1, ev_w_in[j])
            pc = mod_matmul(ctx, sh1c, sc1c, ev_w_in[j])
            out, out_c = even_mixer(p, pc, rope_tabs, ev_sink[j], ev_conv_w[j], ev_conv_b[j],
                                    ev_filt_w1[j], ev_filt_b1[j], ev_filt_freq[j], ev_filt_w2[j], ev_filt_b2[j],
                                    ev_filt_w3[j], ev_hy_bias[j], with_ctx)
            w_out = ev_w_out[j]
        else:
            w_in = od_w_in[j]
            w_in = jnp.concatenate([w_in[:, :4 * C_W], w_in[:, 4 * C_W + C_GATES:],
                                    w_in[:, 4 * C_W:4 * C_W + C_GATES]], axis=1)
            p = mod_matmul(x, sh1, sc1, w_in)
            pc = mod_matmul(ctx, sh1c, sc1c, w_in)
            out, out_c = odd_mixer(p, pc, rope_tabs, od_conv_w[j], od_a_log[j], od_dt_bias[j],
                                   od_gnorm_w[j], od_qnorm_w[j], od_knorm_w[j], with_ctx)
            w_out = od_w_out[j]
        u_bf = peer_u[i].astype(jnp.bfloat16)
        v_bf = peer_v[i].astype(jnp.bfloat16)
        x = proj_residual_ln(out[0], out[1], w_out, x, g1, ln1_g[i], ln1_b[i])
        x = peer_block(x, sh2, sc2, g2, peer_wq[i], peer_k1[i], peer_k2[i], u_bf, v_bf, ln2_g[i], ln2_b[i])
        if with_ctx:
            ctx = proj_residual_ln(out_c[0], out_c[1], w_out, ctx, g1c, ln1_g[i], ln1_b[i])
            ctx = peer_block(ctx, sh2c, sc2c, g2c, peer_wq[i], peer_k1[i], peer_k2[i], u_bf, v_bf,
                             ln2_g[i], ln2_b[i])
    return x
```

```python
import functools
import math

import numpy as np

import jax
import jax.numpy as jnp
from jax import lax
from jax.experimental import pallas as pl
from jax.experimental.pallas import tpu as pltpu

D_MODEL = 1024
DEPTH = 2
GRID_W = 64
HEAD_DIM = 64
BLOCK = 128
ROPE_BASE = 10000.0
EPS = 1e-6

A_HEADS = 8
A_KV_HEADS = 2
WINDOW = 128

HY_CH = 512
HY_ORDER = 2
HY_EMB = 33
HY_BANDS = (HY_EMB - 1) // 2
HY_FAST_DECAY = 0.3
HY_SLOW_DECAY = 1.5
HY_TARGET = 1e-2

C_HEADS = 4
C_DK = 128
C_DV = 128
GDN_CHUNK = 64

D_HEADS = 8
D_KV_HEADS = 2

PEER_HEADS = 8
PEER_NKEYS = 128
PEER_QDIM = 256
PEER_TOPK = 16
PEER_CHUNK = 128

ALPHA = (2 * DEPTH) ** 0.25

A_Q = A_HEADS * HEAD_DIM
A_KV = A_KV_HEADS * HEAD_DIM
C_W = C_HEADS * C_DK
C_GATES = 4 * C_HEADS
D_Q = D_HEADS * HEAD_DIM
D_KV = D_KV_HEADS * HEAD_DIM

VMEM_LIMIT_BYTES = 48 * 1024 * 1024

LANES = 128
_NT = (((1,), (1,)), ((), ()))


def _modmm_kernel(x_ref, sh_ref, sc_ref, w_ref, o_ref):
    h = x_ref[0] * (1.0 + sc_ref[0]) + sh_ref[0]
    o_ref[0] = jnp.dot(h.astype(jnp.bfloat16), w_ref[...], preferred_element_type=jnp.float32)


def mod_matmul(x, shift, scale, w, tm=512, tn=None):
    b, s, k = x.shape
    n = w.shape[1]
    tm = min(tm, s)
    tn = n if tn is None else tn
    wb = w.astype(jnp.bfloat16)
    return pl.pallas_call(
        _modmm_kernel,
        grid=(b, n // tn, s // tm),
        in_specs=[
            pl.BlockSpec((1, tm, k), lambda i, j, m: (i, m, 0)),
            pl.BlockSpec((1, 1, k), lambda i, j, m: (i, 0, 0)),
            pl.BlockSpec((1, 1, k), lambda i, j, m: (i, 0, 0)),
            pl.BlockSpec((k, tn), lambda i, j, m: (0, j)),
        ],
        out_specs=pl.BlockSpec((1, tm, tn), lambda i, j, m: (i, m, j)),
        out_shape=jax.ShapeDtypeStruct((b, s, n), jnp.float32),
        compiler_params=pltpu.CompilerParams(
            dimension_semantics=("arbitrary", "arbitrary", "arbitrary"),
            vmem_limit_bytes=VMEM_LIMIT_BYTES),
        name="mod_matmul",
    )(x, shift, scale, wb)


def _qkv_prep_kernel(q_ref, k_ref, v_ref, cs_ref, sn_ref, qw_ref, kw_ref, gm_ref, qo_ref, ko_ref, vo_ref, *,
                     norm, rope, nq, nkv):
    def prep(x, w, nh):
        if norm:
            ms = jnp.dot(x * x, gm_ref[:x.shape[1], :x.shape[1]], precision=lax.Precision.HIGHEST,
                         preferred_element_type=jnp.float32)
            x = x * lax.rsqrt(ms + EPS) * w
        if rope:
            n = x.shape[1]
            reps = n // cs_ref.shape[1]
            cs = jnp.concatenate([cs_ref[...]] * reps, axis=1) if reps > 1 else cs_ref[...]
            sn = jnp.concatenate([sn_ref[...]] * reps, axis=1) if reps > 1 else sn_ref[...]
            lane = lax.broadcasted_iota(jnp.int32, x.shape, 1)
            nf = HEAD_DIM // 4
            partner = jnp.where((lane & nf) == 0, pltpu.roll(x, n - nf, 1), pltpu.roll(x, nf, 1))
            x = x * cs + partner * sn
        return x

    q = prep(q_ref[0], qw_ref[...], nq) * (HEAD_DIM ** -0.5)
    k = prep(k_ref[0], kw_ref[...], nkv)
    v = v_ref[0]
    for h in range(nq):
        qo_ref[0, h] = q[:, h * HEAD_DIM:(h + 1) * HEAD_DIM].astype(jnp.bfloat16)
    for h in range(nkv):
        ko_ref[0, h] = k[:, h * HEAD_DIM:(h + 1) * HEAD_DIM].astype(jnp.bfloat16)
        vo_ref[0, h] = v[:, h * HEAD_DIM:(h + 1) * HEAD_DIM].astype(jnp.bfloat16)


def qkv_prep(p, col0, nq, nkv, rope_tabs, qw=None, kw=None, tq=512):
    b, s, _ = p.shape
    tq = min(tq, s)
    wq_, wk_ = nq * HEAD_DIM, nkv * HEAD_DIM
    norm = qw is not None
    rope = rope_tabs is not None
    if rope:
        cs, sn = rope_tabs
    else:
        cs = sn = jnp.zeros((s, 2 * HEAD_DIM), jnp.float32)
    qw_t = jnp.tile(qw, nq).reshape(1, wq_) if norm else jnp.ones((1, wq_), jnp.float32)
    kw_t = jnp.tile(kw, nkv).reshape(1, wk_) if norm else jnp.ones((1, wk_), jnp.float32)
    grp = jnp.arange(wq_) // HEAD_DIM
    gm = (grp[:, None] == grp[None, :]).astype(jnp.float32) / HEAD_DIM
    kern = functools.partial(_qkv_prep_kernel, norm=norm, rope=rope, nq=nq, nkv=nkv)
    return pl.pallas_call(
        kern,
        grid=(b, s // tq),
        in_specs=[
            pl.BlockSpec((1, tq, wq_), lambda i, m: (i, m, col0 // wq_)),
            pl.BlockSpec((1, tq, wk_), lambda i, m: (i, m, (col0 + wq_) // wk_)),
            pl.BlockSpec((1, tq, wk_), lambda i, m: (i, m, (col0 + wq_) // wk_ + 1)),
            pl.BlockSpec((tq, 2 * HEAD_DIM), lambda i, m: (m, 0)),
            pl.BlockSpec((tq, 2 * HEAD_DIM), lambda i, m: (m, 0)),
            pl.BlockSpec((1, wq_), lambda i, m: (0, 0)),
            pl.BlockSpec((1, wk_), lambda i, m: (0, 0)),
            pl.BlockSpec((wq_, wq_), lambda i, m: (0, 0)),
        ],
        out_specs=[
            pl.BlockSpec((1, nq, tq, HEAD_DIM), lambda i, m: (i, 0, m, 0)),
            pl.BlockSpec((1, nkv, tq, HEAD_DIM), lambda i, m: (i, 0, m, 0)),
            pl.BlockSpec((1, nkv, tq, HEAD_DIM), lambda i, m: (i, 0, m, 0)),
        ],
        out_shape=[
            jax.ShapeDtypeStruct((b, nq, s, HEAD_DIM), jnp.bfloat16),
            jax.ShapeDtypeStruct((b, nkv, s, HEAD_DIM), jnp.bfloat16),
            jax.ShapeDtypeStruct((b, nkv, s, HEAD_DIM), jnp.bfloat16),
        ],
        compiler_params=pltpu.CompilerParams(
            dimension_semantics=("arbitrary", "arbitrary"), vmem_limit_bytes=VMEM_LIMIT_BYTES),
        name="qkv_prep",
    )(p, p, p, cs, sn, qw_t, kw_t, gm)


def rope_tables(n_tok):
    rows = n_tok // GRID_W
    row = jnp.repeat(jnp.arange(rows, dtype=jnp.float32), GRID_W)
    col = jnp.tile(jnp.arange(GRID_W, dtype=jnp.float32), rows)
    nf = HEAD_DIM // 4
    inv = ROPE_BASE ** (-jnp.arange(nf, dtype=jnp.float32) / nf)
    ar, ac = row[:, None] * inv, col[:, None] * inv
    cs = jnp.concatenate([jnp.cos(ar), jnp.cos(ar), jnp.cos(ac), jnp.cos(ac)], -1)
    sn = jnp.concatenate([-jnp.sin(ar), jnp.sin(ar), -jnp.sin(ac), jnp.sin(ac)], -1)
    return jnp.tile(cs, (1, 2)), jnp.tile(sn, (1, 2))


def _flash_kernel(sink_ref, q_ref, k_ref, v_ref, o_ref, m_s, l_s, acc_s, *, use_sink, grp):
    j = pl.program_id(3)
    tq = q_ref.shape[2]

    @pl.when(j == 0)
    def _():
        m_s[...] = jnp.full_like(m_s, -jnp.inf)
        l_s[...] = jnp.zeros_like(l_s)
        acc_s[...] = jnp.zeros_like(acc_s)

    tk = k_ref.shape[2]
    nt = tk // LANES
    kt = k_ref[0, 0]
    vt = v_ref[0, 0]
    scores = [lax.dot_general(q_ref[0, g], kt, _NT, preferred_element_type=jnp.float32) for g in range(grp)]
    for g, s in enumerate(scores):
        rows = slice(g * tq, (g + 1) * tq)
        tiles = [s[:, c * LANES:(c + 1) * LANES] for c in range(nt)]
        m_tile = functools.reduce(jnp.maximum, tiles)
        m_old = m_s[rows]
        m_new = jnp.maximum(m_old, jnp.broadcast_to(jnp.max(m_tile, axis=1, keepdims=True), m_old.shape))
        alpha = jnp.exp(m_old - m_new)
        p_tiles = [jnp.exp(t - m_new) for t in tiles]
        l_s[rows] = alpha * l_s[rows] + functools.reduce(jnp.add, p_tiles)
        p = jnp.concatenate([t.astype(jnp.bfloat16) for t in p_tiles], axis=1)
        acc_s[rows] = alpha[:, :HEAD_DIM] * acc_s[rows] + jnp.dot(p, vt, preferred_element_type=jnp.float32)
        m_s[rows] = m_new

    @pl.when(j == pl.num_programs(3) - 1)
    def _():
        kvh = pl.program_id(1)
        outs = []
        for g in range(grp):
            rows = slice(g * tq, (g + 1) * tq)
            m = m_s[rows][:, :1]
            l = jnp.sum(l_s[rows], axis=1, keepdims=True)
            acc = acc_s[rows]
            if use_sink:
                sk = sink_ref[kvh * grp + g]
                m2 = jnp.maximum(m, sk)
                a = jnp.exp(m - m2)
                l = a * l + jnp.exp(sk - m2)
                acc = a * acc
            outs.append(acc / l)
        o_ref[0] = jnp.concatenate(outs, axis=1).astype(o_ref.dtype)


def flash_gqa(q, k, v, sink=None, tq=256, tk=768):
    b, h, s, hd = q.shape
    kvh, lk = k.shape[1], k.shape[2]
    grp = h // kvh
    tq = min(tq, s)
    tk = min(tk, lk)
    use_sink = sink is not None
    sink_arr = sink.astype(jnp.float32) if use_sink else jnp.zeros((h,), jnp.float32)
    kern = functools.partial(_flash_kernel, use_sink=use_sink, grp=grp)
    return pl.pallas_call(
        kern,
        grid=(b, kvh, s // tq, lk // tk),
        in_specs=[
            pl.BlockSpec(memory_space=pltpu.SMEM),
            pl.BlockSpec((1, grp, tq, hd), lambda i, c, m, j: (i, c, m, 0)),
            pl.BlockSpec((1, 1, tk, hd), lambda i, c, m, j: (i, c, j, 0)),
            pl.BlockSpec((1, 1, tk, hd), lambda i, c, m, j: (i, c, j, 0)),
        ],
        out_specs=pl.BlockSpec((1, tq, grp * hd), lambda i, c, m, j: (i, m, c)),
        out_shape=jax.ShapeDtypeStruct((b, s, h * hd), jnp.bfloat16),
        scratch_shapes=[pltpu.VMEM((grp * tq, LANES), jnp.float32), pltpu.VMEM((grp * tq, LANES), jnp.float32),
                        pltpu.VMEM((grp * tq, hd), jnp.float32)],
        compiler_params=pltpu.CompilerParams(
            dimension_semantics=("arbitrary",) * 4, vmem_limit_bytes=VMEM_LIMIT_BYTES),
        name="flash_gqa",
    )(sink_arr, q, k, v)


def _window_kernel(sink_ref, q_ref, kp_ref, kc_ref, kn_ref, vp_ref, vc_ref, vn_ref, kx_ref, vx_ref, o_ref, *, grp):
    kvh = pl.program_id(1)
    i = pl.program_id(2)
    nb = pl.num_programs(2)
    kcat = jnp.concatenate([kp_ref[0, 0], kc_ref[0, 0], kn_ref[0, 0], kx_ref[0, 0]], axis=0)
    vcat = jnp.concatenate([vp_ref[0, 0], vc_ref[0, 0], vn_ref[0, 0], vx_ref[0, 0]], axis=0)
    nk = kcat.shape[0]
    r = lax.broadcasted_iota(jnp.int32, (BLOCK, nk), 0)
    c = lax.broadcasted_iota(jnp.int32, (BLOCK, nk), 1)
    off_prev = jnp.where(i > 0, 0, 2 * nk)
    off_next = jnp.where(i < nb - 1, 0, 2 * nk)
    ok_prev = (c >= BLOCK) | (c >= r + off_prev)
    ok_next = (c < 2 * BLOCK) | (c >= 3 * BLOCK) | (c - 2 * BLOCK <= r - off_next)
    valid = ok_prev & ok_next
    scores = [lax.dot_general(q_ref[0, g], kcat, _NT, preferred_element_type=jnp.float32) for g in range(grp)]
    outs = []
    for g, s in enumerate(scores):
        s = jnp.where(valid, s, -jnp.inf)
        sk = sink_ref[kvh * grp + g]
        m = jnp.maximum(jnp.max(s, axis=1, keepdims=True), sk)
        p = jnp.exp(s - m)
        l = jnp.sum(p, axis=1, keepdims=True) + jnp.exp(sk - m)
        o = jnp.dot(p.astype(jnp.bfloat16), vcat, preferred_element_type=jnp.float32)
        outs.append(o / l)
    o_ref[0] = jnp.concatenate(outs, axis=1).astype(o_ref.dtype)


def windowed_sink_gqa(q, k, v, kx, vx, sink):
    b, h, s, hd = q.shape
    kvh = k.shape[1]
    lc = kx.shape[2]
    grp = h // kvh
    nb = s // BLOCK
    kern = functools.partial(_window_kernel, grp=grp)
    blk = lambda f: pl.BlockSpec((1, 1, BLOCK, hd), f)
    prev = lambda i, c, m: (i, c, jnp.maximum(m - 1, 0), 0)
    cur = lambda i, c, m: (i, c, m, 0)
    nxt = lambda i, c, m: (i, c, jnp.minimum(m + 1, nb - 1), 0)
    ctxm = lambda i, c, m: (i, c, 0, 0)
    return pl.pallas_call(
        kern,
        grid=(b, kvh, nb),
        in_specs=[
            pl.BlockSpec(memory_space=pltpu.SMEM),
            pl.BlockSpec((1, grp, BLOCK, hd), cur),
            blk(prev), blk(cur), blk(nxt), blk(prev), blk(cur), blk(nxt),
            pl.BlockSpec((1, 1, lc, hd), ctxm), pl.BlockSpec((1, 1, lc, hd), ctxm),
        ],
        out_specs=pl.BlockSpec((1, BLOCK, grp * hd), lambda i, c, m: (i, m, c)),
        out_shape=jax.ShapeDtypeStruct((b, s, h * hd), jnp.bfloat16),
        compiler_params=pltpu.CompilerParams(
            dimension_semantics=("arbitrary",) * 3, vmem_limit_bytes=VMEM_LIMIT_BYTES),
        name="windowed_sink_gqa",
    )(sink.astype(jnp.float32), q, k, k, k, v, v, v, kx, vx)


def _post_kernel(oa_ref, ob_ref, w_ref, x_ref, g_ref, lg_ref, lb_ref, y_ref):
    ka = oa_ref.shape[2]
    out = jnp.dot(oa_ref[0].astype(jnp.bfloat16), w_ref[:ka], preferred_element_type=jnp.float32)
    out += jnp.dot(ob_ref[0].astype(jnp.bfloat16), w_ref[ka:], preferred_element_type=jnp.float32)
    r = ALPHA * x_ref[0] + g_ref[0] * out
    mu = jnp.mean(r, -1, keepdims=True)
    d = r - mu
    var = jnp.mean(d * d, -1, keepdims=True)
    y_ref[0] = d * lax.rsqrt(var + EPS) * lg_ref[...] + lb_ref[...]


def proj_residual_ln(oa, ob, w, x, gate, ln_g, ln_b, tm=256):
    b, s, ka = oa.shape
    kb = ob.shape[2]
    k = ka + kb
    d = w.shape[1]
    tm = min(tm, s)
    wb = w.astype(jnp.bfloat16)
    return pl.pallas_call(
        _post_kernel,
        grid=(b, s // tm),
        in_specs=[
            pl.BlockSpec((1, tm, ka), lambda i, m: (i, m, 0)),
            pl.BlockSpec((1, tm, kb), lambda i, m: (i, m, 0)),
            pl.BlockSpec((k, d), lambda i, m: (0, 0)),
            pl.BlockSpec((1, tm, d), lambda i, m: (i, m, 0)),
            pl.BlockSpec((1, 1, d), lambda i, m: (i, 0, 0)),
            pl.BlockSpec((1, d), lambda i, m: (0, 0)),
            pl.BlockSpec((1, d), lambda i, m: (0, 0)),
        ],
        out_specs=pl.BlockSpec((1, tm, d), lambda i, m: (i, m, 0)),
        out_shape=jax.ShapeDtypeStruct((b, s, d), jnp.float32),
        compiler_params=pltpu.CompilerParams(
            dimension_semantics=("arbitrary", "arbitrary"),
            vmem_limit_bytes=VMEM_LIMIT_BYTES),
        name="proj_residual_ln",
    )(oa, ob, wb, x, gate, ln_g.reshape(1, d), ln_b.reshape(1, d))


def _top16(s, payload=None):
    n = s.shape[0]
    iota = lax.broadcasted_iota(jnp.int32, s.shape, 0).astype(jnp.float32)
    vals, ids = [], []
    for _ in range(PEER_TOPK):
        m = jnp.max(s, axis=0, keepdims=True)
        pos = jnp.min(jnp.where(s == m, iota, float(n)), axis=0, keepdims=True)
        hit = iota == pos
        vals.append(m)
        ids.append(pos if payload is None else jnp.max(jnp.where(hit, payload, -1.0), axis=0, keepdims=True))
        s = jnp.where(hit, -jnp.inf, s)
    return jnp.concatenate(vals, 0), jnp.concatenate(ids, 0)


def _peer_topk_kernel(q_ref, k1_ref, k2_ref, eid_ref, gate_ref, eid_s, gate_s):
    half = PEER_QDIM // 2

    def head(h, carry):
        off = pl.multiple_of(h * PEER_QDIM, PEER_QDIM)
        q1 = q_ref[:, pl.ds(off, half)]
        q2 = q_ref[:, pl.ds(off + half, half)]
        s1 = lax.dot_general(k1_ref[h], q1, _NT, precision=lax.Precision.HIGHEST,
                             preferred_element_type=jnp.float32)
        s2 = lax.dot_general(k2_ref[h], q2, _NT, precision=lax.Precision.HIGHEST,
                             preferred_element_type=jnp.float32)
        v1, i1 = _top16(s1)
        v2, i2 = _top16(s2)
        k8 = PEER_TOPK // 2
        cand = jnp.concatenate([v1[0:1] + v2] + [v1[i:i + 1] + v2[:k8] for i in range(1, k8)]
                               + [v1[k8:] + v2[0:1]], 0)
        cid = jnp.concatenate([i1[0:1] * PEER_NKEYS + i2]
                              + [i1[i:i + 1] * PEER_NKEYS + i2[:k8] for i in range(1, k8)]
                              + [i1[k8:] * PEER_NKEYS + i2[0:1]], 0)
        best, eid = _top16(cand, cid)
        e = jnp.exp(best - best[0:1])
        gate = e / jnp.sum(e, axis=0, keepdims=True)
        row = pl.multiple_of(h * PEER_TOPK, PEER_TOPK)
        eid_s[pl.ds(row, PEER_TOPK), :] = eid.astype(jnp.int32)
        gate_s[pl.ds(row, PEER_TOPK), :] = gate
        return carry

    lax.fori_loop(0, PEER_HEADS, head, 0)
    eid_ref[...] = eid_s[...].T
    gate_ref[...] = gate_s[...].T


def peer_topk(q, k1, k2, tt=512):
    t = q.shape[0]
    tt = min(tt, t)
    nsel = PEER_HEADS * PEER_TOPK
    return pl.pallas_call(
        _peer_topk_kernel,
        grid=(t // tt,),
        in_specs=[
            pl.BlockSpec((tt, q.shape[1]), lambda i: (i, 0)),
            pl.BlockSpec(k1.shape, lambda i: (0, 0, 0)),
            pl.BlockSpec(k2.shape, lambda i: (0, 0, 0)),
        ],
        out_specs=[pl.BlockSpec((tt, nsel), lambda i: (i, 0)),
                   pl.BlockSpec((tt, nsel), lambda i: (i, 0))],
        out_shape=[jax.ShapeDtypeStruct((t, nsel), jnp.int32),
                   jax.ShapeDtypeStruct((t, nsel), jnp.float32)],
        scratch_shapes=[pltpu.VMEM((nsel, tt), jnp.int32), pltpu.VMEM((nsel, tt), jnp.float32)],
        compiler_params=pltpu.CompilerParams(
            dimension_semantics=("arbitrary",), vmem_limit_bytes=VMEM_LIMIT_BYTES),
        name="peer_topk",
    )(q, k1, k2)


def _peer_w_kernel(e_ref, g_ref, w_ref):
    nk = PEER_NKEYS
    iota = lax.broadcasted_iota(jnp.int32, (nk, e_ref.shape[1]), 0)

    def tok(t, carry):
        e = e_ref[pl.ds(t, 1), :]
        g = g_ref[pl.ds(t, 1), :]
        a_t = jnp.where(iota == (e >> 7), g, 0.0).astype(jnp.bfloat16)
        b_t = jnp.where(iota == (e & (nk - 1)), 1.0, 0.0).astype(jnp.bfloat16)
        w = lax.dot_general(a_t, b_t, _NT, preferred_element_type=jnp.float32)
        w_ref[t] = w.astype(jnp.bfloat16)
        return carry

    lax.fori_loop(0, e_ref.shape[0], tok, 0, unroll=32)


def peer_dense_gates(eid, gate, tt=128):
    t, nsel = eid.shape
    tt = min(tt, t)
    nk = PEER_NKEYS
    w = pl.pallas_call(
        _peer_w_kernel,
        grid=(t // tt,),
        in_specs=[pl.BlockSpec((tt, nsel), lambda i: (i, 0)),
                  pl.BlockSpec((tt, nsel), lambda i: (i, 0))],
        out_specs=pl.BlockSpec((tt, nk, nk), lambda i: (i, 0, 0)),
        out_shape=jax.ShapeDtypeStruct((t, nk, nk), jnp.bfloat16),
        compiler_params=pltpu.CompilerParams(
            dimension_semantics=("arbitrary",), vmem_limit_bytes=VMEM_LIMIT_BYTES),
        name="peer_dense_gates",
    )(eid, gate)
    return w


def _peer_expert_kernel(x_ref, sh_ref, sc_ref, w_ref, u_ref, v_ref, g_ref, lg_ref, lb_ref, y_ref, xm_s, acc_s):
    e = pl.program_id(2)

    @pl.when(e == 0)
    def _():
        xm_s[...] = (x_ref[0] * (1.0 + sc_ref[0]) + sh_ref[0]).astype(jnp.bfloat16)
        acc_s[...] = jnp.zeros_like(acc_s)

    h = lax.dot_general(xm_s[...], u_ref[...], _NT, preferred_element_type=jnp.float32)
    gelu = 0.5 * h * (1.0 + lax.erf(h * (2.0 ** -0.5)))
    w = w_ref[0].reshape(h.shape)
    a = gelu * w.astype(jnp.float32)
    acc_s[...] += jnp.dot(a.astype(jnp.bfloat16), v_ref[...], preferred_element_type=jnp.float32)

    @pl.when(e == pl.num_programs(2) - 1)
    def _():
        r = ALPHA * x_ref[0] + g_ref[0] * acc_s[...]
        mu = jnp.mean(r, -1, keepdims=True)
        d = r - mu
        var = jnp.mean(d * d, -1, keepdims=True)
        y_ref[0] = d * lax.rsqrt(var + EPS) * lg_ref[...] + lb_ref[...]


def peer_experts_ln(x, shift, scale, w, u_tab, v_tab, gate, ln_g, ln_b, tt=512, te=2048):
    b, s, d = x.shape
    tt = min(tt, s)
    ne = u_tab.shape[0]
    nk = PEER_NKEYS
    w3 = w.reshape(b, s, nk, nk)
    return pl.pallas_call(
        _peer_expert_kernel,
        grid=(b, s // tt, ne // te),
        in_specs=[
            pl.BlockSpec((1, tt, d), lambda i, m, e: (i, m, 0)),
            pl.BlockSpec((1, 1, d), lambda i, m, e: (i, 0, 0)),
            pl.BlockSpec((1, 1, d), lambda i, m, e: (i, 0, 0)),
            pl.BlockSpec((1, tt, te // nk, nk), lambda i, m, e: (i, m, e, 0)),
            pl.BlockSpec((te, d), lambda i, m, e: (e, 0)),
            pl.BlockSpec((te, d), lambda i, m, e: (e, 0)),
            pl.BlockSpec((1, 1, d), lambda i, m, e: (i, 0, 0)),
            pl.BlockSpec((1, d), lambda i, m, e: (0, 0)),
            pl.BlockSpec((1, d), lambda i, m, e: (0, 0)),
        ],
        out_specs=pl.BlockSpec((1, tt, d), lambda i, m, e: (i, m, 0)),
        out_shape=jax.ShapeDtypeStruct((b, s, d), jnp.float32),
        scratch_shapes=[pltpu.VMEM((tt, d), jnp.bfloat16), pltpu.VMEM((tt, d), jnp.float32)],
        compiler_params=pltpu.CompilerParams(
            dimension_semantics=("arbitrary", "arbitrary", "arbitrary"),
            vmem_limit_bytes=VMEM_LIMIT_BYTES),
        name="peer_experts_ln",
    )(x, shift, scale, w3, u_tab, v_tab, gate, ln_g.reshape(1, d), ln_b.reshape(1, d))


def peer_block(x, shift, scale, gate, wq, k1, k2, u_bf, v_bf, ln_g, ln_b):
    b, s, d = x.shape
    q_all = mod_matmul(x, shift, scale, wq).reshape(b * s, -1)
    eid, gsel = peer_topk(q_all, k1, k2)
    w = peer_dense_gates(eid, gsel)
    return peer_experts_ln(x, shift, scale, w, u_bf, v_bf, gate, ln_g, ln_b)


HEAD_LANES = 128


def _short_conv_kernel(x_ref, xp_ref, xn_ref, w_ref, b_ref, o_ref, *, silu, n_l2, n_scaled):
    cb = pl.program_id(1)
    m = pl.program_id(2)
    x = x_ref[0]
    tq, wb = x.shape
    prev_row = jnp.where(m > 0, xp_ref[0][7:8], 0.0)
    next_row = jnp.where(m < pl.num_programs(2) - 1, xn_ref[0][0:1], 0.0)
    row = lax.broadcasted_iota(jnp.int32, x.shape, 0)
    x_m1 = jnp.where(row == 0, prev_row, pltpu.roll(x, 1, 0))
    x_p1 = jnp.where(row == tq - 1, next_row, pltpu.roll(x, tq - 1, 0))
    y = w_ref[0:1] * x_m1 + w_ref[1:2] * x + w_ref[2:3] * x_p1 + b_ref[...]
    if silu:
        y = y * jax.nn.sigmoid(y)
    if n_l2 == 0:
        o_ref[0] = y
        return
    hpb = wb // HEAD_LANES
    for hh in range(hpb):
        gh = cb * hpb + hh
        seg = y[:, hh * HEAD_LANES:(hh + 1) * HEAD_LANES]
        inv = lax.rsqrt(jnp.sum(seg * seg, axis=-1, keepdims=True) + EPS)
        f = jnp.where(gh < n_l2, inv, 1.0) * jnp.where(gh < n_scaled, C_DK ** -0.5, 1.0)
        o_ref[0, :, hh * HEAD_LANES:(hh + 1) * HEAD_LANES] = seg * f


def short_conv(p, col0, width, w, bias=None, silu=False, n_l2=0, n_scaled=0, wb=768, tq=512):
    b, l, _ = p.shape
    tq = min(tq, l)
    bias2 = (jnp.zeros((width,), jnp.float32) if bias is None else bias).reshape(1, width)
    c0 = col0 // wb
    kern = functools.partial(_short_conv_kernel, silu=silu, n_l2=n_l2, n_scaled=n_scaled)
    r8 = tq // 8
    return pl.pallas_call(
        kern,
        grid=(b, width // wb, l // tq),
        in_specs=[
            pl.BlockSpec((1, tq, wb), lambda i, c, m: (i, m, c0 + c)),
            pl.BlockSpec((1, 8, wb), lambda i, c, m: (i, jnp.maximum(m * r8 - 1, 0), c0 + c)),
            pl.BlockSpec((1, 8, wb), lambda i, c, m: (i, jnp.minimum((m + 1) * r8, l // 8 - 1), c0 + c)),
            pl.BlockSpec((3, wb), lambda i, c, m: (0, c)),
            pl.BlockSpec((1, wb), lambda i, c, m: (0, c)),
        ],
        out_specs=pl.BlockSpec((1, tq, wb), lambda i, c, m: (i, m, c)),
        out_shape=jax.ShapeDtypeStruct((b, l, width), jnp.float32),
        compiler_params=pltpu.CompilerParams(
            dimension_semantics=("arbitrary",) * 3, vmem_limit_bytes=VMEM_LIMIT_BYTES),
        name="short_conv",
    )(p, p, p, w, bias2)


def _dot3(a, b):
    ah = a.astype(jnp.bfloat16)
    bh = b.astype(jnp.bfloat16)
    al = (a - ah.astype(jnp.float32)).astype(jnp.bfloat16)
    bl = (b - bh.astype(jnp.float32)).astype(jnp.bfloat16)
    d = functools.partial(jnp.dot, preferred_element_type=jnp.float32)
    return d(ah, bh) + (d(ah, bl) + d(al, bh))


def _gdn_chunk_kernel(qkv_ref, beta_ref, g_ref, u_ref, w_ref, qd_ref, kd_ref, in_ref, gl_ref, *, nc):
    d = pl.program_id(0)
    cs = GDN_CHUNK
    ii = lax.broadcasted_iota(jnp.int32, (cs, cs), 0)
    jj = lax.broadcasted_iota(jnp.int32, (cs, cs), 1)
    lo = (ii - jj) * (1 - 2 * d)
    incl = lo >= 0
    strict = lo > 0
    tri = jnp.where(incl, 1.0, 0.0).astype(jnp.bfloat16)
    tri3 = jnp.concatenate([tri, tri, tri], axis=1)
    eye = jnp.where(ii == jj, 1.0, 0.0)

    def chunk_pair(cp, carry):
        probs = []
        for c in (2 * cp, 2 * cp + 1):
            rows = pl.ds(pl.multiple_of(c * cs, cs), cs)
            g_c = g_ref[0, 0, rows, :]
            b_c = beta_ref[0, 0, rows, :]
            g_hi = g_c.astype(jnp.bfloat16)
            r1 = g_c - g_hi.astype(jnp.float32)
            g_mid = r1.astype(jnp.bfloat16)
            g_lo = (r1 - g_mid.astype(jnp.float32)).astype(jnp.bfloat16)
            gc = jnp.dot(tri3, jnp.concatenate([g_hi, g_mid, g_lo], axis=0),
                         preferred_element_type=jnp.float32)
            tot = jnp.sum(g_c, axis=0, keepdims=True)
            for h in range(C_HEADS):
                probs.append(dict(c=c, h=h, rows=rows, gc=gc[:, h:h + 1], bt=b_c[:, h:h + 1], tot=tot[:, h:h + 1]))
        for pr in probs:
            h, rows = pr["h"], pr["rows"]
            q = qkv_ref[0, rows, h * HEAD_LANES:(h + 1) * HEAD_LANES]
            k = qkv_ref[0, rows, C_W + h * HEAD_LANES:C_W + (h + 1) * HEAD_LANES]
            kb = k * pr["bt"]
            kq = lax.dot_general(jnp.concatenate([kb, q], axis=0).astype(jnp.bfloat16), k.astype(jnp.bfloat16),
                                 _NT, preferred_element_type=jnp.float32)
            gc_row = jnp.broadcast_to(pr["gc"], (cs, HEAD_LANES)).T[:cs, :]
            dm = jnp.where(incl, jnp.exp(pr["gc"] - gc_row), 0.0)
            x = jnp.where(strict, -(kq[:cs] * dm), 0.0)
            in_ref[0, 0, pr["c"], h] = (kq[cs:] * dm).astype(in_ref.dtype)
            pr.update(t=eye + x, pw=x)
        for _ in range(5):
            for pr in probs:
                pr["pw"] = _dot3(pr["pw"], pr["pw"])
            for pr in probs:
                pr["t"] = pr["t"] + _dot3(pr["t"], pr["pw"])
        for pr in probs:
            h, rows = pr["h"], pr["rows"]
            lanes = slice(h * HEAD_LANES, (h + 1) * HEAD_LANES)
            q = qkv_ref[0, rows, h * HEAD_LANES:(h + 1) * HEAD_LANES]
            k = qkv_ref[0, rows, C_W + h * HEAD_LANES:C_W + (h + 1) * HEAD_LANES]
            v = qkv_ref[0, rows, 2 * C_W + h * HEAD_LANES:2 * C_W + (h + 1) * HEAD_LANES]
            eg = jnp.exp(pr["gc"])
            uw = _dot3(pr["t"], jnp.concatenate([v * pr["bt"], k * (pr["bt"] * eg)], axis=1))
            u_ref[0, 0, rows, lanes] = uw[:, :HEAD_LANES]
            w_ref[0, 0, rows, lanes] = uw[:, HEAD_LANES:].astype(w_ref.dtype)
            qd_ref[0, 0, rows, lanes] = (q * eg).astype(qd_ref.dtype)
            kd_ref[0, 0, rows, lanes] = (k * jnp.exp(pr["tot"] - pr["gc"])).astype(kd_ref.dtype)
            gl_ref[0, 0, pr["c"], h:h + 1, :] = jnp.broadcast_to(jnp.exp(pr["tot"]), (1, HEAD_LANES))
        return carry

    lax.fori_loop(0, nc // 2, chunk_pair, 0)


def gdn_chunk_prep(qkv, beta, g, nc=4):
    b, l, _ = qkv.shape
    cs = GDN_CHUNK
    tq = nc * cs
    nchunks = l // cs
    bf = jnp.bfloat16
    big = lambda dt: jax.ShapeDtypeStruct((2, b, l, C_W), dt)
    bspec = pl.BlockSpec((1, 1, tq, C_W), lambda d, i, m: (d, i, m, 0))
    gspec = pl.BlockSpec((1, 1, tq, C_HEADS), lambda d, i, m: (d, i, m, 0))
    return pl.pallas_call(
        functools.partial(_gdn_chunk_kernel, nc=nc),
        grid=(2, b, l // tq),
        in_specs=[pl.BlockSpec((1, tq, 3 * C_W), lambda d, i, m: (i, m, 0)), gspec, gspec],
        out_specs=[bspec, bspec, bspec, bspec,
                   pl.BlockSpec((1, 1, nc, C_HEADS, cs, cs), lambda d, i, m: (d, i, m, 0, 0, 0)),
                   pl.BlockSpec((1, 1, nc, C_HEADS, HEAD_LANES), lambda d, i, m: (d, i, m, 0, 0))],
        out_shape=[big(jnp.float32), big(bf), big(bf), big(bf),
                   jax.ShapeDtypeStruct((2, b, nchunks, C_HEADS, cs, cs), bf),
                   jax.ShapeDtypeStruct((2, b, nchunks, C_HEADS, HEAD_LANES), jnp.float32)],
        compiler_params=pltpu.CompilerParams(
            dimension_semantics=("arbitrary",) * 3, vmem_limit_bytes=VMEM_LIMIT_BYTES),
        name="gdn_chunk_prep",
    )(qkv, beta, g)


def _gdn_scan_kernel(*refs):
    ins, (of_ref, ob_ref, s_ref) = refs[:12], refs[12:]
    step = pl.program_id(1)

    @pl.when(step == 0)
    def _():
        s_ref[...] = jnp.zeros_like(s_ref)

    dot = functools.partial(jnp.dot, preferred_element_type=jnp.float32)
    seqs = [(d, h, slice(h * HEAD_LANES, (h + 1) * HEAD_LANES)) for d in range(2) for h in range(C_HEADS)]
    outs = (of_ref, ob_ref)
    sb, vb = {}, {}
    for d, h, lanes in seqs:
        sb[d, h] = s_ref[d * C_HEADS + h].astype(jnp.bfloat16)
    for d, h, lanes in seqs:
        u_ref, w_ref = ins[6 * d], ins[6 * d + 1]
        vb[d, h] = (u_ref[0, 0, :, lanes] - dot(w_ref[0, 0, :, lanes], sb[d, h])).astype(jnp.bfloat16)
    for d, h, lanes in seqs:
        qd_ref, in_ref = ins[6 * d + 2], ins[6 * d + 4]
        outs[d][0, :, lanes] = dot(qd_ref[0, 0, :, lanes], sb[d, h]) + dot(in_ref[0, 0, 0, h], vb[d, h])
    for d, h, lanes in seqs:
        kd_ref, gl_ref = ins[6 * d + 3], ins[6 * d + 5]
        s_ref[d * C_HEADS + h] = s_ref[d * C_HEADS + h] * gl_ref[0, 0, 0, h:h + 1, :] + lax.dot_general(
            kd_ref[0, 0, :, lanes], vb[d, h], (((0,), (0,)), ((), ())), preferred_element_type=jnp.float32)


def gdn_scan(u, w, qd, kd, intra, gl, n_ctx_chunks):
    _, b, l, _ = u.shape
    cs = GDN_CHUNK
    nchunks = l // cs

    def chunk_of(d, s):
        if d == 0:
            return s
        return jnp.where(s < n_ctx_chunks, n_ctx_chunks - 1 - s, nchunks - 1 + n_ctx_chunks - s)

    in_specs, args = [], []
    for d in range(2):
        big = pl.BlockSpec((1, 1, cs, C_W), lambda i, s, d=d: (d, i, chunk_of(d, s), 0))
        in_specs += [big, big, big, big,
                     pl.BlockSpec((1, 1, 1, C_HEADS, cs, cs), lambda i, s, d=d: (d, i, chunk_of(d, s), 0, 0, 0)),
                     pl.BlockSpec((1, 1, 1, C_HEADS, HEAD_LANES), lambda i, s, d=d: (d, i, chunk_of(d, s), 0, 0))]
        args += [u, w, qd, kd, intra, gl]
    out_specs = [pl.BlockSpec((1, cs, C_W), lambda i, s, d=d: (i, chunk_of(d, s), 0)) for d in range(2)]
    return pl.pallas_call(
        _gdn_scan_kernel,
        grid=(b, nchunks),
        in_specs=in_specs,
        out_specs=out_specs,
        out_shape=[jax.ShapeDtypeStruct((b, l, C_W), jnp.float32)] * 2,
        scratch_shapes=[pltpu.VMEM((2 * C_HEADS, C_DK, C_DV), jnp.float32)],
        compiler_params=pltpu.CompilerParams(
            dimension_semantics=("arbitrary",) * 2, vmem_limit_bytes=VMEM_LIMIT_BYTES),
        name="gdn_scan",
    )(*args)


def _gdn_gate_kernel(of_ref, ob_ref, z_ref, gw_ref, y_ref):
    o = of_ref[0] + ob_ref[0]
    z = z_ref[0]
    for h in range(C_HEADS):
        lanes = slice(h * HEAD_LANES, (h + 1) * HEAD_LANES)
        oh = o[:, lanes]
        zh = z[:, lanes]
        n = oh * lax.rsqrt(jnp.mean(oh * oh, axis=-1, keepdims=True) + EPS) * gw_ref[...]
        y_ref[0, :, lanes] = (n * (zh * jax.nn.sigmoid(zh))).astype(y_ref.dtype)


def gdn_gate(o_f, o_b, row0, p, gnorm_w, tq=256):
    b, l, _ = p.shape
    tq = min(tq, l)
    r0 = row0 // tq
    ospec = pl.BlockSpec((1, tq, C_W), lambda i, m: (i, r0 + m, 0))
    return pl.pallas_call(
        _gdn_gate_kernel,
        grid=(b, l // tq),
        in_specs=[ospec, ospec,
                  pl.BlockSpec((1, tq, C_W), lambda i, m: (i, m, 3)),
                  pl.BlockSpec((1, HEAD_LANES), lambda i, m: (0, 0))],
        out_specs=pl.BlockSpec((1, tq, C_W), lambda i, m: (i, m, 0)),
        out_shape=jax.ShapeDtypeStruct((b, l, C_W), jnp.bfloat16),
        compiler_params=pltpu.CompilerParams(
            dimension_semantics=("arbitrary",) * 2, vmem_limit_bytes=VMEM_LIMIT_BYTES),
        name="gdn_gate",
    )(o_f, o_b, p, gnorm_w.reshape(1, HEAD_LANES))


def gdn_mixer(p, pc, conv_w, a_log, dt_bias, gnorm_w, with_ctx):
    lc = pc.shape[1]
    conv = functools.partial(short_conv, col0=0, width=3 * C_W, w=conv_w, silu=True,
                             n_l2=2 * C_HEADS, n_scaled=C_HEADS)
    qkv = jnp.concatenate([conv(pc), conv(p)], axis=1)
    gates = jnp.concatenate([pc[..., -C_GATES:], p[..., -C_GATES:]], axis=1)
    gates = gates.reshape(gates.shape[0], gates.shape[1], 4, C_HEADS)
    beta = jax.nn.sigmoid(gates[:, :, :2])
    g = -jnp.exp(a_log) * jax.nn.softplus(gates[:, :, 2:] + dt_bias)
    beta = jnp.moveaxis(beta, 2, 0)
    g = jnp.moveaxis(g, 2, 0)
    u, w, qd, kd, intra, gl = gdn_chunk_prep(qkv, beta, g)
    o_f, o_b = gdn_scan(u, w, qd, kd, intra, gl, lc // GDN_CHUNK)
    out = gdn_gate(o_f, o_b, lc, p, gnorm_w)
    out_c = gdn_gate(o_f, o_b, 0, pc, gnorm_w) if with_ctx else None
    return out, out_c


FFT_R = 128
FFT_N = FFT_R * FFT_R
SUB = 8


def stage_a_table():
    idx = np.arange(FFT_R)
    ang = 2.0 * np.pi * np.outer(idx, idx) / FFT_R
    return jnp.asarray(np.stack([np.cos(ang), -np.sin(ang)], axis=1).reshape(2 * FFT_R, FFT_R), jnp.float32)


def _fft_stage_a_kernel(x_ref, l_ref, y_ref):
    l = l_ref[...]
    c = x_ref.shape[-1]
    xs = jnp.swapaxes(x_ref[0], 0, 1)
    ys = jnp.stack([_dot3(l, xs[j]) for j in range(SUB)], axis=0)
    y_ref[0] = jnp.swapaxes(ys, 0, 1).reshape(FFT_R, 2, SUB, c)


def fft_stage_a(x, col_blk, width, stage_a):
    b, l, wtot = x.shape
    n1cnt = l // FFT_R
    x4 = x.reshape(b, n1cnt, FFT_R, wtot)
    return pl.pallas_call(
        _fft_stage_a_kernel,
        grid=(b, FFT_R // SUB),
        in_specs=[pl.BlockSpec((1, n1cnt, SUB, width), lambda i, j: (i, 0, j, col_blk)),
                  pl.BlockSpec((2 * FFT_R, n1cnt), lambda i, j: (0, 0))],
        out_specs=pl.BlockSpec((1, FFT_R, 2, SUB, width), lambda i, j: (i, 0, 0, j, 0)),
        out_shape=jax.ShapeDtypeStruct((b, FFT_R, 2, FFT_R, width), jnp.float32),
        compiler_params=pltpu.CompilerParams(
            dimension_semantics=("arbitrary",) * 2, vmem_limit_bytes=VMEM_LIMIT_BYTES),
        name="fft_stage_a",
    )(x4, stage_a[:, :n1cnt])


def _dot3_presplit(ah, al, b):
    bh = b.astype(jnp.bfloat16)
    bl = (b - bh.astype(jnp.float32)).astype(jnp.bfloat16)
    d = functools.partial(jnp.dot, preferred_element_type=jnp.float32)
    return d(ah, bh) + (d(ah, bl) + d(al, bh))


def stage_b_tables():
    r = FFT_R
    k1 = jnp.arange(r, dtype=jnp.int32)[:, None, None]
    k2 = jnp.arange(r, dtype=jnp.int32)[None, :, None]
    n2 = jnp.arange(r, dtype=jnp.int32)[None, None, :]
    th = ((n2 * (r * k2 + k1)) % FFT_N).astype(jnp.float32) * (2.0 * math.pi / FFT_N)
    c, s = jnp.cos(th), jnp.sin(th)
    t = jnp.concatenate([jnp.concatenate([c, s], 2), jnp.concatenate([-s, c], 2)], 1)

    def split(m):
        hi = m.astype(jnp.bfloat16)
        return hi, (m - hi.astype(jnp.float32)).astype(jnp.bfloat16)

    return split(t) + split(jnp.swapaxes(t, 1, 2))


def _fft_mid_kernel(y_ref, h_ref, th_ref, tl_ref, ih_ref, il_ref, o_ref, *, conv):
    nb, r, c = y_ref.shape[0], FFT_R, y_ref.shape[-1]
    xs = [_dot3_presplit(th_ref[0], tl_ref[0], y_ref[i, 0].reshape(2 * r, c)) for i in range(nb)]
    if not conv:
        for i, x in enumerate(xs):
            o_ref[i, 0] = (x * ((1.0 / FFT_N) / h_ref[i])).reshape(2, r, c)
        return
    hr, hi = h_ref[0, 0, 0], h_ref[0, 0, 1]
    ps = [jnp.concatenate([x[:r] * hr - x[r:] * hi, x[:r] * hi + x[r:] * hr], axis=0) for x in xs]
    for i, p in enumerate(ps):
        o_ref[i, 0] = _dot3_presplit(ih_ref[0], il_ref[0], p).reshape(2, r, c)


def fft_mid(y, h, tables, conv, order=0):
    b, r, _, _, c = y.shape
    hh = h if conv else h.reshape(b, 1, c)
    hspec = (pl.BlockSpec((1, 1, 2, r, c), lambda k: (order, k, 0, 0, 0)) if conv
             else pl.BlockSpec((b, 1, c), lambda k: (0, 0, 0)))
    blk = pl.BlockSpec((b, 1, 2, r, c), lambda k: (0, k, 0, 0, 0))
    tspec = pl.BlockSpec((1, 2 * r, 2 * r), lambda k: (k, 0, 0))
    return pl.pallas_call(
        functools.partial(_fft_mid_kernel, conv=conv),
        grid=(r,),
        in_specs=[blk, hspec, tspec, tspec, tspec, tspec],
        out_specs=blk,
        out_shape=jax.ShapeDtypeStruct(y.shape, jnp.float32),
        compiler_params=pltpu.CompilerParams(
            dimension_semantics=("arbitrary",), vmem_limit_bytes=VMEM_LIMIT_BYTES),
        name="fft_mid",
    )(y, hh, *tables)


def _fft_out_kernel(b_ref, l_ref, xg_ref, xin_ref, bias_ref, o_ref):
    l = l_ref[...]
    c = o_ref.shape[-1]
    bs = jnp.swapaxes(b_ref[0].reshape(2 * FFT_R, SUB, c), 0, 1)
    ys = jnp.stack([_dot3(l, bs[j]) for j in range(SUB)], axis=0)
    y = jnp.swapaxes(ys, 0, 1)
    o_ref[0] = xg_ref[0] * (y + bias_ref[...] * xin_ref[0])


def fft_out_gate(bm, stage_a, xg, xg_blk, xin, xin_blk, bias):
    b, r, _, _, c = bm.shape
    l = xg.shape[1]
    n1cnt = l // r
    view = lambda t: t.reshape(b, n1cnt, r, t.shape[-1])
    lhs = stage_a.T[:n1cnt]
    return pl.pallas_call(
        _fft_out_kernel,
        grid=(b, r // SUB),
        in_specs=[pl.BlockSpec((1, r, 2, SUB, c), lambda i, j: (i, 0, 0, j, 0)),
                  pl.BlockSpec((n1cnt, 2 * r), lambda i, j: (0, 0)),
                  pl.BlockSpec((1, n1cnt, SUB, c), lambda i, j: (i, 0, j, xg_blk)),
                  pl.BlockSpec((1, n1cnt, SUB, c), lambda i, j: (i, 0, j, xin_blk)),
                  pl.BlockSpec((1, c), lambda i, j: (0, 0))],
        out_specs=pl.BlockSpec((1, n1cnt, SUB, c), lambda i, j: (i, 0, j, 0)),
        out_shape=jax.ShapeDtypeStruct((b, n1cnt, r, c), jnp.float32),
        compiler_params=pltpu.CompilerParams(
            dimension_semantics=("arbitrary",) * 2, vmem_limit_bytes=VMEM_LIMIT_BYTES),
        name="fft_out_gate",
    )(bm, lhs, view(xg), view(xin), bias.reshape(1, c)).reshape(b, l, c)


def _direct_conv_kernel(xin_ref, xg_ref, kern_ref, d1_ref, d2_ref, norm_ref, bias_ref, o_ref):
    n = xin_ref.shape[1]
    d1 = d1_ref[...]
    x = xin_ref[0]
    xs = _dot3(d1[:, :n], x)
    hs = _dot3(d1, kern_ref[...]) / norm_ref[...]
    xr, xi, hr, hi = xs[:2 * n], xs[2 * n:], hs[:2 * n], hs[2 * n:]
    p = jnp.concatenate([xr * hr - xi * hi, xr * hi + xi * hr], axis=0)
    y = _dot3(d2_ref[...], p)
    o_ref[0] = xg_ref[0] * (y + bias_ref[...] * x)


def direct_long_conv(xin, xin_blk, xg, xg_blk, kern, norm, bias):
    b, n, _ = xin.shape
    c = kern.shape[1]
    idx = np.arange(2 * n)
    ang = 2.0 * np.pi * np.outer(idx, idx) / (2 * n)
    d1 = jnp.asarray(np.concatenate([np.cos(ang), -np.sin(ang)], axis=0), jnp.float32)
    d2 = jnp.asarray(np.concatenate([np.cos(ang[:n]), -np.sin(ang[:n])], axis=1) / (2 * n), jnp.float32)
    return pl.pallas_call(
        _direct_conv_kernel,
        grid=(b,),
        in_specs=[pl.BlockSpec((1, n, c), lambda i: (i, 0, xin_blk)),
                  pl.BlockSpec((1, n, c), lambda i: (i, 0, xg_blk)),
                  pl.BlockSpec((2 * n, c), lambda i: (0, 0)),
                  pl.BlockSpec((4 * n, 2 * n), lambda i: (0, 0)),
                  pl.BlockSpec((n, 4 * n), lambda i: (0, 0)),
                  pl.BlockSpec((1, c), lambda i: (0, 0)),
                  pl.BlockSpec((1, c), lambda i: (0, 0))],
        out_specs=pl.BlockSpec((1, n, c), lambda i: (i, 0, 0)),
        out_shape=jax.ShapeDtypeStruct((b, n, c), jnp.float32),
        compiler_params=pltpu.CompilerParams(
            dimension_semantics=("arbitrary",), vmem_limit_bytes=VMEM_LIMIT_BYTES),
        name="direct_long_conv",
    )(xin, xg, kern, d1, d2, norm.reshape(1, c), bias.reshape(1, c))


def _hy_filter_kernel(w1_ref, b1_ref, fr_ref, w2_ref, b2_ref, w3_ref, dl_ref, k_ref, s_ref, *, n):
    i = pl.program_id(0)
    tp, c = k_ref.shape[1], k_ref.shape[2]

    @pl.when(i == 0)
    def _():
        s_ref[...] = jnp.zeros_like(s_ref)

    def pos(shape):
        idx = i * tp + lax.broadcasted_iota(jnp.int32, shape, 0)
        t = jnp.where(idx < n, idx, jnp.where(idx == n, 0, 2 * n - idx))
        return idx, t.astype(jnp.float32)

    _, t = pos((tp, LANES))
    lane = lax.broadcasted_iota(jnp.int32, (tp, LANES), 1)
    band = jnp.where(lane <= HY_BANDS, lane, lane - HY_BANDS).astype(jnp.float32)
    ang = 2.0 * math.pi * t * band / n
    feat = jnp.where(lane == 0, t / n,
                     jnp.where(lane <= HY_BANDS, jnp.sin(ang), jnp.where(lane < HY_EMB, jnp.cos(ang), 0.0)))
    hid = jnp.sin(fr_ref[...] * (_dot3(feat, w1_ref[...]) + b1_ref[...]))
    hid = jnp.sin(fr_ref[...] * (_dot3(hid, w2_ref[...]) + b2_ref[...]))
    f = _dot3(hid, w3_ref[...])
    idx, t = pos((tp, c))
    decay = jnp.exp(-(t / n) * dl_ref[...])
    for o in range(HY_ORDER):
        fwd = f[:, (2 * o) * c:(2 * o + 1) * c]
        bwd = f[:, (2 * o + 1) * c:(2 * o + 2) * c]
        val = jnp.where(idx < n, fwd, bwd) * decay
        s_ref[o:o + 1, :] += jnp.sum(jnp.abs(val), axis=0, keepdims=True)
        k_ref[o] = jnp.where(idx == n, 0.0, val)


def hyena_kernels(n, w1, b1, freq, w2, b2, w3):
    c = HY_CH
    tp = min(512, n)
    max_decay = math.log(HY_TARGET) / HY_FAST_DECAY
    min_decay = math.log(HY_TARGET) / HY_SLOW_DECAY
    deltas = jnp.abs(jnp.linspace(min_decay, max_decay, c, dtype=jnp.float32)).reshape(1, c)
    w1p = jnp.zeros((LANES, w1.shape[1]), jnp.float32).at[:w1.shape[0]].set(w1)
    hd = w1.shape[1]
    full = lambda shape: pl.BlockSpec(shape, lambda i: (0,) * len(shape))
    return pl.pallas_call(
        functools.partial(_hy_filter_kernel, n=n),
        grid=(2 * n // tp,),
        in_specs=[full((LANES, hd)), full((1, hd)), full((1, hd)), full((hd, hd)), full((1, hd)),
                  full((hd, HY_ORDER * 2 * c)), full((1, c))],
        out_specs=[pl.BlockSpec((HY_ORDER, tp, c), lambda i: (0, i, 0)), full((HY_ORDER, c))],
        out_shape=[jax.ShapeDtypeStruct((HY_ORDER, 2 * n, c), jnp.float32),
                   jax.ShapeDtypeStruct((HY_ORDER, c), jnp.float32)],
        compiler_params=pltpu.CompilerParams(
            dimension_semantics=("arbitrary",), vmem_limit_bytes=VMEM_LIMIT_BYTES),
        name="hyena_kernels",
    )(w1p, b1.reshape(1, hd), freq.reshape(1, hd), w2, b2.reshape(1, hd), w3, deltas)


def hyena_mixer(p, col0, conv_w, conv_b, filt_args, hy_bias):
    n = p.shape[1]
    uc = short_conv(p, col0, 3 * HY_CH, conv_w, conv_b)
    kerns, norm = hyena_kernels(n, *filt_args)
    if 2 * n != FFT_N:
        v = direct_long_conv(uc, 2, uc, 0, kerns[0], norm[0], hy_bias[0])
        return direct_long_conv(v, 0, uc, 1, kerns[1], norm[1], hy_bias[1])
    stage_a, stage_b = stage_a_table(), stage_b_tables()
    spec = fft_mid(fft_stage_a(kerns, 0, HY_CH, stage_a), norm, stage_b, conv=False)
    v = fft_out_gate(fft_mid(fft_stage_a(uc, 2, HY_CH, stage_a), spec, stage_b, conv=True, order=0),
                     stage_a, uc, 0, uc, 2, hy_bias[0])
    return fft_out_gate(fft_mid(fft_stage_a(v, 0, HY_CH, stage_a), spec, stage_b, conv=True, order=1),
                        stage_a, uc, 1, v, 0, hy_bias[1])


def rms_norm(x, w):
    xf = x.astype(jnp.float32)
    return (xf * lax.rsqrt(jnp.mean(xf * xf, -1, keepdims=True) + EPS) * w).astype(x.dtype)


def l2_normalize(x):
    xf = x.astype(jnp.float32)
    return (xf * lax.rsqrt(jnp.sum(xf * xf, -1, keepdims=True) + EPS)).astype(x.dtype)


def axial_rope_tables(n_tok):
    rows = n_tok // GRID_W
    row = jnp.repeat(jnp.arange(rows, dtype=jnp.float32), GRID_W)
    col = jnp.tile(jnp.arange(GRID_W, dtype=jnp.float32), rows)
    nf = HEAD_DIM // 4
    inv = ROPE_BASE ** (-jnp.arange(nf, dtype=jnp.float32) / nf)
    ang = jnp.concatenate([row[:, None] * inv, col[:, None] * inv], -1)
    return jnp.cos(ang), jnp.sin(ang)


def apply_axial_rope(x, cos, sin):
    nf = HEAD_DIM // 4
    c = cos[:, None, :]
    s = sin[:, None, :]
    parts = []
    for a in range(2):
        xa = x[..., a * 2 * nf:(a + 1) * 2 * nf]
        x1, x2 = xa[..., :nf], xa[..., nf:]
        ca, sa = c[..., a * nf:(a + 1) * nf], s[..., a * nf:(a + 1) * nf]
        parts += [x1 * ca - x2 * sa, x2 * ca + x1 * sa]
    return jnp.concatenate(parts, -1).astype(x.dtype)


def dwconv(x, w):
    k, ch = w.shape
    return lax.conv_general_dilated(x, w[:, None, :].astype(x.dtype), (1,), [(k // 2, k // 2)],
                                    dimension_numbers=('NWC', 'WIO', 'NWC'), feature_group_count=ch)


def gqa_softmax(q, k, v, sink=None):
    b, lq, h, hd = q.shape
    kvh = k.shape[2]
    g = h // kvh
    lk = k.shape[1]
    s = jnp.einsum('bqkgd,bjkd->bkgqj', q.reshape(b, lq, kvh, g, hd), k).astype(jnp.float32) * hd ** -0.5
    if sink is not None:
        s = jnp.concatenate([s, jnp.broadcast_to(sink.astype(jnp.float32).reshape(kvh, g, 1, 1), s.shape[:-1] + (1,))], -1)
    p = jax.nn.softmax(s, axis=-1)[..., :lk].astype(v.dtype)
    return jnp.einsum('bkgqj,bjkd->bqkgd', p, v).reshape(b, lq, h * hd)


def windowed_sink_attention(q, k, v, kc, vc, sink):
    b, s, h, hd = q.shape
    kvh = k.shape[2]
    g = h // kvh
    nb = s // BLOCK
    lc = kc.shape[1]
    w3 = 3 * BLOCK
    scale = hd ** -0.5
    qb = q.reshape(b, nb, BLOCK, kvh, g, hd).swapaxes(0, 1)

    def band(t):
        tb = jnp.pad(t.reshape(b, nb, BLOCK, kvh, hd), ((0, 0), (1, 1), (0, 0), (0, 0), (0, 0)))
        return jnp.concatenate([tb[:, :-2], tb[:, 1:-1], tb[:, 2:]], axis=2).swapaxes(0, 1)

    kw, vw = band(k), band(v)
    blk = jnp.arange(nb)[:, None, None]
    qpos = blk * BLOCK + jnp.arange(BLOCK)[None, :, None]
    kpos = (blk - 1) * BLOCK + jnp.arange(w3)[None, None, :]
    valid = (jnp.abs(qpos - kpos) <= WINDOW) & (kpos >= 0) & (kpos < s)
    sink_logit = sink.astype(jnp.float32).reshape(kvh, g, 1, 1)

    def one_block(args):
        qk, kk, vk, vm = args
        s_loc = jnp.einsum('bqkgd,bjkd->bkgqj', qk, kk).astype(jnp.float32) * scale
        s_loc = jnp.where(vm, s_loc, -jnp.inf)
        s_ctx = jnp.einsum('bqkgd,bjkd->bkgqj', qk, kc).astype(jnp.float32) * scale
        s_snk = jnp.broadcast_to(sink_logit, s_loc.shape[:-1] + (1,))
        p = jax.nn.softmax(jnp.concatenate([s_loc, s_ctx, s_snk], -1), axis=-1).astype(v.dtype)
        return (jnp.einsum('bkgqj,bjkd->bqkgd', p[..., :w3], vk)
                + jnp.einsum('bkgqj,bjkd->bqkgd', p[..., w3:w3 + lc], vc))

    o = lax.map(one_block, (qb, kw, vw, valid))
    return o.swapaxes(0, 1).reshape(b, s, h * hd)


def global_block_attention(q, k, v, kc, vc):
    b, s, h, hd = q.shape
    nb = s // BLOCK
    k_all = jnp.concatenate([k, kc], 1)
    v_all = jnp.concatenate([v, vc], 1)
    qb = q.reshape(b, nb, BLOCK, h, hd).swapaxes(0, 1)
    o = lax.map(lambda qk: gqa_softmax(qk, k_all, v_all), qb)
    return o.swapaxes(0, 1).reshape(b, s, h * hd)


def hyena_filters(n, w1, b1, freq, w2, b2, w3):
    t = jnp.arange(n, dtype=jnp.float32)
    tn = t / n
    f = jnp.arange(1, HY_BANDS + 1, dtype=jnp.float32)
    ang = 2.0 * math.pi * t[:, None] * f[None, :] / n
    feat = jnp.concatenate([tn[:, None], jnp.sin(ang), jnp.cos(ang)], -1)
    hid = jnp.sin(freq * (feat @ w1 + b1))
    hid = jnp.sin(freq * (hid @ w2 + b2))
    filt = (hid @ w3).astype(jnp.float32).reshape(n, HY_ORDER, 2, HY_CH)
    max_decay = math.log(HY_TARGET) / HY_FAST_DECAY
    min_decay = math.log(HY_TARGET) / HY_SLOW_DECAY
    deltas = jnp.abs(jnp.linspace(min_decay, max_decay, HY_CH, dtype=jnp.float32))
    filt = filt * jnp.exp(-tn[:, None, None, None] * deltas)
    return filt / jnp.sum(jnp.abs(filt), axis=(0, 2), keepdims=True)


def fft_long_conv(x, hf, hb, bias):
    n = x.shape[1]
    kern = jnp.concatenate([hf, jnp.zeros_like(hf[:1]), hb[1:][::-1]], 0)
    kf = jnp.fft.rfft(kern, n=2 * n, axis=0)
    xf = jnp.fft.rfft(x.astype(jnp.float32), n=2 * n, axis=1)
    y = jnp.fft.irfft(xf * kf[None], n=2 * n, axis=1)[:, :n]
    return (y + x.astype(jnp.float32) * bias.astype(jnp.float32)).astype(x.dtype)


def hyena_operator(u, conv_w, conv_b, fw1, fb1, ffreq, fw2, fb2, fw3, hy_bias):
    uc = dwconv(u, conv_w) + conv_b
    x1, x2, v = uc[..., :HY_CH], uc[..., HY_CH:2 * HY_CH], uc[..., 2 * HY_CH:]
    filt = hyena_filters(u.shape[1], fw1, fb1, ffreq, fw2, fb2, fw3)
    v = x1 * fft_long_conv(v, filt[:, 0, 0], filt[:, 0, 1], hy_bias[0])
    v = x2 * fft_long_conv(v, filt[:, 1, 0], filt[:, 1, 1], hy_bias[1])
    return v


def gated_delta_chunked(q, k, v, beta, g, s0):
    b, n_tok, h, dk = q.shape
    dv = v.shape[-1]
    cs = GDN_CHUNK
    n = n_tok // cs

    def blk(t):
        return t.astype(jnp.float32).reshape(b, n, cs, h, -1).transpose(1, 0, 3, 2, 4)

    q = blk(q) * dk ** -0.5
    k = blk(k)
    v = blk(v)
    beta = blk(beta[..., None])[..., 0]
    gcum = jnp.cumsum(blk(g[..., None])[..., 0], -1)
    idx = jnp.arange(cs)
    incl = idx[:, None] >= idx[None, :]
    strict = idx[:, None] > idx[None, :]
    decay = jnp.exp(jnp.where(incl, gcum[..., :, None] - gcum[..., None, :], -jnp.inf))
    kb = k * beta[..., None]
    a = jnp.where(strict, jnp.einsum('nbhid,nbhjd->nbhij', kb, k) * decay, 0.0)
    rhs = jnp.concatenate([v * beta[..., None], kb * jnp.exp(gcum)[..., None]], -1)
    sol = lax.linalg.triangular_solve(jnp.eye(cs, dtype=jnp.float32) + a, rhs, left_side=True, lower=True)
    u, w = sol[..., :dv], sol[..., dv:]
    intra = jnp.einsum('nbhid,nbhjd->nbhij', q, k) * decay
    q_dec = q * jnp.exp(gcum)[..., None]
    k_dec = k * jnp.exp(gcum[..., -1:] - gcum)[..., None]
    g_last = jnp.exp(gcum[..., -1])

    def step(state, xs):
        u_i, w_i, q_i, k_i, intra_i, gl = xs
        v_new = u_i - jnp.einsum('bhcd,bhde->bhce', w_i, state)
        o_i = jnp.einsum('bhcd,bhde->bhce', q_i, state) + jnp.einsum('bhij,bhje->bhie', intra_i, v_new)
        state = state * gl[..., None, None] + jnp.einsum('bhcd,bhce->bhde', k_i, v_new)
        return state, o_i

    s_final, o = lax.scan(step, s0.astype(jnp.float32), (u, w, q_dec, k_dec, intra, g_last))
    return o.transpose(1, 0, 3, 2, 4).reshape(b, n_tok, h, dv), s_final


def gdn_inputs(p, conv_w, a_log, dt_bias):
    b, n, _ = p.shape
    qkv = jax.nn.silu(dwconv(p[..., :3 * C_W], conv_w))
    q = l2_normalize(qkv[..., :C_W].reshape(b, n, C_HEADS, C_DK))
    k = l2_normalize(qkv[..., C_W:2 * C_W].reshape(b, n, C_HEADS, C_DK))
    v = qkv[..., 2 * C_W:].reshape(b, n, C_HEADS, C_DV)
    z = p[..., 3 * C_W:4 * C_W].reshape(b, n, C_HEADS, C_DV)
    gates = p[..., -C_GATES:].astype(jnp.float32).reshape(b, n, 4, C_HEADS)
    beta = jax.nn.sigmoid(gates[:, :, :2])
    g = -jnp.exp(a_log.astype(jnp.float32)) * jax.nn.softplus(gates[:, :, 2:] + dt_bias.astype(jnp.float32))
    return q, k, v, z, beta, g


def rev(t, flip):
    return t[:, ::-1] if flip else t


def bidirectional_gdn(lat, cx, with_ctx):
    ql, kl, vl, bl, gl = lat
    qc, kc, vc, bc, gcx = cx
    s0 = jnp.zeros((ql.shape[0], C_HEADS, C_DK, C_DV), jnp.float32)
    o_lat = 0.0
    o_ctx = 0.0
    for d in range(2):
        f = d == 1
        oc, sc = gated_delta_chunked(rev(qc, f), rev(kc, f), rev(vc, f), rev(bc[:, :, d], f), rev(gcx[:, :, d], f), s0)
        ol, _ = gated_delta_chunked(rev(ql, f), rev(kl, f), rev(vl, f), rev(bl[:, :, d], f), rev(gl[:, :, d], f), sc)
        o_lat = o_lat + rev(ol, f)
        if with_ctx:
            o_ctx = o_ctx + rev(oc, f)
    return o_lat, o_ctx


def even_mixer(p, pc, rope_tabs, sink, conv_w, conv_b, fw1, fb1, ffreq, fw2, fb2, fw3, hy_bias, with_ctx):
    q, k, v = qkv_prep(p, 0, A_HEADS, A_KV_HEADS, rope_tabs)
    qc, kc, vc = qkv_prep(pc, 0, A_HEADS, A_KV_HEADS, None)
    o_a = windowed_sink_gqa(q, k, v, kc, vc, sink)
    filt_args = (fw1, fb1, ffreq, fw2, fb2, fw3)
    o_b = hyena_mixer(p, A_Q + 2 * A_KV, conv_w, conv_b, filt_args, hy_bias)
    out_c = None
    if with_ctx:
        o_ac = flash_gqa(qc, kc, vc, sink)
        o_bc = hyena_mixer(pc, A_Q + 2 * A_KV, conv_w, conv_b, filt_args, hy_bias)
        out_c = (o_ac, o_bc)
    return (o_a, o_b), out_c


def odd_mixer(p, pc, rope_tabs, conv_w, a_log, dt_bias, gnorm_w, qnorm_w, knorm_w, with_ctx):
    o_l, o_c = gdn_mixer(p, pc, conv_w, a_log, dt_bias, gnorm_w, with_ctx)
    qd, kd, vd = qkv_prep(p, 4 * C_W, D_HEADS, D_KV_HEADS, rope_tabs, qnorm_w, knorm_w)
    qdc, kdc, vdc = qkv_prep(pc, 4 * C_W, D_HEADS, D_KV_HEADS, None, qnorm_w, knorm_w)
    o_d = flash_gqa(qd, jnp.concatenate([kd, kdc], 2), jnp.concatenate([vd, vdc], 2))
    out_c = None
    if with_ctx:
        out_c = (o_c, flash_gqa(qdc, kdc, vdc))
    return (o_l, o_d), out_c


def kernel(x, c, ctx, c_ctx, ada_w, ada_b, ln1_g, ln1_b, ln2_g, ln2_b, peer_wq, peer_k1, peer_k2, peer_u, peer_v, ev_w_in, ev_w_out, ev_sink, ev_conv_w, ev_conv_b, ev_filt_w1, ev_filt_b1, ev_filt_freq, ev_filt_w2, ev_filt_b2, ev_filt_w3, ev_hy_bias, od_w_in, od_w_out, od_conv_w, od_a_log, od_dt_bias, od_gnorm_w, od_qnorm_w, od_knorm_w):
    rope_tabs = rope_tables(x.shape[1])
    bsz = x.shape[0]
    silu_c = jax.nn.silu(c)
    silu_cc = jax.nn.silu(c_ctx)
    for i in range(DEPTH):
        with_ctx = i < DEPTH - 1
        j = i // 2
        mod = (silu_c @ ada_w[i] + ada_b[i])[:, None, :]
        modc = jnp.broadcast_to((silu_cc @ ada_w[i] + ada_b[i])[None, None, :], (bsz, 1, 6 * D_MODEL))
        sh1, sc1, g1, sh2, sc2, g2 = jnp.split(mod, 6, axis=-1)
        sh1c, sc1c, g1c, sh2c, sc2c, g2c = jnp.split(modc, 6, axis=-1)
        if i % 2 == 0:
            p = mod_matmul(x, sh1, sc1, ev_w_in[j])
            pc = mod_matmul(ctx, sh1c, sc1c, ev_w_in[j])
            out, out_c = even_mixer(p, pc, rope_tabs, ev_sink[j], ev_conv_w[j], ev_conv_b[j],
                                    ev_filt_w1[j], ev_filt_b1[j], ev_filt_freq[j], ev_filt_w2[j], ev_filt_b2[j],
                                    ev_filt_w3[j], ev_hy_bias[j], with_ctx)
            w_out = ev_w_out[j]
        else:
            w_in = od_w_in[j]
            w_in = jnp.concatenate([w_in[:, :4 * C_W], w_in[:, 4 * C_W + C_GATES:],
                                    w_in[:, 4 * C_W:4 * C_W + C_GATES]], axis=1)
            p = mod_matmul(x, sh1, sc1, w_in)
            pc = mod_matmul(ctx, sh1c, sc1c, w_in)
            out, out_c = odd_mixer(p, pc, rope_tabs, od_conv_w[j], od_a_log[j], od_dt_bias[j],
                                   od_gnorm_w[j], od_qnorm_w[j], od_knorm_w[j], with_ctx)
            w_out = od_w_out[j]
        u_bf = peer_u[i].astype(jnp.bfloat16)
        v_bf = peer_v[i].astype(jnp.bfloat16)
        x = proj_residual_ln(out[0], out[1], w_out, x, g1, ln1_g[i], ln1_b[i])
        x = peer_block(x, sh2, sc2, g2, peer_wq[i], peer_k1[i], peer_k2[i], u_bf, v_bf, ln2_g[i], ln2_b[i])
        if with_ctx:
            ctx = proj_residual_ln(out_c[0], out_c[1], w_out, ctx, g1c, ln1_g[i], ln1_b[i])
            ctx = peer_block(ctx, sh2c, sc2c, g2c, peer_wq[i], peer_k1[i], peer_k2[i], u_bf, v_bf,
                             ln2_g[i], ln2_b[i])
    return x
```

```python
import functools
import math

import numpy as np

import jax
import jax.numpy as jnp
from jax import lax
from jax.experimental import pallas as pl
from jax.experimental.pallas import tpu as pltpu

D_MODEL = 1024
DEPTH = 2
GRID_W = 64
HEAD_DIM = 64
BLOCK = 128
ROPE_BASE = 10000.0
EPS = 1e-6

A_HEADS = 8
A_KV_HEADS = 2
WINDOW = 128

HY_CH = 512
HY_ORDER = 2
HY_EMB = 33
HY_BANDS = (HY_EMB - 1) // 2
HY_FAST_DECAY = 0.3
HY_SLOW_DECAY = 1.5
HY_TARGET = 1e-2

C_HEADS = 4
C_DK = 128
C_DV = 128
GDN_CHUNK = 64

D_HEADS = 8
D_KV_HEADS = 2

PEER_HEADS = 8
PEER_NKEYS = 128
PEER_QDIM = 256
PEER_TOPK = 16
PEER_CHUNK = 128

ALPHA = (2 * DEPTH) ** 0.25

A_Q = A_HEADS * HEAD_DIM
A_KV = A_KV_HEADS * HEAD_DIM
C_W = C_HEADS * C_DK
C_GATES = 4 * C_HEADS
D_Q = D_HEADS * HEAD_DIM
D_KV = D_KV_HEADS * HEAD_DIM

VMEM_LIMIT_BYTES = 48 * 1024 * 1024

LANES = 128
_NT = (((1,), (1,)), ((), ()))


def _modmm_kernel(x_ref, sh_ref, sc_ref, w_ref, o_ref):
    h = x_ref[0] * (1.0 + sc_ref[0]) + sh_ref[0]
    o_ref[0] = jnp.dot(h.astype(jnp.bfloat16), w_ref[...], preferred_element_type=jnp.float32)


def mod_matmul(x, shift, scale, w, tm=512, tn=None):
    b, s, k = x.shape
    n = w.shape[1]
    tm = min(tm, s)
    tn = n if tn is None else tn
    wb = w.astype(jnp.bfloat16)
    return pl.pallas_call(
        _modmm_kernel,
        grid=(b, n // tn, s // tm),
        in_specs=[
            pl.BlockSpec((1, tm, k), lambda i, j, m: (i, m, 0)),
            pl.BlockSpec((1, 1, k), lambda i, j, m: (i, 0, 0)),
            pl.BlockSpec((1, 1, k), lambda i, j, m: (i, 0, 0)),
            pl.BlockSpec((k, tn), lambda i, j, m: (0, j)),
        ],
        out_specs=pl.BlockSpec((1, tm, tn), lambda i, j, m: (i, m, j)),
        out_shape=jax.ShapeDtypeStruct((b, s, n), jnp.float32),
        compiler_params=pltpu.CompilerParams(
            dimension_semantics=("arbitrary", "arbitrary", "arbitrary"),
            vmem_limit_bytes=VMEM_LIMIT_BYTES),
        name="mod_matmul",
    )(x, shift, scale, wb)


def _qkv_prep_kernel(q_ref, k_ref, v_ref, cs_ref, sn_ref, qw_ref, kw_ref, gm_ref, qo_ref, ko_ref, vo_ref, *,
                     norm, rope, nq, nkv):
    def prep(x, w, nh):
        if norm:
            ms = jnp.dot(x * x, gm_ref[:x.shape[1], :x.shape[1]], precision=lax.Precision.HIGHEST,
                         preferred_element_type=jnp.float32)
            x = x * lax.rsqrt(ms + EPS) * w
        if rope:
            n = x.shape[1]
            reps = n // cs_ref.shape[1]
            cs = jnp.concatenate([cs_ref[...]] * reps, axis=1) if reps > 1 else cs_ref[...]
            sn = jnp.concatenate([sn_ref[...]] * reps, axis=1) if reps > 1 else sn_ref[...]
            lane = lax.broadcasted_iota(jnp.int32, x.shape, 1)
            nf = HEAD_DIM // 4
            partner = jnp.where((lane & nf) == 0, pltpu.roll(x, n - nf, 1), pltpu.roll(x, nf, 1))
            x = x * cs + partner * sn
        return x

    q = prep(q_ref[0], qw_ref[...], nq) * (HEAD_DIM ** -0.5)
    k = prep(k_ref[0], kw_ref[...], nkv)
    v = v_ref[0]
    for h in range(nq):
        qo_ref[0, h] = q[:, h * HEAD_DIM:(h + 1) * HEAD_DIM].astype(jnp.bfloat16)
    for h in range(nkv):
        ko_ref[0, h] = k[:, h * HEAD_DIM:(h + 1) * HEAD_DIM].astype(jnp.bfloat16)
        vo_ref[0, h] = v[:, h * HEAD_DIM:(h + 1) * HEAD_DIM].astype(jnp.bfloat16)


def qkv_prep(p, col0, nq, nkv, rope_tabs, qw=None, kw=None, tq=512):
    b, s, _ = p.shape
    tq = min(tq, s)
    wq_, wk_ = nq * HEAD_DIM, nkv * HEAD_DIM
    norm = qw is not None
    rope = rope_tabs is not None
    if rope:
        cs, sn = rope_tabs
    else:
        cs = sn = jnp.zeros((s, 2 * HEAD_DIM), jnp.float32)
    qw_t = jnp.tile(qw, nq).reshape(1, wq_) if norm else jnp.ones((1, wq_), jnp.float32)
    kw_t = jnp.tile(kw, nkv).reshape(1, wk_) if norm else jnp.ones((1, wk_), jnp.float32)
    grp = jnp.arange(wq_) // HEAD_DIM
    gm = (grp[:, None] == grp[None, :]).astype(jnp.float32) / HEAD_DIM
    kern = functools.partial(_qkv_prep_kernel, norm=norm, rope=rope, nq=nq, nkv=nkv)
    return pl.pallas_call(
        kern,
        grid=(b, s // tq),
        in_specs=[
            pl.BlockSpec((1, tq, wq_), lambda i, m: (i, m, col0 // wq_)),
            pl.BlockSpec((1, tq, wk_), lambda i, m: (i, m, (col0 + wq_) // wk_)),
            pl.BlockSpec((1, tq, wk_), lambda i, m: (i, m, (col0 + wq_) // wk_ + 1)),
            pl.BlockSpec((tq, 2 * HEAD_DIM), lambda i, m: (m, 0)),
            pl.BlockSpec((tq, 2 * HEAD_DIM), lambda i, m: (m, 0)),
            pl.BlockSpec((1, wq_), lambda i, m: (0, 0)),
            pl.BlockSpec((1, wk_), lambda i, m: (0, 0)),
            pl.BlockSpec((wq_, wq_), lambda i, m: (0, 0)),
        ],
        out_specs=[
            pl.BlockSpec((1, nq, tq, HEAD_DIM), lambda i, m: (i, 0, m, 0)),
            pl.BlockSpec((1, nkv, tq, HEAD_DIM), lambda i, m: (i, 0, m, 0)),
            pl.BlockSpec((1, nkv, tq, HEAD_DIM), lambda i, m: (i, 0, m, 0)),
        ],
        out_shape=[
            jax.ShapeDtypeStruct((b, nq, s, HEAD_DIM), jnp.bfloat16),
            jax.ShapeDtypeStruct((b, nkv, s, HEAD_DIM), jnp.bfloat16),
            jax.ShapeDtypeStruct((b, nkv, s, HEAD_DIM), jnp.bfloat16),
        ],
        compiler_params=pltpu.CompilerParams(
            dimension_semantics=("arbitrary", "arbitrary"), vmem_limit_bytes=VMEM_LIMIT_BYTES),
        name="qkv_prep",
    )(p, p, p, cs, sn, qw_t, kw_t, gm)


def rope_tables(n_tok):
    rows = n_tok // GRID_W
    row = jnp.repeat(jnp.arange(rows, dtype=jnp.float32), GRID_W)
    col = jnp.tile(jnp.arange(GRID_W, dtype=jnp.float32), rows)
    nf = HEAD_DIM // 4
    inv = ROPE_BASE ** (-jnp.arange(nf, dtype=jnp.float32) / nf)
    ar, ac = row[:, None] * inv, col[:, None] * inv
    cs = jnp.concatenate([jnp.cos(ar), jnp.cos(ar), jnp.cos(ac), jnp.cos(ac)], -1)
    sn = jnp.concatenate([-jnp.sin(ar), jnp.sin(ar), -jnp.sin(ac), jnp.sin(ac)], -1)
    return jnp.tile(cs, (1, 2)), jnp.tile(sn, (1, 2))


def _flash_kernel(sink_ref, q_ref, k_ref, v_ref, o_ref, m_s, l_s, acc_s, *, use_sink, grp):
    j = pl.program_id(3)
    tq = q_ref.shape[2]

    @pl.when(j == 0)
    def _():
        m_s[...] = jnp.full_like(m_s, -jnp.inf)
        l_s[...] = jnp.zeros_like(l_s)
        acc_s[...] = jnp.zeros_like(acc_s)

    tk = k_ref.shape[2]
    nt = tk // LANES
    kt = k_ref[0, 0]
    vt = v_ref[0, 0]
    scores = [lax.dot_general(q_ref[0, g], kt, _NT, preferred_element_type=jnp.float32) for g in range(grp)]
    for g, s in enumerate(scores):
        rows = slice(g * tq, (g + 1) * tq)
        tiles = [s[:, c * LANES:(c + 1) * LANES] for c in range(nt)]
        m_tile = functools.reduce(jnp.maximum, tiles)
        m_old = m_s[rows]
        m_new = jnp.maximum(m_old, jnp.broadcast_to(jnp.max(m_tile, axis=1, keepdims=True), m_old.shape))
        alpha = jnp.exp(m_old - m_new)
        p_tiles = [jnp.exp(t - m_new) for t in tiles]
        l_s[rows] = alpha * l_s[rows] + functools.reduce(jnp.add, p_tiles)
        p = jnp.concatenate([t.astype(jnp.bfloat16) for t in p_tiles], axis=1)
        acc_s[rows] = alpha[:, :HEAD_DIM] * acc_s[rows] + jnp.dot(p, vt, preferred_element_type=jnp.float32)
        m_s[rows] = m_new

    @pl.when(j == pl.num_programs(3) - 1)
    def _():
        kvh = pl.program_id(1)
        outs = []
        for g in range(grp):
            rows = slice(g * tq, (g + 1) * tq)
            m = m_s[rows][:, :1]
            l = jnp.sum(l_s[rows], axis=1, keepdims=True)
            acc = acc_s[rows]
            if use_sink:
                sk = sink_ref[kvh * grp + g]
                m2 = jnp.maximum(m, sk)
                a = jnp.exp(m - m2)
                l = a * l + jnp.exp(sk - m2)
                acc = a * acc
            outs.append(acc / l)
        o_ref[0] = jnp.concatenate(outs, axis=1).astype(o_ref.dtype)


def flash_gqa(q, k, v, sink=None, tq=256, tk=2816):
    b, h, s, hd = q.shape
    kvh, lk = k.shape[1], k.shape[2]
    grp = h // kvh
    tq = min(tq, s)
    tk = max(t for t in range(LANES, min(tk, lk) + 1, LANES) if lk % t == 0)
    use_sink = sink is not None
    sink_arr = sink.astype(jnp.float32) if use_sink else jnp.zeros((h,), jnp.float32)
    kern = functools.partial(_flash_kernel, use_sink=use_sink, grp=grp)
    return pl.pallas_call(
        kern,
        grid=(b, kvh, s // tq, lk // tk),
        in_specs=[
            pl.BlockSpec(memory_space=pltpu.SMEM),
            pl.BlockSpec((1, grp, tq, hd), lambda i, c, m, j: (i, c, m, 0)),
            pl.BlockSpec((1, 1, tk, hd), lambda i, c, m, j: (i, c, j, 0)),
            pl.BlockSpec((1, 1, tk, hd), lambda i, c, m, j: (i, c, j, 0)),
        ],
        out_specs=pl.BlockSpec((1, tq, grp * hd), lambda i, c, m, j: (i, m, c)),
        out_shape=jax.ShapeDtypeStruct((b, s, h * hd), jnp.bfloat16),
        scratch_shapes=[pltpu.VMEM((grp * tq, LANES), jnp.float32), pltpu.VMEM((grp * tq, LANES), jnp.float32),
                        pltpu.VMEM((grp * tq, hd), jnp.float32)],
        compiler_params=pltpu.CompilerParams(
            dimension_semantics=("arbitrary",) * 4, vmem_limit_bytes=VMEM_LIMIT_BYTES),
        name="flash_gqa",
    )(sink_arr, q, k, v)


def _window_kernel(sink_ref, q_ref, kp_ref, kc_ref, kn_ref, vp_ref, vc_ref, vn_ref, kx_ref, vx_ref, o_ref, *, grp):
    kvh = pl.program_id(1)
    i = pl.program_id(2)
    nb = pl.num_programs(2)
    kcat = jnp.concatenate([kp_ref[0, 0], kc_ref[0, 0], kn_ref[0, 0], kx_ref[0, 0]], axis=0)
    vcat = jnp.concatenate([vp_ref[0, 0], vc_ref[0, 0], vn_ref[0, 0], vx_ref[0, 0]], axis=0)
    nk = kcat.shape[0]
    r = lax.broadcasted_iota(jnp.int32, (BLOCK, nk), 0)
    c = lax.broadcasted_iota(jnp.int32, (BLOCK, nk), 1)
    off_prev = jnp.where(i > 0, 0, 2 * nk)
    off_next = jnp.where(i < nb - 1, 0, 2 * nk)
    ok_prev = (c >= BLOCK) | (c >= r + off_prev)
    ok_next = (c < 2 * BLOCK) | (c >= 3 * BLOCK) | (c - 2 * BLOCK <= r - off_next)
    valid = ok_prev & ok_next
    scores = [lax.dot_general(q_ref[0, g], kcat, _NT, preferred_element_type=jnp.float32) for g in range(grp)]
    outs = []
    for g, s in enumerate(scores):
        s = jnp.where(valid, s, -jnp.inf)
        sk = sink_ref[kvh * grp + g]
        m = jnp.maximum(jnp.max(s, axis=1, keepdims=True), sk)
        p = jnp.exp(s - m)
        l = jnp.sum(p, axis=1, keepdims=True) + jnp.exp(sk - m)
        o = jnp.dot(p.astype(jnp.bfloat16), vcat, preferred_element_type=jnp.float32)
        outs.append(o / l)
    o_ref[0] = jnp.concatenate(outs, axis=1).astype(o_ref.dtype)


def windowed_sink_gqa(q, k, v, kx, vx, sink):
    b, h, s, hd = q.shape
    kvh = k.shape[1]
    lc = kx.shape[2]
    grp = h // kvh
    nb = s // BLOCK
    kern = functools.partial(_window_kernel, grp=grp)
    blk = lambda f: pl.BlockSpec((1, 1, BLOCK, hd), f)
    prev = lambda i, c, m: (i, c, jnp.maximum(m - 1, 0), 0)
    cur = lambda i, c, m: (i, c, m, 0)
    nxt = lambda i, c, m: (i, c, jnp.minimum(m + 1, nb - 1), 0)
    ctxm = lambda i, c, m: (i, c, 0, 0)
    return pl.pallas_call(
        kern,
        grid=(b, kvh, nb),
        in_specs=[
            pl.BlockSpec(memory_space=pltpu.SMEM),
            pl.BlockSpec((1, grp, BLOCK, hd), cur),
            blk(prev), blk(cur), blk(nxt), blk(prev), blk(cur), blk(nxt),
            pl.BlockSpec((1, 1, lc, hd), ctxm), pl.BlockSpec((1, 1, lc, hd), ctxm),
        ],
        out_specs=pl.BlockSpec((1, BLOCK, grp * hd), lambda i, c, m: (i, m, c)),
        out_shape=jax.ShapeDtypeStruct((b, s, h * hd), jnp.bfloat16),
        compiler_params=pltpu.CompilerParams(
            dimension_semantics=("arbitrary",) * 3, vmem_limit_bytes=VMEM_LIMIT_BYTES),
        name="windowed_sink_gqa",
    )(sink.astype(jnp.float32), q, k, k, k, v, v, v, kx, vx)


def _post_kernel(oa_ref, ob_ref, w_ref, x_ref, g_ref, lg_ref, lb_ref, y_ref):
    ka = oa_ref.shape[2]
    out = jnp.dot(oa_ref[0].astype(jnp.bfloat16), w_ref[:ka], preferred_element_type=jnp.float32)
    out += jnp.dot(ob_ref[0].astype(jnp.bfloat16), w_ref[ka:], preferred_element_type=jnp.float32)
    r = ALPHA * x_ref[0] + g_ref[0] * out
    mu = jnp.mean(r, -1, keepdims=True)
    d = r - mu
    var = jnp.mean(d * d, -1, keepdims=True)
    y_ref[0] = d * lax.rsqrt(var + EPS) * lg_ref[...] + lb_ref[...]


def proj_residual_ln(oa, ob, w, x, gate, ln_g, ln_b, tm=256):
    b, s, ka = oa.shape
    kb = ob.shape[2]
    k = ka + kb
    d = w.shape[1]
    tm = min(tm, s)
    wb = w.astype(jnp.bfloat16)
    return pl.pallas_call(
        _post_kernel,
        grid=(b, s // tm),
        in_specs=[
            pl.BlockSpec((1, tm, ka), lambda i, m: (i, m, 0)),
            pl.BlockSpec((1, tm, kb), lambda i, m: (i, m, 0)),
            pl.BlockSpec((k, d), lambda i, m: (0, 0)),
            pl.BlockSpec((1, tm, d), lambda i, m: (i, m, 0)),
            pl.BlockSpec((1, 1, d), lambda i, m: (i, 0, 0)),
            pl.BlockSpec((1, d), lambda i, m: (0, 0)),
            pl.BlockSpec((1, d), lambda i, m: (0, 0)),
        ],
        out_specs=pl.BlockSpec((1, tm, d), lambda i, m: (i, m, 0)),
        out_shape=jax.ShapeDtypeStruct((b, s, d), jnp.float32),
        compiler_params=pltpu.CompilerParams(
            dimension_semantics=("arbitrary", "arbitrary"),
            vmem_limit_bytes=VMEM_LIMIT_BYTES),
        name="proj_residual_ln",
    )(oa, ob, wb, x, gate, ln_g.reshape(1, d), ln_b.reshape(1, d))


def _top16(s, payload=None):
    n = s.shape[0]
    iota = lax.broadcasted_iota(jnp.int32, s.shape, 0).astype(jnp.float32)
    vals, ids = [], []
    for _ in range(PEER_TOPK):
        m = jnp.max(s, axis=0, keepdims=True)
        pos = jnp.min(jnp.where(s == m, iota, float(n)), axis=0, keepdims=True)
        hit = iota == pos
        vals.append(m)
        ids.append(pos if payload is None else jnp.max(jnp.where(hit, payload, -1.0), axis=0, keepdims=True))
        s = jnp.where(hit, -jnp.inf, s)
    return jnp.concatenate(vals, 0), jnp.concatenate(ids, 0)


def _peer_topk_kernel(q_ref, k1_ref, k2_ref, eid_ref, gate_ref, eid_s, gate_s):
    half = PEER_QDIM // 2

    def head(h, carry):
        off = pl.multiple_of(h * PEER_QDIM, PEER_QDIM)
        q1 = q_ref[:, pl.ds(off, half)]
        q2 = q_ref[:, pl.ds(off + half, half)]
        s1 = lax.dot_general(k1_ref[h], q1, _NT, precision=lax.Precision.HIGHEST,
                             preferred_element_type=jnp.float32)
        s2 = lax.dot_general(k2_ref[h], q2, _NT, precision=lax.Precision.HIGHEST,
                             preferred_element_type=jnp.float32)
        v1, i1 = _top16(s1)
        v2, i2 = _top16(s2)
        k8 = PEER_TOPK // 2
        cand = jnp.concatenate([v1[0:1] + v2] + [v1[i:i + 1] + v2[:k8] for i in range(1, k8)]
                               + [v1[k8:] + v2[0:1]], 0)
        cid = jnp.concatenate([i1[0:1] * PEER_NKEYS + i2]
                              + [i1[i:i + 1] * PEER_NKEYS + i2[:k8] for i in range(1, k8)]
                              + [i1[k8:] * PEER_NKEYS + i2[0:1]], 0)
        best, eid = _top16(cand, cid)
        e = jnp.exp(best - best[0:1])
        gate = e / jnp.sum(e, axis=0, keepdims=True)
        row = pl.multiple_of(h * PEER_TOPK, PEER_TOPK)
        eid_s[pl.ds(row, PEER_TOPK), :] = eid.astype(jnp.int32)
        gate_s[pl.ds(row, PEER_TOPK), :] = gate
        return carry

    lax.fori_loop(0, PEER_HEADS, head, 0)
    eid_ref[...] = eid_s[...].T
    gate_ref[...] = gate_s[...].T


def peer_topk(q, k1, k2, tt=1024):
    t = q.shape[0]
    tt = min(tt, t)
    nsel = PEER_HEADS * PEER_TOPK
    return pl.pallas_call(
        _peer_topk_kernel,
        grid=(t // tt,),
        in_specs=[
            pl.BlockSpec((tt, q.shape[1]), lambda i: (i, 0)),
            pl.BlockSpec(k1.shape, lambda i: (0, 0, 0)),
            pl.BlockSpec(k2.shape, lambda i: (0, 0, 0)),
        ],
        out_specs=[pl.BlockSpec((tt, nsel), lambda i: (i, 0)),
                   pl.BlockSpec((tt, nsel), lambda i: (i, 0))],
        out_shape=[jax.ShapeDtypeStruct((t, nsel), jnp.int32),
                   jax.ShapeDtypeStruct((t, nsel), jnp.float32)],
        scratch_shapes=[pltpu.VMEM((nsel, tt), jnp.int32), pltpu.VMEM((nsel, tt), jnp.float32)],
        compiler_params=pltpu.CompilerParams(
            dimension_semantics=("arbitrary",), vmem_limit_bytes=VMEM_LIMIT_BYTES),
        name="peer_topk",
    )(q, k1, k2)


def _peer_w_kernel(e_ref, g_ref, w_ref):
    nk = PEER_NKEYS
    iota = lax.broadcasted_iota(jnp.int32, (nk, e_ref.shape[1]), 0)

    def tok(t, carry):
        e = e_ref[pl.ds(t, 1), :]
        g = g_ref[pl.ds(t, 1), :]
        a_t = jnp.where(iota == (e >> 7), g, 0.0).astype(jnp.bfloat16)
        b_t = jnp.where(iota == (e & (nk - 1)), 1.0, 0.0).astype(jnp.bfloat16)
        w = lax.dot_general(a_t, b_t, _NT, preferred_element_type=jnp.float32)
        w_ref[t] = w.astype(jnp.bfloat16)
        return carry

    lax.fori_loop(0, e_ref.shape[0], tok, 0, unroll=64)


def peer_dense_gates(eid, gate, tt=128):
    t, nsel = eid.shape
    tt = min(tt, t)
    nk = PEER_NKEYS
    w = pl.pallas_call(
        _peer_w_kernel,
        grid=(t // tt,),
        in_specs=[pl.BlockSpec((tt, nsel), lambda i: (i, 0)),
                  pl.BlockSpec((tt, nsel), lambda i: (i, 0))],
        out_specs=pl.BlockSpec((tt, nk, nk), lambda i: (i, 0, 0)),
        out_shape=jax.ShapeDtypeStruct((t, nk, nk), jnp.bfloat16),
        compiler_params=pltpu.CompilerParams(
            dimension_semantics=("arbitrary",), vmem_limit_bytes=VMEM_LIMIT_BYTES),
        name="peer_dense_gates",
    )(eid, gate)
    return w


def _peer_expert_kernel(x_ref, sh_ref, sc_ref, w_ref, u_ref, v_ref, g_ref, lg_ref, lb_ref, y_ref, xm_s, acc_s):
    e = pl.program_id(2)

    @pl.when(e == 0)
    def _():
        xm_s[...] = (x_ref[0] * (1.0 + sc_ref[0]) + sh_ref[0]).astype(jnp.bfloat16)
        acc_s[...] = jnp.zeros_like(acc_s)

    h = lax.dot_general(xm_s[...], u_ref[...], _NT, preferred_element_type=jnp.float32)
    gelu = 0.5 * h * (1.0 + lax.erf(h * (2.0 ** -0.5)))
    w = w_ref[0].reshape(h.shape)
    a = gelu * w.astype(jnp.float32)
    acc_s[...] += jnp.dot(a.astype(jnp.bfloat16), v_ref[...], preferred_element_type=jnp.float32)

    @pl.when(e == pl.num_programs(2) - 1)
    def _():
        r = ALPHA * x_ref[0] + g_ref[0] * acc_s[...]
        mu = jnp.mean(r, -1, keepdims=True)
        d = r - mu
        var = jnp.mean(d * d, -1, keepdims=True)
        y_ref[0] = d * lax.rsqrt(var + EPS) * lg_ref[...] + lb_ref[...]


def peer_experts_ln(x, shift, scale, w, u_tab, v_tab, gate, ln_g, ln_b, tt=512, te=2048):
    b, s, d = x.shape
    tt = min(tt, s)
    ne = u_tab.shape[0]
    nk = PEER_NKEYS
    w3 = w.reshape(b, s, nk, nk)
    return pl.pallas_call(
        _peer_expert_kernel,
        grid=(b, s // tt, ne // te),
        in_specs=[
            pl.BlockSpec((1, tt, d), lambda i, m, e: (i, m, 0)),
            pl.BlockSpec((1, 1, d), lambda i, m, e: (i, 0, 0)),
            pl.BlockSpec((1, 1, d), lambda i, m, e: (i, 0, 0)),
            pl.BlockSpec((1, tt, te // nk, nk), lambda i, m, e: (i, m, e, 0)),
            pl.BlockSpec((te, d), lambda i, m, e: (e, 0)),
            pl.BlockSpec((te, d), lambda i, m, e: (e, 0)),
            pl.BlockSpec((1, 1, d), lambda i, m, e: (i, 0, 0)),
            pl.BlockSpec((1, d), lambda i, m, e: (0, 0)),
            pl.BlockSpec((1, d), lambda i, m, e: (0, 0)),
        ],
        out_specs=pl.BlockSpec((1, tt, d), lambda i, m, e: (i, m, 0)),
        out_shape=jax.ShapeDtypeStruct((b, s, d), jnp.float32),
        scratch_shapes=[pltpu.VMEM((tt, d), jnp.bfloat16), pltpu.VMEM((tt, d), jnp.float32)],
        compiler_params=pltpu.CompilerParams(
            dimension_semantics=("arbitrary", "arbitrary", "arbitrary"),
            vmem_limit_bytes=VMEM_LIMIT_BYTES),
        name="peer_experts_ln",
    )(x, shift, scale, w3, u_tab, v_tab, gate, ln_g.reshape(1, d), ln_b.reshape(1, d))


def peer_block(x, shift, scale, gate, wq, k1, k2, u_bf, v_bf, ln_g, ln_b):
    b, s, d = x.shape
    q_all = mod_matmul(x, shift, scale, wq).reshape(b * s, -1)
    eid, gsel = peer_topk(q_all, k1, k2)
    w = peer_dense_gates(eid, gsel)
    return peer_experts_ln(x, shift, scale, w, u_bf, v_bf, gate, ln_g, ln_b)


HEAD_LANES = 128


def _short_conv_kernel(x_ref, xp_ref, xn_ref, w_ref, b_ref, o_ref, *, silu, n_l2, n_scaled):
    cb = pl.program_id(1)
    m = pl.program_id(2)
    x = x_ref[0]
    tq, wb = x.shape
    prev_row = jnp.where(m > 0, xp_ref[0][7:8], 0.0)
    next_row = jnp.where(m < pl.num_programs(2) - 1, xn_ref[0][0:1], 0.0)
    row = lax.broadcasted_iota(jnp.int32, x.shape, 0)
    x_m1 = jnp.where(row == 0, prev_row, pltpu.roll(x, 1, 0))
    x_p1 = jnp.where(row == tq - 1, next_row, pltpu.roll(x, tq - 1, 0))
    y = w_ref[0:1] * x_m1 + w_ref[1:2] * x + w_ref[2:3] * x_p1 + b_ref[...]
    if silu:
        y = y * jax.nn.sigmoid(y)
    if n_l2 == 0:
        o_ref[0] = y
        return
    hpb = wb // HEAD_LANES
    for hh in range(hpb):
        gh = cb * hpb + hh
        seg = y[:, hh * HEAD_LANES:(hh + 1) * HEAD_LANES]
        inv = lax.rsqrt(jnp.sum(seg * seg, axis=-1, keepdims=True) + EPS)
        f = jnp.where(gh < n_l2, inv, 1.0) * jnp.where(gh < n_scaled, C_DK ** -0.5, 1.0)
        o_ref[0, :, hh * HEAD_LANES:(hh + 1) * HEAD_LANES] = seg * f


def short_conv(p, col0, width, w, bias=None, silu=False, n_l2=0, n_scaled=0, wb=768, tq=512):
    b, l, _ = p.shape
    tq = min(tq, l)
    bias2 = (jnp.zeros((width,), jnp.float32) if bias is None else bias).reshape(1, width)
    c0 = col0 // wb
    kern = functools.partial(_short_conv_kernel, silu=silu, n_l2=n_l2, n_scaled=n_scaled)
    r8 = tq // 8
    return pl.pallas_call(
        kern,
        grid=(b, width // wb, l // tq),
        in_specs=[
            pl.BlockSpec((1, tq, wb), lambda i, c, m: (i, m, c0 + c)),
            pl.BlockSpec((1, 8, wb), lambda i, c, m: (i, jnp.maximum(m * r8 - 1, 0), c0 + c)),
            pl.BlockSpec((1, 8, wb), lambda i, c, m: (i, jnp.minimum((m + 1) * r8, l // 8 - 1), c0 + c)),
            pl.BlockSpec((3, wb), lambda i, c, m: (0, c)),
            pl.BlockSpec((1, wb), lambda i, c, m: (0, c)),
        ],
        out_specs=pl.BlockSpec((1, tq, wb), lambda i, c, m: (i, m, c)),
        out_shape=jax.ShapeDtypeStruct((b, l, width), jnp.float32),
        compiler_params=pltpu.CompilerParams(
            dimension_semantics=("arbitrary",) * 3, vmem_limit_bytes=VMEM_LIMIT_BYTES),
        name="short_conv",
    )(p, p, p, w, bias2)


def _dot3(a, b):
    ah = a.astype(jnp.bfloat16)
    bh = b.astype(jnp.bfloat16)
    al = (a - ah.astype(jnp.float32)).astype(jnp.bfloat16)
    bl = (b - bh.astype(jnp.float32)).astype(jnp.bfloat16)
    d = functools.partial(jnp.dot, preferred_element_type=jnp.float32)
    return d(ah, bh) + (d(ah, bl) + d(al, bh))


GDN_PAR = 4


def _gdn_chunk_kernel(qkv_ref, beta_ref, g_ref, u_ref, w_ref, qd_ref, kd_ref, in_ref, gl_ref, *, nc):
    d = pl.program_id(0)
    cs = GDN_CHUNK
    ii = lax.broadcasted_iota(jnp.int32, (cs, cs), 0)
    jj = lax.broadcasted_iota(jnp.int32, (cs, cs), 1)
    lo = (ii - jj) * (1 - 2 * d)
    incl = lo >= 0
    strict = lo > 0
    tri = jnp.where(incl, 1.0, 0.0).astype(jnp.bfloat16)
    tri3 = jnp.concatenate([tri, tri, tri], axis=1)
    eye = jnp.where(ii == jj, 1.0, 0.0)

    def chunk_pair(cp, carry):
        probs = []
        for c in [GDN_PAR * cp + i for i in range(GDN_PAR)]:
            rows = pl.ds(pl.multiple_of(c * cs, cs), cs)
            g_c = g_ref[0, 0, rows, :]
            b_c = beta_ref[0, 0, rows, :]
            g_hi = g_c.astype(jnp.bfloat16)
            r1 = g_c - g_hi.astype(jnp.float32)
            g_mid = r1.astype(jnp.bfloat16)
            g_lo = (r1 - g_mid.astype(jnp.float32)).astype(jnp.bfloat16)
            gc = jnp.dot(tri3, jnp.concatenate([g_hi, g_mid, g_lo], axis=0),
                         preferred_element_type=jnp.float32)
            tot = jnp.sum(g_c, axis=0, keepdims=True)
            for h in range(C_HEADS):
                probs.append(dict(c=c, h=h, rows=rows, gc=gc[:, h:h + 1], bt=b_c[:, h:h + 1], tot=tot[:, h:h + 1]))
        for pr in probs:
            h, rows = pr["h"], pr["rows"]
            q = qkv_ref[0, rows, h * HEAD_LANES:(h + 1) * HEAD_LANES]
            k = qkv_ref[0, rows, C_W + h * HEAD_LANES:C_W + (h + 1) * HEAD_LANES]
            kb = k * pr["bt"]
            kq = lax.dot_general(jnp.concatenate([kb, q], axis=0).astype(jnp.bfloat16), k.astype(jnp.bfloat16),
                                 _NT, preferred_element_type=jnp.float32)
            gc_row = jnp.broadcast_to(pr["gc"], (cs, HEAD_LANES)).T[:cs, :]
            dm = jnp.where(incl, jnp.exp(pr["gc"] - gc_row), 0.0)
            x = jnp.where(strict, -(kq[:cs] * dm), 0.0)
            in_ref[0, 0, pr["c"], h] = (kq[cs:] * dm).astype(in_ref.dtype)
            pr.update(t=eye + x, pw=x)
        for _ in range(5):
            for pr in probs:
                pr["pw"] = _dot3(pr["pw"], pr["pw"])
            for pr in probs:
                pr["t"] = pr["t"] + _dot3(pr["t"], pr["pw"])
        for pr in probs:
            h, rows = pr["h"], pr["rows"]
            lanes = slice(h * HEAD_LANES, (h + 1) * HEAD_LANES)
            q = qkv_ref[0, rows, h * HEAD_LANES:(h + 1) * HEAD_LANES]
            k = qkv_ref[0, rows, C_W + h * HEAD_LANES:C_W + (h + 1) * HEAD_LANES]
            v = qkv_ref[0, rows, 2 * C_W + h * HEAD_LANES:2 * C_W + (h + 1) * HEAD_LANES]
            eg = jnp.exp(pr["gc"])
            uw = _dot3(pr["t"], jnp.concatenate([v * pr["bt"], k * (pr["bt"] * eg)], axis=1))
            u_ref[0, 0, rows, lanes] = uw[:, :HEAD_LANES]
            w_ref[0, 0, rows, lanes] = uw[:, HEAD_LANES:].astype(w_ref.dtype)
            qd_ref[0, 0, rows, lanes] = (q * eg).astype(qd_ref.dtype)
            kd_ref[0, 0, rows, lanes] = (k * jnp.exp(pr["tot"] - pr["gc"])).astype(kd_ref.dtype)
            gl_ref[0, 0, pr["c"], h:h + 1, :] = jnp.broadcast_to(jnp.exp(pr["tot"]), (1, HEAD_LANES))
        return carry

    lax.fori_loop(0, nc // GDN_PAR, chunk_pair, 0)


def gdn_chunk_prep(qkv, beta, g, nc=4):
    b, l, _ = qkv.shape
    cs = GDN_CHUNK
    tq = nc * cs
    nchunks = l // cs
    bf = jnp.bfloat16
    big = lambda dt: jax.ShapeDtypeStruct((2, b, l, C_W), dt)
    bspec = pl.BlockSpec((1, 1, tq, C_W), lambda d, i, m: (d, i, m, 0))
    gspec = pl.BlockSpec((1, 1, tq, C_HEADS), lambda d, i, m: (d, i, m, 0))
    return pl.pallas_call(
        functools.partial(_gdn_chunk_kernel, nc=nc),
        grid=(2, b, l // tq),
        in_specs=[pl.BlockSpec((1, tq, 3 * C_W), lambda d, i, m: (i, m, 0)), gspec, gspec],
        out_specs=[bspec, bspec, bspec, bspec,
                   pl.BlockSpec((1, 1, nc, C_HEADS, cs, cs), lambda d, i, m: (d, i, m, 0, 0, 0)),
                   pl.BlockSpec((1, 1, nc, C_HEADS, HEAD_LANES), lambda d, i, m: (d, i, m, 0, 0))],
        out_shape=[big(jnp.float32), big(bf), big(bf), big(bf),
                   jax.ShapeDtypeStruct((2, b, nchunks, C_HEADS, cs, cs), bf),
                   jax.ShapeDtypeStruct((2, b, nchunks, C_HEADS, HEAD_LANES), jnp.float32)],
        compiler_params=pltpu.CompilerParams(
            dimension_semantics=("arbitrary",) * 3, vmem_limit_bytes=VMEM_LIMIT_BYTES),
        name="gdn_chunk_prep",
    )(qkv, beta, g)


def _gdn_scan_kernel(*refs):
    ins, (of_ref, ob_ref, s_ref) = refs[:12], refs[12:]
    step = pl.program_id(1)

    @pl.when(step == 0)
    def _():
        s_ref[...] = jnp.zeros_like(s_ref)

    dot = functools.partial(jnp.dot, preferred_element_type=jnp.float32)
    seqs = [(d, h, slice(h * HEAD_LANES, (h + 1) * HEAD_LANES)) for d in range(2) for h in range(C_HEADS)]
    outs = (of_ref, ob_ref)
    sb, vb = {}, {}
    for d, h, lanes in seqs:
        sb[d, h] = s_ref[d * C_HEADS + h].astype(jnp.bfloat16)
    for d, h, lanes in seqs:
        u_ref, w_ref = ins[6 * d], ins[6 * d + 1]
        vb[d, h] = (u_ref[0, 0, :, lanes] - dot(w_ref[0, 0, :, lanes], sb[d, h])).astype(jnp.bfloat16)
    for d, h, lanes in seqs:
        qd_ref, in_ref = ins[6 * d + 2], ins[6 * d + 4]
        outs[d][0, :, lanes] = dot(qd_ref[0, 0, :, lanes], sb[d, h]) + dot(in_ref[0, 0, 0, h], vb[d, h])
    for d, h, lanes in seqs:
        kd_ref, gl_ref = ins[6 * d + 3], ins[6 * d + 5]
        s_ref[d * C_HEADS + h] = s_ref[d * C_HEADS + h] * gl_ref[0, 0, 0, h:h + 1, :] + lax.dot_general(
            kd_ref[0, 0, :, lanes], vb[d, h], (((0,), (0,)), ((), ())), preferred_element_type=jnp.float32)


def gdn_scan(u, w, qd, kd, intra, gl, n_ctx_chunks):
    _, b, l, _ = u.shape
    cs = GDN_CHUNK
    nchunks = l // cs

    def chunk_of(d, s):
        if d == 0:
            return s
        return jnp.where(s < n_ctx_chunks, n_ctx_chunks - 1 - s, nchunks - 1 + n_ctx_chunks - s)

    in_specs, args = [], []
    for d in range(2):
        big = pl.BlockSpec((1, 1, cs, C_W), lambda i, s, d=d: (d, i, chunk_of(d, s), 0))
        in_specs += [big, big, big, big,
                     pl.BlockSpec((1, 1, 1, C_HEADS, cs, cs), lambda i, s, d=d: (d, i, chunk_of(d, s), 0, 0, 0)),
                     pl.BlockSpec((1, 1, 1, C_HEADS, HEAD_LANES), lambda i, s, d=d: (d, i, chunk_of(d, s), 0, 0))]
        args += [u, w, qd, kd, intra, gl]
    out_specs = [pl.BlockSpec((1, cs, C_W), lambda i, s, d=d: (i, chunk_of(d, s), 0)) for d in range(2)]
    return pl.pallas_call(
        _gdn_scan_kernel,
        grid=(b, nchunks),
        in_specs=in_specs,
        out_specs=out_specs,
        out_shape=[jax.ShapeDtypeStruct((b, l, C_W), jnp.float32)] * 2,
        scratch_shapes=[pltpu.VMEM((2 * C_HEADS, C_DK, C_DV), jnp.float32)],
        compiler_params=pltpu.CompilerParams(
            dimension_semantics=("arbitrary",) * 2, vmem_limit_bytes=VMEM_LIMIT_BYTES),
        name="gdn_scan",
    )(*args)


def _gdn_gate_kernel(of_ref, ob_ref, z_ref, gw_ref, y_ref):
    o = of_ref[0] + ob_ref[0]
    z = z_ref[0]
    for h in range(C_HEADS):
        lanes = slice(h * HEAD_LANES, (h + 1) * HEAD_LANES)
        oh = o[:, lanes]
        zh = z[:, lanes]
        n = oh * lax.rsqrt(jnp.mean(oh * oh, axis=-1, keepdims=True) + EPS) * gw_ref[...]
        y_ref[0, :, lanes] = (n * (zh * jax.nn.sigmoid(zh))).astype(y_ref.dtype)


def gdn_gate(o_f, o_b, row0, p, gnorm_w, tq=256):
    b, l, _ = p.shape
    tq = min(tq, l)
    r0 = row0 // tq
    ospec = pl.BlockSpec((1, tq, C_W), lambda i, m: (i, r0 + m, 0))
    return pl.pallas_call(
        _gdn_gate_kernel,
        grid=(b, l // tq),
        in_specs=[ospec, ospec,
                  pl.BlockSpec((1, tq, C_W), lambda i, m: (i, m, 3)),
                  pl.BlockSpec((1, HEAD_LANES), lambda i, m: (0, 0))],
        out_specs=pl.BlockSpec((1, tq, C_W), lambda i, m: (i, m, 0)),
        out_shape=jax.ShapeDtypeStruct((b, l, C_W), jnp.bfloat16),
        compiler_params=pltpu.CompilerParams(
            dimension_semantics=("arbitrary",) * 2, vmem_limit_bytes=VMEM_LIMIT_BYTES),
        name="gdn_gate",
    )(o_f, o_b, p, gnorm_w.reshape(1, HEAD_LANES))


def gdn_mixer(p, pc, conv_w, a_log, dt_bias, gnorm_w, with_ctx):
    lc = pc.shape[1]
    conv = functools.partial(short_conv, col0=0, width=3 * C_W, w=conv_w, silu=True,
                             n_l2=2 * C_HEADS, n_scaled=C_HEADS)
    qkv = jnp.concatenate([conv(pc), conv(p)], axis=1)
    gates = jnp.concatenate([pc[..., -C_GATES:], p[..., -C_GATES:]], axis=1)
    gates = gates.reshape(gates.shape[0], gates.shape[1], 4, C_HEADS)
    beta = jax.nn.sigmoid(gates[:, :, :2])
    g = -jnp.exp(a_log) * jax.nn.softplus(gates[:, :, 2:] + dt_bias)
    beta = jnp.moveaxis(beta, 2, 0)
    g = jnp.moveaxis(g, 2, 0)
    u, w, qd, kd, intra, gl = gdn_chunk_prep(qkv, beta, g)
    o_f, o_b = gdn_scan(u, w, qd, kd, intra, gl, lc // GDN_CHUNK)
    out = gdn_gate(o_f, o_b, lc, p, gnorm_w)
    out_c = gdn_gate(o_f, o_b, 0, pc, gnorm_w) if with_ctx else None
    return out, out_c


FFT_R = 128
FFT_N = FFT_R * FFT_R
SUB = 8


def stage_a_table():
    idx = np.arange(FFT_R)
    ang = 2.0 * np.pi * np.outer(idx, idx) / FFT_R
    return jnp.asarray(np.stack([np.cos(ang), -np.sin(ang)], axis=1).reshape(2 * FFT_R, FFT_R), jnp.float32)


def _fft_stage_a_kernel(x_ref, l_ref, y_ref):
    l = l_ref[...]
    c = x_ref.shape[-1]
    xs = jnp.swapaxes(x_ref[0], 0, 1)
    ys = jnp.stack([_dot3(l, xs[j]) for j in range(SUB)], axis=0)
    y_ref[0] = jnp.swapaxes(ys, 0, 1).reshape(FFT_R, 2, SUB, c)


def fft_stage_a(x, col_blk, width, stage_a):
    b, l, wtot = x.shape
    n1cnt = l // FFT_R
    x4 = x.reshape(b, n1cnt, FFT_R, wtot)
    return pl.pallas_call(
        _fft_stage_a_kernel,
        grid=(b, FFT_R // SUB),
        in_specs=[pl.BlockSpec((1, n1cnt, SUB, width), lambda i, j: (i, 0, j, col_blk)),
                  pl.BlockSpec((2 * FFT_R, n1cnt), lambda i, j: (0, 0))],
        out_specs=pl.BlockSpec((1, FFT_R, 2, SUB, width), lambda i, j: (i, 0, 0, j, 0)),
        out_shape=jax.ShapeDtypeStruct((b, FFT_R, 2, FFT_R, width), jnp.float32),
        compiler_params=pltpu.CompilerParams(
            dimension_semantics=("arbitrary",) * 2, vmem_limit_bytes=VMEM_LIMIT_BYTES),
        name="fft_stage_a",
    )(x4, stage_a[:, :n1cnt])


def _dot3_presplit(ah, al, b):
    bh = b.astype(jnp.bfloat16)
    bl = (b - bh.astype(jnp.float32)).astype(jnp.bfloat16)
    d = functools.partial(jnp.dot, preferred_element_type=jnp.float32)
    return d(ah, bh) + (d(ah, bl) + d(al, bh))


def stage_b_tables():
    r = FFT_R
    k1 = jnp.arange(r, dtype=jnp.int32)[:, None, None]
    k2 = jnp.arange(r, dtype=jnp.int32)[None, :, None]
    n2 = jnp.arange(r, dtype=jnp.int32)[None, None, :]
    th = ((n2 * (r * k2 + k1)) % FFT_N).astype(jnp.float32) * (2.0 * math.pi / FFT_N)
    c, s = jnp.cos(th), jnp.sin(th)
    t = jnp.concatenate([jnp.concatenate([c, s], 2), jnp.concatenate([-s, c], 2)], 1)

    def split(m):
        hi = m.astype(jnp.bfloat16)
        return hi, (m - hi.astype(jnp.float32)).astype(jnp.bfloat16)

    return split(t) + split(jnp.swapaxes(t, 1, 2))


def _fft_mid_kernel(y_ref, h_ref, th_ref, tl_ref, ih_ref, il_ref, o_ref, *, conv):
    nb, r, c = y_ref.shape[0], FFT_R, y_ref.shape[-1]
    xs = [_dot3_presplit(th_ref[0], tl_ref[0], y_ref[i, 0].reshape(2 * r, c)) for i in range(nb)]
    if not conv:
        for i, x in enumerate(xs):
            o_ref[i, 0] = (x * ((1.0 / FFT_N) / h_ref[i])).reshape(2, r, c)
        return
    hr, hi = h_ref[0, 0, 0], h_ref[0, 0, 1]
    ps = [jnp.concatenate([x[:r] * hr - x[r:] * hi, x[:r] * hi + x[r:] * hr], axis=0) for x in xs]
    for i, p in enumerate(ps):
        o_ref[i, 0] = _dot3_presplit(ih_ref[0], il_ref[0], p).reshape(2, r, c)


def fft_mid(y, h, tables, conv, order=0):
    b, r, _, _, c = y.shape
    hh = h if conv else h.reshape(b, 1, c)
    hspec = (pl.BlockSpec((1, 1, 2, r, c), lambda k: (order, k, 0, 0, 0)) if conv
             else pl.BlockSpec((b, 1, c), lambda k: (0, 0, 0)))
    blk = pl.BlockSpec((b, 1, 2, r, c), lambda k: (0, k, 0, 0, 0))
    tspec = pl.BlockSpec((1, 2 * r, 2 * r), lambda k: (k, 0, 0))
    return pl.pallas_call(
        functools.partial(_fft_mid_kernel, conv=conv),
        grid=(r,),
        in_specs=[blk, hspec, tspec, tspec, tspec, tspec],
        out_specs=blk,
        out_shape=jax.ShapeDtypeStruct(y.shape, jnp.float32),
        compiler_params=pltpu.CompilerParams(
            dimension_semantics=("arbitrary",), vmem_limit_bytes=VMEM_LIMIT_BYTES),
        name="fft_mid",
    )(y, hh, *tables)


def _fft_out_kernel(b_ref, l_ref, xg_ref, xin_ref, bias_ref, o_ref):
    l = l_ref[...]
    c = o_ref.shape[-1]
    bs = jnp.swapaxes(b_ref[0].reshape(2 * FFT_R, SUB, c), 0, 1)
    ys = jnp.stack([_dot3(l, bs[j]) for j in range(SUB)], axis=0)
    y = jnp.swapaxes(ys, 0, 1)
    o_ref[0] = xg_ref[0] * (y + bias_ref[...] * xin_ref[0])


def fft_out_gate(bm, stage_a, xg, xg_blk, xin, xin_blk, bias):
    b, r, _, _, c = bm.shape
    l = xg.shape[1]
    n1cnt = l // r
    view = lambda t: t.reshape(b, n1cnt, r, t.shape[-1])
    lhs = stage_a.T[:n1cnt]
    return pl.pallas_call(
        _fft_out_kernel,
        grid=(b, r // SUB),
        in_specs=[pl.BlockSpec((1, r, 2, SUB, c), lambda i, j: (i, 0, 0, j, 0)),
                  pl.BlockSpec((n1cnt, 2 * r), lambda i, j: (0, 0)),
                  pl.BlockSpec((1, n1cnt, SUB, c), lambda i, j: (i, 0, j, xg_blk)),
                  pl.BlockSpec((1, n1cnt, SUB, c), lambda i, j: (i, 0, j, xin_blk)),
                  pl.BlockSpec((1, c), lambda i, j: (0, 0))],
        out_specs=pl.BlockSpec((1, n1cnt, SUB, c), lambda i, j: (i, 0, j, 0)),
        out_shape=jax.ShapeDtypeStruct((b, n1cnt, r, c), jnp.float32),
        compiler_params=pltpu.CompilerParams(
            dimension_semantics=("arbitrary",) * 2, vmem_limit_bytes=VMEM_LIMIT_BYTES),
        name="fft_out_gate",
    )(bm, lhs, view(xg), view(xin), bias.reshape(1, c)).reshape(b, l, c)


def _direct_conv_kernel(xin_ref, xg_ref, kern_ref, d1_ref, d2_ref, norm_ref, bias_ref, o_ref):
    n = xin_ref.shape[1]
    d1 = d1_ref[...]
    x = xin_ref[0]
    xs = _dot3(d1[:, :n], x)
    hs = _dot3(d1, kern_ref[...]) / norm_ref[...]
    xr, xi, hr, hi = xs[:2 * n], xs[2 * n:], hs[:2 * n], hs[2 * n:]
    p = jnp.concatenate([xr * hr - xi * hi, xr * hi + xi * hr], axis=0)
    y = _dot3(d2_ref[...], p)
    o_ref[0] = xg_ref[0] * (y + bias_ref[...] * x)


def direct_long_conv(xin, xin_blk, xg, xg_blk, kern, norm, bias):
    b, n, _ = xin.shape
    c = kern.shape[1]
    idx = np.arange(2 * n)
    ang = 2.0 * np.pi * np.outer(idx, idx) / (2 * n)
    d1 = jnp.asarray(np.concatenate([np.cos(ang), -np.sin(ang)], axis=0), jnp.float32)
    d2 = jnp.asarray(np.concatenate([np.cos(ang[:n]), -np.sin(ang[:n])], axis=1) / (2 * n), jnp.float32)
    return pl.pallas_call(
        _direct_conv_kernel,
        grid=(b,),
        in_specs=[pl.BlockSpec((1, n, c), lambda i: (i, 0, xin_blk)),
                  pl.BlockSpec((1, n, c), lambda i: (i, 0, xg_blk)),
                  pl.BlockSpec((2 * n, c), lambda i: (0, 0)),
                  pl.BlockSpec((4 * n, 2 * n), lambda i: (0, 0)),
                  pl.BlockSpec((n, 4 * n), lambda i: (0, 0)),
                  pl.BlockSpec((1, c), lambda i: (0, 0)),
                  pl.BlockSpec((1, c), lambda i: (0, 0))],
        out_specs=pl.BlockSpec((1, n, c), lambda i: (i, 0, 0)),
        out_shape=jax.ShapeDtypeStruct((b, n, c), jnp.float32),
        compiler_params=pltpu.CompilerParams(
            dimension_semantics=("arbitrary",), vmem_limit_bytes=VMEM_LIMIT_BYTES),
        name="direct_long_conv",
    )(xin, xg, kern, d1, d2, norm.reshape(1, c), bias.reshape(1, c))


def _hy_filter_kernel(w1_ref, b1_ref, fr_ref, w2_ref, b2_ref, w3_ref, dl_ref, k_ref, s_ref, *, n):
    i = pl.program_id(0)
    tp, c = k_ref.shape[1], k_ref.shape[2]

    @pl.when(i == 0)
    def _():
        s_ref[...] = jnp.zeros_like(s_ref)

    def pos(shape):
        idx = i * tp + lax.broadcasted_iota(jnp.int32, shape, 0)
        t = jnp.where(idx < n, idx, jnp.where(idx == n, 0, 2 * n - idx))
        return idx, t.astype(jnp.float32)

    _, t = pos((tp, LANES))
    lane = lax.broadcasted_iota(jnp.int32, (tp, LANES), 1)
    band = jnp.where(lane <= HY_BANDS, lane, lane - HY_BANDS).astype(jnp.float32)
    ang = 2.0 * math.pi * t * band / n
    feat = jnp.where(lane == 0, t / n,
                     jnp.where(lane <= HY_BANDS, jnp.sin(ang), jnp.where(lane < HY_EMB, jnp.cos(ang), 0.0)))
    hid = jnp.sin(fr_ref[...] * (_dot3(feat, w1_ref[...]) + b1_ref[...]))
    hid = jnp.sin(fr_ref[...] * (_dot3(hid, w2_ref[...]) + b2_ref[...]))
    f = _dot3(hid, w3_ref[...])
    idx, t = pos((tp, c))
    decay = jnp.exp(-(t / n) * dl_ref[...])
    for o in range(HY_ORDER):
        fwd = f[:, (2 * o) * c:(2 * o + 1) * c]
        bwd = f[:, (2 * o + 1) * c:(2 * o + 2) * c]
        val = jnp.where(idx < n, fwd, bwd) * decay
        s_ref[o:o + 1, :] += jnp.sum(jnp.abs(val), axis=0, keepdims=True)
        k_ref[o] = jnp.where(idx == n, 0.0, val)


def hyena_kernels(n, w1, b1, freq, w2, b2, w3):
    c = HY_CH
    tp = min(512, n)
    max_decay = math.log(HY_TARGET) / HY_FAST_DECAY
    min_decay = math.log(HY_TARGET) / HY_SLOW_DECAY
    deltas = jnp.abs(jnp.linspace(min_decay, max_decay, c, dtype=jnp.float32)).reshape(1, c)
    w1p = jnp.zeros((LANES, w1.shape[1]), jnp.float32).at[:w1.shape[0]].set(w1)
    hd = w1.shape[1]
    full = lambda shape: pl.BlockSpec(shape, lambda i: (0,) * len(shape))
    return pl.pallas_call(
        functools.partial(_hy_filter_kernel, n=n),
        grid=(2 * n // tp,),
        in_specs=[full((LANES, hd)), full((1, hd)), full((1, hd)), full((hd, hd)), full((1, hd)),
                  full((hd, HY_ORDER * 2 * c)), full((1, c))],
        out_specs=[pl.BlockSpec((HY_ORDER, tp, c), lambda i: (0, i, 0)), full((HY_ORDER, c))],
        out_shape=[jax.ShapeDtypeStruct((HY_ORDER, 2 * n, c), jnp.float32),
                   jax.ShapeDtypeStruct((HY_ORDER, c), jnp.float32)],
        compiler_params=pltpu.CompilerParams(
            dimension_semantics=("arbitrary",), vmem_limit_bytes=VMEM_LIMIT_BYTES),
        name="hyena_kernels",
    )(w1p, b1.reshape(1, hd), freq.reshape(1, hd), w2, b2.reshape(1, hd), w3, deltas)


def hyena_mixer(p, col0, conv_w, conv_b, filt_args, hy_bias):
    n = p.shape[1]
    uc = short_conv(p, col0, 3 * HY_CH, conv_w, conv_b)
    kerns, norm = hyena_kernels(n, *filt_args)
    if 2 * n != FFT_N:
        v = direct_long_conv(uc, 2, uc, 0, kerns[0], norm[0], hy_bias[0])
        return direct_long_conv(v, 0, uc, 1, kerns[1], norm[1], hy_bias[1])
    stage_a, stage_b = stage_a_table(), stage_b_tables()
    spec = fft_mid(fft_stage_a(kerns, 0, HY_CH, stage_a), norm, stage_b, conv=False)
    v = fft_out_gate(fft_mid(fft_stage_a(uc, 2, HY_CH, stage_a), spec, stage_b, conv=True, order=0),
                     stage_a, uc, 0, uc, 2, hy_bias[0])
    return fft_out_gate(fft_mid(fft_stage_a(v, 0, HY_CH, stage_a), spec, stage_b, conv=True, order=1),
                        stage_a, uc, 1, v, 0, hy_bias[1])


def even_mixer(p, pc, rope_tabs, sink, conv_w, conv_b, fw1, fb1, ffreq, fw2, fb2, fw3, hy_bias, with_ctx):
    q, k, v = qkv_prep(p, 0, A_HEADS, A_KV_HEADS, rope_tabs)
    qc, kc, vc = qkv_prep(pc, 0, A_HEADS, A_KV_HEADS, None)
    o_a = windowed_sink_gqa(q, k, v, kc, vc, sink)
    filt_args = (fw1, fb1, ffreq, fw2, fb2, fw3)
    o_b = hyena_mixer(p, A_Q + 2 * A_KV, conv_w, conv_b, filt_args, hy_bias)
    out_c = None
    if with_ctx:
        o_ac = flash_gqa(qc, kc, vc, sink)
        o_bc = hyena_mixer(pc, A_Q + 2 * A_KV, conv_w, conv_b, filt_args, hy_bias)
        out_c = (o_ac, o_bc)
    return (o_a, o_b), out_c


def odd_mixer(p, pc, rope_tabs, conv_w, a_log, dt_bias, gnorm_w, qnorm_w, knorm_w, with_ctx):
    o_l, o_c = gdn_mixer(p, pc, conv_w, a_log, dt_bias, gnorm_w, with_ctx)
    qd, kd, vd = qkv_prep(p, 4 * C_W, D_HEADS, D_KV_HEADS, rope_tabs, qnorm_w, knorm_w)
    qdc, kdc, vdc = qkv_prep(pc, 4 * C_W, D_HEADS, D_KV_HEADS, None, qnorm_w, knorm_w)
    o_d = flash_gqa(qd, jnp.concatenate([kd, kdc], 2), jnp.concatenate([vd, vdc], 2))
    out_c = None
    if with_ctx:
        out_c = (o_c, flash_gqa(qdc, kdc, vdc))
    return (o_l, o_d), out_c


def kernel(x, c, ctx, c_ctx, ada_w, ada_b, ln1_g, ln1_b, ln2_g, ln2_b, peer_wq, peer_k1, peer_k2, peer_u, peer_v, ev_w_in, ev_w_out, ev_sink, ev_conv_w, ev_conv_b, ev_filt_w1, ev_filt_b1, ev_filt_freq, ev_filt_w2, ev_filt_b2, ev_filt_w3, ev_hy_bias, od_w_in, od_w_out, od_conv_w, od_a_log, od_dt_bias, od_gnorm_w, od_qnorm_w, od_knorm_w):
    rope_tabs = rope_tables(x.shape[1])
    bsz = x.shape[0]
    silu_c = jax.nn.silu(c)
    silu_cc = jax.nn.silu(c_ctx)
    for i in range(DEPTH):
        with_ctx = i < DEPTH - 1
        j = i // 2
        mod = (silu_c @ ada_w[i] + ada_b[i])[:, None, :]
        modc = jnp.broadcast_to((silu_cc @ ada_w[i] + ada_b[i])[None, None, :], (bsz, 1, 6 * D_MODEL))
        sh1, sc1, g1, sh2, sc2, g2 = jnp.split(mod, 6, axis=-1)
        sh1c, sc1c, g1c, sh2c, sc2c, g2c = jnp.split(modc, 6, axis=-1)
        if i % 2 == 0:
            p = mod_matmul(x, sh1, sc1, ev_w_in[j])
            pc = mod_matmul(ctx, sh1c, sc1c, ev_w_in[j])
            out, out_c = even_mixer(p, pc, rope_tabs, ev_sink[j], ev_conv_w[j], ev_conv_b[j],
                                    ev_filt_w1[j], ev_filt_b1[j], ev_filt_freq[j], ev_filt_w2[j], ev_filt_b2[j],
                                    ev_filt_w3[j], ev_hy_bias[j], with_ctx)
            w_out = ev_w_out[j]
        else:
            w_in = od_w_in[j]
            w_in = jnp.concatenate([w_in[:, :4 * C_W], w_in[:, 4 * C_W + C_GATES:],
                                    w_in[:, 4 * C_W:4 * C_W + C_GATES]], axis=1)
            p = mod_matmul(x, sh1, sc1, w_in)
            pc = mod_matmul(ctx, sh1c, sc1c, w_in)
            out, out_c = odd_mixer(p, pc, rope_tabs, od_conv_w[j], od_a_log[j], od_dt_bias[j],
                                   od_gnorm_w[j], od_qnorm_w[j], od_knorm_w[j], with_ctx)
            w_out = od_w_out[j]
        u_bf = peer_u[i].astype(jnp.bfloat16)
        v_bf = peer_v[i].astype(jnp.bfloat16)
        x = proj_residual_ln(out[0], out[1], w_out, x, g1, ln1_g[i], ln1_b[i])
        x = peer_block(x, sh2, sc2, g2, peer_wq[i], peer_k1[i], peer_k2[i], u_bf, v_bf, ln2_g[i], ln2_b[i])
        if with_ctx:
            ctx = proj_residual_ln(out_c[0], out_c[1], w_out, ctx, g1c, ln1_g[i], ln1_b[i])
            ctx = peer_block(ctx, sh2c, sc2c, g2c, peer_wq[i], peer_k1[i], peer_k2[i], u_bf, v_bf,
                             ln2_g[i], ln2_b[i])
    return x
```

```python
import functools
import math

import numpy as np

import jax
import jax.numpy as jnp
from jax import lax
from jax.experimental import pallas as pl
from jax.experimental.pallas import tpu as pltpu

D_MODEL = 1024
DEPTH = 2
GRID_W = 64
HEAD_DIM = 64
BLOCK = 128
ROPE_BASE = 10000.0
EPS = 1e-6

A_HEADS = 8
A_KV_HEADS = 2
WINDOW = 128

HY_CH = 512
HY_ORDER = 2
HY_EMB = 33
HY_BANDS = (HY_EMB - 1) // 2
HY_FAST_DECAY = 0.3
HY_SLOW_DECAY = 1.5
HY_TARGET = 1e-2

C_HEADS = 4
C_DK = 128
C_DV = 128
GDN_CHUNK = 64

D_HEADS = 8
D_KV_HEADS = 2

PEER_HEADS = 8
PEER_NKEYS = 128
PEER_QDIM = 256
PEER_TOPK = 16
PEER_CHUNK = 128

ALPHA = (2 * DEPTH) ** 0.25

A_Q = A_HEADS * HEAD_DIM
A_KV = A_KV_HEADS * HEAD_DIM
C_W = C_HEADS * C_DK
C_GATES = 4 * C_HEADS
D_Q = D_HEADS * HEAD_DIM
D_KV = D_KV_HEADS * HEAD_DIM

VMEM_LIMIT_BYTES = 48 * 1024 * 1024

LANES = 128
_NT = (((1,), (1,)), ((), ()))


def _modmm_kernel(x_ref, sh_ref, sc_ref, w_ref, o_ref):
    h = x_ref[0] * (1.0 + sc_ref[0]) + sh_ref[0]
    o_ref[0] = jnp.dot(h.astype(jnp.bfloat16), w_ref[...], preferred_element_type=jnp.float32)


def mod_matmul(x, shift, scale, w, tm=512, tn=None):
    b, s, k = x.shape
    n = w.shape[1]
    tm = min(tm, s)
    tn = n if tn is None else tn
    wb = w.astype(jnp.bfloat16)
    return pl.pallas_call(
        _modmm_kernel,
        grid=(b, n // tn, s // tm),
        in_specs=[
            pl.BlockSpec((1, tm, k), lambda i, j, m: (i, m, 0)),
            pl.BlockSpec((1, 1, k), lambda i, j, m: (i, 0, 0)),
            pl.BlockSpec((1, 1, k), lambda i, j, m: (i, 0, 0)),
            pl.BlockSpec((k, tn), lambda i, j, m: (0, j)),
        ],
        out_specs=pl.BlockSpec((1, tm, tn), lambda i, j, m: (i, m, j)),
        out_shape=jax.ShapeDtypeStruct((b, s, n), jnp.float32),
        compiler_params=pltpu.CompilerParams(
            dimension_semantics=("arbitrary", "arbitrary", "arbitrary"),
            vmem_limit_bytes=VMEM_LIMIT_BYTES),
        name="mod_matmul",
    )(x, shift, scale, wb)


def _qkv_prep_kernel(q_ref, k_ref, v_ref, cs_ref, sn_ref, qw_ref, kw_ref, gm_ref, qo_ref, ko_ref, vo_ref, *,
                     norm, rope, nq, nkv):
    def prep(x, w, nh):
        if norm:
            ms = jnp.dot(x * x, gm_ref[:x.shape[1], :x.shape[1]], precision=lax.Precision.HIGHEST,
                         preferred_element_type=jnp.float32)
            x = x * lax.rsqrt(ms + EPS) * w
        if rope:
            n = x.shape[1]
            reps = n // cs_ref.shape[1]
            cs = jnp.concatenate([cs_ref[...]] * reps, axis=1) if reps > 1 else cs_ref[...]
            sn = jnp.concatenate([sn_ref[...]] * reps, axis=1) if reps > 1 else sn_ref[...]
            lane = lax.broadcasted_iota(jnp.int32, x.shape, 1)
            nf = HEAD_DIM // 4
            partner = jnp.where((lane & nf) == 0, pltpu.roll(x, n - nf, 1), pltpu.roll(x, nf, 1))
            x = x * cs + partner * sn
        return x

    q = prep(q_ref[0], qw_ref[...], nq) * (HEAD_DIM ** -0.5)
    k = prep(k_ref[0], kw_ref[...], nkv)
    v = v_ref[0]
    for h in range(nq):
        qo_ref[0, h] = q[:, h * HEAD_DIM:(h + 1) * HEAD_DIM].astype(jnp.bfloat16)
    for h in range(nkv):
        ko_ref[0, h] = k[:, h * HEAD_DIM:(h + 1) * HEAD_DIM].astype(jnp.bfloat16)
        vo_ref[0, h] = v[:, h * HEAD_DIM:(h + 1) * HEAD_DIM].astype(jnp.bfloat16)


def qkv_prep(p, col0, nq, nkv, rope_tabs, qw=None, kw=None, tq=512):
    b, s, _ = p.shape
    tq = min(tq, s)
    wq_, wk_ = nq * HEAD_DIM, nkv * HEAD_DIM
    norm = qw is not None
    rope = rope_tabs is not None
    if rope:
        cs, sn = rope_tabs
    else:
        cs = sn = jnp.zeros((s, 2 * HEAD_DIM), jnp.float32)
    qw_t = jnp.tile(qw, nq).reshape(1, wq_) if norm else jnp.ones((1, wq_), jnp.float32)
    kw_t = jnp.tile(kw, nkv).reshape(1, wk_) if norm else jnp.ones((1, wk_), jnp.float32)
    grp = jnp.arange(wq_) // HEAD_DIM
    gm = (grp[:, None] == grp[None, :]).astype(jnp.float32) / HEAD_DIM
    kern = functools.partial(_qkv_prep_kernel, norm=norm, rope=rope, nq=nq, nkv=nkv)
    return pl.pallas_call(
        kern,
        grid=(b, s // tq),
        in_specs=[
            pl.BlockSpec((1, tq, wq_), lambda i, m: (i, m, col0 // wq_)),
            pl.BlockSpec((1, tq, wk_), lambda i, m: (i, m, (col0 + wq_) // wk_)),
            pl.BlockSpec((1, tq, wk_), lambda i, m: (i, m, (col0 + wq_) // wk_ + 1)),
            pl.BlockSpec((tq, 2 * HEAD_DIM), lambda i, m: (m, 0)),
            pl.BlockSpec((tq, 2 * HEAD_DIM), lambda i, m: (m, 0)),
            pl.BlockSpec((1, wq_), lambda i, m: (0, 0)),
            pl.BlockSpec((1, wk_), lambda i, m: (0, 0)),
            pl.BlockSpec((wq_, wq_), lambda i, m: (0, 0)),
        ],
        out_specs=[
            pl.BlockSpec((1, nq, tq, HEAD_DIM), lambda i, m: (i, 0, m, 0)),
            pl.BlockSpec((1, nkv, tq, HEAD_DIM), lambda i, m: (i, 0, m, 0)),
            pl.BlockSpec((1, nkv, tq, HEAD_DIM), lambda i, m: (i, 0, m, 0)),
        ],
        out_shape=[
            jax.ShapeDtypeStruct((b, nq, s, HEAD_DIM), jnp.bfloat16),
            jax.ShapeDtypeStruct((b, nkv, s, HEAD_DIM), jnp.bfloat16),
            jax.ShapeDtypeStruct((b, nkv, s, HEAD_DIM), jnp.bfloat16),
        ],
        compiler_params=pltpu.CompilerParams(
            dimension_semantics=("arbitrary", "arbitrary"), vmem_limit_bytes=VMEM_LIMIT_BYTES),
        name="qkv_prep",
    )(p, p, p, cs, sn, qw_t, kw_t, gm)


def rope_tables(n_tok):
    rows = n_tok // GRID_W
    row = jnp.repeat(jnp.arange(rows, dtype=jnp.float32), GRID_W)
    col = jnp.tile(jnp.arange(GRID_W, dtype=jnp.float32), rows)
    nf = HEAD_DIM // 4
    inv = ROPE_BASE ** (-jnp.arange(nf, dtype=jnp.float32) / nf)
    ar, ac = row[:, None] * inv, col[:, None] * inv
    cs = jnp.concatenate([jnp.cos(ar), jnp.cos(ar), jnp.cos(ac), jnp.cos(ac)], -1)
    sn = jnp.concatenate([-jnp.sin(ar), jnp.sin(ar), -jnp.sin(ac), jnp.sin(ac)], -1)
    return jnp.tile(cs, (1, 2)), jnp.tile(sn, (1, 2))


def _flash_kernel(sink_ref, q_ref, k_ref, v_ref, o_ref, m_s, l_s, acc_s, *, use_sink, grp):
    j = pl.program_id(3)
    tq = q_ref.shape[2]

    @pl.when(j == 0)
    def _():
        m_s[...] = jnp.full_like(m_s, -jnp.inf)
        l_s[...] = jnp.zeros_like(l_s)
        acc_s[...] = jnp.zeros_like(acc_s)

    tk = k_ref.shape[2]
    nt = tk // LANES
    kt = k_ref[0, 0]
    vt = v_ref[0, 0]
    scores = [lax.dot_general(q_ref[0, g], kt, _NT, preferred_element_type=jnp.float32) for g in range(grp)]
    for g, s in enumerate(scores):
        rows = slice(g * tq, (g + 1) * tq)
        tiles = [s[:, c * LANES:(c + 1) * LANES] for c in range(nt)]
        m_tile = functools.reduce(jnp.maximum, tiles)
        m_old = m_s[rows]
        m_new = jnp.maximum(m_old, jnp.broadcast_to(jnp.max(m_tile, axis=1, keepdims=True), m_old.shape))
        alpha = jnp.exp(m_old - m_new)
        p_tiles = [jnp.exp(t - m_new) for t in tiles]
        l_s[rows] = alpha * l_s[rows] + functools.reduce(jnp.add, p_tiles)
        p = jnp.concatenate([t.astype(jnp.bfloat16) for t in p_tiles], axis=1)
        acc_s[rows] = alpha[:, :HEAD_DIM] * acc_s[rows] + jnp.dot(p, vt, preferred_element_type=jnp.float32)
        m_s[rows] = m_new

    @pl.when(j == pl.num_programs(3) - 1)
    def _():
        kvh = pl.program_id(1)
        outs = []
        for g in range(grp):
            rows = slice(g * tq, (g + 1) * tq)
            m = m_s[rows][:, :1]
            l = jnp.sum(l_s[rows], axis=1, keepdims=True)
            acc = acc_s[rows]
            if use_sink:
                sk = sink_ref[kvh * grp + g]
                m2 = jnp.maximum(m, sk)
                a = jnp.exp(m - m2)
                l = a * l + jnp.exp(sk - m2)
                acc = a * acc
            outs.append(acc / l)
        o_ref[0] = jnp.concatenate(outs, axis=1).astype(o_ref.dtype)


def flash_gqa(q, k, v, sink=None, tq=256, tk=2816):
    b, h, s, hd = q.shape
    kvh, lk = k.shape[1], k.shape[2]
    grp = h // kvh
    tq = min(tq, s)
    tk = max(t for t in range(LANES, min(tk, lk) + 1, LANES) if lk % t == 0)
    use_sink = sink is not None
    sink_arr = sink.astype(jnp.float32) if use_sink else jnp.zeros((h,), jnp.float32)
    kern = functools.partial(_flash_kernel, use_sink=use_sink, grp=grp)
    return pl.pallas_call(
        kern,
        grid=(b, kvh, s // tq, lk // tk),
        in_specs=[
            pl.BlockSpec(memory_space=pltpu.SMEM),
            pl.BlockSpec((1, grp, tq, hd), lambda i, c, m, j: (i, c, m, 0)),
            pl.BlockSpec((1, 1, tk, hd), lambda i, c, m, j: (i, c, j, 0)),
            pl.BlockSpec((1, 1, tk, hd), lambda i, c, m, j: (i, c, j, 0)),
        ],
        out_specs=pl.BlockSpec((1, tq, grp * hd), lambda i, c, m, j: (i, m, c)),
        out_shape=jax.ShapeDtypeStruct((b, s, h * hd), jnp.bfloat16),
        scratch_shapes=[pltpu.VMEM((grp * tq, LANES), jnp.float32), pltpu.VMEM((grp * tq, LANES), jnp.float32),
                        pltpu.VMEM((grp * tq, hd), jnp.float32)],
        compiler_params=pltpu.CompilerParams(
            dimension_semantics=("arbitrary",) * 4, vmem_limit_bytes=VMEM_LIMIT_BYTES),
        name="flash_gqa",
    )(sink_arr, q, k, v)


def _window_kernel(sink_ref, q_ref, kp_ref, kc_ref, kn_ref, vp_ref, vc_ref, vn_ref, kx_ref, vx_ref, o_ref, *, grp):
    kvh = pl.program_id(1)
    i = pl.program_id(2)
    nb = pl.num_programs(2)
    kcat = jnp.concatenate([kp_ref[0, 0], kc_ref[0, 0], kn_ref[0, 0], kx_ref[0, 0]], axis=0)
    vcat = jnp.concatenate([vp_ref[0, 0], vc_ref[0, 0], vn_ref[0, 0], vx_ref[0, 0]], axis=0)
    nk = kcat.shape[0]
    r = lax.broadcasted_iota(jnp.int32, (BLOCK, nk), 0)
    c = lax.broadcasted_iota(jnp.int32, (BLOCK, nk), 1)
    off_prev = jnp.where(i > 0, 0, 2 * nk)
    off_next = jnp.where(i < nb - 1, 0, 2 * nk)
    ok_prev = (c >= BLOCK) | (c >= r + off_prev)
    ok_next = (c < 2 * BLOCK) | (c >= 3 * BLOCK) | (c - 2 * BLOCK <= r - off_next)
    valid = ok_prev & ok_next
    scores = [lax.dot_general(q_ref[0, g], kcat, _NT, preferred_element_type=jnp.float32) for g in range(grp)]
    outs = []
    for g, s in enumerate(scores):
        s = jnp.where(valid, s, -jnp.inf)
        sk = sink_ref[kvh * grp + g]
        m = jnp.maximum(jnp.max(s, axis=1, keepdims=True), sk)
        p = jnp.exp(s - m)
        l = jnp.sum(p, axis=1, keepdims=True) + jnp.exp(sk - m)
        o = jnp.dot(p.astype(jnp.bfloat16), vcat, preferred_element_type=jnp.float32)
        outs.append(o / l)
    o_ref[0] = jnp.concatenate(outs, axis=1).astype(o_ref.dtype)


def windowed_sink_gqa(q, k, v, kx, vx, sink):
    b, h, s, hd = q.shape
    kvh = k.shape[1]
    lc = kx.shape[2]
    grp = h // kvh
    nb = s // BLOCK
    kern = functools.partial(_window_kernel, grp=grp)
    blk = lambda f: pl.BlockSpec((1, 1, BLOCK, hd), f)
    prev = lambda i, c, m: (i, c, jnp.maximum(m - 1, 0), 0)
    cur = lambda i, c, m: (i, c, m, 0)
    nxt = lambda i, c, m: (i, c, jnp.minimum(m + 1, nb - 1), 0)
    ctxm = lambda i, c, m: (i, c, 0, 0)
    return pl.pallas_call(
        kern,
        grid=(b, kvh, nb),
        in_specs=[
            pl.BlockSpec(memory_space=pltpu.SMEM),
            pl.BlockSpec((1, grp, BLOCK, hd), cur),
            blk(prev), blk(cur), blk(nxt), blk(prev), blk(cur), blk(nxt),
            pl.BlockSpec((1, 1, lc, hd), ctxm), pl.BlockSpec((1, 1, lc, hd), ctxm),
        ],
        out_specs=pl.BlockSpec((1, BLOCK, grp * hd), lambda i, c, m: (i, m, c)),
        out_shape=jax.ShapeDtypeStruct((b, s, h * hd), jnp.bfloat16),
        compiler_params=pltpu.CompilerParams(
            dimension_semantics=("arbitrary",) * 3, vmem_limit_bytes=VMEM_LIMIT_BYTES),
        name="windowed_sink_gqa",
    )(sink.astype(jnp.float32), q, k, k, k, v, v, v, kx, vx)


def _post_kernel(oa_ref, ob_ref, w_ref, x_ref, g_ref, lg_ref, lb_ref, y_ref):
    ka = oa_ref.shape[2]
    out = jnp.dot(oa_ref[0].astype(jnp.bfloat16), w_ref[:ka], preferred_element_type=jnp.float32)
    out += jnp.dot(ob_ref[0].astype(jnp.bfloat16), w_ref[ka:], preferred_element_type=jnp.float32)
    r = ALPHA * x_ref[0] + g_ref[0] * out
    mu = jnp.mean(r, -1, keepdims=True)
    d = r - mu
    var = jnp.mean(d * d, -1, keepdims=True)
    y_ref[0] = d * lax.rsqrt(var + EPS) * lg_ref[...] + lb_ref[...]


def proj_residual_ln(oa, ob, w, x, gate, ln_g, ln_b, tm=256):
    b, s, ka = oa.shape
    kb = ob.shape[2]
    k = ka + kb
    d = w.shape[1]
    tm = min(tm, s)
    wb = w.astype(jnp.bfloat16)
    return pl.pallas_call(
        _post_kernel,
        grid=(b, s // tm),
        in_specs=[
            pl.BlockSpec((1, tm, ka), lambda i, m: (i, m, 0)),
            pl.BlockSpec((1, tm, kb), lambda i, m: (i, m, 0)),
            pl.BlockSpec((k, d), lambda i, m: (0, 0)),
            pl.BlockSpec((1, tm, d), lambda i, m: (i, m, 0)),
            pl.BlockSpec((1, 1, d), lambda i, m: (i, 0, 0)),
            pl.BlockSpec((1, d), lambda i, m: (0, 0)),
            pl.BlockSpec((1, d), lambda i, m: (0, 0)),
        ],
        out_specs=pl.BlockSpec((1, tm, d), lambda i, m: (i, m, 0)),
        out_shape=jax.ShapeDtypeStruct((b, s, d), jnp.float32),
        compiler_params=pltpu.CompilerParams(
            dimension_semantics=("arbitrary", "arbitrary"),
            vmem_limit_bytes=VMEM_LIMIT_BYTES),
        name="proj_residual_ln",
    )(oa, ob, wb, x, gate, ln_g.reshape(1, d), ln_b.reshape(1, d))


def _top16(s, payload=None):
    n = s.shape[0]
    iota = lax.broadcasted_iota(jnp.int32, s.shape, 0).astype(jnp.float32)
    vals, ids = [], []
    for _ in range(PEER_TOPK):
        m = jnp.max(s, axis=0, keepdims=True)
        pos = jnp.min(jnp.where(s == m, iota, float(n)), axis=0, keepdims=True)
        hit = iota == pos
        vals.append(m)
        ids.append(pos if payload is None else jnp.max(jnp.where(hit, payload, -1.0), axis=0, keepdims=True))
        s = jnp.where(hit, -jnp.inf, s)
    return jnp.concatenate(vals, 0), jnp.concatenate(ids, 0)


def _peer_topk_kernel(q_ref, k1_ref, k2_ref, eid_ref, gate_ref, eid_s, gate_s):
    half = PEER_QDIM // 2

    def head(h, carry):
        off = pl.multiple_of(h * PEER_QDIM, PEER_QDIM)
        q1 = q_ref[:, pl.ds(off, half)]
        q2 = q_ref[:, pl.ds(off + half, half)]
        s1 = lax.dot_general(k1_ref[h], q1, _NT, precision=lax.Precision.HIGHEST,
                             preferred_element_type=jnp.float32)
        s2 = lax.dot_general(k2_ref[h], q2, _NT, precision=lax.Precision.HIGHEST,
                             preferred_element_type=jnp.float32)
        v1, i1 = _top16(s1)
        v2, i2 = _top16(s2)
        k8 = PEER_TOPK // 2
        cand = jnp.concatenate([v1[0:1] + v2] + [v1[i:i + 1] + v2[:k8] for i in range(1, k8)]
                               + [v1[k8:] + v2[0:1]], 0)
        cid = jnp.concatenate([i1[0:1] * PEER_NKEYS + i2]
                              + [i1[i:i + 1] * PEER_NKEYS + i2[:k8] for i in range(1, k8)]
                              + [i1[k8:] * PEER_NKEYS + i2[0:1]], 0)
        best, eid = _top16(cand, cid)
        e = jnp.exp(best - best[0:1])
        gate = e / jnp.sum(e, axis=0, keepdims=True)
        row = pl.multiple_of(h * PEER_TOPK, PEER_TOPK)
        eid_s[pl.ds(row, PEER_TOPK), :] = eid.astype(jnp.int32)
        gate_s[pl.ds(row, PEER_TOPK), :] = gate
        return carry

    lax.fori_loop(0, PEER_HEADS, head, 0)
    eid_ref[...] = eid_s[...].T
    gate_ref[...] = gate_s[...].T


def peer_topk(q, k1, k2, tt=1024):
    t = q.shape[0]
    tt = min(tt, t)
    nsel = PEER_HEADS * PEER_TOPK
    return pl.pallas_call(
        _peer_topk_kernel,
        grid=(t // tt,),
        in_specs=[
            pl.BlockSpec((tt, q.shape[1]), lambda i: (i, 0)),
            pl.BlockSpec(k1.shape, lambda i: (0, 0, 0)),
            pl.BlockSpec(k2.shape, lambda i: (0, 0, 0)),
        ],
        out_specs=[pl.BlockSpec((tt, nsel), lambda i: (i, 0)),
                   pl.BlockSpec((tt, nsel), lambda i: (i, 0))],
        out_shape=[jax.ShapeDtypeStruct((t, nsel), jnp.int32),
                   jax.ShapeDtypeStruct((t, nsel), jnp.float32)],
        scratch_shapes=[pltpu.VMEM((nsel, tt), jnp.int32), pltpu.VMEM((nsel, tt), jnp.float32)],
        compiler_params=pltpu.CompilerParams(
            dimension_semantics=("arbitrary",), vmem_limit_bytes=VMEM_LIMIT_BYTES),
        name="peer_topk",
    )(q, k1, k2)


def _peer_w_kernel(e_ref, g_ref, w_ref):
    nk = PEER_NKEYS
    iota = lax.broadcasted_iota(jnp.int32, (nk, e_ref.shape[1]), 0)

    def tok(t, carry):
        e = e_ref[pl.ds(t, 1), :]
        g = g_ref[pl.ds(t, 1), :]
        a_t = jnp.where(iota == (e >> 7), g, 0.0).astype(jnp.bfloat16)
        b_t = jnp.where(iota == (e & (nk - 1)), 1.0, 0.0).astype(jnp.bfloat16)
        w = lax.dot_general(a_t, b_t, _NT, preferred_element_type=jnp.float32)
        w_ref[t] = w.astype(jnp.bfloat16)
        return carry

    lax.fori_loop(0, e_ref.shape[0], tok, 0, unroll=64)


def peer_dense_gates(eid, gate, tt=128):
    t, nsel = eid.shape
    tt = min(tt, t)
    nk = PEER_NKEYS
    w = pl.pallas_call(
        _peer_w_kernel,
        grid=(t // tt,),
        in_specs=[pl.BlockSpec((tt, nsel), lambda i: (i, 0)),
                  pl.BlockSpec((tt, nsel), lambda i: (i, 0))],
        out_specs=pl.BlockSpec((tt, nk, nk), lambda i: (i, 0, 0)),
        out_shape=jax.ShapeDtypeStruct((t, nk, nk), jnp.bfloat16),
        compiler_params=pltpu.CompilerParams(
            dimension_semantics=("arbitrary",), vmem_limit_bytes=VMEM_LIMIT_BYTES),
        name="peer_dense_gates",
    )(eid, gate)
    return w


def _peer_expert_kernel(x_ref, sh_ref, sc_ref, w_ref, u_ref, v_ref, g_ref, lg_ref, lb_ref, y_ref, xm_s, acc_s):
    e = pl.program_id(2)

    @pl.when(e == 0)
    def _():
        xm_s[...] = (x_ref[0] * (1.0 + sc_ref[0]) + sh_ref[0]).astype(jnp.bfloat16)
        acc_s[...] = jnp.zeros_like(acc_s)

    h = lax.dot_general(xm_s[...], u_ref[...], _NT, preferred_element_type=jnp.float32)
    gelu = 0.5 * h * (1.0 + lax.erf(h * (2.0 ** -0.5)))
    w = w_ref[0].reshape(h.shape)
    a = gelu * w.astype(jnp.float32)
    acc_s[...] += jnp.dot(a.astype(jnp.bfloat16), v_ref[...], preferred_element_type=jnp.float32)

    @pl.when(e == pl.num_programs(2) - 1)
    def _():
        r = ALPHA * x_ref[0] + g_ref[0] * acc_s[...]
        mu = jnp.mean(r, -1, keepdims=True)
        d = r - mu
        var = jnp.mean(d * d, -1, keepdims=True)
        y_ref[0] = d * lax.rsqrt(var + EPS) * lg_ref[...] + lb_ref[...]


def peer_experts_ln(x, shift, scale, w, u_tab, v_tab, gate, ln_g, ln_b, tt=512, te=2048):
    b, s, d = x.shape
    tt = min(tt, s)
    ne = u_tab.shape[0]
    nk = PEER_NKEYS
    w3 = w.reshape(b, s, nk, nk)
    return pl.pallas_call(
        _peer_expert_kernel,
        grid=(b, s // tt, ne // te),
        in_specs=[
            pl.BlockSpec((1, tt, d), lambda i, m, e: (i, m, 0)),
            pl.BlockSpec((1, 1, d), lambda i, m, e: (i, 0, 0)),
            pl.BlockSpec((1, 1, d), lambda i, m, e: (i, 0, 0)),
            pl.BlockSpec((1, tt, te // nk, nk), lambda i, m, e: (i, m, e, 0)),
            pl.BlockSpec((te, d), lambda i, m, e: (e, 0)),
            pl.BlockSpec((te, d), lambda i, m, e: (e, 0)),
            pl.BlockSpec((1, 1, d), lambda i, m, e: (i, 0, 0)),
            pl.BlockSpec((1, d), lambda i, m, e: (0, 0)),
            pl.BlockSpec((1, d), lambda i, m, e: (0, 0)),
        ],
        out_specs=pl.BlockSpec((1, tt, d), lambda i, m, e: (i, m, 0)),
        out_shape=jax.ShapeDtypeStruct((b, s, d), jnp.float32),
        scratch_shapes=[pltpu.VMEM((tt, d), jnp.bfloat16), pltpu.VMEM((tt, d), jnp.float32)],
        compiler_params=pltpu.CompilerParams(
            dimension_semantics=("arbitrary", "arbitrary", "arbitrary"),
            vmem_limit_bytes=VMEM_LIMIT_BYTES),
        name="peer_experts_ln",
    )(x, shift, scale, w3, u_tab, v_tab, gate, ln_g.reshape(1, d), ln_b.reshape(1, d))


def peer_block(x, shift, scale, gate, wq, k1, k2, u_bf, v_bf, ln_g, ln_b):
    b, s, d = x.shape
    q_all = mod_matmul(x, shift, scale, wq).reshape(b * s, -1)
    eid, gsel = peer_topk(q_all, k1, k2)
    w = peer_dense_gates(eid, gsel)
    return peer_experts_ln(x, shift, scale, w, u_bf, v_bf, gate, ln_g, ln_b)


HEAD_LANES = 128


def _short_conv_kernel(x_ref, xp_ref, xn_ref, w_ref, b_ref, o_ref, *, silu, n_l2, n_scaled):
    cb = pl.program_id(1)
    m = pl.program_id(2)
    x = x_ref[0]
    tq, wb = x.shape
    prev_row = jnp.where(m > 0, xp_ref[0][7:8], 0.0)
    next_row = jnp.where(m < pl.num_programs(2) - 1, xn_ref[0][0:1], 0.0)
    row = lax.broadcasted_iota(jnp.int32, x.shape, 0)
    x_m1 = jnp.where(row == 0, prev_row, pltpu.roll(x, 1, 0))
    x_p1 = jnp.where(row == tq - 1, next_row, pltpu.roll(x, tq - 1, 0))
    y = w_ref[0:1] * x_m1 + w_ref[1:2] * x + w_ref[2:3] * x_p1 + b_ref[...]
    if silu:
        y = y * jax.nn.sigmoid(y)
    if n_l2 == 0:
        o_ref[0] = y
        return
    hpb = wb // HEAD_LANES
    for hh in range(hpb):
        gh = cb * hpb + hh
        seg = y[:, hh * HEAD_LANES:(hh + 1) * HEAD_LANES]
        inv = lax.rsqrt(jnp.sum(seg * seg, axis=-1, keepdims=True) + EPS)
        f = jnp.where(gh < n_l2, inv, 1.0) * jnp.where(gh < n_scaled, C_DK ** -0.5, 1.0)
        o_ref[0, :, hh * HEAD_LANES:(hh + 1) * HEAD_LANES] = seg * f


def short_conv(p, col0, width, w, bias=None, silu=False, n_l2=0, n_scaled=0, wb=768, tq=512):
    b, l, _ = p.shape
    tq = min(tq, l)
    bias2 = (jnp.zeros((width,), jnp.float32) if bias is None else bias).reshape(1, width)
    c0 = col0 // wb
    kern = functools.partial(_short_conv_kernel, silu=silu, n_l2=n_l2, n_scaled=n_scaled)
    r8 = tq // 8
    return pl.pallas_call(
        kern,
        grid=(b, width // wb, l // tq),
        in_specs=[
            pl.BlockSpec((1, tq, wb), lambda i, c, m: (i, m, c0 + c)),
            pl.BlockSpec((1, 8, wb), lambda i, c, m: (i, jnp.maximum(m * r8 - 1, 0), c0 + c)),
            pl.BlockSpec((1, 8, wb), lambda i, c, m: (i, jnp.minimum((m + 1) * r8, l // 8 - 1), c0 + c)),
            pl.BlockSpec((3, wb), lambda i, c, m: (0, c)),
            pl.BlockSpec((1, wb), lambda i, c, m: (0, c)),
        ],
        out_specs=pl.BlockSpec((1, tq, wb), lambda i, c, m: (i, m, c)),
        out_shape=jax.ShapeDtypeStruct((b, l, width), jnp.float32),
        compiler_params=pltpu.CompilerParams(
            dimension_semantics=("arbitrary",) * 3, vmem_limit_bytes=VMEM_LIMIT_BYTES),
        name="short_conv",
    )(p, p, p, w, bias2)


def _dot3(a, b):
    ah = a.astype(jnp.bfloat16)
    bh = b.astype(jnp.bfloat16)
    al = (a - ah.astype(jnp.float32)).astype(jnp.bfloat16)
    bl = (b - bh.astype(jnp.float32)).astype(jnp.bfloat16)
    d = functools.partial(jnp.dot, preferred_element_type=jnp.float32)
    return d(ah, bh) + (d(ah, bl) + d(al, bh))


GDN_PAR = 4


def _gdn_chunk_kernel(qkv_ref, beta_ref, g_ref, u_ref, w_ref, qd_ref, kd_ref, in_ref, gl_ref, *, nc):
    d = pl.program_id(0)
    cs = GDN_CHUNK
    ii = lax.broadcasted_iota(jnp.int32, (cs, cs), 0)
    jj = lax.broadcasted_iota(jnp.int32, (cs, cs), 1)
    lo = (ii - jj) * (1 - 2 * d)
    incl = lo >= 0
    strict = lo > 0
    tri = jnp.where(incl, 1.0, 0.0).astype(jnp.bfloat16)
    tri3 = jnp.concatenate([tri, tri, tri], axis=1)
    eye = jnp.where(ii == jj, 1.0, 0.0)

    def chunk_pair(cp, carry):
        probs = []
        for c in [GDN_PAR * cp + i for i in range(GDN_PAR)]:
            rows = pl.ds(pl.multiple_of(c * cs, cs), cs)
            g_c = g_ref[0, 0, rows, :]
            b_c = beta_ref[0, 0, rows, :]
            g_hi = g_c.astype(jnp.bfloat16)
            r1 = g_c - g_hi.astype(jnp.float32)
            g_mid = r1.astype(jnp.bfloat16)
            g_lo = (r1 - g_mid.astype(jnp.float32)).astype(jnp.bfloat16)
            gc = jnp.dot(tri3, jnp.concatenate([g_hi, g_mid, g_lo], axis=0),
                         preferred_element_type=jnp.float32)
            tot = jnp.sum(g_c, axis=0, keepdims=True)
            for h in range(C_HEADS):
                probs.append(dict(c=c, h=h, rows=rows, gc=gc[:, h:h + 1], bt=b_c[:, h:h + 1], tot=tot[:, h:h + 1]))
        for pr in probs:
            h, rows = pr["h"], pr["rows"]
            q = qkv_ref[0, rows, h * HEAD_LANES:(h + 1) * HEAD_LANES]
            k = qkv_ref[0, rows, C_W + h * HEAD_LANES:C_W + (h + 1) * HEAD_LANES]
            kb = k * pr["bt"]
            kq = lax.dot_general(jnp.concatenate([kb, q], axis=0).astype(jnp.bfloat16), k.astype(jnp.bfloat16),
                                 _NT, preferred_element_type=jnp.float32)
            gc_row = jnp.broadcast_to(pr["gc"], (cs, HEAD_LANES)).T[:cs, :]
            dm = jnp.where(incl, jnp.exp(pr["gc"] - gc_row), 0.0)
            x = jnp.where(strict, -(kq[:cs] * dm), 0.0)
            in_ref[0, 0, pr["c"], h] = (kq[cs:] * dm).astype(in_ref.dtype)
            pr.update(t=eye + x, pw=x)
        for _ in range(5):
            for pr in probs:
                pr["pw"] = _dot3(pr["pw"], pr["pw"])
            for pr in probs:
                pr["t"] = pr["t"] + _dot3(pr["t"], pr["pw"])
        for pr in probs:
            h, rows = pr["h"], pr["rows"]
            lanes = slice(h * HEAD_LANES, (h + 1) * HEAD_LANES)
            q = qkv_ref[0, rows, h * HEAD_LANES:(h + 1) * HEAD_LANES]
            k = qkv_ref[0, rows, C_W + h * HEAD_LANES:C_W + (h + 1) * HEAD_LANES]
            v = qkv_ref[0, rows, 2 * C_W + h * HEAD_LANES:2 * C_W + (h + 1) * HEAD_LANES]
            eg = jnp.exp(pr["gc"])
            uw = _dot3(pr["t"], jnp.concatenate([v * pr["bt"], k * (pr["bt"] * eg)], axis=1))
            u_ref[0, 0, rows, lanes] = uw[:, :HEAD_LANES]
            w_ref[0, 0, rows, lanes] = uw[:, HEAD_LANES:].astype(w_ref.dtype)
            qd_ref[0, 0, rows, lanes] = (q * eg).astype(qd_ref.dtype)
            kd_ref[0, 0, rows, lanes] = (k * jnp.exp(pr["tot"] - pr["gc"])).astype(kd_ref.dtype)
            gl_ref[0, 0, pr["c"], h:h + 1, :] = jnp.broadcast_to(jnp.exp(pr["tot"]), (1, HEAD_LANES))
        return carry

    lax.fori_loop(0, nc // GDN_PAR, chunk_pair, 0)


def gdn_chunk_prep(qkv, beta, g, nc=4):
    b, l, _ = qkv.shape
    cs = GDN_CHUNK
    tq = nc * cs
    nchunks = l // cs
    bf = jnp.bfloat16
    big = lambda dt: jax.ShapeDtypeStruct((2, b, l, C_W), dt)
    bspec = pl.BlockSpec((1, 1, tq, C_W), lambda d, i, m: (d, i, m, 0))
    gspec = pl.BlockSpec((1, 1, tq, C_HEADS), lambda d, i, m: (d, i, m, 0))
    return pl.pallas_call(
        functools.partial(_gdn_chunk_kernel, nc=nc),
        grid=(2, b, l // tq),
        in_specs=[pl.BlockSpec((1, tq, 3 * C_W), lambda d, i, m: (i, m, 0)), gspec, gspec],
        out_specs=[bspec, bspec, bspec, bspec,
                   pl.BlockSpec((1, 1, nc, C_HEADS, cs, cs), lambda d, i, m: (d, i, m, 0, 0, 0)),
                   pl.BlockSpec((1, 1, nc, C_HEADS, HEAD_LANES), lambda d, i, m: (d, i, m, 0, 0))],
        out_shape=[big(jnp.float32), big(bf), big(bf), big(bf),
                   jax.ShapeDtypeStruct((2, b, nchunks, C_HEADS, cs, cs), bf),
                   jax.ShapeDtypeStruct((2, b, nchunks, C_HEADS, HEAD_LANES), jnp.float32)],
        compiler_params=pltpu.CompilerParams(
            dimension_semantics=("arbitrary",) * 3, vmem_limit_bytes=VMEM_LIMIT_BYTES),
        name="gdn_chunk_prep",
    )(qkv, beta, g)


def _gdn_scan_kernel(*refs):
    ins, (of_ref, ob_ref, s_ref) = refs[:12], refs[12:]
    step = pl.program_id(1)

    @pl.when(step == 0)
    def _():
        s_ref[...] = jnp.zeros_like(s_ref)

    dot = functools.partial(jnp.dot, preferred_element_type=jnp.float32)
    seqs = [(d, h, slice(h * HEAD_LANES, (h + 1) * HEAD_LANES)) for d in range(2) for h in range(C_HEADS)]
    outs = (of_ref, ob_ref)
    sb, vb = {}, {}
    for d, h, lanes in seqs:
        sb[d, h] = s_ref[d * C_HEADS + h].astype(jnp.bfloat16)
    for d, h, lanes in seqs:
        u_ref, w_ref = ins[6 * d], ins[6 * d + 1]
        vb[d, h] = (u_ref[0, 0, :, lanes] - dot(w_ref[0, 0, :, lanes], sb[d, h])).astype(jnp.bfloat16)
    for d, h, lanes in seqs:
        qd_ref, in_ref = ins[6 * d + 2], ins[6 * d + 4]
        outs[d][0, :, lanes] = dot(qd_ref[0, 0, :, lanes], sb[d, h]) + dot(in_ref[0, 0, 0, h], vb[d, h])
    for d, h, lanes in seqs:
        kd_ref, gl_ref = ins[6 * d + 3], ins[6 * d + 5]
        s_ref[d * C_HEADS + h] = s_ref[d * C_HEADS + h] * gl_ref[0, 0, 0, h:h + 1, :] + lax.dot_general(
            kd_ref[0, 0, :, lanes], vb[d, h], (((0,), (0,)), ((), ())), preferred_element_type=jnp.float32)


def gdn_scan(u, w, qd, kd, intra, gl, n_ctx_chunks):
    _, b, l, _ = u.shape
    cs = GDN_CHUNK
    nchunks = l // cs

    def chunk_of(d, s):
        if d == 0:
            return s
        return jnp.where(s < n_ctx_chunks, n_ctx_chunks - 1 - s, nchunks - 1 + n_ctx_chunks - s)

    in_specs, args = [], []
    for d in range(2):
        big = pl.BlockSpec((1, 1, cs, C_W), lambda i, s, d=d: (d, i, chunk_of(d, s), 0))
        in_specs += [big, big, big, big,
                     pl.BlockSpec((1, 1, 1, C_HEADS, cs, cs), lambda i, s, d=d: (d, i, chunk_of(d, s), 0, 0, 0)),
                     pl.BlockSpec((1, 1, 1, C_HEADS, HEAD_LANES), lambda i, s, d=d: (d, i, chunk_of(d, s), 0, 0))]
        args += [u, w, qd, kd, intra, gl]
    out_specs = [pl.BlockSpec((1, cs, C_W), lambda i, s, d=d: (i, chunk_of(d, s), 0)) for d in range(2)]
    return pl.pallas_call(
        _gdn_scan_kernel,
        grid=(b, nchunks),
        in_specs=in_specs,
        out_specs=out_specs,
        out_shape=[jax.ShapeDtypeStruct((b, l, C_W), jnp.float32)] * 2,
        scratch_shapes=[pltpu.VMEM((2 * C_HEADS, C_DK, C_DV), jnp.float32)],
        compiler_params=pltpu.CompilerParams(
            dimension_semantics=("arbitrary",) * 2, vmem_limit_bytes=VMEM_LIMIT_BYTES),
        name="gdn_scan",
    )(*args)


def _gdn_gate_kernel(of_ref, ob_ref, z_ref, gw_ref, y_ref):
    o = of_ref[0] + ob_ref[0]
    z = z_ref[0]
    for h in range(C_HEADS):
        lanes = slice(h * HEAD_LANES, (h + 1) * HEAD_LANES)
        oh = o[:, lanes]
        zh = z[:, lanes]
        n = oh * lax.rsqrt(jnp.mean(oh * oh, axis=-1, keepdims=True) + EPS) * gw_ref[...]
        y_ref[0, :, lanes] = (n * (zh * jax.nn.sigmoid(zh))).astype(y_ref.dtype)


def gdn_gate(o_f, o_b, row0, p, gnorm_w, tq=256):
    b, l, _ = p.shape
    tq = min(tq, l)
    r0 = row0 // tq
    ospec = pl.BlockSpec((1, tq, C_W), lambda i, m: (i, r0 + m, 0))
    return pl.pallas_call(
        _gdn_gate_kernel,
        grid=(b, l // tq),
        in_specs=[ospec, ospec,
                  pl.BlockSpec((1, tq, C_W), lambda i, m: (i, m, 3)),
                  pl.BlockSpec((1, HEAD_LANES), lambda i, m: (0, 0))],
        out_specs=pl.BlockSpec((1, tq, C_W), lambda i, m: (i, m, 0)),
        out_shape=jax.ShapeDtypeStruct((b, l, C_W), jnp.bfloat16),
        compiler_params=pltpu.CompilerParams(
            dimension_semantics=("arbitrary",) * 2, vmem_limit_bytes=VMEM_LIMIT_BYTES),
        name="gdn_gate",
    )(o_f, o_b, p, gnorm_w.reshape(1, HEAD_LANES))


def gdn_mixer(p, pc, gates_l, gates_c, conv_w, a_log, dt_bias, gnorm_w, with_ctx):
    lc = pc.shape[1]
    conv = functools.partial(short_conv, col0=0, width=3 * C_W, w=conv_w, silu=True,
                             n_l2=2 * C_HEADS, n_scaled=C_HEADS)
    qkv = jnp.concatenate([conv(pc), conv(p)], axis=1)
    gates = jnp.concatenate([gates_c, gates_l], axis=1)
    gates = gates.reshape(gates.shape[0], gates.shape[1], 4, C_HEADS)
    beta = jax.nn.sigmoid(gates[:, :, :2])
    g = -jnp.exp(a_log) * jax.nn.softplus(gates[:, :, 2:] + dt_bias)
    beta = jnp.moveaxis(beta, 2, 0)
    g = jnp.moveaxis(g, 2, 0)
    u, w, qd, kd, intra, gl = gdn_chunk_prep(qkv, beta, g)
    o_f, o_b = gdn_scan(u, w, qd, kd, intra, gl, lc // GDN_CHUNK)
    out = gdn_gate(o_f, o_b, lc, p, gnorm_w)
    out_c = gdn_gate(o_f, o_b, 0, pc, gnorm_w) if with_ctx else None
    return out, out_c


FFT_R = 128
FFT_N = FFT_R * FFT_R
SUB = 8
K1_PER_STEP = 2


def stage_a_table():
    idx = np.arange(FFT_R)
    ang = 2.0 * np.pi * np.outer(idx, idx) / FFT_R
    return jnp.asarray(np.stack([np.cos(ang), -np.sin(ang)], axis=1).reshape(2 * FFT_R, FFT_R), jnp.float32)


def _fft_stage_a_kernel(x_ref, l_ref, y_ref):
    l = l_ref[...]
    c = x_ref.shape[-1]
    xs = jnp.swapaxes(x_ref[0], 0, 1)
    ys = jnp.stack([_dot3(l, xs[j]) for j in range(SUB)], axis=0)
    y_ref[0] = jnp.swapaxes(ys, 0, 1).reshape(FFT_R, 2, SUB, c)


def fft_stage_a(x, col_blk, width, stage_a):
    b, l, wtot = x.shape
    n1cnt = l // FFT_R
    x4 = x.reshape(b, n1cnt, FFT_R, wtot)
    return pl.pallas_call(
        _fft_stage_a_kernel,
        grid=(b, FFT_R // SUB),
        in_specs=[pl.BlockSpec((1, n1cnt, SUB, width), lambda i, j: (i, 0, j, col_blk)),
                  pl.BlockSpec((2 * FFT_R, n1cnt), lambda i, j: (0, 0))],
        out_specs=pl.BlockSpec((1, FFT_R, 2, SUB, width), lambda i, j: (i, 0, 0, j, 0)),
        out_shape=jax.ShapeDtypeStruct((b, FFT_R, 2, FFT_R, width), jnp.float32),
        compiler_params=pltpu.CompilerParams(
            dimension_semantics=("arbitrary",) * 2, vmem_limit_bytes=VMEM_LIMIT_BYTES),
        name="fft_stage_a",
    )(x4, stage_a[:, :n1cnt])


def _dot3_presplit(ah, al, b):
    bh = b.astype(jnp.bfloat16)
    bl = (b - bh.astype(jnp.float32)).astype(jnp.bfloat16)
    d = functools.partial(jnp.dot, preferred_element_type=jnp.float32)
    return d(ah, bh) + (d(ah, bl) + d(al, bh))


def stage_b_tables():
    r = FFT_R
    k1 = jnp.arange(r, dtype=jnp.int32)[:, None, None]
    k2 = jnp.arange(r, dtype=jnp.int32)[None, :, None]
    n2 = jnp.arange(r, dtype=jnp.int32)[None, None, :]
    th = ((n2 * (r * k2 + k1)) % FFT_N).astype(jnp.float32) * (2.0 * math.pi / FFT_N)
    c, s = jnp.cos(th), jnp.sin(th)
    t = jnp.concatenate([jnp.concatenate([c, s], 2), jnp.concatenate([-s, c], 2)], 1)

    def split(m):
        hi = m.astype(jnp.bfloat16)
        return hi, (m - hi.astype(jnp.float32)).astype(jnp.bfloat16)

    return split(t) + split(jnp.swapaxes(t, 1, 2))


def _fft_mid_kernel(y_ref, h_ref, th_ref, tl_ref, ih_ref, il_ref, o_ref, *, conv):
    nb, nk, r, c = y_ref.shape[0], y_ref.shape[1], FFT_R, y_ref.shape[-1]
    probs = [(i, k) for k in range(nk) for i in range(nb)]
    xs = [_dot3_presplit(th_ref[k], tl_ref[k], y_ref[i, k].reshape(2 * r, c)) for i, k in probs]
    if not conv:
        for (i, k), x in zip(probs, xs):
            o_ref[i, k] = (x * ((1.0 / FFT_N) / h_ref[i])).reshape(2, r, c)
        return
    ps = [jnp.concatenate([x[:r] * h_ref[0, k, 0] - x[r:] * h_ref[0, k, 1],
                           x[:r] * h_ref[0, k, 1] + x[r:] * h_ref[0, k, 0]], axis=0) for (i, k), x in zip(probs, xs)]
    for (i, k), p in zip(probs, ps):
        o_ref[i, k] = _dot3_presplit(ih_ref[k], il_ref[k], p).reshape(2, r, c)


def fft_mid(y, h, tables, conv, order=0):
    b, r, _, _, c = y.shape
    hh = h if conv else h.reshape(b, 1, c)
    hspec = (pl.BlockSpec((1, K1_PER_STEP, 2, r, c), lambda k: (order, k, 0, 0, 0)) if conv
             else pl.BlockSpec((b, 1, c), lambda k: (0, 0, 0)))
    blk = pl.BlockSpec((b, K1_PER_STEP, 2, r, c), lambda k: (0, k, 0, 0, 0))
    tspec = pl.BlockSpec((K1_PER_STEP, 2 * r, 2 * r), lambda k: (k, 0, 0))
    return pl.pallas_call(
        functools.partial(_fft_mid_kernel, conv=conv),
        grid=(r // K1_PER_STEP,),
        in_specs=[blk, hspec, tspec, tspec, tspec, tspec],
        out_specs=blk,
        out_shape=jax.ShapeDtypeStruct(y.shape, jnp.float32),
        compiler_params=pltpu.CompilerParams(
            dimension_semantics=("arbitrary",), vmem_limit_bytes=VMEM_LIMIT_BYTES),
        name="fft_mid",
    )(y, hh, *tables)


def _fft_out_kernel(b_ref, l_ref, xg_ref, xin_ref, bias_ref, o_ref):
    l = l_ref[...]
    c = o_ref.shape[-1]
    bs = jnp.swapaxes(b_ref[0].reshape(2 * FFT_R, SUB, c), 0, 1)
    ys = jnp.stack([_dot3(l, bs[j]) for j in range(SUB)], axis=0)
    y = jnp.swapaxes(ys, 0, 1)
    o_ref[0] = xg_ref[0] * (y + bias_ref[...] * xin_ref[0])


def fft_out_gate(bm, stage_a, xg, xg_blk, xin, xin_blk, bias):
    b, r, _, _, c = bm.shape
    l = xg.shape[1]
    n1cnt = l // r
    view = lambda t: t.reshape(b, n1cnt, r, t.shape[-1])
    lhs = stage_a.T[:n1cnt]
    return pl.pallas_call(
        _fft_out_kernel,
        grid=(b, r // SUB),
        in_specs=[pl.BlockSpec((1, r, 2, SUB, c), lambda i, j: (i, 0, 0, j, 0)),
                  pl.BlockSpec((n1cnt, 2 * r), lambda i, j: (0, 0)),
                  pl.BlockSpec((1, n1cnt, SUB, c), lambda i, j: (i, 0, j, xg_blk)),
                  pl.BlockSpec((1, n1cnt, SUB, c), lambda i, j: (i, 0, j, xin_blk)),
                  pl.BlockSpec((1, c), lambda i, j: (0, 0))],
        out_specs=pl.BlockSpec((1, n1cnt, SUB, c), lambda i, j: (i, 0, j, 0)),
        out_shape=jax.ShapeDtypeStruct((b, n1cnt, r, c), jnp.float32),
        compiler_params=pltpu.CompilerParams(
            dimension_semantics=("arbitrary",) * 2, vmem_limit_bytes=VMEM_LIMIT_BYTES),
        name="fft_out_gate",
    )(bm, lhs, view(xg), view(xin), bias.reshape(1, c)).reshape(b, l, c)


def _direct_conv_kernel(xin_ref, xg_ref, kern_ref, d1_ref, d2_ref, norm_ref, bias_ref, o_ref):
    n = xin_ref.shape[1]
    d1 = d1_ref[...]
    x = xin_ref[0]
    xs = _dot3(d1[:, :n], x)
    hs = _dot3(d1, kern_ref[...]) / norm_ref[...]
    xr, xi, hr, hi = xs[:2 * n], xs[2 * n:], hs[:2 * n], hs[2 * n:]
    p = jnp.concatenate([xr * hr - xi * hi, xr * hi + xi * hr], axis=0)
    y = _dot3(d2_ref[...], p)
    o_ref[0] = xg_ref[0] * (y + bias_ref[...] * x)


def direct_long_conv(xin, xin_blk, xg, xg_blk, kern, norm, bias):
    b, n, _ = xin.shape
    c = kern.shape[1]
    idx = np.arange(2 * n)
    ang = 2.0 * np.pi * np.outer(idx, idx) / (2 * n)
    d1 = jnp.asarray(np.concatenate([np.cos(ang), -np.sin(ang)], axis=0), jnp.float32)
    d2 = jnp.asarray(np.concatenate([np.cos(ang[:n]), -np.sin(ang[:n])], axis=1) / (2 * n), jnp.float32)
    return pl.pallas_call(
        _direct_conv_kernel,
        grid=(b,),
        in_specs=[pl.BlockSpec((1, n, c), lambda i: (i, 0, xin_blk)),
                  pl.BlockSpec((1, n, c), lambda i: (i, 0, xg_blk)),
                  pl.BlockSpec((2 * n, c), lambda i: (0, 0)),
                  pl.BlockSpec((4 * n, 2 * n), lambda i: (0, 0)),
                  pl.BlockSpec((n, 4 * n), lambda i: (0, 0)),
                  pl.BlockSpec((1, c), lambda i: (0, 0)),
                  pl.BlockSpec((1, c), lambda i: (0, 0))],
        out_specs=pl.BlockSpec((1, n, c), lambda i: (i, 0, 0)),
        out_shape=jax.ShapeDtypeStruct((b, n, c), jnp.float32),
        compiler_params=pltpu.CompilerParams(
            dimension_semantics=("arbitrary",), vmem_limit_bytes=VMEM_LIMIT_BYTES),
        name="direct_long_conv",
    )(xin, xg, kern, d1, d2, norm.reshape(1, c), bias.reshape(1, c))


def _hy_filter_kernel(w1_ref, b1_ref, fr_ref, w2_ref, b2_ref, w3_ref, dl_ref, k_ref, s_ref, *, n):
    i = pl.program_id(0)
    tp, c = k_ref.shape[1], k_ref.shape[2]

    @pl.when(i == 0)
    def _():
        s_ref[...] = jnp.zeros_like(s_ref)

    def pos(shape):
        idx = i * tp + lax.broadcasted_iota(jnp.int32, shape, 0)
        t = jnp.where(idx < n, idx, jnp.where(idx == n, 0, 2 * n - idx))
        return idx, t.astype(jnp.float32)

    _, t = pos((tp, LANES))
    lane = lax.broadcasted_iota(jnp.int32, (tp, LANES), 1)
    band = jnp.where(lane <= HY_BANDS, lane, lane - HY_BANDS).astype(jnp.float32)
    ang = 2.0 * math.pi * t * band / n
    feat = jnp.where(lane == 0, t / n,
                     jnp.where(lane <= HY_BANDS, jnp.sin(ang), jnp.where(lane < HY_EMB, jnp.cos(ang), 0.0)))
    hid = jnp.sin(fr_ref[...] * (_dot3(feat, w1_ref[...]) + b1_ref[...]))
    hid = jnp.sin(fr_ref[...] * (_dot3(hid, w2_ref[...]) + b2_ref[...]))
    f = _dot3(hid, w3_ref[...])
    idx, t = pos((tp, c))
    decay = jnp.exp(-(t / n) * dl_ref[...])
    for o in range(HY_ORDER):
        fwd = f[:, (2 * o) * c:(2 * o + 1) * c]
        bwd = f[:, (2 * o + 1) * c:(2 * o + 2) * c]
        val = jnp.where(idx < n, fwd, bwd) * decay
        s_ref[o:o + 1, :] += jnp.sum(jnp.abs(val), axis=0, keepdims=True)
        k_ref[o] = jnp.where(idx == n, 0.0, val)


def hyena_kernels(n, w1, b1, freq, w2, b2, w3):
    c = HY_CH
    tp = min(512, n)
    max_decay = math.log(HY_TARGET) / HY_FAST_DECAY
    min_decay = math.log(HY_TARGET) / HY_SLOW_DECAY
    deltas = jnp.abs(jnp.linspace(min_decay, max_decay, c, dtype=jnp.float32)).reshape(1, c)
    w1p = jnp.zeros((LANES, w1.shape[1]), jnp.float32).at[:w1.shape[0]].set(w1)
    hd = w1.shape[1]
    full = lambda shape: pl.BlockSpec(shape, lambda i: (0,) * len(shape))
    return pl.pallas_call(
        functools.partial(_hy_filter_kernel, n=n),
        grid=(2 * n // tp,),
        in_specs=[full((LANES, hd)), full((1, hd)), full((1, hd)), full((hd, hd)), full((1, hd)),
                  full((hd, HY_ORDER * 2 * c)), full((1, c))],
        out_specs=[pl.BlockSpec((HY_ORDER, tp, c), lambda i: (0, i, 0)), full((HY_ORDER, c))],
        out_shape=[jax.ShapeDtypeStruct((HY_ORDER, 2 * n, c), jnp.float32),
                   jax.ShapeDtypeStruct((HY_ORDER, c), jnp.float32)],
        compiler_params=pltpu.CompilerParams(
            dimension_semantics=("arbitrary",), vmem_limit_bytes=VMEM_LIMIT_BYTES),
        name="hyena_kernels",
    )(w1p, b1.reshape(1, hd), freq.reshape(1, hd), w2, b2.reshape(1, hd), w3, deltas)


def hyena_mixer(p, col0, conv_w, conv_b, filt_args, hy_bias):
    n = p.shape[1]
    uc = short_conv(p, col0, 3 * HY_CH, conv_w, conv_b)
    kerns, norm = hyena_kernels(n, *filt_args)
    if 2 * n != FFT_N:
        v = direct_long_conv(uc, 2, uc, 0, kerns[0], norm[0], hy_bias[0])
        return direct_long_conv(v, 0, uc, 1, kerns[1], norm[1], hy_bias[1])
    stage_a, stage_b = stage_a_table(), stage_b_tables()
    spec = fft_mid(fft_stage_a(kerns, 0, HY_CH, stage_a), norm, stage_b, conv=False)
    v = fft_out_gate(fft_mid(fft_stage_a(uc, 2, HY_CH, stage_a), spec, stage_b, conv=True, order=0),
                     stage_a, uc, 0, uc, 2, hy_bias[0])
    return fft_out_gate(fft_mid(fft_stage_a(v, 0, HY_CH, stage_a), spec, stage_b, conv=True, order=1),
                        stage_a, uc, 1, v, 0, hy_bias[1])


def even_mixer(p, pc, rope_tabs, sink, conv_w, conv_b, fw1, fb1, ffreq, fw2, fb2, fw3, hy_bias, with_ctx):
    q, k, v = qkv_prep(p, 0, A_HEADS, A_KV_HEADS, rope_tabs)
    qc, kc, vc = qkv_prep(pc, 0, A_HEADS, A_KV_HEADS, None)
    o_a = windowed_sink_gqa(q, k, v, kc, vc, sink)
    filt_args = (fw1, fb1, ffreq, fw2, fb2, fw3)
    o_b = hyena_mixer(p, A_Q + 2 * A_KV, conv_w, conv_b, filt_args, hy_bias)
    out_c = None
    if with_ctx:
        o_ac = flash_gqa(qc, kc, vc, sink)
        o_bc = hyena_mixer(pc, A_Q + 2 * A_KV, conv_w, conv_b, filt_args, hy_bias)
        out_c = (o_ac, o_bc)
    return (o_a, o_b), out_c


def odd_mixer(p, pc, gates_l, gates_c, rope_tabs, conv_w, a_log, dt_bias, gnorm_w, qnorm_w, knorm_w, with_ctx):
    o_l, o_c = gdn_mixer(p, pc, gates_l, gates_c, conv_w, a_log, dt_bias, gnorm_w, with_ctx)
    qd, kd, vd = qkv_prep(p, 4 * C_W, D_HEADS, D_KV_HEADS, rope_tabs, qnorm_w, knorm_w)
    qdc, kdc, vdc = qkv_prep(pc, 4 * C_W, D_HEADS, D_KV_HEADS, None, qnorm_w, knorm_w)
    o_d = flash_gqa(qd, jnp.concatenate([kd, kdc], 2), jnp.concatenate([vd, vdc], 2))
    out_c = None
    if with_ctx:
        out_c = (o_c, flash_gqa(qdc, kdc, vdc))
    return (o_l, o_d), out_c


def kernel(x, c, ctx, c_ctx, ada_w, ada_b, ln1_g, ln1_b, ln2_g, ln2_b, peer_wq, peer_k1, peer_k2, peer_u, peer_v, ev_w_in, ev_w_out, ev_sink, ev_conv_w, ev_conv_b, ev_filt_w1, ev_filt_b1, ev_filt_freq, ev_filt_w2, ev_filt_b2, ev_filt_w3, ev_hy_bias, od_w_in, od_w_out, od_conv_w, od_a_log, od_dt_bias, od_gnorm_w, od_qnorm_w, od_knorm_w):
    rope_tabs = rope_tables(x.shape[1])
    bsz = x.shape[0]
    silu_c = jax.nn.silu(c)
    silu_cc = jax.nn.silu(c_ctx)
    for i in range(DEPTH):
        with_ctx = i < DEPTH - 1
        j = i // 2
        mod = (silu_c @ ada_w[i] + ada_b[i])[:, None, :]
        modc = jnp.broadcast_to((silu_cc @ ada_w[i] + ada_b[i])[None, None, :], (bsz, 1, 6 * D_MODEL))
        sh1, sc1, g1, sh2, sc2, g2 = jnp.split(mod, 6, axis=-1)
        sh1c, sc1c, g1c, sh2c, sc2c, g2c = jnp.split(modc, 6, axis=-1)
        if i % 2 == 0:
            p = mod_matmul(x, sh1, sc1, ev_w_in[j])
            pc = mod_matmul(ctx, sh1c, sc1c, ev_w_in[j])
            out, out_c = even_mixer(p, pc, rope_tabs, ev_sink[j], ev_conv_w[j], ev_conv_b[j],
                                    ev_filt_w1[j], ev_filt_b1[j], ev_filt_freq[j], ev_filt_w2[j], ev_filt_b2[j],
                                    ev_filt_w3[j], ev_hy_bias[j], with_ctx)
            w_out = ev_w_out[j]
        else:
            w_in = od_w_in[j]
            w_gate = jnp.pad(w_in[:, 4 * C_W:4 * C_W + C_GATES], ((0, 0), (0, LANES - C_GATES)))
            w_in = jnp.concatenate([w_in[:, :4 * C_W], w_in[:, 4 * C_W + C_GATES:]], axis=1)
            p = mod_matmul(x, sh1, sc1, w_in)
            pc = mod_matmul(ctx, sh1c, sc1c, w_in)
            gates_l = mod_matmul(x, sh1, sc1, w_gate)[..., :C_GATES]
            gates_c = mod_matmul(ctx, sh1c, sc1c, w_gate)[..., :C_GATES]
            out, out_c = odd_mixer(p, pc, gates_l, gates_c, rope_tabs, od_conv_w[j], od_a_log[j], od_dt_bias[j],
                                   od_gnorm_w[j], od_qnorm_w[j], od_knorm_w[j], with_ctx)
            w_out = od_w_out[j]
        u_bf = peer_u[i].astype(jnp.bfloat16)
        v_bf = peer_v[i].astype(jnp.bfloat16)
        x = proj_residual_ln(out[0], out[1], w_out, x, g1, ln1_g[i], ln1_b[i])
        x = peer_block(x, sh2, sc2, g2, peer_wq[i], peer_k1[i], peer_k2[i], u_bf, v_bf, ln2_g[i], ln2_b[i])
        if with_ctx:
            ctx = proj_residual_ln(out_c[0], out_c[1], w_out, ctx, g1c, ln1_g[i], ln1_b[i])
            ctx = peer_block(ctx, sh2c, sc2c, g2c, peer_wq[i], peer_k1[i], peer_k2[i], u_bf, v_bf,
                             ln2_g[i], ln2_b[i])
    return x
```

```python
import functools
import math

import numpy as np

import jax
import jax.numpy as jnp
from jax import lax
from jax.experimental import pallas as pl
from jax.experimental.pallas import tpu as pltpu

D_MODEL = 1024
DEPTH = 2
GRID_W = 64
HEAD_DIM = 64
BLOCK = 128
ROPE_BASE = 10000.0
EPS = 1e-6

A_HEADS = 8
A_KV_HEADS = 2
WINDOW = 128

HY_CH = 512
HY_ORDER = 2
HY_EMB = 33
HY_BANDS = (HY_EMB - 1) // 2
HY_FAST_DECAY = 0.3
HY_SLOW_DECAY = 1.5
HY_TARGET = 1e-2

C_HEADS = 4
C_DK = 128
C_DV = 128
GDN_CHUNK = 64

D_HEADS = 8
D_KV_HEADS = 2

PEER_HEADS = 8
PEER_NKEYS = 128
PEER_QDIM = 256
PEER_TOPK = 16
PEER_CHUNK = 128

ALPHA = (2 * DEPTH) ** 0.25

A_Q = A_HEADS * HEAD_DIM
A_KV = A_KV_HEADS * HEAD_DIM
C_W = C_HEADS * C_DK
C_GATES = 4 * C_HEADS
D_Q = D_HEADS * HEAD_DIM
D_KV = D_KV_HEADS * HEAD_DIM

VMEM_LIMIT_BYTES = 48 * 1024 * 1024

LANES = 128
_NT = (((1,), (1,)), ((), ()))


def _modmm_kernel(x_ref, sh_ref, sc_ref, w_ref, o_ref):
    h = x_ref[0] * (1.0 + sc_ref[0]) + sh_ref[0]
    o_ref[0] = jnp.dot(h.astype(jnp.bfloat16), w_ref[...], preferred_element_type=jnp.float32)


def mod_matmul(x, shift, scale, w, tm=512, tn=None):
    b, s, k = x.shape
    n = w.shape[1]
    tm = min(tm, s)
    tn = n if tn is None else tn
    wb = w.astype(jnp.bfloat16)
    return pl.pallas_call(
        _modmm_kernel,
        grid=(b, n // tn, s // tm),
        in_specs=[
            pl.BlockSpec((1, tm, k), lambda i, j, m: (i, m, 0)),
            pl.BlockSpec((1, 1, k), lambda i, j, m: (i, 0, 0)),
            pl.BlockSpec((1, 1, k), lambda i, j, m: (i, 0, 0)),
            pl.BlockSpec((k, tn), lambda i, j, m: (0, j)),
        ],
        out_specs=pl.BlockSpec((1, tm, tn), lambda i, j, m: (i, m, j)),
        out_shape=jax.ShapeDtypeStruct((b, s, n), jnp.float32),
        compiler_params=pltpu.CompilerParams(
            dimension_semantics=("arbitrary", "arbitrary", "arbitrary"),
            vmem_limit_bytes=VMEM_LIMIT_BYTES),
        name="mod_matmul",
    )(x, shift, scale, wb)


def _qkv_prep_kernel(q_ref, k_ref, v_ref, cs_ref, sn_ref, qw_ref, kw_ref, gm_ref, qo_ref, ko_ref, vo_ref, *,
                     norm, rope, nq, nkv):
    def prep(x, w, nh):
        if norm:
            ms = jnp.dot(x * x, gm_ref[:x.shape[1], :x.shape[1]], precision=lax.Precision.HIGHEST,
                         preferred_element_type=jnp.float32)
            x = x * lax.rsqrt(ms + EPS) * w
        if rope:
            n = x.shape[1]
            reps = n // cs_ref.shape[1]
            cs = jnp.concatenate([cs_ref[...]] * reps, axis=1) if reps > 1 else cs_ref[...]
            sn = jnp.concatenate([sn_ref[...]] * reps, axis=1) if reps > 1 else sn_ref[...]
            lane = lax.broadcasted_iota(jnp.int32, x.shape, 1)
            nf = HEAD_DIM // 4
            partner = jnp.where((lane & nf) == 0, pltpu.roll(x, n - nf, 1), pltpu.roll(x, nf, 1))
            x = x * cs + partner * sn
        return x

    q = prep(q_ref[0], qw_ref[...], nq) * (HEAD_DIM ** -0.5)
    k = prep(k_ref[0], kw_ref[...], nkv)
    v = v_ref[0]
    for h in range(nq):
        qo_ref[0, h] = q[:, h * HEAD_DIM:(h + 1) * HEAD_DIM].astype(jnp.bfloat16)
    for h in range(nkv):
        ko_ref[0, h] = k[:, h * HEAD_DIM:(h + 1) * HEAD_DIM].astype(jnp.bfloat16)
        vo_ref[0, h] = v[:, h * HEAD_DIM:(h + 1) * HEAD_DIM].astype(jnp.bfloat16)


def qkv_prep(p, col0, nq, nkv, rope_tabs, qw=None, kw=None, tq=512):
    b, s, _ = p.shape
    tq = min(tq, s)
    wq_, wk_ = nq * HEAD_DIM, nkv * HEAD_DIM
    norm = qw is not None
    rope = rope_tabs is not None
    if rope:
        cs, sn = rope_tabs
    else:
        cs = sn = jnp.zeros((s, 2 * HEAD_DIM), jnp.float32)
    qw_t = jnp.tile(qw, nq).reshape(1, wq_) if norm else jnp.ones((1, wq_), jnp.float32)
    kw_t = jnp.tile(kw, nkv).reshape(1, wk_) if norm else jnp.ones((1, wk_), jnp.float32)
    grp = jnp.arange(wq_) // HEAD_DIM
    gm = (grp[:, None] == grp[None, :]).astype(jnp.float32) / HEAD_DIM
    kern = functools.partial(_qkv_prep_kernel, norm=norm, rope=rope, nq=nq, nkv=nkv)
    return pl.pallas_call(
        kern,
        grid=(b, s // tq),
        in_specs=[
            pl.BlockSpec((1, tq, wq_), lambda i, m: (i, m, col0 // wq_)),
            pl.BlockSpec((1, tq, wk_), lambda i, m: (i, m, (col0 + wq_) // wk_)),
            pl.BlockSpec((1, tq, wk_), lambda i, m: (i, m, (col0 + wq_) // wk_ + 1)),
            pl.BlockSpec((tq, 2 * HEAD_DIM), lambda i, m: (m, 0)),
            pl.BlockSpec((tq, 2 * HEAD_DIM), lambda i, m: (m, 0)),
            pl.BlockSpec((1, wq_), lambda i, m: (0, 0)),
            pl.BlockSpec((1, wk_), lambda i, m: (0, 0)),
            pl.BlockSpec((wq_, wq_), lambda i, m: (0, 0)),
        ],
        out_specs=[
            pl.BlockSpec((1, nq, tq, HEAD_DIM), lambda i, m: (i, 0, m, 0)),
            pl.BlockSpec((1, nkv, tq, HEAD_DIM), lambda i, m: (i, 0, m, 0)),
            pl.BlockSpec((1, nkv, tq, HEAD_DIM), lambda i, m: (i, 0, m, 0)),
        ],
        out_shape=[
            jax.ShapeDtypeStruct((b, nq, s, HEAD_DIM), jnp.bfloat16),
            jax.ShapeDtypeStruct((b, nkv, s, HEAD_DIM), jnp.bfloat16),
            jax.ShapeDtypeStruct((b, nkv, s, HEAD_DIM), jnp.bfloat16),
        ],
        compiler_params=pltpu.CompilerParams(
            dimension_semantics=("arbitrary", "arbitrary"), vmem_limit_bytes=VMEM_LIMIT_BYTES),
        name="qkv_prep",
    )(p, p, p, cs, sn, qw_t, kw_t, gm)


def rope_tables(n_tok):
    rows = n_tok // GRID_W
    row = jnp.repeat(jnp.arange(rows, dtype=jnp.float32), GRID_W)
    col = jnp.tile(jnp.arange(GRID_W, dtype=jnp.float32), rows)
    nf = HEAD_DIM // 4
    inv = ROPE_BASE ** (-jnp.arange(nf, dtype=jnp.float32) / nf)
    ar, ac = row[:, None] * inv, col[:, None] * inv
    cs = jnp.concatenate([jnp.cos(ar), jnp.cos(ar), jnp.cos(ac), jnp.cos(ac)], -1)
    sn = jnp.concatenate([-jnp.sin(ar), jnp.sin(ar), -jnp.sin(ac), jnp.sin(ac)], -1)
    return jnp.tile(cs, (1, 2)), jnp.tile(sn, (1, 2))


def _flash_kernel(sink_ref, q_ref, k_ref, v_ref, o_ref, m_s, l_s, acc_s, *, use_sink, grp):
    j = pl.program_id(3)
    tq = q_ref.shape[2]

    @pl.when(j == 0)
    def _():
        m_s[...] = jnp.full_like(m_s, -jnp.inf)
        l_s[...] = jnp.zeros_like(l_s)
        acc_s[...] = jnp.zeros_like(acc_s)

    tk = k_ref.shape[2]
    nt = tk // LANES
    kt = k_ref[0, 0]
    vt = v_ref[0, 0]
    scores = [lax.dot_general(q_ref[0, g], kt, _NT, preferred_element_type=jnp.float32) for g in range(grp)]
    for g, s in enumerate(scores):
        rows = slice(g * tq, (g + 1) * tq)
        tiles = [s[:, c * LANES:(c + 1) * LANES] for c in range(nt)]
        m_tile = functools.reduce(jnp.maximum, tiles)
        m_old = m_s[rows]
        m_new = jnp.maximum(m_old, jnp.broadcast_to(jnp.max(m_tile, axis=1, keepdims=True), m_old.shape))
        alpha = jnp.exp(m_old - m_new)
        p_tiles = [jnp.exp(t - m_new) for t in tiles]
        l_s[rows] = alpha * l_s[rows] + functools.reduce(jnp.add, p_tiles)
        p = jnp.concatenate([t.astype(jnp.bfloat16) for t in p_tiles], axis=1)
        acc_s[rows] = alpha[:, :HEAD_DIM] * acc_s[rows] + jnp.dot(p, vt, preferred_element_type=jnp.float32)
        m_s[rows] = m_new

    @pl.when(j == pl.num_programs(3) - 1)
    def _():
        kvh = pl.program_id(1)
        outs = []
        for g in range(grp):
            rows = slice(g * tq, (g + 1) * tq)
            m = m_s[rows][:, :1]
            l = jnp.sum(l_s[rows], axis=1, keepdims=True)
            acc = acc_s[rows]
            if use_sink:
                sk = sink_ref[kvh * grp + g]
                m2 = jnp.maximum(m, sk)
                a = jnp.exp(m - m2)
                l = a * l + jnp.exp(sk - m2)
                acc = a * acc
            outs.append(acc / l)
        o_ref[0] = jnp.concatenate(outs, axis=1).astype(o_ref.dtype)


def flash_gqa(q, k, v, sink=None, tq=256, tk=2816):
    b, h, s, hd = q.shape
    kvh, lk = k.shape[1], k.shape[2]
    grp = h // kvh
    tq = min(tq, s)
    tk = max(t for t in range(LANES, min(tk, lk) + 1, LANES) if lk % t == 0)
    use_sink = sink is not None
    sink_arr = sink.astype(jnp.float32) if use_sink else jnp.zeros((h,), jnp.float32)
    kern = functools.partial(_flash_kernel, use_sink=use_sink, grp=grp)
    return pl.pallas_call(
        kern,
        grid=(b, kvh, s // tq, lk // tk),
        in_specs=[
            pl.BlockSpec(memory_space=pltpu.SMEM),
            pl.BlockSpec((1, grp, tq, hd), lambda i, c, m, j: (i, c, m, 0)),
            pl.BlockSpec((1, 1, tk, hd), lambda i, c, m, j: (i, c, j, 0)),
            pl.BlockSpec((1, 1, tk, hd), lambda i, c, m, j: (i, c, j, 0)),
        ],
        out_specs=pl.BlockSpec((1, tq, grp * hd), lambda i, c, m, j: (i, m, c)),
        out_shape=jax.ShapeDtypeStruct((b, s, h * hd), jnp.bfloat16),
        scratch_shapes=[pltpu.VMEM((grp * tq, LANES), jnp.float32), pltpu.VMEM((grp * tq, LANES), jnp.float32),
                        pltpu.VMEM((grp * tq, hd), jnp.float32)],
        compiler_params=pltpu.CompilerParams(
            dimension_semantics=("arbitrary",) * 4, vmem_limit_bytes=VMEM_LIMIT_BYTES),
        name="flash_gqa",
    )(sink_arr, q, k, v)


def _window_kernel(sink_ref, q_ref, kp_ref, kc_ref, kn_ref, vp_ref, vc_ref, vn_ref, kx_ref, vx_ref, o_ref, *, grp):
    kvh = pl.program_id(1)
    i = pl.program_id(2)
    nb = pl.num_programs(2)
    kcat = jnp.concatenate([kp_ref[0, 0], kc_ref[0, 0], kn_ref[0, 0], kx_ref[0, 0]], axis=0)
    vcat = jnp.concatenate([vp_ref[0, 0], vc_ref[0, 0], vn_ref[0, 0], vx_ref[0, 0]], axis=0)
    nk = kcat.shape[0]
    r = lax.broadcasted_iota(jnp.int32, (BLOCK, nk), 0)
    c = lax.broadcasted_iota(jnp.int32, (BLOCK, nk), 1)
    off_prev = jnp.where(i > 0, 0, 2 * nk)
    off_next = jnp.where(i < nb - 1, 0, 2 * nk)
    ok_prev = (c >= BLOCK) | (c >= r + off_prev)
    ok_next = (c < 2 * BLOCK) | (c >= 3 * BLOCK) | (c - 2 * BLOCK <= r - off_next)
    valid = ok_prev & ok_next
    scores = [lax.dot_general(q_ref[0, g], kcat, _NT, preferred_element_type=jnp.float32) for g in range(grp)]
    outs = []
    for g, s in enumerate(scores):
        s = jnp.where(valid, s, -jnp.inf)
        sk = sink_ref[kvh * grp + g]
        m = jnp.maximum(jnp.max(s, axis=1, keepdims=True), sk)
        p = jnp.exp(s - m)
        l = jnp.sum(p, axis=1, keepdims=True) + jnp.exp(sk - m)
        o = jnp.dot(p.astype(jnp.bfloat16), vcat, preferred_element_type=jnp.float32)
        outs.append(o / l)
    o_ref[0] = jnp.concatenate(outs, axis=1).astype(o_ref.dtype)


def windowed_sink_gqa(q, k, v, kx, vx, sink):
    b, h, s, hd = q.shape
    kvh = k.shape[1]
    lc = kx.shape[2]
    grp = h // kvh
    nb = s // BLOCK
    kern = functools.partial(_window_kernel, grp=grp)
    blk = lambda f: pl.BlockSpec((1, 1, BLOCK, hd), f)
    prev = lambda i, c, m: (i, c, jnp.maximum(m - 1, 0), 0)
    cur = lambda i, c, m: (i, c, m, 0)
    nxt = lambda i, c, m: (i, c, jnp.minimum(m + 1, nb - 1), 0)
    ctxm = lambda i, c, m: (i, c, 0, 0)
    return pl.pallas_call(
        kern,
        grid=(b, kvh, nb),
        in_specs=[
            pl.BlockSpec(memory_space=pltpu.SMEM),
            pl.BlockSpec((1, grp, BLOCK, hd), cur),
            blk(prev), blk(cur), blk(nxt), blk(prev), blk(cur), blk(nxt),
            pl.BlockSpec((1, 1, lc, hd), ctxm), pl.BlockSpec((1, 1, lc, hd), ctxm),
        ],
        out_specs=pl.BlockSpec((1, BLOCK, grp * hd), lambda i, c, m: (i, m, c)),
        out_shape=jax.ShapeDtypeStruct((b, s, h * hd), jnp.bfloat16),
        compiler_params=pltpu.CompilerParams(
            dimension_semantics=("arbitrary",) * 3, vmem_limit_bytes=VMEM_LIMIT_BYTES),
        name="windowed_sink_gqa",
    )(sink.astype(jnp.float32), q, k, k, k, v, v, v, kx, vx)


def _post_kernel(oa_ref, ob_ref, w_ref, x_ref, g_ref, lg_ref, lb_ref, y_ref):
    ka = oa_ref.shape[2]
    out = jnp.dot(oa_ref[0].astype(jnp.bfloat16), w_ref[:ka], preferred_element_type=jnp.float32)
    out += jnp.dot(ob_ref[0].astype(jnp.bfloat16), w_ref[ka:], preferred_element_type=jnp.float32)
    r = ALPHA * x_ref[0] + g_ref[0] * out
    mu = jnp.mean(r, -1, keepdims=True)
    d = r - mu
    var = jnp.mean(d * d, -1, keepdims=True)
    y_ref[0] = d * lax.rsqrt(var + EPS) * lg_ref[...] + lb_ref[...]


def proj_residual_ln(oa, ob, w, x, gate, ln_g, ln_b, tm=512):
    b, s, ka = oa.shape
    kb = ob.shape[2]
    k = ka + kb
    d = w.shape[1]
    tm = min(tm, s)
    wb = w.astype(jnp.bfloat16)
    return pl.pallas_call(
        _post_kernel,
        grid=(b, s // tm),
        in_specs=[
            pl.BlockSpec((1, tm, ka), lambda i, m: (i, m, 0)),
            pl.BlockSpec((1, tm, kb), lambda i, m: (i, m, 0)),
            pl.BlockSpec((k, d), lambda i, m: (0, 0)),
            pl.BlockSpec((1, tm, d), lambda i, m: (i, m, 0)),
            pl.BlockSpec((1, 1, d), lambda i, m: (i, 0, 0)),
            pl.BlockSpec((1, d), lambda i, m: (0, 0)),
            pl.BlockSpec((1, d), lambda i, m: (0, 0)),
        ],
        out_specs=pl.BlockSpec((1, tm, d), lambda i, m: (i, m, 0)),
        out_shape=jax.ShapeDtypeStruct((b, s, d), jnp.float32),
        compiler_params=pltpu.CompilerParams(
            dimension_semantics=("arbitrary", "arbitrary"),
            vmem_limit_bytes=VMEM_LIMIT_BYTES),
        name="proj_residual_ln",
    )(oa, ob, wb, x, gate, ln_g.reshape(1, d), ln_b.reshape(1, d))


def _top16(s, payload=None):
    n = s.shape[0]
    iota = lax.broadcasted_iota(jnp.int32, s.shape, 0).astype(jnp.float32)
    vals, ids = [], []
    for _ in range(PEER_TOPK):
        m = jnp.max(s, axis=0, keepdims=True)
        pos = jnp.min(jnp.where(s == m, iota, float(n)), axis=0, keepdims=True)
        hit = iota == pos
        vals.append(m)
        ids.append(pos if payload is None else jnp.max(jnp.where(hit, payload, -1.0), axis=0, keepdims=True))
        s = jnp.where(hit, -jnp.inf, s)
    return jnp.concatenate(vals, 0), jnp.concatenate(ids, 0)


def _peer_topk_kernel(q_ref, k1_ref, k2_ref, eid_ref, gate_ref, eid_s, gate_s):
    half = PEER_QDIM // 2

    def head(h, carry):
        off = pl.multiple_of(h * PEER_QDIM, PEER_QDIM)
        q1 = q_ref[:, pl.ds(off, half)]
        q2 = q_ref[:, pl.ds(off + half, half)]
        s1 = lax.dot_general(k1_ref[h], q1, _NT, precision=lax.Precision.HIGHEST,
                             preferred_element_type=jnp.float32)
        s2 = lax.dot_general(k2_ref[h], q2, _NT, precision=lax.Precision.HIGHEST,
                             preferred_element_type=jnp.float32)
        v1, i1 = _top16(s1)
        v2, i2 = _top16(s2)
        k8 = PEER_TOPK // 2
        cand = jnp.concatenate([v1[0:1] + v2] + [v1[i:i + 1] + v2[:k8] for i in range(1, k8)]
                               + [v1[k8:] + v2[0:1]], 0)
        cid = jnp.concatenate([i1[0:1] * PEER_NKEYS + i2]
                              + [i1[i:i + 1] * PEER_NKEYS + i2[:k8] for i in range(1, k8)]
                              + [i1[k8:] * PEER_NKEYS + i2[0:1]], 0)
        best, eid = _top16(cand, cid)
        e = jnp.exp(best - best[0:1])
        gate = e / jnp.sum(e, axis=0, keepdims=True)
        row = pl.multiple_of(h * PEER_TOPK, PEER_TOPK)
        eid_s[pl.ds(row, PEER_TOPK), :] = eid.astype(jnp.int32)
        gate_s[pl.ds(row, PEER_TOPK), :] = gate
        return carry

    lax.fori_loop(0, PEER_HEADS, head, 0)
    eid_ref[...] = eid_s[...].T
    gate_ref[...] = gate_s[...].T


def peer_topk(q, k1, k2, tt=1024):
    t = q.shape[0]
    tt = min(tt, t)
    nsel = PEER_HEADS * PEER_TOPK
    return pl.pallas_call(
        _peer_topk_kernel,
        grid=(t // tt,),
        in_specs=[
            pl.BlockSpec((tt, q.shape[1]), lambda i: (i, 0)),
            pl.BlockSpec(k1.shape, lambda i: (0, 0, 0)),
            pl.BlockSpec(k2.shape, lambda i: (0, 0, 0)),
        ],
        out_specs=[pl.BlockSpec((tt, nsel), lambda i: (i, 0)),
                   pl.BlockSpec((tt, nsel), lambda i: (i, 0))],
        out_shape=[jax.ShapeDtypeStruct((t, nsel), jnp.int32),
                   jax.ShapeDtypeStruct((t, nsel), jnp.float32)],
        scratch_shapes=[pltpu.VMEM((nsel, tt), jnp.int32), pltpu.VMEM((nsel, tt), jnp.float32)],
        compiler_params=pltpu.CompilerParams(
            dimension_semantics=("arbitrary",), vmem_limit_bytes=VMEM_LIMIT_BYTES),
        name="peer_topk",
    )(q, k1, k2)


def _peer_w_kernel(e_ref, g_ref, w_ref):
    nk = PEER_NKEYS
    iota = lax.broadcasted_iota(jnp.int32, (nk, e_ref.shape[1]), 0)

    def tok(t, carry):
        e = e_ref[pl.ds(t, 1), :]
        g = g_ref[pl.ds(t, 1), :]
        a_t = jnp.where(iota == (e >> 7), g, 0.0).astype(jnp.bfloat16)
        b_t = jnp.where(iota == (e & (nk - 1)), 1.0, 0.0).astype(jnp.bfloat16)
        w = lax.dot_general(a_t, b_t, _NT, preferred_element_type=jnp.float32)
        w_ref[t] = w.astype(jnp.bfloat16)
        return carry

    lax.fori_loop(0, e_ref.shape[0], tok, 0, unroll=64)


def peer_dense_gates(eid, gate, tt=128):
    t, nsel = eid.shape
    tt = min(tt, t)
    nk = PEER_NKEYS
    w = pl.pallas_call(
        _peer_w_kernel,
        grid=(t // tt,),
        in_specs=[pl.BlockSpec((tt, nsel), lambda i: (i, 0)),
                  pl.BlockSpec((tt, nsel), lambda i: (i, 0))],
        out_specs=pl.BlockSpec((tt, nk, nk), lambda i: (i, 0, 0)),
        out_shape=jax.ShapeDtypeStruct((t, nk, nk), jnp.bfloat16),
        compiler_params=pltpu.CompilerParams(
            dimension_semantics=("arbitrary",), vmem_limit_bytes=VMEM_LIMIT_BYTES),
        name="peer_dense_gates",
    )(eid, gate)
    return w


def _peer_expert_kernel(x_ref, sh_ref, sc_ref, w_ref, u_ref, v_ref, g_ref, lg_ref, lb_ref, y_ref, xm_s, acc_s):
    e = pl.program_id(2)

    @pl.when(e == 0)
    def _():
        xm_s[...] = (x_ref[0] * (1.0 + sc_ref[0]) + sh_ref[0]).astype(jnp.bfloat16)
        acc_s[...] = jnp.zeros_like(acc_s)

    h = lax.dot_general(xm_s[...], u_ref[...], _NT, preferred_element_type=jnp.float32)
    gelu = 0.5 * h * (1.0 + lax.erf(h * (2.0 ** -0.5)))
    w = w_ref[0].reshape(h.shape)
    a = gelu * w.astype(jnp.float32)
    acc_s[...] += jnp.dot(a.astype(jnp.bfloat16), v_ref[...], preferred_element_type=jnp.float32)

    @pl.when(e == pl.num_programs(2) - 1)
    def _():
        r = ALPHA * x_ref[0] + g_ref[0] * acc_s[...]
        mu = jnp.mean(r, -1, keepdims=True)
        d = r - mu
        var = jnp.mean(d * d, -1, keepdims=True)
        y_ref[0] = d * lax.rsqrt(var + EPS) * lg_ref[...] + lb_ref[...]


def peer_experts_ln(x, shift, scale, w, u_tab, v_tab, gate, ln_g, ln_b, tt=512, te=2048):
    b, s, d = x.shape
    tt = min(tt, s)
    ne = u_tab.shape[0]
    nk = PEER_NKEYS
    w3 = w.reshape(b, s, nk, nk)
    return pl.pallas_call(
        _peer_expert_kernel,
        grid=(b, s // tt, ne // te),
        in_specs=[
            pl.BlockSpec((1, tt, d), lambda i, m, e: (i, m, 0)),
            pl.BlockSpec((1, 1, d), lambda i, m, e: (i, 0, 0)),
            pl.BlockSpec((1, 1, d), lambda i, m, e: (i, 0, 0)),
            pl.BlockSpec((1, tt, te // nk, nk), lambda i, m, e: (i, m, e, 0)),
            pl.BlockSpec((te, d), lambda i, m, e: (e, 0)),
            pl.BlockSpec((te, d), lambda i, m, e: (e, 0)),
            pl.BlockSpec((1, 1, d), lambda i, m, e: (i, 0, 0)),
            pl.BlockSpec((1, d), lambda i, m, e: (0, 0)),
            pl.BlockSpec((1, d), lambda i, m, e: (0, 0)),
        ],
        out_specs=pl.BlockSpec((1, tt, d), lambda i, m, e: (i, m, 0)),
        out_shape=jax.ShapeDtypeStruct((b, s, d), jnp.float32),
        scratch_shapes=[pltpu.VMEM((tt, d), jnp.bfloat16), pltpu.VMEM((tt, d), jnp.float32)],
        compiler_params=pltpu.CompilerParams(
            dimension_semantics=("arbitrary", "arbitrary", "arbitrary"),
            vmem_limit_bytes=VMEM_LIMIT_BYTES),
        name="peer_experts_ln",
    )(x, shift, scale, w3, u_tab, v_tab, gate, ln_g.reshape(1, d), ln_b.reshape(1, d))


def _cast_kernel(x_ref, o_ref):
    o_ref[...] = x_ref[0].astype(o_ref.dtype)


def layer_table_bf16(tab, layer, tr=2048):
    _, ne, d = tab.shape
    return pl.pallas_call(
        _cast_kernel,
        grid=(ne // tr,),
        in_specs=[pl.BlockSpec((1, tr, d), lambda r: (layer, r, 0))],
        out_specs=pl.BlockSpec((tr, d), lambda r: (r, 0)),
        out_shape=jax.ShapeDtypeStruct((ne, d), jnp.bfloat16),
        compiler_params=pltpu.CompilerParams(
            dimension_semantics=("arbitrary",), vmem_limit_bytes=VMEM_LIMIT_BYTES),
        name="layer_table_bf16",
    )(tab)


def peer_block(x, shift, scale, gate, wq, k1, k2, u_bf, v_bf, ln_g, ln_b):
    b, s, d = x.shape
    q_all = mod_matmul(x, shift, scale, wq).reshape(b * s, -1)
    eid, gsel = peer_topk(q_all, k1, k2)
    w = peer_dense_gates(eid, gsel)
    return peer_experts_ln(x, shift, scale, w, u_bf, v_bf, gate, ln_g, ln_b)


HEAD_LANES = 128


def _short_conv_kernel(x_ref, xp_ref, xn_ref, w_ref, b_ref, o_ref, *, silu, n_l2, n_scaled):
    cb = pl.program_id(1)
    m = pl.program_id(2)
    x = x_ref[0]
    tq, wb = x.shape
    prev_row = jnp.where(m > 0, xp_ref[0][7:8], 0.0)
    next_row = jnp.where(m < pl.num_programs(2) - 1, xn_ref[0][0:1], 0.0)
    row = lax.broadcasted_iota(jnp.int32, x.shape, 0)
    x_m1 = jnp.where(row == 0, prev_row, pltpu.roll(x, 1, 0))
    x_p1 = jnp.where(row == tq - 1, next_row, pltpu.roll(x, tq - 1, 0))
    y = w_ref[0:1] * x_m1 + w_ref[1:2] * x + w_ref[2:3] * x_p1 + b_ref[...]
    if silu:
        y = y * jax.nn.sigmoid(y)
    if n_l2 == 0:
        o_ref[0] = y
        return
    hpb = wb // HEAD_LANES
    for hh in range(hpb):
        gh = cb * hpb + hh
        seg = y[:, hh * HEAD_LANES:(hh + 1) * HEAD_LANES]
        inv = lax.rsqrt(jnp.sum(seg * seg, axis=-1, keepdims=True) + EPS)
        f = jnp.where(gh < n_l2, inv, 1.0) * jnp.where(gh < n_scaled, C_DK ** -0.5, 1.0)
        o_ref[0, :, hh * HEAD_LANES:(hh + 1) * HEAD_LANES] = seg * f


def short_conv(p, col0, width, w, bias=None, silu=False, n_l2=0, n_scaled=0, wb=768, tq=512):
    b, l, _ = p.shape
    tq = min(tq, l)
    bias2 = (jnp.zeros((width,), jnp.float32) if bias is None else bias).reshape(1, width)
    c0 = col0 // wb
    kern = functools.partial(_short_conv_kernel, silu=silu, n_l2=n_l2, n_scaled=n_scaled)
    r8 = tq // 8
    return pl.pallas_call(
        kern,
        grid=(b, width // wb, l // tq),
        in_specs=[
            pl.BlockSpec((1, tq, wb), lambda i, c, m: (i, m, c0 + c)),
            pl.BlockSpec((1, 8, wb), lambda i, c, m: (i, jnp.maximum(m * r8 - 1, 0), c0 + c)),
            pl.BlockSpec((1, 8, wb), lambda i, c, m: (i, jnp.minimum((m + 1) * r8, l // 8 - 1), c0 + c)),
            pl.BlockSpec((3, wb), lambda i, c, m: (0, c)),
            pl.BlockSpec((1, wb), lambda i, c, m: (0, c)),
        ],
        out_specs=pl.BlockSpec((1, tq, wb), lambda i, c, m: (i, m, c)),
        out_shape=jax.ShapeDtypeStruct((b, l, width), jnp.float32),
        compiler_params=pltpu.CompilerParams(
            dimension_semantics=("arbitrary",) * 3, vmem_limit_bytes=VMEM_LIMIT_BYTES),
        name="short_conv",
    )(p, p, p, w, bias2)


def _dot3(a, b):
    ah = a.astype(jnp.bfloat16)
    bh = b.astype(jnp.bfloat16)
    al = (a - ah.astype(jnp.float32)).astype(jnp.bfloat16)
    bl = (b - bh.astype(jnp.float32)).astype(jnp.bfloat16)
    d = functools.partial(jnp.dot, preferred_element_type=jnp.float32)
    return d(ah, bh) + (d(ah, bl) + d(al, bh))


GDN_PAR = 4


def _gdn_chunk_kernel(qkv_ref, beta_ref, g_ref, u_ref, w_ref, qd_ref, kd_ref, in_ref, gl_ref, *, nc):
    d = pl.program_id(0)
    cs = GDN_CHUNK
    ii = lax.broadcasted_iota(jnp.int32, (cs, cs), 0)
    jj = lax.broadcasted_iota(jnp.int32, (cs, cs), 1)
    lo = (ii - jj) * (1 - 2 * d)
    incl = lo >= 0
    strict = lo > 0
    tri = jnp.where(incl, 1.0, 0.0).astype(jnp.bfloat16)
    tri3 = jnp.concatenate([tri, tri, tri], axis=1)
    eye = jnp.where(ii == jj, 1.0, 0.0)

    def chunk_pair(cp, carry):
        probs = []
        for c in [GDN_PAR * cp + i for i in range(GDN_PAR)]:
            rows = pl.ds(pl.multiple_of(c * cs, cs), cs)
            g_c = g_ref[0, 0, rows, :]
            b_c = beta_ref[0, 0, rows, :]
            g_hi = g_c.astype(jnp.bfloat16)
            r1 = g_c - g_hi.astype(jnp.float32)
            g_mid = r1.astype(jnp.bfloat16)
            g_lo = (r1 - g_mid.astype(jnp.float32)).astype(jnp.bfloat16)
            gc = jnp.dot(tri3, jnp.concatenate([g_hi, g_mid, g_lo], axis=0),
                         preferred_element_type=jnp.float32)
            tot = jnp.sum(g_c, axis=0, keepdims=True)
            for h in range(C_HEADS):
                probs.append(dict(c=c, h=h, rows=rows, gc=gc[:, h:h + 1], bt=b_c[:, h:h + 1], tot=tot[:, h:h + 1]))
        for pr in probs:
            h, rows = pr["h"], pr["rows"]
            q = qkv_ref[0, rows, h * HEAD_LANES:(h + 1) * HEAD_LANES]
            k = qkv_ref[0, rows, C_W + h * HEAD_LANES:C_W + (h + 1) * HEAD_LANES]
            kb = k * pr["bt"]
            kq = lax.dot_general(jnp.concatenate([kb, q], axis=0).astype(jnp.bfloat16), k.astype(jnp.bfloat16),
                                 _NT, preferred_element_type=jnp.float32)
            gc_row = jnp.broadcast_to(pr["gc"], (cs, HEAD_LANES)).T[:cs, :]
            dm = jnp.where(incl, jnp.exp(pr["gc"] - gc_row), 0.0)
            x = jnp.where(strict, -(kq[:cs] * dm), 0.0)
            in_ref[0, 0, pr["c"], h] = (kq[cs:] * dm).astype(in_ref.dtype)
            pr.update(t=eye + x, pw=x)
        for _ in range(5):
            for pr in probs:
                pr["pw"] = _dot3(pr["pw"], pr["pw"])
            for pr in probs:
                pr["t"] = pr["t"] + _dot3(pr["t"], pr["pw"])
        for pr in probs:
            h, rows = pr["h"], pr["rows"]
            lanes = slice(h * HEAD_LANES, (h + 1) * HEAD_LANES)
            q = qkv_ref[0, rows, h * HEAD_LANES:(h + 1) * HEAD_LANES]
            k = qkv_ref[0, rows, C_W + h * HEAD_LANES:C_W + (h + 1) * HEAD_LANES]
            v = qkv_ref[0, rows, 2 * C_W + h * HEAD_LANES:2 * C_W + (h + 1) * HEAD_LANES]
            eg = jnp.exp(pr["gc"])
            uw = _dot3(pr["t"], jnp.concatenate([v * pr["bt"], k * (pr["bt"] * eg)], axis=1))
            u_ref[0, 0, rows, lanes] = uw[:, :HEAD_LANES]
            w_ref[0, 0, rows, lanes] = uw[:, HEAD_LANES:].astype(w_ref.dtype)
            qd_ref[0, 0, rows, lanes] = (q * eg).astype(qd_ref.dtype)
            kd_ref[0, 0, rows, lanes] = (k * jnp.exp(pr["tot"] - pr["gc"])).astype(kd_ref.dtype)
            gl_ref[0, 0, pr["c"], h:h + 1, :] = jnp.broadcast_to(jnp.exp(pr["tot"]), (1, HEAD_LANES))
        return carry

    lax.fori_loop(0, nc // GDN_PAR, chunk_pair, 0)


def gdn_chunk_prep(qkv, beta, g, nc=4):
    b, l, _ = qkv.shape
    cs = GDN_CHUNK
    tq = nc * cs
    nchunks = l // cs
    bf = jnp.bfloat16
    big = lambda dt: jax.ShapeDtypeStruct((2, b, l, C_W), dt)
    bspec = pl.BlockSpec((1, 1, tq, C_W), lambda d, i, m: (d, i, m, 0))
    gspec = pl.BlockSpec((1, 1, tq, C_HEADS), lambda d, i, m: (d, i, m, 0))
    return pl.pallas_call(
        functools.partial(_gdn_chunk_kernel, nc=nc),
        grid=(2, b, l // tq),
        in_specs=[pl.BlockSpec((1, tq, 3 * C_W), lambda d, i, m: (i, m, 0)), gspec, gspec],
        out_specs=[bspec, bspec, bspec, bspec,
                   pl.BlockSpec((1, 1, nc, C_HEADS, cs, cs), lambda d, i, m: (d, i, m, 0, 0, 0)),
                   pl.BlockSpec((1, 1, nc, C_HEADS, HEAD_LANES), lambda d, i, m: (d, i, m, 0, 0))],
        out_shape=[big(jnp.float32), big(bf), big(bf), big(bf),
                   jax.ShapeDtypeStruct((2, b, nchunks, C_HEADS, cs, cs), bf),
                   jax.ShapeDtypeStruct((2, b, nchunks, C_HEADS, HEAD_LANES), jnp.float32)],
        compiler_params=pltpu.CompilerParams(
            dimension_semantics=("arbitrary",) * 3, vmem_limit_bytes=VMEM_LIMIT_BYTES),
        name="gdn_chunk_prep",
    )(qkv, beta, g)


def _gdn_scan_kernel(*refs):
    ins, (of_ref, ob_ref, s_ref) = refs[:12], refs[12:]
    step = pl.program_id(1)

    @pl.when(step == 0)
    def _():
        s_ref[...] = jnp.zeros_like(s_ref)

    dot = functools.partial(jnp.dot, preferred_element_type=jnp.float32)
    seqs = [(d, h, slice(h * HEAD_LANES, (h + 1) * HEAD_LANES)) for d in range(2) for h in range(C_HEADS)]
    outs = (of_ref, ob_ref)
    sb, vb = {}, {}
    for d, h, lanes in seqs:
        sb[d, h] = s_ref[d * C_HEADS + h].astype(jnp.bfloat16)
    for d, h, lanes in seqs:
        u_ref, w_ref = ins[6 * d], ins[6 * d + 1]
        vb[d, h] = (u_ref[0, 0, :, lanes] - dot(w_ref[0, 0, :, lanes], sb[d, h])).astype(jnp.bfloat16)
    for d, h, lanes in seqs:
        qd_ref, in_ref = ins[6 * d + 2], ins[6 * d + 4]
        outs[d][0, :, lanes] = dot(qd_ref[0, 0, :, lanes], sb[d, h]) + dot(in_ref[0, 0, 0, h], vb[d, h])
    for d, h, lanes in seqs:
        kd_ref, gl_ref = ins[6 * d + 3], ins[6 * d + 5]
        s_ref[d * C_HEADS + h] = s_ref[d * C_HEADS + h] * gl_ref[0, 0, 0, h:h + 1, :] + lax.dot_general(
            kd_ref[0, 0, :, lanes], vb[d, h], (((0,), (0,)), ((), ())), preferred_element_type=jnp.float32)


def gdn_scan(u, w, qd, kd, intra, gl, n_ctx_chunks):
    _, b, l, _ = u.shape
    cs = GDN_CHUNK
    nchunks = l // cs

    def chunk_of(d, s):
        if d == 0:
            return s
        return jnp.where(s < n_ctx_chunks, n_ctx_chunks - 1 - s, nchunks - 1 + n_ctx_chunks - s)

    in_specs, args = [], []
    for d in range(2):
        big = pl.BlockSpec((1, 1, cs, C_W), lambda i, s, d=d: (d, i, chunk_of(d, s), 0))
        in_specs += [big, big, big, big,
                     pl.BlockSpec((1, 1, 1, C_HEADS, cs, cs), lambda i, s, d=d: (d, i, chunk_of(d, s), 0, 0, 0)),
                     pl.BlockSpec((1, 1, 1, C_HEADS, HEAD_LANES), lambda i, s, d=d: (d, i, chunk_of(d, s), 0, 0))]
        args += [u, w, qd, kd, intra, gl]
    out_specs = [pl.BlockSpec((1, cs, C_W), lambda i, s, d=d: (i, chunk_of(d, s), 0)) for d in range(2)]
    return pl.pallas_call(
        _gdn_scan_kernel,
        grid=(b, nchunks),
        in_specs=in_specs,
        out_specs=out_specs,
        out_shape=[jax.ShapeDtypeStruct((b, l, C_W), jnp.float32)] * 2,
        scratch_shapes=[pltpu.VMEM((2 * C_HEADS, C_DK, C_DV), jnp.float32)],
        compiler_params=pltpu.CompilerParams(
            dimension_semantics=("arbitrary",) * 2, vmem_limit_bytes=VMEM_LIMIT_BYTES),
        name="gdn_scan",
    )(*args)


def _gdn_gate_kernel(of_ref, ob_ref, z_ref, gw_ref, y_ref):
    o = of_ref[0] + ob_ref[0]
    z = z_ref[0]
    for h in range(C_HEADS):
        lanes = slice(h * HEAD_LANES, (h + 1) * HEAD_LANES)
        oh = o[:, lanes]
        zh = z[:, lanes]
        n = oh * lax.rsqrt(jnp.mean(oh * oh, axis=-1, keepdims=True) + EPS) * gw_ref[...]
        y_ref[0, :, lanes] = (n * (zh * jax.nn.sigmoid(zh))).astype(y_ref.dtype)


def gdn_gate(o_f, o_b, row0, p, gnorm_w, tq=256):
    b, l, _ = p.shape
    tq = min(tq, l)
    r0 = row0 // tq
    ospec = pl.BlockSpec((1, tq, C_W), lambda i, m: (i, r0 + m, 0))
    return pl.pallas_call(
        _gdn_gate_kernel,
        grid=(b, l // tq),
        in_specs=[ospec, ospec,
                  pl.BlockSpec((1, tq, C_W), lambda i, m: (i, m, 3)),
                  pl.BlockSpec((1, HEAD_LANES), lambda i, m: (0, 0))],
        out_specs=pl.BlockSpec((1, tq, C_W), lambda i, m: (i, m, 0)),
        out_shape=jax.ShapeDtypeStruct((b, l, C_W), jnp.bfloat16),
        compiler_params=pltpu.CompilerParams(
            dimension_semantics=("arbitrary",) * 2, vmem_limit_bytes=VMEM_LIMIT_BYTES),
        name="gdn_gate",
    )(o_f, o_b, p, gnorm_w.reshape(1, HEAD_LANES))


def gdn_mixer(p, pc, gates_l, gates_c, conv_w, a_log, dt_bias, gnorm_w, with_ctx):
    lc = pc.shape[1]
    conv = functools.partial(short_conv, col0=0, width=3 * C_W, w=conv_w, silu=True,
                             n_l2=2 * C_HEADS, n_scaled=C_HEADS)
    qkv = jnp.concatenate([conv(pc), conv(p)], axis=1)
    gates = jnp.concatenate([gates_c, gates_l], axis=1)
    gates = gates.reshape(gates.shape[0], gates.shape[1], 4, C_HEADS)
    beta = jax.nn.sigmoid(gates[:, :, :2])
    g = -jnp.exp(a_log) * jax.nn.softplus(gates[:, :, 2:] + dt_bias)
    beta = jnp.moveaxis(beta, 2, 0)
    g = jnp.moveaxis(g, 2, 0)
    u, w, qd, kd, intra, gl = gdn_chunk_prep(qkv, beta, g)
    o_f, o_b = gdn_scan(u, w, qd, kd, intra, gl, lc // GDN_CHUNK)
    out = gdn_gate(o_f, o_b, lc, p, gnorm_w)
    out_c = gdn_gate(o_f, o_b, 0, pc, gnorm_w) if with_ctx else None
    return out, out_c


FFT_R = 128
FFT_N = FFT_R * FFT_R
SUB = 8
K1_PER_STEP = 2


def stage_a_table():
    idx = np.arange(FFT_R)
    ang = 2.0 * np.pi * np.outer(idx, idx) / FFT_R
    return jnp.asarray(np.stack([np.cos(ang), -np.sin(ang)], axis=1).reshape(2 * FFT_R, FFT_R), jnp.float32)


def _fft_stage_a_kernel(x_ref, l_ref, y_ref):
    l = l_ref[...]
    c = x_ref.shape[-1]
    xs = jnp.swapaxes(x_ref[0], 0, 1)
    ys = jnp.stack([_dot3(l, xs[j]) for j in range(SUB)], axis=0)
    y_ref[0] = jnp.swapaxes(ys, 0, 1).reshape(FFT_R, 2, SUB, c)


def fft_stage_a(x, col_blk, width, stage_a):
    b, l, wtot = x.shape
    n1cnt = l // FFT_R
    x4 = x.reshape(b, n1cnt, FFT_R, wtot)
    return pl.pallas_call(
        _fft_stage_a_kernel,
        grid=(b, FFT_R // SUB),
        in_specs=[pl.BlockSpec((1, n1cnt, SUB, width), lambda i, j: (i, 0, j, col_blk)),
                  pl.BlockSpec((2 * FFT_R, n1cnt), lambda i, j: (0, 0))],
        out_specs=pl.BlockSpec((1, FFT_R, 2, SUB, width), lambda i, j: (i, 0, 0, j, 0)),
        out_shape=jax.ShapeDtypeStruct((b, FFT_R, 2, FFT_R, width), jnp.float32),
        compiler_params=pltpu.CompilerParams(
            dimension_semantics=("arbitrary",) * 2, vmem_limit_bytes=VMEM_LIMIT_BYTES),
        name="fft_stage_a",
    )(x4, stage_a[:, :n1cnt])


def _dot3_presplit(ah, al, b):
    bh = b.astype(jnp.bfloat16)
    bl = (b - bh.astype(jnp.float32)).astype(jnp.bfloat16)
    d = functools.partial(jnp.dot, preferred_element_type=jnp.float32)
    return d(ah, bh) + (d(ah, bl) + d(al, bh))


def stage_b_tables():
    r = FFT_R
    k1 = jnp.arange(r, dtype=jnp.int32)[:, None, None]
    k2 = jnp.arange(r, dtype=jnp.int32)[None, :, None]
    n2 = jnp.arange(r, dtype=jnp.int32)[None, None, :]
    th = ((n2 * (r * k2 + k1)) % FFT_N).astype(jnp.float32) * (2.0 * math.pi / FFT_N)
    c, s = jnp.cos(th), jnp.sin(th)
    t = jnp.concatenate([jnp.concatenate([c, s], 2), jnp.concatenate([-s, c], 2)], 1)

    def split(m):
        hi = m.astype(jnp.bfloat16)
        return hi, (m - hi.astype(jnp.float32)).astype(jnp.bfloat16)

    return split(t) + split(jnp.swapaxes(t, 1, 2))


def _fft_mid_kernel(y_ref, h_ref, th_ref, tl_ref, ih_ref, il_ref, o_ref, *, conv):
    nb, nk, r, c = y_ref.shape[0], y_ref.shape[1], FFT_R, y_ref.shape[-1]
    probs = [(i, k) for k in range(nk) for i in range(nb)]
    xs = [_dot3_presplit(th_ref[k], tl_ref[k], y_ref[i, k].reshape(2 * r, c)) for i, k in probs]
    if not conv:
        for (i, k), x in zip(probs, xs):
            o_ref[i, k] = (x * ((1.0 / FFT_N) / h_ref[i])).reshape(2, r, c)
        return
    ps = [jnp.concatenate([x[:r] * h_ref[0, k, 0] - x[r:] * h_ref[0, k, 1],
                           x[:r] * h_ref[0, k, 1] + x[r:] * h_ref[0, k, 0]], axis=0) for (i, k), x in zip(probs, xs)]
    for (i, k), p in zip(probs, ps):
        o_ref[i, k] = _dot3_presplit(ih_ref[k], il_ref[k], p).reshape(2, r, c)


def fft_mid(y, h, tables, conv, order=0):
    b, r, _, _, c = y.shape
    hh = h if conv else h.reshape(b, 1, c)
    hspec = (pl.BlockSpec((1, K1_PER_STEP, 2, r, c), lambda k: (order, k, 0, 0, 0)) if conv
             else pl.BlockSpec((b, 1, c), lambda k: (0, 0, 0)))
    blk = pl.BlockSpec((b, K1_PER_STEP, 2, r, c), lambda k: (0, k, 0, 0, 0))
    tspec = pl.BlockSpec((K1_PER_STEP, 2 * r, 2 * r), lambda k: (k, 0, 0))
    return pl.pallas_call(
        functools.partial(_fft_mid_kernel, conv=conv),
        grid=(r // K1_PER_STEP,),
        in_specs=[blk, hspec, tspec, tspec, tspec, tspec],
        out_specs=blk,
        out_shape=jax.ShapeDtypeStruct(y.shape, jnp.float32),
        compiler_params=pltpu.CompilerParams(
            dimension_semantics=("arbitrary",), vmem_limit_bytes=VMEM_LIMIT_BYTES),
        name="fft_mid",
    )(y, hh, *tables)


def _fft_out_kernel(b_ref, l_ref, xg_ref, xin_ref, bias_ref, o_ref):
    l = l_ref[...]
    c = o_ref.shape[-1]
    bs = jnp.swapaxes(b_ref[0].reshape(2 * FFT_R, SUB, c), 0, 1)
    ys = jnp.stack([_dot3(l, bs[j]) for j in range(SUB)], axis=0)
    y = jnp.swapaxes(ys, 0, 1)
    o_ref[0] = xg_ref[0] * (y + bias_ref[...] * xin_ref[0])


def fft_out_gate(bm, stage_a, xg, xg_blk, xin, xin_blk, bias):
    b, r, _, _, c = bm.shape
    l = xg.shape[1]
    n1cnt = l // r
    view = lambda t: t.reshape(b, n1cnt, r, t.shape[-1])
    lhs = stage_a.T[:n1cnt]
    return pl.pallas_call(
        _fft_out_kernel,
        grid=(b, r // SUB),
        in_specs=[pl.BlockSpec((1, r, 2, SUB, c), lambda i, j: (i, 0, 0, j, 0)),
                  pl.BlockSpec((n1cnt, 2 * r), lambda i, j: (0, 0)),
                  pl.BlockSpec((1, n1cnt, SUB, c), lambda i, j: (i, 0, j, xg_blk)),
                  pl.BlockSpec((1, n1cnt, SUB, c), lambda i, j: (i, 0, j, xin_blk)),
                  pl.BlockSpec((1, c), lambda i, j: (0, 0))],
        out_specs=pl.BlockSpec((1, n1cnt, SUB, c), lambda i, j: (i, 0, j, 0)),
        out_shape=jax.ShapeDtypeStruct((b, n1cnt, r, c), jnp.float32),
        compiler_params=pltpu.CompilerParams(
            dimension_semantics=("arbitrary",) * 2, vmem_limit_bytes=VMEM_LIMIT_BYTES),
        name="fft_out_gate",
    )(bm, lhs, view(xg), view(xin), bias.reshape(1, c)).reshape(b, l, c)


def _direct_conv_kernel(xin_ref, xg_ref, kern_ref, d1_ref, d2_ref, norm_ref, bias_ref, o_ref):
    n = xin_ref.shape[1]
    d1 = d1_ref[...]
    x = xin_ref[0]
    xs = _dot3(d1[:, :n], x)
    hs = _dot3(d1, kern_ref[...]) / norm_ref[...]
    xr, xi, hr, hi = xs[:2 * n], xs[2 * n:], hs[:2 * n], hs[2 * n:]
    p = jnp.concatenate([xr * hr - xi * hi, xr * hi + xi * hr], axis=0)
    y = _dot3(d2_ref[...], p)
    o_ref[0] = xg_ref[0] * (y + bias_ref[...] * x)


def direct_long_conv(xin, xin_blk, xg, xg_blk, kern, norm, bias):
    b, n, _ = xin.shape
    c = kern.shape[1]
    idx = np.arange(2 * n)
    ang = 2.0 * np.pi * np.outer(idx, idx) / (2 * n)
    d1 = jnp.asarray(np.concatenate([np.cos(ang), -np.sin(ang)], axis=0), jnp.float32)
    d2 = jnp.asarray(np.concatenate([np.cos(ang[:n]), -np.sin(ang[:n])], axis=1) / (2 * n), jnp.float32)
    return pl.pallas_call(
        _direct_conv_kernel,
        grid=(b,),
        in_specs=[pl.BlockSpec((1, n, c), lambda i: (i, 0, xin_blk)),
                  pl.BlockSpec((1, n, c), lambda i: (i, 0, xg_blk)),
                  pl.BlockSpec((2 * n, c), lambda i: (0, 0)),
                  pl.BlockSpec((4 * n, 2 * n), lambda i: (0, 0)),
                  pl.BlockSpec((n, 4 * n), lambda i: (0, 0)),
                  pl.BlockSpec((1, c), lambda i: (0, 0)),
                  pl.BlockSpec((1, c), lambda i: (0, 0))],
        out_specs=pl.BlockSpec((1, n, c), lambda i: (i, 0, 0)),
        out_shape=jax.ShapeDtypeStruct((b, n, c), jnp.float32),
        compiler_params=pltpu.CompilerParams(
            dimension_semantics=("arbitrary",), vmem_limit_bytes=VMEM_LIMIT_BYTES),
        name="direct_long_conv",
    )(xin, xg, kern, d1, d2, norm.reshape(1, c), bias.reshape(1, c))


def _hy_filter_kernel(w1_ref, b1_ref, fr_ref, w2_ref, b2_ref, w3_ref, dl_ref, k_ref, s_ref, *, n):
    i = pl.program_id(0)
    tp, c = k_ref.shape[1], k_ref.shape[2]

    @pl.when(i == 0)
    def _():
        s_ref[...] = jnp.zeros_like(s_ref)

    def pos(shape):
        idx = i * tp + lax.broadcasted_iota(jnp.int32, shape, 0)
        t = jnp.where(idx < n, idx, jnp.where(idx == n, 0, 2 * n - idx))
        return idx, t.astype(jnp.float32)

    _, t = pos((tp, LANES))
    lane = lax.broadcasted_iota(jnp.int32, (tp, LANES), 1)
    band = jnp.where(lane <= HY_BANDS, lane, lane - HY_BANDS).astype(jnp.float32)
    ang = 2.0 * math.pi * t * band / n
    feat = jnp.where(lane == 0, t / n,
                     jnp.where(lane <= HY_BANDS, jnp.sin(ang), jnp.where(lane < HY_EMB, jnp.cos(ang), 0.0)))
    hid = jnp.sin(fr_ref[...] * (_dot3(feat, w1_ref[...]) + b1_ref[...]))
    hid = jnp.sin(fr_ref[...] * (_dot3(hid, w2_ref[...]) + b2_ref[...]))
    f = _dot3(hid, w3_ref[...])
    idx, t = pos((tp, c))
    decay = jnp.exp(-(t / n) * dl_ref[...])
    for o in range(HY_ORDER):
        fwd = f[:, (2 * o) * c:(2 * o + 1) * c]
        bwd = f[:, (2 * o + 1) * c:(2 * o + 2) * c]
        val = jnp.where(idx < n, fwd, bwd) * decay
        s_ref[o:o + 1, :] += jnp.sum(jnp.abs(val), axis=0, keepdims=True)
        k_ref[o] = jnp.where(idx == n, 0.0, val)


def hyena_kernels(n, w1, b1, freq, w2, b2, w3):
    c = HY_CH
    tp = min(512, n)
    max_decay = math.log(HY_TARGET) / HY_FAST_DECAY
    min_decay = math.log(HY_TARGET) / HY_SLOW_DECAY
    deltas = jnp.abs(jnp.linspace(min_decay, max_decay, c, dtype=jnp.float32)).reshape(1, c)
    w1p = jnp.zeros((LANES, w1.shape[1]), jnp.float32).at[:w1.shape[0]].set(w1)
    hd = w1.shape[1]
    full = lambda shape: pl.BlockSpec(shape, lambda i: (0,) * len(shape))
    return pl.pallas_call(
        functools.partial(_hy_filter_kernel, n=n),
        grid=(2 * n // tp,),
        in_specs=[full((LANES, hd)), full((1, hd)), full((1, hd)), full((hd, hd)), full((1, hd)),
                  full((hd, HY_ORDER * 2 * c)), full((1, c))],
        out_specs=[pl.BlockSpec((HY_ORDER, tp, c), lambda i: (0, i, 0)), full((HY_ORDER, c))],
        out_shape=[jax.ShapeDtypeStruct((HY_ORDER, 2 * n, c), jnp.float32),
                   jax.ShapeDtypeStruct((HY_ORDER, c), jnp.float32)],
        compiler_params=pltpu.CompilerParams(
            dimension_semantics=("arbitrary",), vmem_limit_bytes=VMEM_LIMIT_BYTES),
        name="hyena_kernels",
    )(w1p, b1.reshape(1, hd), freq.reshape(1, hd), w2, b2.reshape(1, hd), w3, deltas)


def hyena_mixer(p, col0, conv_w, conv_b, filt_args, hy_bias):
    n = p.shape[1]
    uc = short_conv(p, col0, 3 * HY_CH, conv_w, conv_b)
    kerns, norm = hyena_kernels(n, *filt_args)
    if 2 * n != FFT_N:
        v = direct_long_conv(uc, 2, uc, 0, kerns[0], norm[0], hy_bias[0])
        return direct_long_conv(v, 0, uc, 1, kerns[1], norm[1], hy_bias[1])
    stage_a, stage_b = stage_a_table(), stage_b_tables()
    spec = fft_mid(fft_stage_a(kerns, 0, HY_CH, stage_a), norm, stage_b, conv=False)
    v = fft_out_gate(fft_mid(fft_stage_a(uc, 2, HY_CH, stage_a), spec, stage_b, conv=True, order=0),
                     stage_a, uc, 0, uc, 2, hy_bias[0])
    return fft_out_gate(fft_mid(fft_stage_a(v, 0, HY_CH, stage_a), spec, stage_b, conv=True, order=1),
                        stage_a, uc, 1, v, 0, hy_bias[1])


def even_mixer(p, pc, rope_tabs, sink, conv_w, conv_b, fw1, fb1, ffreq, fw2, fb2, fw3, hy_bias, with_ctx):
    q, k, v = qkv_prep(p, 0, A_HEADS, A_KV_HEADS, rope_tabs)
    qc, kc, vc = qkv_prep(pc, 0, A_HEADS, A_KV_HEADS, None)
    o_a = windowed_sink_gqa(q, k, v, kc, vc, sink)
    filt_args = (fw1, fb1, ffreq, fw2, fb2, fw3)
    o_b = hyena_mixer(p, A_Q + 2 * A_KV, conv_w, conv_b, filt_args, hy_bias)
    out_c = None
    if with_ctx:
        o_ac = flash_gqa(qc, kc, vc, sink)
        o_bc = hyena_mixer(pc, A_Q + 2 * A_KV, conv_w, conv_b, filt_args, hy_bias)
        out_c = (o_ac, o_bc)
    return (o_a, o_b), out_c


def odd_mixer(p, pc, gates_l, gates_c, rope_tabs, conv_w, a_log, dt_bias, gnorm_w, qnorm_w, knorm_w, with_ctx):
    o_l, o_c = gdn_mixer(p, pc, gates_l, gates_c, conv_w, a_log, dt_bias, gnorm_w, with_ctx)
    qd, kd, vd = qkv_prep(p, 4 * C_W, D_HEADS, D_KV_HEADS, rope_tabs, qnorm_w, knorm_w)
    qdc, kdc, vdc = qkv_prep(pc, 4 * C_W, D_HEADS, D_KV_HEADS, None, qnorm_w, knorm_w)
    o_d = flash_gqa(qd, jnp.concatenate([kd, kdc], 2), jnp.concatenate([vd, vdc], 2))
    out_c = None
    if with_ctx:
        out_c = (o_c, flash_gqa(qdc, kdc, vdc))
    return (o_l, o_d), out_c


def kernel(x, c, ctx, c_ctx, ada_w, ada_b, ln1_g, ln1_b, ln2_g, ln2_b, peer_wq, peer_k1, peer_k2, peer_u, peer_v, ev_w_in, ev_w_out, ev_sink, ev_conv_w, ev_conv_b, ev_filt_w1, ev_filt_b1, ev_filt_freq, ev_filt_w2, ev_filt_b2, ev_filt_w3, ev_hy_bias, od_w_in, od_w_out, od_conv_w, od_a_log, od_dt_bias, od_gnorm_w, od_qnorm_w, od_knorm_w):
    rope_tabs = rope_tables(x.shape[1])
    bsz = x.shape[0]
    silu_c = jax.nn.silu(c)
    silu_cc = jax.nn.silu(c_ctx)
    for i in range(DEPTH):
        with_ctx = i < DEPTH - 1
        j = i // 2
        mod = (silu_c @ ada_w[i] + ada_b[i])[:, None, :]
        modc = jnp.broadcast_to((silu_cc @ ada_w[i] + ada_b[i])[None, None, :], (bsz, 1, 6 * D_MODEL))
        sh1, sc1, g1, sh2, sc2, g2 = jnp.split(mod, 6, axis=-1)
        sh1c, sc1c, g1c, sh2c, sc2c, g2c = jnp.split(modc, 6, axis=-1)
        if i % 2 == 0:
            p = mod_matmul(x, sh1, sc1, ev_w_in[j])
            pc = mod_matmul(ctx, sh1c, sc1c, ev_w_in[j])
            out, out_c = even_mixer(p, pc, rope_tabs, ev_sink[j], ev_conv_w[j], ev_conv_b[j],
                                    ev_filt_w1[j], ev_filt_b1[j], ev_filt_freq[j], ev_filt_w2[j], ev_filt_b2[j],
                                    ev_filt_w3[j], ev_hy_bias[j], with_ctx)
            w_out = ev_w_out[j]
        else:
            w_in = od_w_in[j]
            w_gate = jnp.pad(w_in[:, 4 * C_W:4 * C_W + C_GATES], ((0, 0), (0, LANES - C_GATES)))
            w_in = jnp.concatenate([w_in[:, :4 * C_W], w_in[:, 4 * C_W + C_GATES:]], axis=1)
            p = mod_matmul(x, sh1, sc1, w_in)
            pc = mod_matmul(ctx, sh1c, sc1c, w_in)
            gates_l = mod_matmul(x, sh1, sc1, w_gate)[..., :C_GATES]
            gates_c = mod_matmul(ctx, sh1c, sc1c, w_gate)[..., :C_GATES]
            out, out_c = odd_mixer(p, pc, gates_l, gates_c, rope_tabs, od_conv_w[j], od_a_log[j], od_dt_bias[j],
                                   od_gnorm_w[j], od_qnorm_w[j], od_knorm_w[j], with_ctx)
            w_out = od_w_out[j]
        u_bf = layer_table_bf16(peer_u, i)
        v_bf = layer_table_bf16(peer_v, i)
        x = proj_residual_ln(out[0], out[1], w_out, x, g1, ln1_g[i], ln1_b[i])
        x = peer_block(x, sh2, sc2, g2, peer_wq[i], peer_k1[i], peer_k2[i], u_bf, v_bf, ln2_g[i], ln2_b[i])
        if with_ctx:
            ctx = proj_residual_ln(out_c[0], out_c[1], w_out, ctx, g1c, ln1_g[i], ln1_b[i])
            ctx = peer_block(ctx, sh2c, sc2c, g2c, peer_wq[i], peer_k1[i], peer_k2[i], u_bf, v_bf,
                             ln2_g[i], ln2_b[i])
    return x
```

```python
import functools
import math

import numpy as np

import jax
import jax.numpy as jnp
from jax import lax
from jax.experimental import pallas as pl
from jax.experimental.pallas import tpu as pltpu

D_MODEL = 1024
DEPTH = 2
GRID_W = 64
HEAD_DIM = 64
BLOCK = 128
ROPE_BASE = 10000.0
EPS = 1e-6

A_HEADS = 8
A_KV_HEADS = 2
WINDOW = 128

HY_CH = 512
HY_ORDER = 2
HY_EMB = 33
HY_BANDS = (HY_EMB - 1) // 2
HY_FAST_DECAY = 0.3
HY_SLOW_DECAY = 1.5
HY_TARGET = 1e-2

C_HEADS = 4
C_DK = 128
C_DV = 128
GDN_CHUNK = 64

D_HEADS = 8
D_KV_HEADS = 2

PEER_HEADS = 8
PEER_NKEYS = 128
PEER_QDIM = 256
PEER_TOPK = 16
PEER_CHUNK = 128

ALPHA = (2 * DEPTH) ** 0.25

A_Q = A_HEADS * HEAD_DIM
A_KV = A_KV_HEADS * HEAD_DIM
C_W = C_HEADS * C_DK
C_GATES = 4 * C_HEADS
D_Q = D_HEADS * HEAD_DIM
D_KV = D_KV_HEADS * HEAD_DIM

VMEM_LIMIT_BYTES = 48 * 1024 * 1024

LANES = 128
_NT = (((1,), (1,)), ((), ()))


def _modmm_kernel(x_ref, sh_ref, sc_ref, w_ref, o_ref):
    h = x_ref[0] * (1.0 + sc_ref[0]) + sh_ref[0]
    o_ref[0] = jnp.dot(h.astype(jnp.bfloat16), w_ref[...], preferred_element_type=jnp.float32)


def mod_matmul(x, shift, scale, w, tm=512, tn=None):
    b, s, k = x.shape
    n = w.shape[1]
    tm = min(tm, s)
    tn = n if tn is None else tn
    wb = w.astype(jnp.bfloat16)
    return pl.pallas_call(
        _modmm_kernel,
        grid=(b, n // tn, s // tm),
        in_specs=[
            pl.BlockSpec((1, tm, k), lambda i, j, m: (i, m, 0)),
            pl.BlockSpec((1, 1, k), lambda i, j, m: (i, 0, 0)),
            pl.BlockSpec((1, 1, k), lambda i, j, m: (i, 0, 0)),
            pl.BlockSpec((k, tn), lambda i, j, m: (0, j)),
        ],
        out_specs=pl.BlockSpec((1, tm, tn), lambda i, j, m: (i, m, j)),
        out_shape=jax.ShapeDtypeStruct((b, s, n), jnp.float32),
        compiler_params=pltpu.CompilerParams(
            dimension_semantics=("arbitrary", "arbitrary", "arbitrary"),
            vmem_limit_bytes=VMEM_LIMIT_BYTES),
        name="mod_matmul",
    )(x, shift, scale, wb)


def _qkv_prep_kernel(q_ref, k_ref, v_ref, cs_ref, sn_ref, qw_ref, kw_ref, gm_ref, qo_ref, ko_ref, vo_ref, *,
                     norm, rope, nq, nkv):
    def prep(x, w, nh):
        if norm:
            ms = jnp.dot(x * x, gm_ref[:x.shape[1], :x.shape[1]], precision=lax.Precision.HIGHEST,
                         preferred_element_type=jnp.float32)
            x = x * lax.rsqrt(ms + EPS) * w
        if rope:
            n = x.shape[1]
            reps = n // cs_ref.shape[1]
            cs = jnp.concatenate([cs_ref[...]] * reps, axis=1) if reps > 1 else cs_ref[...]
            sn = jnp.concatenate([sn_ref[...]] * reps, axis=1) if reps > 1 else sn_ref[...]
            lane = lax.broadcasted_iota(jnp.int32, x.shape, 1)
            nf = HEAD_DIM // 4
            partner = jnp.where((lane & nf) == 0, pltpu.roll(x, n - nf, 1), pltpu.roll(x, nf, 1))
            x = x * cs + partner * sn
        return x

    q = prep(q_ref[0], qw_ref[...], nq) * (HEAD_DIM ** -0.5)
    k = prep(k_ref[0], kw_ref[...], nkv)
    v = v_ref[0]
    for h in range(nq):
        qo_ref[0, h] = q[:, h * HEAD_DIM:(h + 1) * HEAD_DIM].astype(jnp.bfloat16)
    for h in range(nkv):
        ko_ref[0, h] = k[:, h * HEAD_DIM:(h + 1) * HEAD_DIM].astype(jnp.bfloat16)
        vo_ref[0, h] = v[:, h * HEAD_DIM:(h + 1) * HEAD_DIM].astype(jnp.bfloat16)


def qkv_prep(p, col0, nq, nkv, rope_tabs, qw=None, kw=None, tq=512):
    b, s, _ = p.shape
    tq = min(tq, s)
    wq_, wk_ = nq * HEAD_DIM, nkv * HEAD_DIM
    norm = qw is not None
    rope = rope_tabs is not None
    if rope:
        cs, sn = rope_tabs
    else:
        cs = sn = jnp.zeros((s, 2 * HEAD_DIM), jnp.float32)
    qw_t = jnp.tile(qw, nq).reshape(1, wq_) if norm else jnp.ones((1, wq_), jnp.float32)
    kw_t = jnp.tile(kw, nkv).reshape(1, wk_) if norm else jnp.ones((1, wk_), jnp.float32)
    grp = jnp.arange(wq_) // HEAD_DIM
    gm = (grp[:, None] == grp[None, :]).astype(jnp.float32) / HEAD_DIM
    kern = functools.partial(_qkv_prep_kernel, norm=norm, rope=rope, nq=nq, nkv=nkv)
    return pl.pallas_call(
        kern,
        grid=(b, s // tq),
        in_specs=[
            pl.BlockSpec((1, tq, wq_), lambda i, m: (i, m, col0 // wq_)),
            pl.BlockSpec((1, tq, wk_), lambda i, m: (i, m, (col0 + wq_) // wk_)),
            pl.BlockSpec((1, tq, wk_), lambda i, m: (i, m, (col0 + wq_) // wk_ + 1)),
            pl.BlockSpec((tq, 2 * HEAD_DIM), lambda i, m: (m, 0)),
            pl.BlockSpec((tq, 2 * HEAD_DIM), lambda i, m: (m, 0)),
            pl.BlockSpec((1, wq_), lambda i, m: (0, 0)),
            pl.BlockSpec((1, wk_), lambda i, m: (0, 0)),
            pl.BlockSpec((wq_, wq_), lambda i, m: (0, 0)),
        ],
        out_specs=[
            pl.BlockSpec((1, nq, tq, HEAD_DIM), lambda i, m: (i, 0, m, 0)),
            pl.BlockSpec((1, nkv, tq, HEAD_DIM), lambda i, m: (i, 0, m, 0)),
            pl.BlockSpec((1, nkv, tq, HEAD_DIM), lambda i, m: (i, 0, m, 0)),
        ],
        out_shape=[
            jax.ShapeDtypeStruct((b, nq, s, HEAD_DIM), jnp.bfloat16),
            jax.ShapeDtypeStruct((b, nkv, s, HEAD_DIM), jnp.bfloat16),
            jax.ShapeDtypeStruct((b, nkv, s, HEAD_DIM), jnp.bfloat16),
        ],
        compiler_params=pltpu.CompilerParams(
            dimension_semantics=("arbitrary", "arbitrary"), vmem_limit_bytes=VMEM_LIMIT_BYTES),
        name="qkv_prep",
    )(p, p, p, cs, sn, qw_t, kw_t, gm)


def rope_tables(n_tok):
    rows = n_tok // GRID_W
    row = jnp.repeat(jnp.arange(rows, dtype=jnp.float32), GRID_W)
    col = jnp.tile(jnp.arange(GRID_W, dtype=jnp.float32), rows)
    nf = HEAD_DIM // 4
    inv = ROPE_BASE ** (-jnp.arange(nf, dtype=jnp.float32) / nf)
    ar, ac = row[:, None] * inv, col[:, None] * inv
    cs = jnp.concatenate([jnp.cos(ar), jnp.cos(ar), jnp.cos(ac), jnp.cos(ac)], -1)
    sn = jnp.concatenate([-jnp.sin(ar), jnp.sin(ar), -jnp.sin(ac), jnp.sin(ac)], -1)
    return jnp.tile(cs, (1, 2)), jnp.tile(sn, (1, 2))


def _flash_kernel(sink_ref, q_ref, k_ref, v_ref, o_ref, m_s, l_s, acc_s, *, use_sink, grp):
    j = pl.program_id(3)
    tq = q_ref.shape[2]

    @pl.when(j == 0)
    def _():
        m_s[...] = jnp.full_like(m_s, -jnp.inf)
        l_s[...] = jnp.zeros_like(l_s)
        acc_s[...] = jnp.zeros_like(acc_s)

    tk = k_ref.shape[2]
    nt = tk // LANES
    kt = k_ref[0, 0]
    vt = v_ref[0, 0]
    scores = [lax.dot_general(q_ref[0, g], kt, _NT, preferred_element_type=jnp.float32) for g in range(grp)]
    for g, s in enumerate(scores):
        rows = slice(g * tq, (g + 1) * tq)
        tiles = [s[:, c * LANES:(c + 1) * LANES] for c in range(nt)]
        m_tile = functools.reduce(jnp.maximum, tiles)
        m_old = m_s[rows]
        m_new = jnp.maximum(m_old, jnp.broadcast_to(jnp.max(m_tile, axis=1, keepdims=True), m_old.shape))
        alpha = jnp.exp(m_old - m_new)
        p_tiles = [jnp.exp(t - m_new) for t in tiles]
        l_s[rows] = alpha * l_s[rows] + functools.reduce(jnp.add, p_tiles)
        p = jnp.concatenate([t.astype(jnp.bfloat16) for t in p_tiles], axis=1)
        acc_s[rows] = alpha[:, :HEAD_DIM] * acc_s[rows] + jnp.dot(p, vt, preferred_element_type=jnp.float32)
        m_s[rows] = m_new

    @pl.when(j == pl.num_programs(3) - 1)
    def _():
        kvh = pl.program_id(1)
        outs = []
        for g in range(grp):
            rows = slice(g * tq, (g + 1) * tq)
            m = m_s[rows][:, :1]
            l = jnp.sum(l_s[rows], axis=1, keepdims=True)
            acc = acc_s[rows]
            if use_sink:
                sk = sink_ref[kvh * grp + g]
                m2 = jnp.maximum(m, sk)
                a = jnp.exp(m - m2)
                l = a * l + jnp.exp(sk - m2)
                acc = a * acc
            outs.append(acc / l)
        o_ref[0] = jnp.concatenate(outs, axis=1).astype(o_ref.dtype)


def flash_gqa(q, k, v, sink=None, tq=256, tk=2816):
    b, h, s, hd = q.shape
    kvh, lk = k.shape[1], k.shape[2]
    grp = h // kvh
    tq = min(tq, s)
    tk = max(t for t in range(LANES, min(tk, lk) + 1, LANES) if lk % t == 0)
    use_sink = sink is not None
    sink_arr = sink.astype(jnp.float32) if use_sink else jnp.zeros((h,), jnp.float32)
    kern = functools.partial(_flash_kernel, use_sink=use_sink, grp=grp)
    return pl.pallas_call(
        kern,
        grid=(b, kvh, s // tq, lk // tk),
        in_specs=[
            pl.BlockSpec(memory_space=pltpu.SMEM),
            pl.BlockSpec((1, grp, tq, hd), lambda i, c, m, j: (i, c, m, 0)),
            pl.BlockSpec((1, 1, tk, hd), lambda i, c, m, j: (i, c, j, 0)),
            pl.BlockSpec((1, 1, tk, hd), lambda i, c, m, j: (i, c, j, 0)),
        ],
        out_specs=pl.BlockSpec((1, tq, grp * hd), lambda i, c, m, j: (i, m, c)),
        out_shape=jax.ShapeDtypeStruct((b, s, h * hd), jnp.bfloat16),
        scratch_shapes=[pltpu.VMEM((grp * tq, LANES), jnp.float32), pltpu.VMEM((grp * tq, LANES), jnp.float32),
                        pltpu.VMEM((grp * tq, hd), jnp.float32)],
        compiler_params=pltpu.CompilerParams(
            dimension_semantics=("arbitrary",) * 4, vmem_limit_bytes=VMEM_LIMIT_BYTES),
        name="flash_gqa",
    )(sink_arr, q, k, v)


def _window_kernel(sink_ref, q_ref, kp_ref, kc_ref, kn_ref, vp_ref, vc_ref, vn_ref, kx_ref, vx_ref, o_ref, *, grp):
    kvh = pl.program_id(1)
    i = pl.program_id(2)
    nb = pl.num_programs(2)
    kcat = jnp.concatenate([kp_ref[0, 0], kc_ref[0, 0], kn_ref[0, 0], kx_ref[0, 0]], axis=0)
    vcat = jnp.concatenate([vp_ref[0, 0], vc_ref[0, 0], vn_ref[0, 0], vx_ref[0, 0]], axis=0)
    nk = kcat.shape[0]
    r = lax.broadcasted_iota(jnp.int32, (BLOCK, nk), 0)
    c = lax.broadcasted_iota(jnp.int32, (BLOCK, nk), 1)
    off_prev = jnp.where(i > 0, 0, 2 * nk)
    off_next = jnp.where(i < nb - 1, 0, 2 * nk)
    ok_prev = (c >= BLOCK) | (c >= r + off_prev)
    ok_next = (c < 2 * BLOCK) | (c >= 3 * BLOCK) | (c - 2 * BLOCK <= r - off_next)
    valid = ok_prev & ok_next
    scores = [lax.dot_general(q_ref[0, g], kcat, _NT, preferred_element_type=jnp.float32) for g in range(grp)]
    outs = []
    for g, s in enumerate(scores):
        s = jnp.where(valid, s, -jnp.inf)
        sk = sink_ref[kvh * grp + g]
        m = jnp.maximum(jnp.max(s, axis=1, keepdims=True), sk)
        p = jnp.exp(s - m)
        l = jnp.sum(p, axis=1, keepdims=True) + jnp.exp(sk - m)
        o = jnp.dot(p.astype(jnp.bfloat16), vcat, preferred_element_type=jnp.float32)
        outs.append(o / l)
    o_ref[0] = jnp.concatenate(outs, axis=1).astype(o_ref.dtype)


def windowed_sink_gqa(q, k, v, kx, vx, sink):
    b, h, s, hd = q.shape
    kvh = k.shape[1]
    lc = kx.shape[2]
    grp = h // kvh
    nb = s // BLOCK
    kern = functools.partial(_window_kernel, grp=grp)
    blk = lambda f: pl.BlockSpec((1, 1, BLOCK, hd), f)
    prev = lambda i, c, m: (i, c, jnp.maximum(m - 1, 0), 0)
    cur = lambda i, c, m: (i, c, m, 0)
    nxt = lambda i, c, m: (i, c, jnp.minimum(m + 1, nb - 1), 0)
    ctxm = lambda i, c, m: (i, c, 0, 0)
    return pl.pallas_call(
        kern,
        grid=(b, kvh, nb),
        in_specs=[
            pl.BlockSpec(memory_space=pltpu.SMEM),
            pl.BlockSpec((1, grp, BLOCK, hd), cur),
            blk(prev), blk(cur), blk(nxt), blk(prev), blk(cur), blk(nxt),
            pl.BlockSpec((1, 1, lc, hd), ctxm), pl.BlockSpec((1, 1, lc, hd), ctxm),
        ],
        out_specs=pl.BlockSpec((1, BLOCK, grp * hd), lambda i, c, m: (i, m, c)),
        out_shape=jax.ShapeDtypeStruct((b, s, h * hd), jnp.bfloat16),
        compiler_params=pltpu.CompilerParams(
            dimension_semantics=("arbitrary",) * 3, vmem_limit_bytes=VMEM_LIMIT_BYTES),
        name="windowed_sink_gqa",
    )(sink.astype(jnp.float32), q, k, k, k, v, v, v, kx, vx)


def _post_kernel(oa_ref, ob_ref, w_ref, x_ref, g_ref, lg_ref, lb_ref, y_ref):
    ka = oa_ref.shape[2]
    out = jnp.dot(oa_ref[0].astype(jnp.bfloat16), w_ref[:ka], preferred_element_type=jnp.float32)
    out += jnp.dot(ob_ref[0].astype(jnp.bfloat16), w_ref[ka:], preferred_element_type=jnp.float32)
    r = ALPHA * x_ref[0] + g_ref[0] * out
    mu = jnp.mean(r, -1, keepdims=True)
    d = r - mu
    var = jnp.mean(d * d, -1, keepdims=True)
    y_ref[0] = d * lax.rsqrt(var + EPS) * lg_ref[...] + lb_ref[...]


def proj_residual_ln(oa, ob, w, x, gate, ln_g, ln_b, tm=512):
    b, s, ka = oa.shape
    kb = ob.shape[2]
    k = ka + kb
    d = w.shape[1]
    tm = min(tm, s)
    wb = w.astype(jnp.bfloat16)
    return pl.pallas_call(
        _post_kernel,
        grid=(b, s // tm),
        in_specs=[
            pl.BlockSpec((1, tm, ka), lambda i, m: (i, m, 0)),
            pl.BlockSpec((1, tm, kb), lambda i, m: (i, m, 0)),
            pl.BlockSpec((k, d), lambda i, m: (0, 0)),
            pl.BlockSpec((1, tm, d), lambda i, m: (i, m, 0)),
            pl.BlockSpec((1, 1, d), lambda i, m: (i, 0, 0)),
            pl.BlockSpec((1, d), lambda i, m: (0, 0)),
            pl.BlockSpec((1, d), lambda i, m: (0, 0)),
        ],
        out_specs=pl.BlockSpec((1, tm, d), lambda i, m: (i, m, 0)),
        out_shape=jax.ShapeDtypeStruct((b, s, d), jnp.float32),
        compiler_params=pltpu.CompilerParams(
            dimension_semantics=("arbitrary", "arbitrary"),
            vmem_limit_bytes=VMEM_LIMIT_BYTES),
        name="proj_residual_ln",
    )(oa, ob, wb, x, gate, ln_g.reshape(1, d), ln_b.reshape(1, d))


def _top16(s, payload=None):
    n = s.shape[0]
    iota = lax.broadcasted_iota(jnp.int32, s.shape, 0).astype(jnp.float32)
    vals, ids = [], []
    for _ in range(PEER_TOPK):
        m = jnp.max(s, axis=0, keepdims=True)
        pos = jnp.min(jnp.where(s == m, iota, float(n)), axis=0, keepdims=True)
        hit = iota == pos
        vals.append(m)
        ids.append(pos if payload is None else jnp.max(jnp.where(hit, payload, -1.0), axis=0, keepdims=True))
        s = jnp.where(hit, -jnp.inf, s)
    return jnp.concatenate(vals, 0), jnp.concatenate(ids, 0)


def _peer_topk_kernel(q_ref, k1_ref, k2_ref, eid_ref, gate_ref, eid_s, gate_s):
    half = PEER_QDIM // 2

    def head(h, carry):
        off = pl.multiple_of(h * PEER_QDIM, PEER_QDIM)
        q1 = q_ref[:, pl.ds(off, half)]
        q2 = q_ref[:, pl.ds(off + half, half)]
        s1 = lax.dot_general(k1_ref[h], q1, _NT, precision=lax.Precision.HIGHEST,
                             preferred_element_type=jnp.float32)
        s2 = lax.dot_general(k2_ref[h], q2, _NT, precision=lax.Precision.HIGHEST,
                             preferred_element_type=jnp.float32)
        v1, i1 = _top16(s1)
        v2, i2 = _top16(s2)
        k8 = PEER_TOPK // 2
        cand = jnp.concatenate([v1[0:1] + v2] + [v1[i:i + 1] + v2[:k8] for i in range(1, k8)]
                               + [v1[k8:] + v2[0:1]], 0)
        cid = jnp.concatenate([i1[0:1] * PEER_NKEYS + i2]
                              + [i1[i:i + 1] * PEER_NKEYS + i2[:k8] for i in range(1, k8)]
                              + [i1[k8:] * PEER_NKEYS + i2[0:1]], 0)
        best, eid = _top16(cand, cid)
        e = jnp.exp(best - best[0:1])
        gate = e / jnp.sum(e, axis=0, keepdims=True)
        row = pl.multiple_of(h * PEER_TOPK, PEER_TOPK)
        eid_s[pl.ds(row, PEER_TOPK), :] = eid.astype(jnp.int32)
        gate_s[pl.ds(row, PEER_TOPK), :] = gate
        return carry

    lax.fori_loop(0, PEER_HEADS, head, 0)
    eid_ref[...] = eid_s[...].T
    gate_ref[...] = gate_s[...].T


def peer_topk(q, k1, k2, tt=1024):
    t = q.shape[0]
    tt = min(tt, t)
    nsel = PEER_HEADS * PEER_TOPK
    return pl.pallas_call(
        _peer_topk_kernel,
        grid=(t // tt,),
        in_specs=[
            pl.BlockSpec((tt, q.shape[1]), lambda i: (i, 0)),
            pl.BlockSpec(k1.shape, lambda i: (0, 0, 0)),
            pl.BlockSpec(k2.shape, lambda i: (0, 0, 0)),
        ],
        out_specs=[pl.BlockSpec((tt, nsel), lambda i: (i, 0)),
                   pl.BlockSpec((tt, nsel), lambda i: (i, 0))],
        out_shape=[jax.ShapeDtypeStruct((t, nsel), jnp.int32),
                   jax.ShapeDtypeStruct((t, nsel), jnp.float32)],
        scratch_shapes=[pltpu.VMEM((nsel, tt), jnp.int32), pltpu.VMEM((nsel, tt), jnp.float32)],
        compiler_params=pltpu.CompilerParams(
            dimension_semantics=("arbitrary",), vmem_limit_bytes=VMEM_LIMIT_BYTES),
        name="peer_topk",
    )(q, k1, k2)


def _peer_w_kernel(e_ref, g_ref, w_ref):
    nk = PEER_NKEYS
    iota = lax.broadcasted_iota(jnp.int32, (nk, e_ref.shape[1]), 0)

    def tok(t, carry):
        e = e_ref[pl.ds(t, 1), :]
        g = g_ref[pl.ds(t, 1), :]
        a_t = jnp.where(iota == (e >> 7), g, 0.0).astype(jnp.bfloat16)
        b_t = jnp.where(iota == (e & (nk - 1)), 1.0, 0.0).astype(jnp.bfloat16)
        w = lax.dot_general(a_t, b_t, _NT, preferred_element_type=jnp.float32)
        w_ref[t] = w.astype(jnp.bfloat16)
        return carry

    lax.fori_loop(0, e_ref.shape[0], tok, 0, unroll=64)


def peer_dense_gates(eid, gate, tt=128):
    t, nsel = eid.shape
    tt = min(tt, t)
    nk = PEER_NKEYS
    w = pl.pallas_call(
        _peer_w_kernel,
        grid=(t // tt,),
        in_specs=[pl.BlockSpec((tt, nsel), lambda i: (i, 0)),
                  pl.BlockSpec((tt, nsel), lambda i: (i, 0))],
        out_specs=pl.BlockSpec((tt, nk, nk), lambda i: (i, 0, 0)),
        out_shape=jax.ShapeDtypeStruct((t, nk, nk), jnp.bfloat16),
        compiler_params=pltpu.CompilerParams(
            dimension_semantics=("arbitrary",), vmem_limit_bytes=VMEM_LIMIT_BYTES),
        name="peer_dense_gates",
    )(eid, gate)
    return w


def _peer_expert_kernel(x_ref, sh_ref, sc_ref, w_ref, u_ref, v_ref, g_ref, lg_ref, lb_ref, y_ref, xm_s, acc_s):
    e = pl.program_id(2)

    @pl.when(e == 0)
    def _():
        xm_s[...] = (x_ref[0] * (1.0 + sc_ref[0]) + sh_ref[0]).astype(jnp.bfloat16)
        acc_s[...] = jnp.zeros_like(acc_s)

    h = lax.dot_general(xm_s[...], u_ref[...], _NT, preferred_element_type=jnp.float32)
    gelu = 0.5 * h * (1.0 + lax.erf(h * (2.0 ** -0.5)))
    w = w_ref[0].reshape(h.shape)
    a = gelu * w.astype(jnp.float32)
    acc_s[...] += jnp.dot(a.astype(jnp.bfloat16), v_ref[...], preferred_element_type=jnp.float32)

    @pl.when(e == pl.num_programs(2) - 1)
    def _():
        r = ALPHA * x_ref[0] + g_ref[0] * acc_s[...]
        mu = jnp.mean(r, -1, keepdims=True)
        d = r - mu
        var = jnp.mean(d * d, -1, keepdims=True)
        y_ref[0] = d * lax.rsqrt(var + EPS) * lg_ref[...] + lb_ref[...]


def peer_experts_ln(x, shift, scale, w, u_tab, v_tab, gate, ln_g, ln_b, tt=512, te=2048):
    b, s, d = x.shape
    tt = min(tt, s)
    ne = u_tab.shape[0]
    nk = PEER_NKEYS
    w3 = w.reshape(b, s, nk, nk)
    return pl.pallas_call(
        _peer_expert_kernel,
        grid=(b, s // tt, ne // te),
        in_specs=[
            pl.BlockSpec((1, tt, d), lambda i, m, e: (i, m, 0)),
            pl.BlockSpec((1, 1, d), lambda i, m, e: (i, 0, 0)),
            pl.BlockSpec((1, 1, d), lambda i, m, e: (i, 0, 0)),
            pl.BlockSpec((1, tt, te // nk, nk), lambda i, m, e: (i, m, e, 0)),
            pl.BlockSpec((te, d), lambda i, m, e: (e, 0)),
            pl.BlockSpec((te, d), lambda i, m, e: (e, 0)),
            pl.BlockSpec((1, 1, d), lambda i, m, e: (i, 0, 0)),
            pl.BlockSpec((1, d), lambda i, m, e: (0, 0)),
            pl.BlockSpec((1, d), lambda i, m, e: (0, 0)),
        ],
        out_specs=pl.BlockSpec((1, tt, d), lambda i, m, e: (i, m, 0)),
        out_shape=jax.ShapeDtypeStruct((b, s, d), jnp.float32),
        scratch_shapes=[pltpu.VMEM((tt, d), jnp.bfloat16), pltpu.VMEM((tt, d), jnp.float32)],
        compiler_params=pltpu.CompilerParams(
            dimension_semantics=("arbitrary", "arbitrary", "arbitrary"),
            vmem_limit_bytes=VMEM_LIMIT_BYTES),
        name="peer_experts_ln",
    )(x, shift, scale, w3, u_tab, v_tab, gate, ln_g.reshape(1, d), ln_b.reshape(1, d))


def _cast_kernel(x_ref, o_ref):
    o_ref[...] = x_ref[0].astype(o_ref.dtype)


def layer_table_bf16(tab, layer, tr=2048):
    _, ne, d = tab.shape
    return pl.pallas_call(
        _cast_kernel,
        grid=(ne // tr,),
        in_specs=[pl.BlockSpec((1, tr, d), lambda r: (layer, r, 0))],
        out_specs=pl.BlockSpec((tr, d), lambda r: (r, 0)),
        out_shape=jax.ShapeDtypeStruct((ne, d), jnp.bfloat16),
        compiler_params=pltpu.CompilerParams(
            dimension_semantics=("arbitrary",), vmem_limit_bytes=VMEM_LIMIT_BYTES),
        name="layer_table_bf16",
    )(tab)


def peer_block(x, shift, scale, gate, wq, k1, k2, u_bf, v_bf, ln_g, ln_b):
    b, s, d = x.shape
    q_all = mod_matmul(x, shift, scale, wq).reshape(b * s, -1)
    eid, gsel = peer_topk(q_all, k1, k2)
    w = peer_dense_gates(eid, gsel)
    return peer_experts_ln(x, shift, scale, w, u_bf, v_bf, gate, ln_g, ln_b)


HEAD_LANES = 128


def _short_conv_kernel(x_ref, xp_ref, xn_ref, w_ref, b_ref, o_ref, *, silu, n_l2, n_scaled):
    cb = pl.program_id(1)
    m = pl.program_id(2)
    x = x_ref[0]
    tq, wb = x.shape
    prev_row = jnp.where(m > 0, xp_ref[0][7:8], 0.0)
    next_row = jnp.where(m < pl.num_programs(2) - 1, xn_ref[0][0:1], 0.0)
    row = lax.broadcasted_iota(jnp.int32, x.shape, 0)
    x_m1 = jnp.where(row == 0, prev_row, pltpu.roll(x, 1, 0))
    x_p1 = jnp.where(row == tq - 1, next_row, pltpu.roll(x, tq - 1, 0))
    y = w_ref[0:1] * x_m1 + w_ref[1:2] * x + w_ref[2:3] * x_p1 + b_ref[...]
    if silu:
        y = y * jax.nn.sigmoid(y)
    if n_l2 == 0:
        o_ref[0] = y
        return
    hpb = wb // HEAD_LANES
    for hh in range(hpb):
        gh = cb * hpb + hh
        seg = y[:, hh * HEAD_LANES:(hh + 1) * HEAD_LANES]
        inv = lax.rsqrt(jnp.sum(seg * seg, axis=-1, keepdims=True) + EPS)
        f = jnp.where(gh < n_l2, inv, 1.0) * jnp.where(gh < n_scaled, C_DK ** -0.5, 1.0)
        o_ref[0, :, hh * HEAD_LANES:(hh + 1) * HEAD_LANES] = seg * f


def short_conv(p, col0, width, w, bias=None, silu=False, n_l2=0, n_scaled=0, wb=768, tq=512):
    b, l, _ = p.shape
    tq = min(tq, l)
    bias2 = (jnp.zeros((width,), jnp.float32) if bias is None else bias).reshape(1, width)
    c0 = col0 // wb
    kern = functools.partial(_short_conv_kernel, silu=silu, n_l2=n_l2, n_scaled=n_scaled)
    r8 = tq // 8
    return pl.pallas_call(
        kern,
        grid=(b, width // wb, l // tq),
        in_specs=[
            pl.BlockSpec((1, tq, wb), lambda i, c, m: (i, m, c0 + c)),
            pl.BlockSpec((1, 8, wb), lambda i, c, m: (i, jnp.maximum(m * r8 - 1, 0), c0 + c)),
            pl.BlockSpec((1, 8, wb), lambda i, c, m: (i, jnp.minimum((m + 1) * r8, l // 8 - 1), c0 + c)),
            pl.BlockSpec((3, wb), lambda i, c, m: (0, c)),
            pl.BlockSpec((1, wb), lambda i, c, m: (0, c)),
        ],
        out_specs=pl.BlockSpec((1, tq, wb), lambda i, c, m: (i, m, c)),
        out_shape=jax.ShapeDtypeStruct((b, l, width), jnp.float32),
        compiler_params=pltpu.CompilerParams(
            dimension_semantics=("arbitrary",) * 3, vmem_limit_bytes=VMEM_LIMIT_BYTES),
        name="short_conv",
    )(p, p, p, w, bias2)


def _dot3(a, b):
    ah = a.astype(jnp.bfloat16)
    bh = b.astype(jnp.bfloat16)
    al = (a - ah.astype(jnp.float32)).astype(jnp.bfloat16)
    bl = (b - bh.astype(jnp.float32)).astype(jnp.bfloat16)
    d = functools.partial(jnp.dot, preferred_element_type=jnp.float32)
    return d(ah, bh) + (d(ah, bl) + d(al, bh))


GDN_PAR = 4


def _gdn_chunk_kernel(qkvc_ref, qkvl_ref, beta_ref, g_ref, u_ref, w_ref, qd_ref, kd_ref, in_ref, gl_ref, qkv_s, *, nc):
    d = pl.program_id(0)
    m = pl.program_id(2)

    @pl.when(m == 0)
    def _():
        qkv_s[...] = qkvc_ref[0]

    @pl.when(m > 0)
    def _():
        qkv_s[...] = qkvl_ref[0]

    cs = GDN_CHUNK
    ii = lax.broadcasted_iota(jnp.int32, (cs, cs), 0)
    jj = lax.broadcasted_iota(jnp.int32, (cs, cs), 1)
    lo = (ii - jj) * (1 - 2 * d)
    incl = lo >= 0
    strict = lo > 0
    tri = jnp.where(incl, 1.0, 0.0).astype(jnp.bfloat16)
    tri3 = jnp.concatenate([tri, tri, tri], axis=1)
    eye = jnp.where(ii == jj, 1.0, 0.0)

    def chunk_pair(cp, carry):
        probs = []
        for c in [GDN_PAR * cp + i for i in range(GDN_PAR)]:
            rows = pl.ds(pl.multiple_of(c * cs, cs), cs)
            g_c = g_ref[0, 0, rows, :]
            b_c = beta_ref[0, 0, rows, :]
            g_hi = g_c.astype(jnp.bfloat16)
            r1 = g_c - g_hi.astype(jnp.float32)
            g_mid = r1.astype(jnp.bfloat16)
            g_lo = (r1 - g_mid.astype(jnp.float32)).astype(jnp.bfloat16)
            gc = jnp.dot(tri3, jnp.concatenate([g_hi, g_mid, g_lo], axis=0),
                         preferred_element_type=jnp.float32)
            tot = jnp.sum(g_c, axis=0, keepdims=True)
            for h in range(C_HEADS):
                probs.append(dict(c=c, h=h, rows=rows, gc=gc[:, h:h + 1], bt=b_c[:, h:h + 1], tot=tot[:, h:h + 1]))
        for pr in probs:
            h, rows = pr["h"], pr["rows"]
            q = qkv_s[rows, h *HEAD_LANES:(h + 1) * HEAD_LANES]
            k = qkv_s[rows, C_W + h * HEAD_LANES:C_W + (h + 1) * HEAD_LANES]
            kb = k * pr["bt"]
            kq = lax.dot_general(jnp.concatenate([kb, q], axis=0).astype(jnp.bfloat16), k.astype(jnp.bfloat16),
                                 _NT, preferred_element_type=jnp.float32)
            gc_row = jnp.broadcast_to(pr["gc"], (cs, HEAD_LANES)).T[:cs, :]
            dm = jnp.where(incl, jnp.exp(pr["gc"] - gc_row), 0.0)
            x = jnp.where(strict, -(kq[:cs] * dm), 0.0)
            in_ref[0, 0, pr["c"], h] = (kq[cs:] * dm).astype(in_ref.dtype)
            pr.update(t=eye + x, pw=x)
        for _ in range(5):
            for pr in probs:
                pr["pw"] = _dot3(pr["pw"], pr["pw"])
            for pr in probs:
                pr["t"] = pr["t"] + _dot3(pr["t"], pr["pw"])
        for pr in probs:
            h, rows = pr["h"], pr["rows"]
            lanes = slice(h * HEAD_LANES, (h + 1) * HEAD_LANES)
            q = qkv_s[rows, h *HEAD_LANES:(h + 1) * HEAD_LANES]
            k = qkv_s[rows, C_W + h * HEAD_LANES:C_W + (h + 1) * HEAD_LANES]
            v = qkv_s[rows, 2 * C_W + h * HEAD_LANES:2 * C_W + (h + 1) * HEAD_LANES]
            eg = jnp.exp(pr["gc"])
            uw = _dot3(pr["t"], jnp.concatenate([v * pr["bt"], k * (pr["bt"] * eg)], axis=1))
            u_ref[0, 0, rows, lanes] = uw[:, :HEAD_LANES]
            w_ref[0, 0, rows, lanes] = uw[:, HEAD_LANES:].astype(w_ref.dtype)
            qd_ref[0, 0, rows, lanes] = (q * eg).astype(qd_ref.dtype)
            kd_ref[0, 0, rows, lanes] = (k * jnp.exp(pr["tot"] - pr["gc"])).astype(kd_ref.dtype)
            gl_ref[0, 0, pr["c"], h:h + 1, :] = jnp.broadcast_to(jnp.exp(pr["tot"]), (1, HEAD_LANES))
        return carry

    lax.fori_loop(0, nc // GDN_PAR, chunk_pair, 0)


def gdn_chunk_prep(qkv_c, qkv_l, beta, g, nc=4):
    b, lc, _ = qkv_c.shape
    l = lc + qkv_l.shape[1]
    cs = GDN_CHUNK
    tq = nc * cs
    assert lc == tq, "the context must fill exactly the first token block"
    nchunks = l // cs
    bf = jnp.bfloat16
    big = lambda dt: jax.ShapeDtypeStruct((2, b, l, C_W), dt)
    bspec = pl.BlockSpec((1, 1, tq, C_W), lambda d, i, m: (d, i, m, 0))
    gspec = pl.BlockSpec((1, 1, tq, C_HEADS), lambda d, i, m: (d, i, m, 0))
    return pl.pallas_call(
        functools.partial(_gdn_chunk_kernel, nc=nc),
        grid=(2, b, l // tq),
        in_specs=[pl.BlockSpec((1, tq, 3 * C_W), lambda d, i, m: (i, 0, 0)),
                  pl.BlockSpec((1, tq, 3 * C_W), lambda d, i, m: (i, jnp.maximum(m - 1, 0), 0)), gspec, gspec],
        out_specs=[bspec, bspec, bspec, bspec,
                   pl.BlockSpec((1, 1, nc, C_HEADS, cs, cs), lambda d, i, m: (d, i, m, 0, 0, 0)),
                   pl.BlockSpec((1, 1, nc, C_HEADS, HEAD_LANES), lambda d, i, m: (d, i, m, 0, 0))],
        out_shape=[big(jnp.float32), big(bf), big(bf), big(bf),
                   jax.ShapeDtypeStruct((2, b, nchunks, C_HEADS, cs, cs), bf),
                   jax.ShapeDtypeStruct((2, b, nchunks, C_HEADS, HEAD_LANES), jnp.float32)],
        scratch_shapes=[pltpu.VMEM((tq, 3 * C_W), jnp.float32)],
        compiler_params=pltpu.CompilerParams(
            dimension_semantics=("arbitrary",) * 3, vmem_limit_bytes=VMEM_LIMIT_BYTES),
        name="gdn_chunk_prep",
    )(qkv_c, qkv_l, beta, g)


def _gdn_scan_kernel(*refs):
    ins, (of_ref, ob_ref, s_ref) = refs[:12], refs[12:]
    step = pl.program_id(1)

    @pl.when(step == 0)
    def _():
        s_ref[...] = jnp.zeros_like(s_ref)

    dot = functools.partial(jnp.dot, preferred_element_type=jnp.float32)
    seqs = [(d, h, slice(h * HEAD_LANES, (h + 1) * HEAD_LANES)) for d in range(2) for h in range(C_HEADS)]
    outs = (of_ref, ob_ref)
    sb, vb = {}, {}
    for d, h, lanes in seqs:
        sb[d, h] = s_ref[d * C_HEADS + h].astype(jnp.bfloat16)
    for d, h, lanes in seqs:
        u_ref, w_ref = ins[6 * d], ins[6 * d + 1]
        vb[d, h] = (u_ref[0, 0, :, lanes] - dot(w_ref[0, 0, :, lanes], sb[d, h])).astype(jnp.bfloat16)
    for d, h, lanes in seqs:
        qd_ref, in_ref = ins[6 * d + 2], ins[6 * d + 4]
        outs[d][0, :, lanes] = dot(qd_ref[0, 0, :, lanes], sb[d, h]) + dot(in_ref[0, 0, 0, h], vb[d, h])
    for d, h, lanes in seqs:
        kd_ref, gl_ref = ins[6 * d + 3], ins[6 * d + 5]
        s_ref[d * C_HEADS + h] = s_ref[d * C_HEADS + h] * gl_ref[0, 0, 0, h:h + 1, :] + lax.dot_general(
            kd_ref[0, 0, :, lanes], vb[d, h], (((0,), (0,)), ((), ())), preferred_element_type=jnp.float32)


def gdn_scan(u, w, qd, kd, intra, gl, n_ctx_chunks):
    _, b, l, _ = u.shape
    cs = GDN_CHUNK
    nchunks = l // cs

    def chunk_of(d, s):
        if d == 0:
            return s
        return jnp.where(s < n_ctx_chunks, n_ctx_chunks - 1 - s, nchunks - 1 + n_ctx_chunks - s)

    in_specs, args = [], []
    for d in range(2):
        big = pl.BlockSpec((1, 1, cs, C_W), lambda i, s, d=d: (d, i, chunk_of(d, s), 0))
        in_specs += [big, big, big, big,
                     pl.BlockSpec((1, 1, 1, C_HEADS, cs, cs), lambda i, s, d=d: (d, i, chunk_of(d, s), 0, 0, 0)),
                     pl.BlockSpec((1, 1, 1, C_HEADS, HEAD_LANES), lambda i, s, d=d: (d, i, chunk_of(d, s), 0, 0))]
        args += [u, w, qd, kd, intra, gl]
    out_specs = [pl.BlockSpec((1, cs, C_W), lambda i, s, d=d: (i, chunk_of(d, s), 0)) for d in range(2)]
    return pl.pallas_call(
        _gdn_scan_kernel,
        grid=(b, nchunks),
        in_specs=in_specs,
        out_specs=out_specs,
        out_shape=[jax.ShapeDtypeStruct((b, l, C_W), jnp.float32)] * 2,
        scratch_shapes=[pltpu.VMEM((2 * C_HEADS, C_DK, C_DV), jnp.float32)],
        compiler_params=pltpu.CompilerParams(
            dimension_semantics=("arbitrary",) * 2, vmem_limit_bytes=VMEM_LIMIT_BYTES),
        name="gdn_scan",
    )(*args)


def _gdn_gate_kernel(of_ref, ob_ref, z_ref, gw_ref, y_ref):
    o = of_ref[0] + ob_ref[0]
    z = z_ref[0]
    for h in range(C_HEADS):
        lanes = slice(h * HEAD_LANES, (h + 1) * HEAD_LANES)
        oh = o[:, lanes]
        zh = z[:, lanes]
        n = oh * lax.rsqrt(jnp.mean(oh * oh, axis=-1, keepdims=True) + EPS) * gw_ref[...]
        y_ref[0, :, lanes] = (n * (zh * jax.nn.sigmoid(zh))).astype(y_ref.dtype)


def gdn_gate(o_f, o_b, row0, p, gnorm_w, tq=256):
    b, l, _ = p.shape
    tq = min(tq, l)
    r0 = row0 // tq
    ospec = pl.BlockSpec((1, tq, C_W), lambda i, m: (i, r0 + m, 0))
    return pl.pallas_call(
        _gdn_gate_kernel,
        grid=(b, l // tq),
        in_specs=[ospec, ospec,
                  pl.BlockSpec((1, tq, C_W), lambda i, m: (i, m, 3)),
                  pl.BlockSpec((1, HEAD_LANES), lambda i, m: (0, 0))],
        out_specs=pl.BlockSpec((1, tq, C_W), lambda i, m: (i, m, 0)),
        out_shape=jax.ShapeDtypeStruct((b, l, C_W), jnp.bfloat16),
        compiler_params=pltpu.CompilerParams(
            dimension_semantics=("arbitrary",) * 2, vmem_limit_bytes=VMEM_LIMIT_BYTES),
        name="gdn_gate",
    )(o_f, o_b, p, gnorm_w.reshape(1, HEAD_LANES))


def gdn_mixer(p, pc, gates_l, gates_c, conv_w, a_log, dt_bias, gnorm_w, with_ctx):
    lc = pc.shape[1]
    conv = functools.partial(short_conv, col0=0, width=3 * C_W, w=conv_w, silu=True,
                             n_l2=2 * C_HEADS, n_scaled=C_HEADS)
    gates = jnp.concatenate([gates_c, gates_l], axis=1)
    gates = gates.reshape(gates.shape[0], gates.shape[1], 4, C_HEADS)
    beta = jax.nn.sigmoid(gates[:, :, :2])
    g = -jnp.exp(a_log) * jax.nn.softplus(gates[:, :, 2:] + dt_bias)
    beta = jnp.moveaxis(beta, 2, 0)
    g = jnp.moveaxis(g, 2, 0)
    u, w, qd, kd, intra, gl = gdn_chunk_prep(conv(pc), conv(p), beta, g)
    o_f, o_b = gdn_scan(u, w, qd, kd, intra, gl, lc // GDN_CHUNK)
    out = gdn_gate(o_f, o_b, lc, p, gnorm_w)
    out_c = gdn_gate(o_f, o_b, 0, pc, gnorm_w) if with_ctx else None
    return out, out_c


FFT_R = 128
FFT_N = FFT_R * FFT_R
SUB = 8
K1_PER_STEP = 2


def stage_a_table():
    idx = np.arange(FFT_R)
    ang = 2.0 * np.pi * np.outer(idx, idx) / FFT_R
    return jnp.asarray(np.stack([np.cos(ang), -np.sin(ang)], axis=1).reshape(2 * FFT_R, FFT_R), jnp.float32)


def _fft_stage_a_kernel(x_ref, l_ref, y_ref):
    l = l_ref[...]
    c = x_ref.shape[-1]
    xs = jnp.swapaxes(x_ref[0], 0, 1)
    ys = jnp.stack([_dot3(l, xs[j]) for j in range(SUB)], axis=0)
    y_ref[0] = jnp.swapaxes(ys, 0, 1).reshape(FFT_R, 2, SUB, c)


def fft_stage_a(x, col_blk, width, stage_a):
    b, l, wtot = x.shape
    n1cnt = l // FFT_R
    x4 = x.reshape(b, n1cnt, FFT_R, wtot)
    return pl.pallas_call(
        _fft_stage_a_kernel,
        grid=(b, FFT_R // SUB),
        in_specs=[pl.BlockSpec((1, n1cnt, SUB, width), lambda i, j: (i, 0, j, col_blk)),
                  pl.BlockSpec((2 * FFT_R, n1cnt), lambda i, j: (0, 0))],
        out_specs=pl.BlockSpec((1, FFT_R, 2, SUB, width), lambda i, j: (i, 0, 0, j, 0)),
        out_shape=jax.ShapeDtypeStruct((b, FFT_R, 2, FFT_R, width), jnp.float32),
        compiler_params=pltpu.CompilerParams(
            dimension_semantics=("arbitrary",) * 2, vmem_limit_bytes=VMEM_LIMIT_BYTES),
        name="fft_stage_a",
    )(x4, stage_a[:, :n1cnt])


def _dot3_presplit(ah, al, b, transpose_lhs=False):
    bh = b.astype(jnp.bfloat16)
    bl = (b - bh.astype(jnp.float32)).astype(jnp.bfloat16)
    dims = (((0,), (0,)), ((), ())) if transpose_lhs else (((1,), (0,)), ((), ()))
    d = functools.partial(lax.dot_general, dimension_numbers=dims, preferred_element_type=jnp.float32)
    return d(ah, bh) + (d(ah, bl) + d(al, bh))


def stage_b_tables():
    r = FFT_R
    k1 = jnp.arange(r, dtype=jnp.int32)[:, None, None]
    k2 = jnp.arange(r, dtype=jnp.int32)[None, :, None]
    n2 = jnp.arange(r, dtype=jnp.int32)[None, None, :]
    th = ((n2 * (r * k2 + k1)) % FFT_N).astype(jnp.float32) * (2.0 * math.pi / FFT_N)
    c, s = jnp.cos(th), jnp.sin(th)
    t = jnp.concatenate([jnp.concatenate([c, s], 2), jnp.concatenate([-s, c], 2)], 1)

    def split(m):
        hi = m.astype(jnp.bfloat16)
        return hi, (m - hi.astype(jnp.float32)).astype(jnp.bfloat16)

    return split(t)


def _fft_mid_kernel(y_ref, h_ref, th_ref, tl_ref, o_ref, *, conv):
    nb, nk, r, c = y_ref.shape[0], y_ref.shape[1], FFT_R, y_ref.shape[-1]
    probs = [(i, k) for k in range(nk) for i in range(nb)]
    xs = [_dot3_presplit(th_ref[k], tl_ref[k], y_ref[i, k].reshape(2 * r, c)) for i, k in probs]
    if not conv:
        for (i, k), x in zip(probs, xs):
            o_ref[i, k] = (x * ((1.0 / FFT_N) / h_ref[i])).reshape(2, r, c)
        return
    ps = [jnp.concatenate([x[:r] * h_ref[0, k, 0] - x[r:] * h_ref[0, k, 1],
                           x[:r] * h_ref[0, k, 1] + x[r:] * h_ref[0, k, 0]], axis=0) for (i, k), x in zip(probs, xs)]
    for (i, k), p in zip(probs, ps):
        o_ref[i, k] = _dot3_presplit(th_ref[k], tl_ref[k], p, transpose_lhs=True).reshape(2, r, c)


def fft_mid(y, h, tables, conv, order=0):
    b, r, _, _, c = y.shape
    hh = h if conv else h.reshape(b, 1, c)
    hspec = (pl.BlockSpec((1, K1_PER_STEP, 2, r, c), lambda k: (order, k, 0, 0, 0)) if conv
             else pl.BlockSpec((b, 1, c), lambda k: (0, 0, 0)))
    blk = pl.BlockSpec((b, K1_PER_STEP, 2, r, c), lambda k: (0, k, 0, 0, 0))
    tspec = pl.BlockSpec((K1_PER_STEP, 2 * r, 2 * r), lambda k: (k, 0, 0))
    return pl.pallas_call(
        functools.partial(_fft_mid_kernel, conv=conv),
        grid=(r // K1_PER_STEP,),
        in_specs=[blk, hspec, tspec, tspec],
        out_specs=blk,
        out_shape=jax.ShapeDtypeStruct(y.shape, jnp.float32),
        compiler_params=pltpu.CompilerParams(
            dimension_semantics=("arbitrary",), vmem_limit_bytes=VMEM_LIMIT_BYTES),
        name="fft_mid",
    )(y, hh, *tables)


def _fft_out_kernel(b_ref, l_ref, xg_ref, xin_ref, bias_ref, o_ref):
    l = l_ref[...]
    c = o_ref.shape[-1]
    bs = jnp.swapaxes(b_ref[0].reshape(2 * FFT_R, SUB, c), 0, 1)
    ys = jnp.stack([_dot3(l, bs[j]) for j in range(SUB)], axis=0)
    y = jnp.swapaxes(ys, 0, 1)
    o_ref[0] = xg_ref[0] * (y + bias_ref[...] * xin_ref[0])


def fft_out_gate(bm, stage_a, xg, xg_blk, xin, xin_blk, bias):
    b, r, _, _, c = bm.shape
    l = xg.shape[1]
    n1cnt = l // r
    view = lambda t: t.reshape(b, n1cnt, r, t.shape[-1])
    lhs = stage_a.T[:n1cnt]
    return pl.pallas_call(
        _fft_out_kernel,
        grid=(b, r // SUB),
        in_specs=[pl.BlockSpec((1, r, 2, SUB, c), lambda i, j: (i, 0, 0, j, 0)),
                  pl.BlockSpec((n1cnt, 2 * r), lambda i, j: (0, 0)),
                  pl.BlockSpec((1, n1cnt, SUB, c), lambda i, j: (i, 0, j, xg_blk)),
                  pl.BlockSpec((1, n1cnt, SUB, c), lambda i, j: (i, 0, j, xin_blk)),
                  pl.BlockSpec((1, c), lambda i, j: (0, 0))],
        out_specs=pl.BlockSpec((1, n1cnt, SUB, c), lambda i, j: (i, 0, j, 0)),
        out_shape=jax.ShapeDtypeStruct((b, n1cnt, r, c), jnp.float32),
        compiler_params=pltpu.CompilerParams(
            dimension_semantics=("arbitrary",) * 2, vmem_limit_bytes=VMEM_LIMIT_BYTES),
        name="fft_out_gate",
    )(bm, lhs, view(xg), view(xin), bias.reshape(1, c)).reshape(b, l, c)


def _direct_conv_kernel(xin_ref, xg_ref, kern_ref, d1_ref, d2_ref, norm_ref, bias_ref, o_ref):
    n = xin_ref.shape[1]
    d1 = d1_ref[...]
    x = xin_ref[0]
    xs = _dot3(d1[:, :n], x)
    hs = _dot3(d1, kern_ref[...]) / norm_ref[...]
    xr, xi, hr, hi = xs[:2 * n], xs[2 * n:], hs[:2 * n], hs[2 * n:]
    p = jnp.concatenate([xr * hr - xi * hi, xr * hi + xi * hr], axis=0)
    y = _dot3(d2_ref[...], p)
    o_ref[0] = xg_ref[0] * (y + bias_ref[...] * x)


def direct_long_conv(xin, xin_blk, xg, xg_blk, kern, norm, bias):
    b, n, _ = xin.shape
    c = kern.shape[1]
    idx = np.arange(2 * n)
    ang = 2.0 * np.pi * np.outer(idx, idx) / (2 * n)
    d1 = jnp.asarray(np.concatenate([np.cos(ang), -np.sin(ang)], axis=0), jnp.float32)
    d2 = jnp.asarray(np.concatenate([np.cos(ang[:n]), -np.sin(ang[:n])], axis=1) / (2 * n), jnp.float32)
    return pl.pallas_call(
        _direct_conv_kernel,
        grid=(b,),
        in_specs=[pl.BlockSpec((1, n, c), lambda i: (i, 0, xin_blk)),
                  pl.BlockSpec((1, n, c), lambda i: (i, 0, xg_blk)),
                  pl.BlockSpec((2 * n, c), lambda i: (0, 0)),
                  pl.BlockSpec((4 * n, 2 * n), lambda i: (0, 0)),
                  pl.BlockSpec((n, 4 * n), lambda i: (0, 0)),
                  pl.BlockSpec((1, c), lambda i: (0, 0)),
                  pl.BlockSpec((1, c), lambda i: (0, 0))],
        out_specs=pl.BlockSpec((1, n, c), lambda i: (i, 0, 0)),
        out_shape=jax.ShapeDtypeStruct((b, n, c), jnp.float32),
        compiler_params=pltpu.CompilerParams(
            dimension_semantics=("arbitrary",), vmem_limit_bytes=VMEM_LIMIT_BYTES),
        name="direct_long_conv",
    )(xin, xg, kern, d1, d2, norm.reshape(1, c), bias.reshape(1, c))


def _hy_filter_kernel(w1_ref, b1_ref, fr_ref, w2_ref, b2_ref, w3_ref, dl_ref, k_ref, s_ref, *, n):
    i = pl.program_id(0)
    tp, c = k_ref.shape[1], k_ref.shape[2]

    @pl.when(i == 0)
    def _():
        s_ref[...] = jnp.zeros_like(s_ref)

    def pos(shape):
        idx = i * tp + lax.broadcasted_iota(jnp.int32, shape, 0)
        t = jnp.where(idx < n, idx, jnp.where(idx == n, 0, 2 * n - idx))
        return idx, t.astype(jnp.float32)

    _, t = pos((tp, LANES))
    lane = lax.broadcasted_iota(jnp.int32, (tp, LANES), 1)
    band = jnp.where(lane <= HY_BANDS, lane, lane - HY_BANDS).astype(jnp.float32)
    ang = 2.0 * math.pi * t * band / n
    feat = jnp.where(lane == 0, t / n,
                     jnp.where(lane <= HY_BANDS, jnp.sin(ang), jnp.where(lane < HY_EMB, jnp.cos(ang), 0.0)))
    hid = jnp.sin(fr_ref[...] * (_dot3(feat, w1_ref[...]) + b1_ref[...]))
    hid = jnp.sin(fr_ref[...] * (_dot3(hid, w2_ref[...]) + b2_ref[...]))
    f = _dot3(hid, w3_ref[...])
    idx, t = pos((tp, c))
    decay = jnp.exp(-(t / n) * dl_ref[...])
    for o in range(HY_ORDER):
        fwd = f[:, (2 * o) * c:(2 * o + 1) * c]
        bwd = f[:, (2 * o + 1) * c:(2 * o + 2) * c]
        val = jnp.where(idx < n, fwd, bwd) * decay
        s_ref[o:o + 1, :] += jnp.sum(jnp.abs(val), axis=0, keepdims=True)
        k_ref[o] = jnp.where(idx == n, 0.0, val)


def hyena_kernels(n, w1, b1, freq, w2, b2, w3):
    c = HY_CH
    tp = min(512, n)
    max_decay = math.log(HY_TARGET) / HY_FAST_DECAY
    min_decay = math.log(HY_TARGET) / HY_SLOW_DECAY
    deltas = jnp.abs(jnp.linspace(min_decay, max_decay, c, dtype=jnp.float32)).reshape(1, c)
    w1p = jnp.zeros((LANES, w1.shape[1]), jnp.float32).at[:w1.shape[0]].set(w1)
    hd = w1.shape[1]
    full = lambda shape: pl.BlockSpec(shape, lambda i: (0,) * len(shape))
    return pl.pallas_call(
        functools.partial(_hy_filter_kernel, n=n),
        grid=(2 * n // tp,),
        in_specs=[full((LANES, hd)), full((1, hd)), full((1, hd)), full((hd, hd)), full((1, hd)),
                  full((hd, HY_ORDER * 2 * c)), full((1, c))],
        out_specs=[pl.BlockSpec((HY_ORDER, tp, c), lambda i: (0, i, 0)), full((HY_ORDER, c))],
        out_shape=[jax.ShapeDtypeStruct((HY_ORDER, 2 * n, c), jnp.float32),
                   jax.ShapeDtypeStruct((HY_ORDER, c), jnp.float32)],
        compiler_params=pltpu.CompilerParams(
            dimension_semantics=("arbitrary",), vmem_limit_bytes=VMEM_LIMIT_BYTES),
        name="hyena_kernels",
    )(w1p, b1.reshape(1, hd), freq.reshape(1, hd), w2, b2.reshape(1, hd), w3, deltas)


def hyena_mixer(p, col0, conv_w, conv_b, filt_args, hy_bias):
    n = p.shape[1]
    uc = short_conv(p, col0, 3 * HY_CH, conv_w, conv_b)
    kerns, norm = hyena_kernels(n, *filt_args)
    if 2 * n != FFT_N:
        v = direct_long_conv(uc, 2, uc, 0, kerns[0], norm[0], hy_bias[0])
        return direct_long_conv(v, 0, uc, 1, kerns[1], norm[1], hy_bias[1])
    stage_a, stage_b = stage_a_table(), stage_b_tables()
    spec = fft_mid(fft_stage_a(kerns, 0, HY_CH, stage_a), norm, stage_b, conv=False)
    v = fft_out_gate(fft_mid(fft_stage_a(uc, 2, HY_CH, stage_a), spec, stage_b, conv=True, order=0),
                     stage_a, uc, 0, uc, 2, hy_bias[0])
    return fft_out_gate(fft_mid(fft_stage_a(v, 0, HY_CH, stage_a), spec, stage_b, conv=True, order=1),
                        stage_a, uc, 1, v, 0, hy_bias[1])


def even_mixer(p, pc, rope_tabs, sink, conv_w, conv_b, fw1, fb1, ffreq, fw2, fb2, fw3, hy_bias, with_ctx):
    q, k, v = qkv_prep(p, 0, A_HEADS, A_KV_HEADS, rope_tabs)
    qc, kc, vc = qkv_prep(pc, 0, A_HEADS, A_KV_HEADS, None)
    o_a = windowed_sink_gqa(q, k, v, kc, vc, sink)
    filt_args = (fw1, fb1, ffreq, fw2, fb2, fw3)
    o_b = hyena_mixer(p, A_Q + 2 * A_KV, conv_w, conv_b, filt_args, hy_bias)
    out_c = None
    if with_ctx:
        o_ac = flash_gqa(qc, kc, vc, sink)
        o_bc = hyena_mixer(pc, A_Q + 2 * A_KV, conv_w, conv_b, filt_args, hy_bias)
        out_c = (o_ac, o_bc)
    return (o_a, o_b), out_c


def odd_mixer(p, pc, gates_l, gates_c, rope_tabs, conv_w, a_log, dt_bias, gnorm_w, qnorm_w, knorm_w, with_ctx):
    o_l, o_c = gdn_mixer(p, pc, gates_l, gates_c, conv_w, a_log, dt_bias, gnorm_w, with_ctx)
    qd, kd, vd = qkv_prep(p, 4 * C_W, D_HEADS, D_KV_HEADS, rope_tabs, qnorm_w, knorm_w)
    qdc, kdc, vdc = qkv_prep(pc, 4 * C_W, D_HEADS, D_KV_HEADS, None, qnorm_w, knorm_w)
    o_d = flash_gqa(qd, jnp.concatenate([kd, kdc], 2), jnp.concatenate([vd, vdc], 2))
    out_c = None
    if with_ctx:
        out_c = (o_c, flash_gqa(qdc, kdc, vdc))
    return (o_l, o_d), out_c


def kernel(x, c, ctx, c_ctx, ada_w, ada_b, ln1_g, ln1_b, ln2_g, ln2_b, peer_wq, peer_k1, peer_k2, peer_u, peer_v, ev_w_in, ev_w_out, ev_sink, ev_conv_w, ev_conv_b, ev_filt_w1, ev_filt_b1, ev_filt_freq, ev_filt_w2, ev_filt_b2, ev_filt_w3, ev_hy_bias, od_w_in, od_w_out, od_conv_w, od_a_log, od_dt_bias, od_gnorm_w, od_qnorm_w, od_knorm_w):
    rope_tabs = rope_tables(x.shape[1])
    bsz = x.shape[0]
    silu_c = jax.nn.silu(c)
    silu_cc = jax.nn.silu(c_ctx)
    for i in range(DEPTH):
        with_ctx = i < DEPTH - 1
        j = i // 2
        mod = (silu_c @ ada_w[i] + ada_b[i])[:, None, :]
        modc = jnp.broadcast_to((silu_cc @ ada_w[i] + ada_b[i])[None, None, :], (bsz, 1, 6 * D_MODEL))
        sh1, sc1, g1, sh2, sc2, g2 = jnp.split(mod, 6, axis=-1)
        sh1c, sc1c, g1c, sh2c, sc2c, g2c = jnp.split(modc, 6, axis=-1)
        if i % 2 == 0:
            p = mod_matmul(x, sh1, sc1, ev_w_in[j])
            pc = mod_matmul(ctx, sh1c, sc1c, ev_w_in[j])
            out, out_c = even_mixer(p, pc, rope_tabs, ev_sink[j], ev_conv_w[j], ev_conv_b[j],
                                    ev_filt_w1[j], ev_filt_b1[j], ev_filt_freq[j], ev_filt_w2[j], ev_filt_b2[j],
                                    ev_filt_w3[j], ev_hy_bias[j], with_ctx)
            w_out = ev_w_out[j]
        else:
            w_in = od_w_in[j]
            w_gate = jnp.pad(w_in[:, 4 * C_W:4 * C_W + C_GATES], ((0, 0), (0, LANES - C_GATES)))
            w_in = jnp.concatenate([w_in[:, :4 * C_W], w_in[:, 4 * C_W + C_GATES:]], axis=1)
            p = mod_matmul(x, sh1, sc1, w_in)
            pc = mod_matmul(ctx, sh1c, sc1c, w_in)
            gates_l = mod_matmul(x, sh1, sc1, w_gate)[..., :C_GATES]
            gates_c = mod_matmul(ctx, sh1c, sc1c, w_gate)[..., :C_GATES]
            out, out_c = odd_mixer(p, pc, gates_l, gates_c, rope_tabs, od_conv_w[j], od_a_log[j], od_dt_bias[j],
                                   od_gnorm_w[j], od_qnorm_w[j], od_knorm_w[j], with_ctx)
            w_out = od_w_out[j]
        u_bf = layer_table_bf16(peer_u, i)
        v_bf = layer_table_bf16(peer_v, i)
        x = proj_residual_ln(out[0], out[1], w_out, x, g1, ln1_g[i], ln1_b[i])
        x = peer_block(x, sh2, sc2, g2, peer_wq[i], peer_k1[i], peer_k2[i], u_bf, v_bf, ln2_g[i], ln2_b[i])
        if with_ctx:
            ctx = proj_residual_ln(out_c[0], out_c[1], w_out, ctx, g1c, ln1_g[i], ln1_b[i])
            ctx = peer_block(ctx, sh2c, sc2c, g2c, peer_wq[i], peer_k1[i], peer_k2[i], u_bf, v_bf,
                             ln2_g[i], ln2_b[i])
    return x
```

```python
import functools
import math

import numpy as np

import jax
import jax.numpy as jnp
from jax import lax
from jax.experimental import pallas as pl
from jax.experimental.pallas import tpu as pltpu

D_MODEL = 1024
DEPTH = 2
GRID_W = 64
HEAD_DIM = 64
BLOCK = 128
ROPE_BASE = 10000.0
EPS = 1e-6

A_HEADS = 8
A_KV_HEADS = 2
WINDOW = 128

HY_CH = 512
HY_ORDER = 2
HY_EMB = 33
HY_BANDS = (HY_EMB - 1) // 2
HY_FAST_DECAY = 0.3
HY_SLOW_DECAY = 1.5
HY_TARGET = 1e-2

C_HEADS = 4
C_DK = 128
C_DV = 128
GDN_CHUNK = 64

D_HEADS = 8
D_KV_HEADS = 2

PEER_HEADS = 8
PEER_NKEYS = 128
PEER_QDIM = 256
PEER_TOPK = 16

ALPHA = (2 * DEPTH) ** 0.25

A_Q = A_HEADS * HEAD_DIM
A_KV = A_KV_HEADS * HEAD_DIM
C_W = C_HEADS * C_DK
C_GATES = 4 * C_HEADS

VMEM_LIMIT_BYTES = 48 * 1024 * 1024

LANES = 128
_NT = (((1,), (1,)), ((), ()))


def _modmm_kernel(x_ref, sh_ref, sc_ref, w_ref, o_ref):
    h = x_ref[0] * (1.0 + sc_ref[0]) + sh_ref[0]
    o_ref[0] = jnp.dot(h.astype(jnp.bfloat16), w_ref[...], preferred_element_type=jnp.float32)


def mod_matmul(x, shift, scale, w, tm=512, tn=None):
    b, s, k = x.shape
    n = w.shape[1]
    tm = min(tm, s)
    tn = n if tn is None else tn
    wb = w.astype(jnp.bfloat16)
    return pl.pallas_call(
        _modmm_kernel,
        grid=(b, n // tn, s // tm),
        in_specs=[
            pl.BlockSpec((1, tm, k), lambda i, j, m: (i, m, 0)),
            pl.BlockSpec((1, 1, k), lambda i, j, m: (i, 0, 0)),
            pl.BlockSpec((1, 1, k), lambda i, j, m: (i, 0, 0)),
            pl.BlockSpec((k, tn), lambda i, j, m: (0, j)),
        ],
        out_specs=pl.BlockSpec((1, tm, tn), lambda i, j, m: (i, m, j)),
        out_shape=jax.ShapeDtypeStruct((b, s, n), jnp.float32),
        compiler_params=pltpu.CompilerParams(
            dimension_semantics=("arbitrary", "arbitrary", "arbitrary"),
            vmem_limit_bytes=VMEM_LIMIT_BYTES),
        name="mod_matmul",
    )(x, shift, scale, wb)


def _qkv_prep_kernel(q_ref, k_ref, v_ref, cs_ref, sn_ref, qw_ref, kw_ref, gm_ref, qo_ref, ko_ref, vo_ref, *,
                     norm, rope, nq, nkv):
    def prep(x, w, nh):
        if norm:
            ms = jnp.dot(x * x, gm_ref[:x.shape[1], :x.shape[1]], precision=lax.Precision.HIGHEST,
                         preferred_element_type=jnp.float32)
            x = x * lax.rsqrt(ms + EPS) * w
        if rope:
            n = x.shape[1]
            reps = n // cs_ref.shape[1]
            cs = jnp.concatenate([cs_ref[...]] * reps, axis=1) if reps > 1 else cs_ref[...]
            sn = jnp.concatenate([sn_ref[...]] * reps, axis=1) if reps > 1 else sn_ref[...]
            lane = lax.broadcasted_iota(jnp.int32, x.shape, 1)
            nf = HEAD_DIM // 4
            partner = jnp.where((lane & nf) == 0, pltpu.roll(x, n - nf, 1), pltpu.roll(x, nf, 1))
            x = x * cs + partner * sn
        return x

    q = prep(q_ref[0], qw_ref[...], nq) * (HEAD_DIM ** -0.5)
    k = prep(k_ref[0], kw_ref[...], nkv)
    v = v_ref[0]
    for h in range(nq):
        qo_ref[0, h] = q[:, h * HEAD_DIM:(h + 1) * HEAD_DIM].astype(jnp.bfloat16)
    for h in range(nkv):
        ko_ref[0, h] = k[:, h * HEAD_DIM:(h + 1) * HEAD_DIM].astype(jnp.bfloat16)
        vo_ref[0, h] = v[:, h * HEAD_DIM:(h + 1) * HEAD_DIM].astype(jnp.bfloat16)


def qkv_prep(p, col0, nq, nkv, rope_tabs, qw=None, kw=None, tq=512):
    b, s, _ = p.shape
    tq = min(tq, s)
    wq_, wk_ = nq * HEAD_DIM, nkv * HEAD_DIM
    norm = qw is not None
    rope = rope_tabs is not None
    if rope:
        cs, sn = rope_tabs
    else:
        cs = sn = jnp.zeros((s, 2 * HEAD_DIM), jnp.float32)
    qw_t = jnp.tile(qw, nq).reshape(1, wq_) if norm else jnp.ones((1, wq_), jnp.float32)
    kw_t = jnp.tile(kw, nkv).reshape(1, wk_) if norm else jnp.ones((1, wk_), jnp.float32)
    grp = jnp.arange(wq_) // HEAD_DIM
    gm = (grp[:, None] == grp[None, :]).astype(jnp.float32) / HEAD_DIM
    kern = functools.partial(_qkv_prep_kernel, norm=norm, rope=rope, nq=nq, nkv=nkv)
    return pl.pallas_call(
        kern,
        grid=(b, s // tq),
        in_specs=[
            pl.BlockSpec((1, tq, wq_), lambda i, m: (i, m, col0 // wq_)),
            pl.BlockSpec((1, tq, wk_), lambda i, m: (i, m, (col0 + wq_) // wk_)),
            pl.BlockSpec((1, tq, wk_), lambda i, m: (i, m, (col0 + wq_) // wk_ + 1)),
            pl.BlockSpec((tq, 2 * HEAD_DIM), lambda i, m: (m, 0)),
            pl.BlockSpec((tq, 2 * HEAD_DIM), lambda i, m: (m, 0)),
            pl.BlockSpec((1, wq_), lambda i, m: (0, 0)),
            pl.BlockSpec((1, wk_), lambda i, m: (0, 0)),
            pl.BlockSpec((wq_, wq_), lambda i, m: (0, 0)),
        ],
        out_specs=[
            pl.BlockSpec((1, nq, tq, HEAD_DIM), lambda i, m: (i, 0, m, 0)),
            pl.BlockSpec((1, nkv, tq, HEAD_DIM), lambda i, m: (i, 0, m, 0)),
            pl.BlockSpec((1, nkv, tq, HEAD_DIM), lambda i, m: (i, 0, m, 0)),
        ],
        out_shape=[
            jax.ShapeDtypeStruct((b, nq, s, HEAD_DIM), jnp.bfloat16),
            jax.ShapeDtypeStruct((b, nkv, s, HEAD_DIM), jnp.bfloat16),
            jax.ShapeDtypeStruct((b, nkv, s, HEAD_DIM), jnp.bfloat16),
        ],
        compiler_params=pltpu.CompilerParams(
            dimension_semantics=("arbitrary", "arbitrary"), vmem_limit_bytes=VMEM_LIMIT_BYTES),
        name="qkv_prep",
    )(p, p, p, cs, sn, qw_t, kw_t, gm)


def rope_tables(n_tok):
    rows = n_tok // GRID_W
    row = jnp.repeat(jnp.arange(rows, dtype=jnp.float32), GRID_W)
    col = jnp.tile(jnp.arange(GRID_W, dtype=jnp.float32), rows)
    nf = HEAD_DIM // 4
    inv = ROPE_BASE ** (-jnp.arange(nf, dtype=jnp.float32) / nf)
    ar, ac = row[:, None] * inv, col[:, None] * inv
    cs = jnp.concatenate([jnp.cos(ar), jnp.cos(ar), jnp.cos(ac), jnp.cos(ac)], -1)
    sn = jnp.concatenate([-jnp.sin(ar), jnp.sin(ar), -jnp.sin(ac), jnp.sin(ac)], -1)
    return jnp.tile(cs, (1, 2)), jnp.tile(sn, (1, 2))


def _flash_kernel(sink_ref, q_ref, k_ref, v_ref, o_ref, m_s, l_s, acc_s, *, use_sink, grp):
    j = pl.program_id(3)
    tq = q_ref.shape[2]

    @pl.when(j == 0)
    def _():
        m_s[...] = jnp.full_like(m_s, -jnp.inf)
        l_s[...] = jnp.zeros_like(l_s)
        acc_s[...] = jnp.zeros_like(acc_s)

    tk = k_ref.shape[2]
    nt = tk // LANES
    kt = k_ref[0, 0]
    vt = v_ref[0, 0]
    scores = [lax.dot_general(q_ref[0, g], kt, _NT, preferred_element_type=jnp.float32) for g in range(grp)]
    for g, s in enumerate(scores):
        rows = slice(g * tq, (g + 1) * tq)
        tiles = [s[:, c * LANES:(c + 1) * LANES] for c in range(nt)]
        m_tile = functools.reduce(jnp.maximum, tiles)
        m_old = m_s[rows]
        m_new = jnp.maximum(m_old, jnp.broadcast_to(jnp.max(m_tile, axis=1, keepdims=True), m_old.shape))
        alpha = jnp.exp(m_old - m_new)
        p_tiles = [jnp.exp(t - m_new) for t in tiles]
        l_s[rows] = alpha * l_s[rows] + functools.reduce(jnp.add, p_tiles)
        p = jnp.concatenate([t.astype(jnp.bfloat16) for t in p_tiles], axis=1)
        acc_s[rows] = alpha[:, :HEAD_DIM] * acc_s[rows] + jnp.dot(p, vt, preferred_element_type=jnp.float32)
        m_s[rows] = m_new

    @pl.when(j == pl.num_programs(3) - 1)
    def _():
        kvh = pl.program_id(1)
        outs = []
        for g in range(grp):
            rows = slice(g * tq, (g + 1) * tq)
            m = m_s[rows][:, :1]
            l = jnp.sum(l_s[rows], axis=1, keepdims=True)
            acc = acc_s[rows]
            if use_sink:
                sk = sink_ref[kvh * grp + g]
                m2 = jnp.maximum(m, sk)
                a = jnp.exp(m - m2)
                l = a * l + jnp.exp(sk - m2)
                acc = a * acc
            outs.append(acc / l)
        o_ref[0] = jnp.concatenate(outs, axis=1).astype(o_ref.dtype)


def flash_gqa(q, k, v, sink=None, tq=256, tk=2816):
    b, h, s, hd = q.shape
    kvh, lk = k.shape[1], k.shape[2]
    grp = h // kvh
    tq = min(tq, s)
    tk = max(t for t in range(LANES, min(tk, lk) + 1, LANES) if lk % t == 0)
    use_sink = sink is not None
    sink_arr = sink.astype(jnp.float32) if use_sink else jnp.zeros((h,), jnp.float32)
    kern = functools.partial(_flash_kernel, use_sink=use_sink, grp=grp)
    return pl.pallas_call(
        kern,
        grid=(b, kvh, s // tq, lk // tk),
        in_specs=[
            pl.BlockSpec(memory_space=pltpu.SMEM),
            pl.BlockSpec((1, grp, tq, hd), lambda i, c, m, j: (i, c, m, 0)),
            pl.BlockSpec((1, 1, tk, hd), lambda i, c, m, j: (i, c, j, 0)),
            pl.BlockSpec((1, 1, tk, hd), lambda i, c, m, j: (i, c, j, 0)),
        ],
        out_specs=pl.BlockSpec((1, tq, grp * hd), lambda i, c, m, j: (i, m, c)),
        out_shape=jax.ShapeDtypeStruct((b, s, h * hd), jnp.bfloat16),
        scratch_shapes=[pltpu.VMEM((grp * tq, LANES), jnp.float32), pltpu.VMEM((grp * tq, LANES), jnp.float32),
                        pltpu.VMEM((grp * tq, hd), jnp.float32)],
        compiler_params=pltpu.CompilerParams(
            dimension_semantics=("arbitrary",) * 4, vmem_limit_bytes=VMEM_LIMIT_BYTES),
        name="flash_gqa",
    )(sink_arr, q, k, v)


def _window_kernel(sink_ref, q_ref, kp_ref, kc_ref, kn_ref, vp_ref, vc_ref, vn_ref, kx_ref, vx_ref, o_ref, *, grp):
    i = pl.program_id(1)
    nb = pl.num_programs(1)
    kvh = kc_ref.shape[1]
    kcats = [jnp.concatenate([kp_ref[0, c], kc_ref[0, c], kn_ref[0, c], kx_ref[0, c]], axis=0) for c in range(kvh)]
    vcats = [jnp.concatenate([vp_ref[0, c], vc_ref[0, c], vn_ref[0, c], vx_ref[0, c]], axis=0) for c in range(kvh)]
    nk = kcats[0].shape[0]
    r = lax.broadcasted_iota(jnp.int32, (BLOCK, nk), 0)
    c = lax.broadcasted_iota(jnp.int32, (BLOCK, nk), 1)
    off_prev = jnp.where(i > 0, 0, 2 * nk)
    off_next = jnp.where(i < nb - 1, 0, 2 * nk)
    ok_prev = (c >= BLOCK) | (c >= r + off_prev)
    ok_next = (c < 2 * BLOCK) | (c >= 3 * BLOCK) | (c - 2 * BLOCK <= r - off_next)
    valid = ok_prev & ok_next
    scores = [lax.dot_general(q_ref[0, hq], kcats[hq // grp], _NT, preferred_element_type=jnp.float32)
              for hq in range(kvh * grp)]
    outs = []
    for hq, s in enumerate(scores):
        s = jnp.where(valid, s, -jnp.inf)
        sk = sink_ref[hq]
        m = jnp.maximum(jnp.max(s, axis=1, keepdims=True), sk)
        p = jnp.exp(s - m)
        l = jnp.sum(p, axis=1, keepdims=True) + jnp.exp(sk - m)
        o = jnp.dot(p.astype(jnp.bfloat16), vcats[hq // grp], preferred_element_type=jnp.float32)
        outs.append(o / l)
    o_ref[0] = jnp.concatenate(outs, axis=1).astype(o_ref.dtype)


def windowed_sink_gqa(q, k, v, kx, vx, sink):
    b, h, s, hd = q.shape
    kvh = k.shape[1]
    lc = kx.shape[2]
    grp = h // kvh
    nb = s // BLOCK
    kern = functools.partial(_window_kernel, grp=grp)
    assert WINDOW == BLOCK, "the kernel's mask assumes a one-block window on each side"
    blk = lambda f: pl.BlockSpec((1, kvh, BLOCK, hd), f)
    prev = lambda i, m: (i, 0, jnp.maximum(m - 1, 0), 0)
    cur = lambda i, m: (i, 0, m, 0)
    nxt = lambda i, m: (i, 0, jnp.minimum(m + 1, nb - 1), 0)
    ctxm = lambda i, m: (i, 0, 0, 0)
    return pl.pallas_call(
        kern,
        grid=(b, nb),
        in_specs=[
            pl.BlockSpec(memory_space=pltpu.SMEM),
            pl.BlockSpec((1, h, BLOCK, hd), cur),
            blk(prev), blk(cur), blk(nxt), blk(prev), blk(cur), blk(nxt),
            pl.BlockSpec((1, kvh, lc, hd), ctxm), pl.BlockSpec((1, kvh, lc, hd), ctxm),
        ],
        out_specs=pl.BlockSpec((1, BLOCK, h * hd), lambda i, m: (i, m, 0)),
        out_shape=jax.ShapeDtypeStruct((b, s, h * hd), jnp.bfloat16),
        compiler_params=pltpu.CompilerParams(
            dimension_semantics=("arbitrary",) * 2, vmem_limit_bytes=VMEM_LIMIT_BYTES),
        name="windowed_sink_gqa",
    )(sink.astype(jnp.float32), q, k, k, k, v, v, v, kx, vx)


def _post_kernel(oa_ref, ob_ref, w_ref, x_ref, g_ref, lg_ref, lb_ref, y_ref):
    ka = oa_ref.shape[2]
    out = jnp.dot(oa_ref[0].astype(jnp.bfloat16), w_ref[:ka], preferred_element_type=jnp.float32)
    out += jnp.dot(ob_ref[0].astype(jnp.bfloat16), w_ref[ka:], preferred_element_type=jnp.float32)
    r = ALPHA * x_ref[0] + g_ref[0] * out
    mu = jnp.mean(r, -1, keepdims=True)
    d = r - mu
    var = jnp.mean(d * d, -1, keepdims=True)
    y_ref[0] = d * lax.rsqrt(var + EPS) * lg_ref[...] + lb_ref[...]


def proj_residual_ln(oa, ob, w, x, gate, ln_g, ln_b, tm=512):
    b, s, ka = oa.shape
    kb = ob.shape[2]
    k = ka + kb
    d = w.shape[1]
    tm = min(tm, s)
    wb = w.astype(jnp.bfloat16)
    return pl.pallas_call(
        _post_kernel,
        grid=(b, s // tm),
        in_specs=[
            pl.BlockSpec((1, tm, ka), lambda i, m: (i, m, 0)),
            pl.BlockSpec((1, tm, kb), lambda i, m: (i, m, 0)),
            pl.BlockSpec((k, d), lambda i, m: (0, 0)),
            pl.BlockSpec((1, tm, d), lambda i, m: (i, m, 0)),
            pl.BlockSpec((1, 1, d), lambda i, m: (i, 0, 0)),
            pl.BlockSpec((1, d), lambda i, m: (0, 0)),
            pl.BlockSpec((1, d), lambda i, m: (0, 0)),
        ],
        out_specs=pl.BlockSpec((1, tm, d), lambda i, m: (i, m, 0)),
        out_shape=jax.ShapeDtypeStruct((b, s, d), jnp.float32),
        compiler_params=pltpu.CompilerParams(
            dimension_semantics=("arbitrary", "arbitrary"),
            vmem_limit_bytes=VMEM_LIMIT_BYTES),
        name="proj_residual_ln",
    )(oa, ob, wb, x, gate, ln_g.reshape(1, d), ln_b.reshape(1, d))


def _top16(s, payload=None):
    n = s.shape[0]
    iota = lax.broadcasted_iota(jnp.int32, s.shape, 0).astype(jnp.float32)
    vals, ids = [], []
    for _ in range(PEER_TOPK):
        m = jnp.max(s, axis=0, keepdims=True)
        pos = jnp.min(jnp.where(s == m, iota, float(n)), axis=0, keepdims=True)
        hit = iota == pos
        vals.append(m)
        ids.append(pos if payload is None else jnp.max(jnp.where(hit, payload, -1.0), axis=0, keepdims=True))
        s = jnp.where(hit, -jnp.inf, s)
    return jnp.concatenate(vals, 0), jnp.concatenate(ids, 0)


def _peer_topk_kernel(q_ref, k1_ref, k2_ref, eid_ref, gate_ref, eid_s, gate_s):
    half = PEER_QDIM // 2

    def head(h, carry):
        off = pl.multiple_of(h * PEER_QDIM, PEER_QDIM)
        q1 = q_ref[:, pl.ds(off, half)]
        q2 = q_ref[:, pl.ds(off + half, half)]
        s1 = lax.dot_general(k1_ref[h], q1, _NT, precision=lax.Precision.HIGHEST,
                             preferred_element_type=jnp.float32)
        s2 = lax.dot_general(k2_ref[h], q2, _NT, precision=lax.Precision.HIGHEST,
                             preferred_element_type=jnp.float32)
        v1, i1 = _top16(s1)
        v2, i2 = _top16(s2)
        k8 = PEER_TOPK // 2
        cand = jnp.concatenate([v1[0:1] + v2] + [v1[i:i + 1] + v2[:k8] for i in range(1, k8)]
                               + [v1[k8:] + v2[0:1]], 0)
        cid = jnp.concatenate([i1[0:1] * PEER_NKEYS + i2]
                              + [i1[i:i + 1] * PEER_NKEYS + i2[:k8] for i in range(1, k8)]
                              + [i1[k8:] * PEER_NKEYS + i2[0:1]], 0)
        best, eid = _top16(cand, cid)
        e = jnp.exp(best - best[0:1])
        gate = e / jnp.sum(e, axis=0, keepdims=True)
        row = pl.multiple_of(h * PEER_TOPK, PEER_TOPK)
        eid_s[pl.ds(row, PEER_TOPK), :] = eid.astype(jnp.int32)
        gate_s[pl.ds(row, PEER_TOPK), :] = gate
        return carry

    lax.fori_loop(0, PEER_HEADS, head, 0)
    eid_ref[...] = eid_s[...].T
    gate_ref[...] = gate_s[...].T


def peer_topk(q, k1, k2, tt=1024):
    t = q.shape[0]
    tt = min(tt, t)
    nsel = PEER_HEADS * PEER_TOPK
    return pl.pallas_call(
        _peer_topk_kernel,
        grid=(t // tt,),
        in_specs=[
            pl.BlockSpec((tt, q.shape[1]), lambda i: (i, 0)),
            pl.BlockSpec(k1.shape, lambda i: (0, 0, 0)),
            pl.BlockSpec(k2.shape, lambda i: (0, 0, 0)),
        ],
        out_specs=[pl.BlockSpec((tt, nsel), lambda i: (i, 0)),
                   pl.BlockSpec((tt, nsel), lambda i: (i, 0))],
        out_shape=[jax.ShapeDtypeStruct((t, nsel), jnp.int32),
                   jax.ShapeDtypeStruct((t, nsel), jnp.float32)],
        scratch_shapes=[pltpu.VMEM((nsel, tt), jnp.int32), pltpu.VMEM((nsel, tt), jnp.float32)],
        compiler_params=pltpu.CompilerParams(
            dimension_semantics=("arbitrary",), vmem_limit_bytes=VMEM_LIMIT_BYTES),
        name="peer_topk",
    )(q, k1, k2)


def _peer_w_kernel(e_ref, g_ref, w_ref):
    nk = PEER_NKEYS
    iota = lax.broadcasted_iota(jnp.int32, (nk, e_ref.shape[1]), 0)

    def tok(t, carry):
        e = e_ref[pl.ds(t, 1), :]
        g = g_ref[pl.ds(t, 1), :]
        a_t = jnp.where(iota == (e >> 7), g, 0.0).astype(jnp.bfloat16)
        b_t = jnp.where(iota == (e & (nk - 1)), 1.0, 0.0).astype(jnp.bfloat16)
        w = lax.dot_general(a_t, b_t, _NT, preferred_element_type=jnp.float32)
        w_ref[t] = w.astype(jnp.bfloat16)
        return carry

    lax.fori_loop(0, e_ref.shape[0], tok, 0, unroll=64)


def peer_dense_gates(eid, gate, tt=128):
    t, nsel = eid.shape
    tt = min(tt, t)
    nk = PEER_NKEYS
    w = pl.pallas_call(
        _peer_w_kernel,
        grid=(t // tt,),
        in_specs=[pl.BlockSpec((tt, nsel), lambda i: (i, 0)),
                  pl.BlockSpec((tt, nsel), lambda i: (i, 0))],
        out_specs=pl.BlockSpec((tt, nk, nk), lambda i: (i, 0, 0)),
        out_shape=jax.ShapeDtypeStruct((t, nk, nk), jnp.bfloat16),
        compiler_params=pltpu.CompilerParams(
            dimension_semantics=("arbitrary",), vmem_limit_bytes=VMEM_LIMIT_BYTES),
        name="peer_dense_gates",
    )(eid, gate)
    return w


def _peer_expert_kernel(x_ref, sh_ref, sc_ref, w_ref, u_ref, v_ref, g_ref, lg_ref, lb_ref, y_ref, xm_s, acc_s):
    e = pl.program_id(2)

    @pl.when(e == 0)
    def _():
        xm_s[...] = (x_ref[0] * (1.0 + sc_ref[0]) + sh_ref[0]).astype(jnp.bfloat16)
        acc_s[...] = jnp.zeros_like(acc_s)

    h = lax.dot_general(xm_s[...], u_ref[...], _NT, preferred_element_type=jnp.float32)
    gelu = 0.5 * h * (1.0 + lax.erf(h * (2.0 ** -0.5)))
    w = w_ref[0].reshape(h.shape)
    a = gelu * w.astype(jnp.float32)
    acc_s[...] += jnp.dot(a.astype(jnp.bfloat16), v_ref[...], preferred_element_type=jnp.float32)

    @pl.when(e == pl.num_programs(2) - 1)
    def _():
        r = ALPHA * x_ref[0] + g_ref[0] * acc_s[...]
        mu = jnp.mean(r, -1, keepdims=True)
        d = r - mu
        var = jnp.mean(d * d, -1, keepdims=True)
        y_ref[0] = d * lax.rsqrt(var + EPS) * lg_ref[...] + lb_ref[...]


def peer_experts_ln(x, shift, scale, w, u_tab, v_tab, gate, ln_g, ln_b, tt=512, te=2048):
    b, s, d = x.shape
    tt = min(tt, s)
    ne = u_tab.shape[0]
    nk = PEER_NKEYS
    w3 = w.reshape(b, s, nk, nk)
    return pl.pallas_call(
        _peer_expert_kernel,
        grid=(b, s // tt, ne // te),
        in_specs=[
            pl.BlockSpec((1, tt, d), lambda i, m, e: (i, m, 0)),
            pl.BlockSpec((1, 1, d), lambda i, m, e: (i, 0, 0)),
            pl.BlockSpec((1, 1, d), lambda i, m, e: (i, 0, 0)),
            pl.BlockSpec((1, tt, te // nk, nk), lambda i, m, e: (i, m, e, 0)),
            pl.BlockSpec((te, d), lambda i, m, e: (e, 0)),
            pl.BlockSpec((te, d), lambda i, m, e: (e, 0)),
            pl.BlockSpec((1, 1, d), lambda i, m, e: (i, 0, 0)),
            pl.BlockSpec((1, d), lambda i, m, e: (0, 0)),
            pl.BlockSpec((1, d), lambda i, m, e: (0, 0)),
        ],
        out_specs=pl.BlockSpec((1, tt, d), lambda i, m, e: (i, m, 0)),
        out_shape=jax.ShapeDtypeStruct((b, s, d), jnp.float32),
        scratch_shapes=[pltpu.VMEM((tt, d), jnp.bfloat16), pltpu.VMEM((tt, d), jnp.float32)],
        compiler_params=pltpu.CompilerParams(
            dimension_semantics=("arbitrary", "arbitrary", "arbitrary"),
            vmem_limit_bytes=VMEM_LIMIT_BYTES),
        name="peer_experts_ln",
    )(x, shift, scale, w3, u_tab, v_tab, gate, ln_g.reshape(1, d), ln_b.reshape(1, d))


def _cast_kernel(x_ref, o_ref):
    o_ref[...] = x_ref[0].astype(o_ref.dtype)


def layer_table_bf16(tab, layer, tr=2048):
    _, ne, d = tab.shape
    return pl.pallas_call(
        _cast_kernel,
        grid=(ne // tr,),
        in_specs=[pl.BlockSpec((1, tr, d), lambda r: (layer, r, 0))],
        out_specs=pl.BlockSpec((tr, d), lambda r: (r, 0)),
        out_shape=jax.ShapeDtypeStruct((ne, d), jnp.bfloat16),
        compiler_params=pltpu.CompilerParams(
            dimension_semantics=("arbitrary",), vmem_limit_bytes=VMEM_LIMIT_BYTES),
        name="layer_table_bf16",
    )(tab)


def peer_block(x, shift, scale, gate, wq, k1, k2, u_bf, v_bf, ln_g, ln_b):
    b, s, d = x.shape
    q_all = mod_matmul(x, shift, scale, wq).reshape(b * s, -1)
    eid, gsel = peer_topk(q_all, k1, k2)
    w = peer_dense_gates(eid, gsel)
    return peer_experts_ln(x, shift, scale, w, u_bf, v_bf, gate, ln_g, ln_b)


HEAD_LANES = 128


def _short_conv_kernel(x_ref, xp_ref, xn_ref, w_ref, b_ref, o_ref, *, silu, n_l2, n_scaled):
    cb = pl.program_id(1)
    m = pl.program_id(2)
    x = x_ref[0]
    tq, wb = x.shape
    prev_row = jnp.where(m > 0, xp_ref[0][7:8], 0.0)
    next_row = jnp.where(m < pl.num_programs(2) - 1, xn_ref[0][0:1], 0.0)
    row = lax.broadcasted_iota(jnp.int32, x.shape, 0)
    x_m1 = jnp.where(row == 0, prev_row, pltpu.roll(x, 1, 0))
    x_p1 = jnp.where(row == tq - 1, next_row, pltpu.roll(x, tq - 1, 0))
    y = w_ref[0:1] * x_m1 + w_ref[1:2] * x + w_ref[2:3] * x_p1 + b_ref[...]
    if silu:
        y = y * jax.nn.sigmoid(y)
    if n_l2 == 0:
        o_ref[0] = y
        return
    hpb = wb // HEAD_LANES
    for hh in range(hpb):
        gh = cb * hpb + hh
        seg = y[:, hh * HEAD_LANES:(hh + 1) * HEAD_LANES]
        inv = lax.rsqrt(jnp.sum(seg * seg, axis=-1, keepdims=True) + EPS)
        f = jnp.where(gh < n_l2, inv, 1.0) * jnp.where(gh < n_scaled, C_DK ** -0.5, 1.0)
        o_ref[0, :, hh * HEAD_LANES:(hh + 1) * HEAD_LANES] = seg * f


def short_conv(p, col0, width, w, bias=None, silu=False, n_l2=0, n_scaled=0, wb=768, tq=512):
    b, l, _ = p.shape
    tq = min(tq, l)
    bias2 = (jnp.zeros((width,), jnp.float32) if bias is None else bias).reshape(1, width)
    c0 = col0 // wb
    kern = functools.partial(_short_conv_kernel, silu=silu, n_l2=n_l2, n_scaled=n_scaled)
    r8 = tq // 8
    return pl.pallas_call(
        kern,
        grid=(b, width // wb, l // tq),
        in_specs=[
            pl.BlockSpec((1, tq, wb), lambda i, c, m: (i, m, c0 + c)),
            pl.BlockSpec((1, 8, wb), lambda i, c, m: (i, jnp.maximum(m * r8 - 1, 0), c0 + c)),
            pl.BlockSpec((1, 8, wb), lambda i, c, m: (i, jnp.minimum((m + 1) * r8, l // 8 - 1), c0 + c)),
            pl.BlockSpec((3, wb), lambda i, c, m: (0, c)),
            pl.BlockSpec((1, wb), lambda i, c, m: (0, c)),
        ],
        out_specs=pl.BlockSpec((1, tq, wb), lambda i, c, m: (i, m, c)),
        out_shape=jax.ShapeDtypeStruct((b, l, width), jnp.float32),
        compiler_params=pltpu.CompilerParams(
            dimension_semantics=("arbitrary",) * 3, vmem_limit_bytes=VMEM_LIMIT_BYTES),
        name="short_conv",
    )(p, p, p, w, bias2)


def _dot3(a, b):
    ah = a.astype(jnp.bfloat16)
    bh = b.astype(jnp.bfloat16)
    al = (a - ah.astype(jnp.float32)).astype(jnp.bfloat16)
    bl = (b - bh.astype(jnp.float32)).astype(jnp.bfloat16)
    d = functools.partial(jnp.dot, preferred_element_type=jnp.float32)
    return d(ah, bh) + (d(ah, bl) + d(al, bh))


GDN_PAR = 4


def _gdn_chunk_kernel(qkvc_ref, qkvl_ref, beta_ref, g_ref, u_ref, w_ref, qd_ref, kd_ref, in_ref, gl_ref, qkv_s, *, nc):
    d = pl.program_id(0)
    m = pl.program_id(2)

    @pl.when(m == 0)
    def _():
        qkv_s[...] = qkvc_ref[0]

    @pl.when(m > 0)
    def _():
        qkv_s[...] = qkvl_ref[0]

    cs = GDN_CHUNK
    ii = lax.broadcasted_iota(jnp.int32, (cs, cs), 0)
    jj = lax.broadcasted_iota(jnp.int32, (cs, cs), 1)
    lo = (ii - jj) * (1 - 2 * d)
    incl = lo >= 0
    strict = lo > 0
    tri = jnp.where(incl, 1.0, 0.0).astype(jnp.bfloat16)
    tri3 = jnp.concatenate([tri, tri, tri], axis=1)
    eye = jnp.where(ii == jj, 1.0, 0.0)

    def chunk_pair(cp, carry):
        probs = []
        for c in [GDN_PAR * cp + i for i in range(GDN_PAR)]:
            rows = pl.ds(pl.multiple_of(c * cs, cs), cs)
            g_c = g_ref[0, 0, rows, :]
            b_c = beta_ref[0, 0, rows, :]
            g_hi = g_c.astype(jnp.bfloat16)
            r1 = g_c - g_hi.astype(jnp.float32)
            g_mid = r1.astype(jnp.bfloat16)
            g_lo = (r1 - g_mid.astype(jnp.float32)).astype(jnp.bfloat16)
            gc = jnp.dot(tri3, jnp.concatenate([g_hi, g_mid, g_lo], axis=0),
                         preferred_element_type=jnp.float32)
            tot = jnp.sum(g_c, axis=0, keepdims=True)
            for h in range(C_HEADS):
                probs.append(dict(c=c, h=h, rows=rows, gc=gc[:, h:h + 1], bt=b_c[:, h:h + 1], tot=tot[:, h:h + 1]))
        for pr in probs:
            h, rows = pr["h"], pr["rows"]
            q = qkv_s[rows, h *HEAD_LANES:(h + 1) * HEAD_LANES]
            k = qkv_s[rows, C_W + h * HEAD_LANES:C_W + (h + 1) * HEAD_LANES]
            kb = k * pr["bt"]
            kq = lax.dot_general(jnp.concatenate([kb, q], axis=0).astype(jnp.bfloat16), k.astype(jnp.bfloat16),
                                 _NT, preferred_element_type=jnp.float32)
            gc_row = jnp.broadcast_to(pr["gc"], (cs, HEAD_LANES)).T[:cs, :]
            dm = jnp.where(incl, jnp.exp(pr["gc"] - gc_row), 0.0)
            x = jnp.where(strict, -(kq[:cs] * dm), 0.0)
            in_ref[0, 0, pr["c"], h] = (kq[cs:] * dm).astype(in_ref.dtype)
            pr.update(t=eye + x, pw=x)
        for _ in range(5):
            for pr in probs:
                pr["pw"] = _dot3(pr["pw"], pr["pw"])
            for pr in probs:
                pr["t"] = pr["t"] + _dot3(pr["t"], pr["pw"])
        for pr in probs:
            h, rows = pr["h"], pr["rows"]
            lanes = slice(h * HEAD_LANES, (h + 1) * HEAD_LANES)
            q = qkv_s[rows, h *HEAD_LANES:(h + 1) * HEAD_LANES]
            k = qkv_s[rows, C_W + h * HEAD_LANES:C_W + (h + 1) * HEAD_LANES]
            v = qkv_s[rows, 2 * C_W + h * HEAD_LANES:2 * C_W + (h + 1) * HEAD_LANES]
            eg = jnp.exp(pr["gc"])
            uw = _dot3(pr["t"], jnp.concatenate([v * pr["bt"], k * (pr["bt"] * eg)], axis=1))
            u_ref[0, 0, rows, lanes] = uw[:, :HEAD_LANES]
            w_ref[0, 0, rows, lanes] = uw[:, HEAD_LANES:].astype(w_ref.dtype)
            qd_ref[0, 0, rows, lanes] = (q * eg).astype(qd_ref.dtype)
            kd_ref[0, 0, rows, lanes] = (k * jnp.exp(pr["tot"] - pr["gc"])).astype(kd_ref.dtype)
            gl_ref[0, 0, pr["c"], h:h + 1, :] = jnp.broadcast_to(jnp.exp(pr["tot"]), (1, HEAD_LANES))
        return carry

    lax.fori_loop(0, nc // GDN_PAR, chunk_pair, 0)


def gdn_chunk_prep(qkv_c, qkv_l, beta, g, nc=4):
    b, lc, _ = qkv_c.shape
    l = lc + qkv_l.shape[1]
    cs = GDN_CHUNK
    tq = nc * cs
    assert lc == tq, "the context must fill exactly the first token block"
    nchunks = l // cs
    bf = jnp.bfloat16
    big = lambda dt: jax.ShapeDtypeStruct((2, b, l, C_W), dt)
    bspec = pl.BlockSpec((1, 1, tq, C_W), lambda d, i, m: (d, i, m, 0))
    gspec = pl.BlockSpec((1, 1, tq, C_HEADS), lambda d, i, m: (d, i, m, 0))
    return pl.pallas_call(
        functools.partial(_gdn_chunk_kernel, nc=nc),
        grid=(2, b, l // tq),
        in_specs=[pl.BlockSpec((1, tq, 3 * C_W), lambda d, i, m: (i, 0, 0)),
                  pl.BlockSpec((1, tq, 3 * C_W), lambda d, i, m: (i, jnp.maximum(m - 1, 0), 0)), gspec, gspec],
        out_specs=[bspec, bspec, bspec, bspec,
                   pl.BlockSpec((1, 1, nc, C_HEADS, cs, cs), lambda d, i, m: (d, i, m, 0, 0, 0)),
                   pl.BlockSpec((1, 1, nc, C_HEADS, HEAD_LANES), lambda d, i, m: (d, i, m, 0, 0))],
        out_shape=[big(jnp.float32), big(bf), big(bf), big(bf),
                   jax.ShapeDtypeStruct((2, b, nchunks, C_HEADS, cs, cs), bf),
                   jax.ShapeDtypeStruct((2, b, nchunks, C_HEADS, HEAD_LANES), jnp.float32)],
        scratch_shapes=[pltpu.VMEM((tq, 3 * C_W), jnp.float32)],
        compiler_params=pltpu.CompilerParams(
            dimension_semantics=("arbitrary",) * 3, vmem_limit_bytes=VMEM_LIMIT_BYTES),
        name="gdn_chunk_prep",
    )(qkv_c, qkv_l, beta, g)


def _gdn_scan_kernel(*refs):
    ins, (of_ref, ob_ref, s_ref) = refs[:12], refs[12:]
    step = pl.program_id(1)

    @pl.when(step == 0)
    def _():
        s_ref[...] = jnp.zeros_like(s_ref)

    dot = functools.partial(jnp.dot, preferred_element_type=jnp.float32)
    seqs = [(d, h, slice(h * HEAD_LANES, (h + 1) * HEAD_LANES)) for d in range(2) for h in range(C_HEADS)]
    outs = (of_ref, ob_ref)
    sb, vb = {}, {}
    for d, h, lanes in seqs:
        sb[d, h] = s_ref[d * C_HEADS + h].astype(jnp.bfloat16)
    for d, h, lanes in seqs:
        u_ref, w_ref = ins[6 * d], ins[6 * d + 1]
        vb[d, h] = (u_ref[0, 0, :, lanes] - dot(w_ref[0, 0, :, lanes], sb[d, h])).astype(jnp.bfloat16)
    for d, h, lanes in seqs:
        qd_ref, in_ref = ins[6 * d + 2], ins[6 * d + 4]
        outs[d][0, :, lanes] = dot(qd_ref[0, 0, :, lanes], sb[d, h]) + dot(in_ref[0, 0, 0, h], vb[d, h])
    for d, h, lanes in seqs:
        kd_ref, gl_ref = ins[6 * d + 3], ins[6 * d + 5]
        s_ref[d * C_HEADS + h] = s_ref[d * C_HEADS + h] * gl_ref[0, 0, 0, h:h + 1, :] + lax.dot_general(
            kd_ref[0, 0, :, lanes], vb[d, h], (((0,), (0,)), ((), ())), preferred_element_type=jnp.float32)


def gdn_scan(u, w, qd, kd, intra, gl, n_ctx_chunks):
    _, b, l, _ = u.shape
    cs = GDN_CHUNK
    nchunks = l // cs

    def chunk_of(d, s):
        if d == 0:
            return s
        return jnp.where(s < n_ctx_chunks, n_ctx_chunks - 1 - s, nchunks - 1 + n_ctx_chunks - s)

    in_specs, args = [], []
    for d in range(2):
        big = pl.BlockSpec((1, 1, cs, C_W), lambda i, s, d=d: (d, i, chunk_of(d, s), 0))
        in_specs += [big, big, big, big,
                     pl.BlockSpec((1, 1, 1, C_HEADS, cs, cs), lambda i, s, d=d: (d, i, chunk_of(d, s), 0, 0, 0)),
                     pl.BlockSpec((1, 1, 1, C_HEADS, HEAD_LANES), lambda i, s, d=d: (d, i, chunk_of(d, s), 0, 0))]
        args += [u, w, qd, kd, intra, gl]
    out_specs = [pl.BlockSpec((1, cs, C_W), lambda i, s, d=d: (i, chunk_of(d, s), 0)) for d in range(2)]
    return pl.pallas_call(
        _gdn_scan_kernel,
        grid=(b, nchunks),
        in_specs=in_specs,
        out_specs=out_specs,
        out_shape=[jax.ShapeDtypeStruct((b, l, C_W), jnp.float32)] * 2,
        scratch_shapes=[pltpu.VMEM((2 * C_HEADS, C_DK, C_DV), jnp.float32)],
        compiler_params=pltpu.CompilerParams(
            dimension_semantics=("arbitrary",) * 2, vmem_limit_bytes=VMEM_LIMIT_BYTES),
        name="gdn_scan",
    )(*args)


def _gdn_gate_kernel(of_ref, ob_ref, z_ref, gw_ref, y_ref):
    o = of_ref[0] + ob_ref[0]
    z = z_ref[0]
    for h in range(C_HEADS):
        lanes = slice(h * HEAD_LANES, (h + 1) * HEAD_LANES)
        oh = o[:, lanes]
        zh = z[:, lanes]
        n = oh * lax.rsqrt(jnp.mean(oh * oh, axis=-1, keepdims=True) + EPS) * gw_ref[...]
        y_ref[0, :, lanes] = (n * (zh * jax.nn.sigmoid(zh))).astype(y_ref.dtype)


def gdn_gate(o_f, o_b, row0, p, gnorm_w, tq=256):
    b, l, _ = p.shape
    tq = min(tq, l)
    r0 = row0 // tq
    ospec = pl.BlockSpec((1, tq, C_W), lambda i, m: (i, r0 + m, 0))
    return pl.pallas_call(
        _gdn_gate_kernel,
        grid=(b, l // tq),
        in_specs=[ospec, ospec,
                  pl.BlockSpec((1, tq, C_W), lambda i, m: (i, m, 3)),
                  pl.BlockSpec((1, HEAD_LANES), lambda i, m: (0, 0))],
        out_specs=pl.BlockSpec((1, tq, C_W), lambda i, m: (i, m, 0)),
        out_shape=jax.ShapeDtypeStruct((b, l, C_W), jnp.bfloat16),
        compiler_params=pltpu.CompilerParams(
            dimension_semantics=("arbitrary",) * 2, vmem_limit_bytes=VMEM_LIMIT_BYTES),
        name="gdn_gate",
    )(o_f, o_b, p, gnorm_w.reshape(1, HEAD_LANES))


def gdn_mixer(p, pc, gates_l, gates_c, conv_w, a_log, dt_bias, gnorm_w, with_ctx):
    lc = pc.shape[1]
    conv = functools.partial(short_conv, col0=0, width=3 * C_W, w=conv_w, silu=True,
                             n_l2=2 * C_HEADS, n_scaled=C_HEADS)
    gates = jnp.concatenate([gates_c, gates_l], axis=1)
    gates = gates.reshape(gates.shape[0], gates.shape[1], 4, C_HEADS)
    beta = jax.nn.sigmoid(gates[:, :, :2])
    g = -jnp.exp(a_log) * jax.nn.softplus(gates[:, :, 2:] + dt_bias)
    beta = jnp.moveaxis(beta, 2, 0)
    g = jnp.moveaxis(g, 2, 0)
    u, w, qd, kd, intra, gl = gdn_chunk_prep(conv(pc), conv(p), beta, g)
    o_f, o_b = gdn_scan(u, w, qd, kd, intra, gl, lc // GDN_CHUNK)
    out = gdn_gate(o_f, o_b, lc, p, gnorm_w)
    out_c = gdn_gate(o_f, o_b, 0, pc, gnorm_w) if with_ctx else None
    return out, out_c


FFT_R = 128
FFT_N = FFT_R * FFT_R
SUB = 8
K1_PER_STEP = 2


def stage_a_table():
    idx = np.arange(FFT_R)
    ang = 2.0 * np.pi * np.outer(idx, idx) / FFT_R
    return jnp.asarray(np.stack([np.cos(ang), -np.sin(ang)], axis=1).reshape(2 * FFT_R, FFT_R), jnp.float32)


def _fft_stage_a_kernel(x_ref, l_ref, y_ref):
    l = l_ref[...]
    c = x_ref.shape[-1]
    xs = jnp.swapaxes(x_ref[0], 0, 1)
    ys = jnp.stack([_dot3(l, xs[j]) for j in range(SUB)], axis=0)
    y_ref[0] = jnp.swapaxes(ys, 0, 1).reshape(FFT_R, 2, SUB, c)


def fft_stage_a(x, col_blk, width, stage_a):
    b, l, wtot = x.shape
    n1cnt = l // FFT_R
    x4 = x.reshape(b, n1cnt, FFT_R, wtot)
    return pl.pallas_call(
        _fft_stage_a_kernel,
        grid=(b, FFT_R // SUB),
        in_specs=[pl.BlockSpec((1, n1cnt, SUB, width), lambda i, j: (i, 0, j, col_blk)),
                  pl.BlockSpec((2 * FFT_R, n1cnt), lambda i, j: (0, 0))],
        out_specs=pl.BlockSpec((1, FFT_R, 2, SUB, width), lambda i, j: (i, 0, 0, j, 0)),
        out_shape=jax.ShapeDtypeStruct((b, FFT_R, 2, FFT_R, width), jnp.float32),
        compiler_params=pltpu.CompilerParams(
            dimension_semantics=("arbitrary",) * 2, vmem_limit_bytes=VMEM_LIMIT_BYTES),
        name="fft_stage_a",
    )(x4, stage_a[:, :n1cnt])


def _dot3_presplit(ah, al, b, transpose_lhs=False):
    bh = b.astype(jnp.bfloat16)
    bl = (b - bh.astype(jnp.float32)).astype(jnp.bfloat16)
    dims = (((0,), (0,)), ((), ())) if transpose_lhs else (((1,), (0,)), ((), ()))
    d = functools.partial(lax.dot_general, dimension_numbers=dims, preferred_element_type=jnp.float32)
    return d(ah, bh) + (d(ah, bl) + d(al, bh))


def stage_b_tables():
    r = FFT_R
    k1 = jnp.arange(r, dtype=jnp.int32)[:, None, None]
    k2 = jnp.arange(r, dtype=jnp.int32)[None, :, None]
    n2 = jnp.arange(r, dtype=jnp.int32)[None, None, :]
    th = ((n2 * (r * k2 + k1)) % FFT_N).astype(jnp.float32) * (2.0 * math.pi / FFT_N)
    c, s = jnp.cos(th), jnp.sin(th)
    t = jnp.concatenate([jnp.concatenate([c, s], 2), jnp.concatenate([-s, c], 2)], 1)

    def split(m):
        hi = m.astype(jnp.bfloat16)
        return hi, (m - hi.astype(jnp.float32)).astype(jnp.bfloat16)

    return split(t)


def _fft_mid_kernel(y_ref, h_ref, th_ref, tl_ref, o_ref, *, conv):
    nb, nk, r, c = y_ref.shape[0], y_ref.shape[1], FFT_R, y_ref.shape[-1]
    probs = [(i, k) for k in range(nk) for i in range(nb)]
    xs = [_dot3_presplit(th_ref[k], tl_ref[k], y_ref[i, k].reshape(2 * r, c)) for i, k in probs]
    if not conv:
        for (i, k), x in zip(probs, xs):
            o_ref[i, k] = (x * ((1.0 / FFT_N) / h_ref[i])).reshape(2, r, c)
        return
    ps = [jnp.concatenate([x[:r] * h_ref[0, k, 0] - x[r:] * h_ref[0, k, 1],
                           x[:r] * h_ref[0, k, 1] + x[r:] * h_ref[0, k, 0]], axis=0) for (i, k), x in zip(probs, xs)]
    for (i, k), p in zip(probs, ps):
        o_ref[i, k] = _dot3_presplit(th_ref[k], tl_ref[k], p, transpose_lhs=True).reshape(2, r, c)


def fft_mid(y, h, tables, conv, order=0):
    b, r, _, _, c = y.shape
    hh = h if conv else h.reshape(b, 1, c)
    hspec = (pl.BlockSpec((1, K1_PER_STEP, 2, r, c), lambda k: (order, k, 0, 0, 0)) if conv
             else pl.BlockSpec((b, 1, c), lambda k: (0, 0, 0)))
    blk = pl.BlockSpec((b, K1_PER_STEP, 2, r, c), lambda k: (0, k, 0, 0, 0))
    tspec = pl.BlockSpec((K1_PER_STEP, 2 * r, 2 * r), lambda k: (k, 0, 0))
    return pl.pallas_call(
        functools.partial(_fft_mid_kernel, conv=conv),
        grid=(r // K1_PER_STEP,),
        in_specs=[blk, hspec, tspec, tspec],
        out_specs=blk,
        out_shape=jax.ShapeDtypeStruct(y.shape, jnp.float32),
        compiler_params=pltpu.CompilerParams(
            dimension_semantics=("arbitrary",), vmem_limit_bytes=VMEM_LIMIT_BYTES),
        name="fft_mid",
    )(y, hh, *tables)


def _fft_out_kernel(b_ref, l_ref, xg_ref, xin_ref, bias_ref, o_ref):
    l = l_ref[...]
    c = o_ref.shape[-1]
    bs = jnp.swapaxes(b_ref[0].reshape(2 * FFT_R, SUB, c), 0, 1)
    ys = jnp.stack([_dot3(l, bs[j]) for j in range(SUB)], axis=0)
    y = jnp.swapaxes(ys, 0, 1)
    o_ref[0] = xg_ref[0] * (y + bias_ref[...] * xin_ref[0])


def fft_out_gate(bm, stage_a, xg, xg_blk, xin, xin_blk, bias):
    b, r, _, _, c = bm.shape
    l = xg.shape[1]
    n1cnt = l // r
    view = lambda t: t.reshape(b, n1cnt, r, t.shape[-1])
    lhs = stage_a.T[:n1cnt]
    return pl.pallas_call(
        _fft_out_kernel,
        grid=(b, r // SUB),
        in_specs=[pl.BlockSpec((1, r, 2, SUB, c), lambda i, j: (i, 0, 0, j, 0)),
                  pl.BlockSpec((n1cnt, 2 * r), lambda i, j: (0, 0)),
                  pl.BlockSpec((1, n1cnt, SUB, c), lambda i, j: (i, 0, j, xg_blk)),
                  pl.BlockSpec((1, n1cnt, SUB, c), lambda i, j: (i, 0, j, xin_blk)),
                  pl.BlockSpec((1, c), lambda i, j: (0, 0))],
        out_specs=pl.BlockSpec((1, n1cnt, SUB, c), lambda i, j: (i, 0, j, 0)),
        out_shape=jax.ShapeDtypeStruct((b, n1cnt, r, c), jnp.float32),
        compiler_params=pltpu.CompilerParams(
            dimension_semantics=("arbitrary",) * 2, vmem_limit_bytes=VMEM_LIMIT_BYTES),
        name="fft_out_gate",
    )(bm, lhs, view(xg), view(xin), bias.reshape(1, c)).reshape(b, l, c)


def _direct_conv_kernel(xin_ref, xg_ref, kern_ref, d1_ref, d2_ref, norm_ref, bias_ref, o_ref):
    n = xin_ref.shape[1]
    d1 = d1_ref[...]
    x = xin_ref[0]
    xs = _dot3(d1[:, :n], x)
    hs = _dot3(d1, kern_ref[...]) / norm_ref[...]
    xr, xi, hr, hi = xs[:2 * n], xs[2 * n:], hs[:2 * n], hs[2 * n:]
    p = jnp.concatenate([xr * hr - xi * hi, xr * hi + xi * hr], axis=0)
    y = _dot3(d2_ref[...], p)
    o_ref[0] = xg_ref[0] * (y + bias_ref[...] * x)


def direct_long_conv(xin, xin_blk, xg, xg_blk, kern, norm, bias):
    b, n, _ = xin.shape
    c = kern.shape[1]
    idx = np.arange(2 * n)
    ang = 2.0 * np.pi * np.outer(idx, idx) / (2 * n)
    d1 = jnp.asarray(np.concatenate([np.cos(ang), -np.sin(ang)], axis=0), jnp.float32)
    d2 = jnp.asarray(np.concatenate([np.cos(ang[:n]), -np.sin(ang[:n])], axis=1) / (2 * n), jnp.float32)
    return pl.pallas_call(
        _direct_conv_kernel,
        grid=(b,),
        in_specs=[pl.BlockSpec((1, n, c), lambda i: (i, 0, xin_blk)),
                  pl.BlockSpec((1, n, c), lambda i: (i, 0, xg_blk)),
                  pl.BlockSpec((2 * n, c), lambda i: (0, 0)),
                  pl.BlockSpec((4 * n, 2 * n), lambda i: (0, 0)),
                  pl.BlockSpec((n, 4 * n), lambda i: (0, 0)),
                  pl.BlockSpec((1, c), lambda i: (0, 0)),
                  pl.BlockSpec((1, c), lambda i: (0, 0))],
        out_specs=pl.BlockSpec((1, n, c), lambda i: (i, 0, 0)),
        out_shape=jax.ShapeDtypeStruct((b, n, c), jnp.float32),
        compiler_params=pltpu.CompilerParams(
            dimension_semantics=("arbitrary",), vmem_limit_bytes=VMEM_LIMIT_BYTES),
        name="direct_long_conv",
    )(xin, xg, kern, d1, d2, norm.reshape(1, c), bias.reshape(1, c))


def _hy_filter_kernel(w1_ref, b1_ref, fr_ref, w2_ref, b2_ref, w3_ref, dl_ref, k_ref, s_ref, *, n):
    i = pl.program_id(0)
    tp, c = k_ref.shape[1], k_ref.shape[2]

    @pl.when(i == 0)
    def _():
        s_ref[...] = jnp.zeros_like(s_ref)

    def pos(shape):
        idx = i * tp + lax.broadcasted_iota(jnp.int32, shape, 0)
        t = jnp.where(idx < n, idx, jnp.where(idx == n, 0, 2 * n - idx))
        return idx, t.astype(jnp.float32)

    _, t = pos((tp, LANES))
    lane = lax.broadcasted_iota(jnp.int32, (tp, LANES), 1)
    band = jnp.where(lane <= HY_BANDS, lane, lane - HY_BANDS).astype(jnp.float32)
    ang = 2.0 * math.pi * t * band / n
    feat = jnp.where(lane == 0, t / n,
                     jnp.where(lane <= HY_BANDS, jnp.sin(ang), jnp.where(lane < HY_EMB, jnp.cos(ang), 0.0)))
    hid = jnp.sin(fr_ref[...] * (_dot3(feat, w1_ref[...]) + b1_ref[...]))
    hid = jnp.sin(fr_ref[...] * (_dot3(hid, w2_ref[...]) + b2_ref[...]))
    f = _dot3(hid, w3_ref[...])
    idx, t = pos((tp, c))
    decay = jnp.exp(-(t / n) * dl_ref[...])
    for o in range(HY_ORDER):
        fwd = f[:, (2 * o) * c:(2 * o + 1) * c]
        bwd = f[:, (2 * o + 1) * c:(2 * o + 2) * c]
        val = jnp.where(idx < n, fwd, bwd) * decay
        s_ref[o:o + 1, :] += jnp.sum(jnp.abs(val), axis=0, keepdims=True)
        k_ref[o] = jnp.where(idx == n, 0.0, val)


def hyena_kernels(n, w1, b1, freq, w2, b2, w3):
    c = HY_CH
    tp = min(512, n)
    max_decay = math.log(HY_TARGET) / HY_FAST_DECAY
    min_decay = math.log(HY_TARGET) / HY_SLOW_DECAY
    deltas = jnp.abs(jnp.linspace(min_decay, max_decay, c, dtype=jnp.float32)).reshape(1, c)
    w1p = jnp.zeros((LANES, w1.shape[1]), jnp.float32).at[:w1.shape[0]].set(w1)
    hd = w1.shape[1]
    full = lambda shape: pl.BlockSpec(shape, lambda i: (0,) * len(shape))
    return pl.pallas_call(
        functools.partial(_hy_filter_kernel, n=n),
        grid=(2 * n // tp,),
        in_specs=[full((LANES, hd)), full((1, hd)), full((1, hd)), full((hd, hd)), full((1, hd)),
                  full((hd, HY_ORDER * 2 * c)), full((1, c))],
        out_specs=[pl.BlockSpec((HY_ORDER, tp, c), lambda i: (0, i, 0)), full((HY_ORDER, c))],
        out_shape=[jax.ShapeDtypeStruct((HY_ORDER, 2 * n, c), jnp.float32),
                   jax.ShapeDtypeStruct((HY_ORDER, c), jnp.float32)],
        compiler_params=pltpu.CompilerParams(
            dimension_semantics=("arbitrary",), vmem_limit_bytes=VMEM_LIMIT_BYTES),
        name="hyena_kernels",
    )(w1p, b1.reshape(1, hd), freq.reshape(1, hd), w2, b2.reshape(1, hd), w3, deltas)


def hyena_mixer(p, col0, conv_w, conv_b, filt_args, hy_bias):
    n = p.shape[1]
    uc = short_conv(p, col0, 3 * HY_CH, conv_w, conv_b)
    kerns, norm = hyena_kernels(n, *filt_args)
    if 2 * n != FFT_N:
        v = direct_long_conv(uc, 2, uc, 0, kerns[0], norm[0], hy_bias[0])
        return direct_long_conv(v, 0, uc, 1, kerns[1], norm[1], hy_bias[1])
    stage_a, stage_b = stage_a_table(), stage_b_tables()
    spec = fft_mid(fft_stage_a(kerns, 0, HY_CH, stage_a), norm, stage_b, conv=False)
    v = fft_out_gate(fft_mid(fft_stage_a(uc, 2, HY_CH, stage_a), spec, stage_b, conv=True, order=0),
                     stage_a, uc, 0, uc, 2, hy_bias[0])
    return fft_out_gate(fft_mid(fft_stage_a(v, 0, HY_CH, stage_a), spec, stage_b, conv=True, order=1),
                        stage_a, uc, 1, v, 0, hy_bias[1])


def even_mixer(p, pc, rope_tabs, sink, conv_w, conv_b, fw1, fb1, ffreq, fw2, fb2, fw3, hy_bias, with_ctx):
    q, k, v = qkv_prep(p, 0, A_HEADS, A_KV_HEADS, rope_tabs)
    qc, kc, vc = qkv_prep(pc, 0, A_HEADS, A_KV_HEADS, None)
    o_a = windowed_sink_gqa(q, k, v, kc, vc, sink)
    filt_args = (fw1, fb1, ffreq, fw2, fb2, fw3)
    o_b = hyena_mixer(p, A_Q + 2 * A_KV, conv_w, conv_b, filt_args, hy_bias)
    out_c = None
    if with_ctx:
        o_ac = flash_gqa(qc, kc, vc, sink)
        o_bc = hyena_mixer(pc, A_Q + 2 * A_KV, conv_w, conv_b, filt_args, hy_bias)
        out_c = (o_ac, o_bc)
    return (o_a, o_b), out_c


def odd_mixer(p, pc, gates_l, gates_c, rope_tabs, conv_w, a_log, dt_bias, gnorm_w, qnorm_w, knorm_w, with_ctx):
    o_l, o_c = gdn_mixer(p, pc, gates_l, gates_c, conv_w, a_log, dt_bias, gnorm_w, with_ctx)
    qd, kd, vd = qkv_prep(p, 4 * C_W, D_HEADS, D_KV_HEADS, rope_tabs, qnorm_w, knorm_w)
    qdc, kdc, vdc = qkv_prep(pc, 4 * C_W, D_HEADS, D_KV_HEADS, None, qnorm_w, knorm_w)
    o_d = flash_gqa(qd, jnp.concatenate([kd, kdc], 2), jnp.concatenate([vd, vdc], 2))
    out_c = None
    if with_ctx:
        out_c = (o_c, flash_gqa(qdc, kdc, vdc))
    return (o_l, o_d), out_c


def kernel(x, c, ctx, c_ctx, ada_w, ada_b, ln1_g, ln1_b, ln2_g, ln2_b, peer_wq, peer_k1, peer_k2, peer_u, peer_v, ev_w_in, ev_w_out, ev_sink, ev_conv_w, ev_conv_b, ev_filt_w1, ev_filt_b1, ev_filt_freq, ev_filt_w2, ev_filt_b2, ev_filt_w3, ev_hy_bias, od_w_in, od_w_out, od_conv_w, od_a_log, od_dt_bias, od_gnorm_w, od_qnorm_w, od_knorm_w):
    rope_tabs = rope_tables(x.shape[1])
    bsz = x.shape[0]
    silu_c = jax.nn.silu(c)
    silu_cc = jax.nn.silu(c_ctx)
    for i in range(DEPTH):
        with_ctx = i < DEPTH - 1
        j = i // 2
        mod = (silu_c @ ada_w[i] + ada_b[i])[:, None, :]
        modc = jnp.broadcast_to((silu_cc @ ada_w[i] + ada_b[i])[None, None, :], (bsz, 1, 6 * D_MODEL))
        sh1, sc1, g1, sh2, sc2, g2 = jnp.split(mod, 6, axis=-1)
        sh1c, sc1c, g1c, sh2c, sc2c, g2c = jnp.split(modc, 6, axis=-1)
        if i % 2 == 0:
            p = mod_matmul(x, sh1, sc1, ev_w_in[j])
            pc = mod_matmul(ctx, sh1c, sc1c, ev_w_in[j])
            out, out_c = even_mixer(p, pc, rope_tabs, ev_sink[j], ev_conv_w[j], ev_conv_b[j],
                                    ev_filt_w1[j], ev_filt_b1[j], ev_filt_freq[j], ev_filt_w2[j], ev_filt_b2[j],
                                    ev_filt_w3[j], ev_hy_bias[j], with_ctx)
            w_out = ev_w_out[j]
        else:
            w_in = od_w_in[j]
            w_gate = jnp.pad(w_in[:, 4 * C_W:4 * C_W + C_GATES], ((0, 0), (0, LANES - C_GATES)))
            w_in = jnp.concatenate([w_in[:, :4 * C_W], w_in[:, 4 * C_W + C_GATES:]], axis=1)
            p = mod_matmul(x, sh1, sc1, w_in)
            pc = mod_matmul(ctx, sh1c, sc1c, w_in)
            gates_l = mod_matmul(x, sh1, sc1, w_gate)[..., :C_GATES]
            gates_c = mod_matmul(ctx, sh1c, sc1c, w_gate)[..., :C_GATES]
            out, out_c = odd_mixer(p, pc, gates_l, gates_c, rope_tabs, od_conv_w[j], od_a_log[j], od_dt_bias[j],
                                   od_gnorm_w[j], od_qnorm_w[j], od_knorm_w[j], with_ctx)
            w_out = od_w_out[j]
        u_bf = layer_table_bf16(peer_u, i)
        v_bf = layer_table_bf16(peer_v, i)
        x = proj_residual_ln(out[0], out[1], w_out, x, g1, ln1_g[i], ln1_b[i])
        x = peer_block(x, sh2, sc2, g2, peer_wq[i], peer_k1[i], peer_k2[i], u_bf, v_bf, ln2_g[i], ln2_b[i])
        if with_ctx:
            ctx = proj_residual_ln(out_c[0], out_c[1], w_out, ctx, g1c, ln1_g[i], ln1_b[i])
            ctx = peer_block(ctx, sh2c, sc2c, g2c, peer_wq[i], peer_k1[i], peer_k2[i], u_bf, v_bf,
                             ln2_g[i], ln2_b[i])
    return x
```

```python
import functools
import math

import numpy as np

import jax
import jax.numpy as jnp
from jax import lax
from jax.experimental import pallas as pl
from jax.experimental.pallas import tpu as pltpu

D_MODEL = 1024
DEPTH = 2
GRID_W = 64
HEAD_DIM = 64
BLOCK = 128
ROPE_BASE = 10000.0
EPS = 1e-6

A_HEADS = 8
A_KV_HEADS = 2
WINDOW = 128

HY_CH = 512
HY_ORDER = 2
HY_EMB = 33
HY_BANDS = (HY_EMB - 1) // 2
HY_FAST_DECAY = 0.3
HY_SLOW_DECAY = 1.5
HY_TARGET = 1e-2

C_HEADS = 4
C_DK = 128
C_DV = 128
GDN_CHUNK = 64

D_HEADS = 8
D_KV_HEADS = 2

PEER_HEADS = 8
PEER_NKEYS = 128
PEER_QDIM = 256
PEER_TOPK = 16

ALPHA = (2 * DEPTH) ** 0.25

A_Q = A_HEADS * HEAD_DIM
A_KV = A_KV_HEADS * HEAD_DIM
C_W = C_HEADS * C_DK
C_GATES = 4 * C_HEADS

VMEM_LIMIT_BYTES = 48 * 1024 * 1024

LANES = 128
_NT = (((1,), (1,)), ((), ()))


def _modmm_kernel(x_ref, sh_ref, sc_ref, w_ref, o_ref):
    h = x_ref[0] * (1.0 + sc_ref[0]) + sh_ref[0]
    o_ref[0] = jnp.dot(h.astype(jnp.bfloat16), w_ref[...], preferred_element_type=jnp.float32)


def mod_matmul(x, shift, scale, w, tm=512, tn=None):
    b, s, k = x.shape
    n = w.shape[1]
    tm = min(tm, s)
    tn = n if tn is None else tn
    wb = w.astype(jnp.bfloat16)
    return pl.pallas_call(
        _modmm_kernel,
        grid=(b, n // tn, s // tm),
        in_specs=[
            pl.BlockSpec((1, tm, k), lambda i, j, m: (i, m, 0)),
            pl.BlockSpec((1, 1, k), lambda i, j, m: (i, 0, 0)),
            pl.BlockSpec((1, 1, k), lambda i, j, m: (i, 0, 0)),
            pl.BlockSpec((k, tn), lambda i, j, m: (0, j)),
        ],
        out_specs=pl.BlockSpec((1, tm, tn), lambda i, j, m: (i, m, j)),
        out_shape=jax.ShapeDtypeStruct((b, s, n), jnp.float32),
        compiler_params=pltpu.CompilerParams(
            dimension_semantics=("arbitrary", "arbitrary", "arbitrary"),
            vmem_limit_bytes=VMEM_LIMIT_BYTES),
        name="mod_matmul",
    )(x, shift, scale, wb)


def _qkv_prep_kernel(q_ref, k_ref, v_ref, cs_ref, sn_ref, qw_ref, kw_ref, gm_ref, qo_ref, ko_ref, vo_ref, *,
                     norm, rope, nq, nkv):
    def prep(x, w, nh):
        if norm:
            ms = jnp.dot(x * x, gm_ref[:x.shape[1], :x.shape[1]], precision=lax.Precision.HIGHEST,
                         preferred_element_type=jnp.float32)
            x = x * lax.rsqrt(ms + EPS) * w
        if rope:
            n = x.shape[1]
            reps = n // cs_ref.shape[1]
            cs = jnp.concatenate([cs_ref[...]] * reps, axis=1) if reps > 1 else cs_ref[...]
            sn = jnp.concatenate([sn_ref[...]] * reps, axis=1) if reps > 1 else sn_ref[...]
            lane = lax.broadcasted_iota(jnp.int32, x.shape, 1)
            nf = HEAD_DIM // 4
            partner = jnp.where((lane & nf) == 0, pltpu.roll(x, n - nf, 1), pltpu.roll(x, nf, 1))
            x = x * cs + partner * sn
        return x

    q = prep(q_ref[0], qw_ref[...], nq) * (HEAD_DIM ** -0.5)
    k = prep(k_ref[0], kw_ref[...], nkv)
    v = v_ref[0]
    for h in range(nq):
        qo_ref[0, h] = q[:, h * HEAD_DIM:(h + 1) * HEAD_DIM].astype(jnp.bfloat16)
    for h in range(nkv):
        ko_ref[0, h] = k[:, h * HEAD_DIM:(h + 1) * HEAD_DIM].astype(jnp.bfloat16)
        vo_ref[0, h] = v[:, h * HEAD_DIM:(h + 1) * HEAD_DIM].astype(jnp.bfloat16)


def qkv_prep(p, col0, nq, nkv, rope_tabs, qw=None, kw=None, tq=512):
    b, s, _ = p.shape
    tq = min(tq, s)
    wq_, wk_ = nq * HEAD_DIM, nkv * HEAD_DIM
    norm = qw is not None
    rope = rope_tabs is not None
    if rope:
        cs, sn = rope_tabs
    else:
        cs = sn = jnp.zeros((s, 2 * HEAD_DIM), jnp.float32)
    qw_t = jnp.tile(qw, nq).reshape(1, wq_) if norm else jnp.ones((1, wq_), jnp.float32)
    kw_t = jnp.tile(kw, nkv).reshape(1, wk_) if norm else jnp.ones((1, wk_), jnp.float32)
    grp = jnp.arange(wq_) // HEAD_DIM
    gm = (grp[:, None] == grp[None, :]).astype(jnp.float32) / HEAD_DIM
    kern = functools.partial(_qkv_prep_kernel, norm=norm, rope=rope, nq=nq, nkv=nkv)
    return pl.pallas_call(
        kern,
        grid=(b, s // tq),
        in_specs=[
            pl.BlockSpec((1, tq, wq_), lambda i, m: (i, m, col0 // wq_)),
            pl.BlockSpec((1, tq, wk_), lambda i, m: (i, m, (col0 + wq_) // wk_)),
            pl.BlockSpec((1, tq, wk_), lambda i, m: (i, m, (col0 + wq_) // wk_ + 1)),
            pl.BlockSpec((tq, 2 * HEAD_DIM), lambda i, m: (m, 0)),
            pl.BlockSpec((tq, 2 * HEAD_DIM), lambda i, m: (m, 0)),
            pl.BlockSpec((1, wq_), lambda i, m: (0, 0)),
            pl.BlockSpec((1, wk_), lambda i, m: (0, 0)),
            pl.BlockSpec((wq_, wq_), lambda i, m: (0, 0)),
        ],
        out_specs=[
            pl.BlockSpec((1, nq, tq, HEAD_DIM), lambda i, m: (i, 0, m, 0)),
            pl.BlockSpec((1, nkv, tq, HEAD_DIM), lambda i, m: (i, 0, m, 0)),
            pl.BlockSpec((1, nkv, tq, HEAD_DIM), lambda i, m: (i, 0, m, 0)),
        ],
        out_shape=[
            jax.ShapeDtypeStruct((b, nq, s, HEAD_DIM), jnp.bfloat16),
            jax.ShapeDtypeStruct((b, nkv, s, HEAD_DIM), jnp.bfloat16),
            jax.ShapeDtypeStruct((b, nkv, s, HEAD_DIM), jnp.bfloat16),
        ],
        compiler_params=pltpu.CompilerParams(
            dimension_semantics=("arbitrary", "arbitrary"), vmem_limit_bytes=VMEM_LIMIT_BYTES),
        name="qkv_prep",
    )(p, p, p, cs, sn, qw_t, kw_t, gm)


def rope_tables(n_tok):
    rows = n_tok // GRID_W
    row = jnp.repeat(jnp.arange(rows, dtype=jnp.float32), GRID_W)
    col = jnp.tile(jnp.arange(GRID_W, dtype=jnp.float32), rows)
    nf = HEAD_DIM // 4
    inv = ROPE_BASE ** (-jnp.arange(nf, dtype=jnp.float32) / nf)
    ar, ac = row[:, None] * inv, col[:, None] * inv
    cs = jnp.concatenate([jnp.cos(ar), jnp.cos(ar), jnp.cos(ac), jnp.cos(ac)], -1)
    sn = jnp.concatenate([-jnp.sin(ar), jnp.sin(ar), -jnp.sin(ac), jnp.sin(ac)], -1)
    return jnp.tile(cs, (1, 2)), jnp.tile(sn, (1, 2))


def _flash_kernel(sink_ref, q_ref, k_ref, v_ref, o_ref, m_s, l_s, acc_s, *, use_sink, grp):
    j = pl.program_id(3)
    tq = q_ref.shape[2]

    @pl.when(j == 0)
    def _():
        m_s[...] = jnp.full_like(m_s, -jnp.inf)
        l_s[...] = jnp.zeros_like(l_s)
        acc_s[...] = jnp.zeros_like(acc_s)

    tk = k_ref.shape[2]
    nt = tk // LANES
    kt = k_ref[0, 0]
    vt = v_ref[0, 0]
    scores = [lax.dot_general(q_ref[0, g], kt, _NT, preferred_element_type=jnp.float32) for g in range(grp)]
    for g, s in enumerate(scores):
        rows = slice(g * tq, (g + 1) * tq)
        tiles = [s[:, c * LANES:(c + 1) * LANES] for c in range(nt)]
        m_tile = functools.reduce(jnp.maximum, tiles)
        m_old = m_s[rows]
        m_new = jnp.maximum(m_old, jnp.broadcast_to(jnp.max(m_tile, axis=1, keepdims=True), m_old.shape))
        alpha = jnp.exp(m_old - m_new)
        p_tiles = [jnp.exp(t - m_new) for t in tiles]
        l_s[rows] = alpha * l_s[rows] + functools.reduce(jnp.add, p_tiles)
        p = jnp.concatenate([t.astype(jnp.bfloat16) for t in p_tiles], axis=1)
        acc_s[rows] = alpha[:, :HEAD_DIM] * acc_s[rows] + jnp.dot(p, vt, preferred_element_type=jnp.float32)
        m_s[rows] = m_new

    @pl.when(j == pl.num_programs(3) - 1)
    def _():
        kvh = pl.program_id(1)
        outs = []
        for g in range(grp):
            rows = slice(g * tq, (g + 1) * tq)
            m = m_s[rows][:, :1]
            l = jnp.sum(l_s[rows], axis=1, keepdims=True)
            acc = acc_s[rows]
            if use_sink:
                sk = sink_ref[kvh * grp + g]
                m2 = jnp.maximum(m, sk)
                a = jnp.exp(m - m2)
                l = a * l + jnp.exp(sk - m2)
                acc = a * acc
            outs.append(acc / l)
        o_ref[0] = jnp.concatenate(outs, axis=1).astype(o_ref.dtype)


def flash_gqa(q, k, v, sink=None, tq=256, tk=2816):
    b, h, s, hd = q.shape
    kvh, lk = k.shape[1], k.shape[2]
    grp = h // kvh
    tq = min(tq, s)
    tk = max(t for t in range(LANES, min(tk, lk) + 1, LANES) if lk % t == 0)
    use_sink = sink is not None
    sink_arr = sink.astype(jnp.float32) if use_sink else jnp.zeros((h,), jnp.float32)
    kern = functools.partial(_flash_kernel, use_sink=use_sink, grp=grp)
    return pl.pallas_call(
        kern,
        grid=(b, kvh, s // tq, lk // tk),
        in_specs=[
            pl.BlockSpec(memory_space=pltpu.SMEM),
            pl.BlockSpec((1, grp, tq, hd), lambda i, c, m, j: (i, c, m, 0)),
            pl.BlockSpec((1, 1, tk, hd), lambda i, c, m, j: (i, c, j, 0)),
            pl.BlockSpec((1, 1, tk, hd), lambda i, c, m, j: (i, c, j, 0)),
        ],
        out_specs=pl.BlockSpec((1, tq, grp * hd), lambda i, c, m, j: (i, m, c)),
        out_shape=jax.ShapeDtypeStruct((b, s, h * hd), jnp.bfloat16),
        scratch_shapes=[pltpu.VMEM((grp * tq, LANES), jnp.float32), pltpu.VMEM((grp * tq, LANES), jnp.float32),
                        pltpu.VMEM((grp * tq, hd), jnp.float32)],
        compiler_params=pltpu.CompilerParams(
            dimension_semantics=("arbitrary",) * 4, vmem_limit_bytes=VMEM_LIMIT_BYTES),
        name="flash_gqa",
    )(sink_arr, q, k, v)


def _window_kernel(sink_ref, q_ref, kp_ref, kc_ref, kn_ref, vp_ref, vc_ref, vn_ref, kx_ref, vx_ref, o_ref, *, grp):
    i = pl.program_id(1)
    nb = pl.num_programs(1)
    kvh = kc_ref.shape[1]
    kcats = [jnp.concatenate([kp_ref[0, c], kc_ref[0, c], kn_ref[0, c], kx_ref[0, c]], axis=0) for c in range(kvh)]
    vcats = [jnp.concatenate([vp_ref[0, c], vc_ref[0, c], vn_ref[0, c], vx_ref[0, c]], axis=0) for c in range(kvh)]
    nk = kcats[0].shape[0]
    r = lax.broadcasted_iota(jnp.int32, (BLOCK, nk), 0)
    c = lax.broadcasted_iota(jnp.int32, (BLOCK, nk), 1)
    off_prev = jnp.where(i > 0, 0, 2 * nk)
    off_next = jnp.where(i < nb - 1, 0, 2 * nk)
    ok_prev = (c >= BLOCK) | (c >= r + off_prev)
    ok_next = (c < 2 * BLOCK) | (c >= 3 * BLOCK) | (c - 2 * BLOCK <= r - off_next)
    valid = ok_prev & ok_next
    scores = [lax.dot_general(q_ref[0, hq], kcats[hq // grp], _NT, preferred_element_type=jnp.float32)
              for hq in range(kvh * grp)]
    outs = []
    for hq, s in enumerate(scores):
        s = jnp.where(valid, s, -jnp.inf)
        sk = sink_ref[hq]
        m = jnp.maximum(jnp.max(s, axis=1, keepdims=True), sk)
        p = jnp.exp(s - m)
        l = jnp.sum(p, axis=1, keepdims=True) + jnp.exp(sk - m)
        o = jnp.dot(p.astype(jnp.bfloat16), vcats[hq // grp], preferred_element_type=jnp.float32)
        outs.append(o / l)
    o_ref[0] = jnp.concatenate(outs, axis=1).astype(o_ref.dtype)


def windowed_sink_gqa(q, k, v, kx, vx, sink):
    b, h, s, hd = q.shape
    kvh = k.shape[1]
    lc = kx.shape[2]
    grp = h // kvh
    nb = s // BLOCK
    kern = functools.partial(_window_kernel, grp=grp)
    assert WINDOW == BLOCK, "the kernel's mask assumes a one-block window on each side"
    blk = lambda f: pl.BlockSpec((1, kvh, BLOCK, hd), f)
    prev = lambda i, m: (i, 0, jnp.maximum(m - 1, 0), 0)
    cur = lambda i, m: (i, 0, m, 0)
    nxt = lambda i, m: (i, 0, jnp.minimum(m + 1, nb - 1), 0)
    ctxm = lambda i, m: (i, 0, 0, 0)
    return pl.pallas_call(
        kern,
        grid=(b, nb),
        in_specs=[
            pl.BlockSpec(memory_space=pltpu.SMEM),
            pl.BlockSpec((1, h, BLOCK, hd), cur),
            blk(prev), blk(cur), blk(nxt), blk(prev), blk(cur), blk(nxt),
            pl.BlockSpec((1, kvh, lc, hd), ctxm), pl.BlockSpec((1, kvh, lc, hd), ctxm),
        ],
        out_specs=pl.BlockSpec((1, BLOCK, h * hd), lambda i, m: (i, m, 0)),
        out_shape=jax.ShapeDtypeStruct((b, s, h * hd), jnp.bfloat16),
        compiler_params=pltpu.CompilerParams(
            dimension_semantics=("arbitrary",) * 2, vmem_limit_bytes=VMEM_LIMIT_BYTES),
        name="windowed_sink_gqa",
    )(sink.astype(jnp.float32), q, k, k, k, v, v, v, kx, vx)


def _post_kernel(oa_ref, ob_ref, w_ref, x_ref, g_ref, lg_ref, lb_ref, y_ref):
    ka = oa_ref.shape[2]
    out = jnp.dot(oa_ref[0].astype(jnp.bfloat16), w_ref[:ka], preferred_element_type=jnp.float32)
    out += jnp.dot(ob_ref[0].astype(jnp.bfloat16), w_ref[ka:], preferred_element_type=jnp.float32)
    r = ALPHA * x_ref[0] + g_ref[0] * out
    mu = jnp.mean(r, -1, keepdims=True)
    d = r - mu
    var = jnp.mean(d * d, -1, keepdims=True)
    y_ref[0] = d * lax.rsqrt(var + EPS) * lg_ref[...] + lb_ref[...]


def proj_residual_ln(oa, ob, w, x, gate, ln_g, ln_b, tm=512):
    b, s, ka = oa.shape
    kb = ob.shape[2]
    k = ka + kb
    d = w.shape[1]
    tm = min(tm, s)
    wb = w.astype(jnp.bfloat16)
    return pl.pallas_call(
        _post_kernel,
        grid=(b, s // tm),
        in_specs=[
            pl.BlockSpec((1, tm, ka), lambda i, m: (i, m, 0)),
            pl.BlockSpec((1, tm, kb), lambda i, m: (i, m, 0)),
            pl.BlockSpec((k, d), lambda i, m: (0, 0)),
            pl.BlockSpec((1, tm, d), lambda i, m: (i, m, 0)),
            pl.BlockSpec((1, 1, d), lambda i, m: (i, 0, 0)),
            pl.BlockSpec((1, d), lambda i, m: (0, 0)),
            pl.BlockSpec((1, d), lambda i, m: (0, 0)),
        ],
        out_specs=pl.BlockSpec((1, tm, d), lambda i, m: (i, m, 0)),
        out_shape=jax.ShapeDtypeStruct((b, s, d), jnp.float32),
        compiler_params=pltpu.CompilerParams(
            dimension_semantics=("arbitrary", "arbitrary"),
            vmem_limit_bytes=VMEM_LIMIT_BYTES),
        name="proj_residual_ln",
    )(oa, ob, wb, x, gate, ln_g.reshape(1, d), ln_b.reshape(1, d))


def _top16(s, payload=None):
    n = s.shape[0]
    iota = lax.broadcasted_iota(jnp.int32, s.shape, 0).astype(jnp.float32)
    vals, ids = [], []
    for _ in range(PEER_TOPK):
        m = jnp.max(s, axis=0, keepdims=True)
        pos = jnp.min(jnp.where(s == m, iota, float(n)), axis=0, keepdims=True)
        hit = iota == pos
        vals.append(m)
        ids.append(pos if payload is None else jnp.max(jnp.where(hit, payload, -1.0), axis=0, keepdims=True))
        s = jnp.where(hit, -jnp.inf, s)
    return jnp.concatenate(vals, 0), jnp.concatenate(ids, 0)


def _peer_topk_kernel(q_ref, k1_ref, k2_ref, eid_ref, gate_ref, eid_s, gate_s):
    half = PEER_QDIM // 2

    def head(h, carry):
        off = pl.multiple_of(h * PEER_QDIM, PEER_QDIM)
        q1 = q_ref[:, pl.ds(off, half)]
        q2 = q_ref[:, pl.ds(off + half, half)]
        s1 = lax.dot_general(k1_ref[h], q1, _NT, precision=lax.Precision.HIGHEST,
                             preferred_element_type=jnp.float32)
        s2 = lax.dot_general(k2_ref[h], q2, _NT, precision=lax.Precision.HIGHEST,
                             preferred_element_type=jnp.float32)
        v1, i1 = _top16(s1)
        v2, i2 = _top16(s2)
        k8 = PEER_TOPK // 2
        cand = jnp.concatenate([v1[0:1] + v2] + [v1[i:i + 1] + v2[:k8] for i in range(1, k8)]
                               + [v1[k8:] + v2[0:1]], 0)
        cid = jnp.concatenate([i1[0:1] * PEER_NKEYS + i2]
                              + [i1[i:i + 1] * PEER_NKEYS + i2[:k8] for i in range(1, k8)]
                              + [i1[k8:] * PEER_NKEYS + i2[0:1]], 0)
        best, eid = _top16(cand, cid)
        e = jnp.exp(best - best[0:1])
        gate = e / jnp.sum(e, axis=0, keepdims=True)
        row = pl.multiple_of(h * PEER_TOPK, PEER_TOPK)
        eid_s[pl.ds(row, PEER_TOPK), :] = eid.astype(jnp.int32)
        gate_s[pl.ds(row, PEER_TOPK), :] = gate
        return carry

    lax.fori_loop(0, PEER_HEADS, head, 0)
    eid_ref[...] = eid_s[...].T
    gate_ref[...] = gate_s[...].T


def peer_topk(q, k1, k2, tt=2048):
    t = q.shape[0]
    tt = min(tt, t)
    nsel = PEER_HEADS * PEER_TOPK
    return pl.pallas_call(
        _peer_topk_kernel,
        grid=(t // tt,),
        in_specs=[
            pl.BlockSpec((tt, q.shape[1]), lambda i: (i, 0)),
            pl.BlockSpec(k1.shape, lambda i: (0, 0, 0)),
            pl.BlockSpec(k2.shape, lambda i: (0, 0, 0)),
        ],
        out_specs=[pl.BlockSpec((tt, nsel), lambda i: (i, 0)),
                   pl.BlockSpec((tt, nsel), lambda i: (i, 0))],
        out_shape=[jax.ShapeDtypeStruct((t, nsel), jnp.int32),
                   jax.ShapeDtypeStruct((t, nsel), jnp.float32)],
        scratch_shapes=[pltpu.VMEM((nsel, tt), jnp.int32), pltpu.VMEM((nsel, tt), jnp.float32)],
        compiler_params=pltpu.CompilerParams(
            dimension_semantics=("arbitrary",), vmem_limit_bytes=VMEM_LIMIT_BYTES),
        name="peer_topk",
    )(q, k1, k2)


def _peer_w_kernel(e_ref, g_ref, w_ref):
    nk = PEER_NKEYS
    iota = lax.broadcasted_iota(jnp.int32, (nk, e_ref.shape[1]), 0)

    def tok(t, carry):
        e = e_ref[pl.ds(t, 1), :]
        g = g_ref[pl.ds(t, 1), :]
        a_t = jnp.where(iota == (e >> 7), g, 0.0).astype(jnp.bfloat16)
        b_t = jnp.where(iota == (e & (nk - 1)), 1.0, 0.0).astype(jnp.bfloat16)
        w = lax.dot_general(a_t, b_t, _NT, preferred_element_type=jnp.float32)
        w_ref[t] = w.astype(jnp.bfloat16)
        return carry

    lax.fori_loop(0, e_ref.shape[0], tok, 0, unroll=64)


def peer_dense_gates(eid, gate, tt=128):
    t, nsel = eid.shape
    tt = min(tt, t)
    nk = PEER_NKEYS
    w = pl.pallas_call(
        _peer_w_kernel,
        grid=(t // tt,),
        in_specs=[pl.BlockSpec((tt, nsel), lambda i: (i, 0)),
                  pl.BlockSpec((tt, nsel), lambda i: (i, 0))],
        out_specs=pl.BlockSpec((tt, nk, nk), lambda i: (i, 0, 0)),
        out_shape=jax.ShapeDtypeStruct((t, nk, nk), jnp.bfloat16),
        compiler_params=pltpu.CompilerParams(
            dimension_semantics=("arbitrary",), vmem_limit_bytes=VMEM_LIMIT_BYTES),
        name="peer_dense_gates",
    )(eid, gate)
    return w


def _peer_expert_kernel(x_ref, sh_ref, sc_ref, w_ref, u_ref, v_ref, g_ref, lg_ref, lb_ref, y_ref, xm_s, acc_s):
    e = pl.program_id(2)

    @pl.when(e == 0)
    def _():
        xm_s[...] = (x_ref[0] * (1.0 + sc_ref[0]) + sh_ref[0]).astype(jnp.bfloat16)
        acc_s[...] = jnp.zeros_like(acc_s)

    h = lax.dot_general(xm_s[...], u_ref[...], _NT, preferred_element_type=jnp.float32)
    gelu = 0.5 * h * (1.0 + lax.erf(h * (2.0 ** -0.5)))
    w = w_ref[0].reshape(h.shape)
    a = gelu * w.astype(jnp.float32)
    acc_s[...] += jnp.dot(a.astype(jnp.bfloat16), v_ref[...], preferred_element_type=jnp.float32)

    @pl.when(e == pl.num_programs(2) - 1)
    def _():
        r = ALPHA * x_ref[0] + g_ref[0] * acc_s[...]
        mu = jnp.mean(r, -1, keepdims=True)
        d = r - mu
        var = jnp.mean(d * d, -1, keepdims=True)
        y_ref[0] = d * lax.rsqrt(var + EPS) * lg_ref[...] + lb_ref[...]


def peer_experts_ln(x, shift, scale, w, u_tab, v_tab, gate, ln_g, ln_b, tt=512, te=2048):
    b, s, d = x.shape
    tt = min(tt, s)
    ne = u_tab.shape[0]
    nk = PEER_NKEYS
    w3 = w.reshape(b, s, nk, nk)
    return pl.pallas_call(
        _peer_expert_kernel,
        grid=(b, s // tt, ne // te),
        in_specs=[
            pl.BlockSpec((1, tt, d), lambda i, m, e: (i, m, 0)),
            pl.BlockSpec((1, 1, d), lambda i, m, e: (i, 0, 0)),
            pl.BlockSpec((1, 1, d), lambda i, m, e: (i, 0, 0)),
            pl.BlockSpec((1, tt, te // nk, nk), lambda i, m, e: (i, m, e, 0)),
            pl.BlockSpec((te, d), lambda i, m, e: (e, 0)),
            pl.BlockSpec((te, d), lambda i, m, e: (e, 0)),
            pl.BlockSpec((1, 1, d), lambda i, m, e: (i, 0, 0)),
            pl.BlockSpec((1, d), lambda i, m, e: (0, 0)),
            pl.BlockSpec((1, d), lambda i, m, e: (0, 0)),
        ],
        out_specs=pl.BlockSpec((1, tt, d), lambda i, m, e: (i, m, 0)),
        out_shape=jax.ShapeDtypeStruct((b, s, d), jnp.float32),
        scratch_shapes=[pltpu.VMEM((tt, d), jnp.bfloat16), pltpu.VMEM((tt, d), jnp.float32)],
        compiler_params=pltpu.CompilerParams(
            dimension_semantics=("arbitrary", "arbitrary", "arbitrary"),
            vmem_limit_bytes=VMEM_LIMIT_BYTES),
        name="peer_experts_ln",
    )(x, shift, scale, w3, u_tab, v_tab, gate, ln_g.reshape(1, d), ln_b.reshape(1, d))


def _cast_kernel(x_ref, o_ref):
    o_ref[...] = x_ref[0].astype(o_ref.dtype)


def layer_table_bf16(tab, layer, tr=2048):
    _, ne, d = tab.shape
    return pl.pallas_call(
        _cast_kernel,
        grid=(ne // tr,),
        in_specs=[pl.BlockSpec((1, tr, d), lambda r: (layer, r, 0))],
        out_specs=pl.BlockSpec((tr, d), lambda r: (r, 0)),
        out_shape=jax.ShapeDtypeStruct((ne, d), jnp.bfloat16),
        compiler_params=pltpu.CompilerParams(
            dimension_semantics=("arbitrary",), vmem_limit_bytes=VMEM_LIMIT_BYTES),
        name="layer_table_bf16",
    )(tab)


def peer_block(x, shift, scale, gate, wq, k1, k2, u_bf, v_bf, ln_g, ln_b):
    b, s, d = x.shape
    q_all = mod_matmul(x, shift, scale, wq).reshape(b * s, -1)
    eid, gsel = peer_topk(q_all, k1, k2)
    w = peer_dense_gates(eid, gsel)
    return peer_experts_ln(x, shift, scale, w, u_bf, v_bf, gate, ln_g, ln_b)


HEAD_LANES = 128


def _short_conv_kernel(x_ref, xp_ref, xn_ref, w_ref, b_ref, o_ref, *, silu, n_l2, n_scaled):
    cb = pl.program_id(1)
    m = pl.program_id(2)
    x = x_ref[0]
    tq, wb = x.shape
    prev_row = jnp.where(m > 0, xp_ref[0][7:8], 0.0)
    next_row = jnp.where(m < pl.num_programs(2) - 1, xn_ref[0][0:1], 0.0)
    row = lax.broadcasted_iota(jnp.int32, x.shape, 0)
    x_m1 = jnp.where(row == 0, prev_row, pltpu.roll(x, 1, 0))
    x_p1 = jnp.where(row == tq - 1, next_row, pltpu.roll(x, tq - 1, 0))
    y = w_ref[0:1] * x_m1 + w_ref[1:2] * x + w_ref[2:3] * x_p1 + b_ref[...]
    if silu:
        y = y * jax.nn.sigmoid(y)
    if n_l2 == 0:
        o_ref[0] = y
        return
    hpb = wb // HEAD_LANES
    for hh in range(hpb):
        gh = cb * hpb + hh
        seg = y[:, hh * HEAD_LANES:(hh + 1) * HEAD_LANES]
        inv = lax.rsqrt(jnp.sum(seg * seg, axis=-1, keepdims=True) + EPS)
        f = jnp.where(gh < n_l2, inv, 1.0) * jnp.where(gh < n_scaled, C_DK ** -0.5, 1.0)
        o_ref[0, :, hh * HEAD_LANES:(hh + 1) * HEAD_LANES] = seg * f


def short_conv(p, col0, width, w, bias=None, silu=False, n_l2=0, n_scaled=0, wb=768, tq=512):
    b, l, _ = p.shape
    tq = min(tq, l)
    bias2 = (jnp.zeros((width,), jnp.float32) if bias is None else bias).reshape(1, width)
    c0 = col0 // wb
    kern = functools.partial(_short_conv_kernel, silu=silu, n_l2=n_l2, n_scaled=n_scaled)
    r8 = tq // 8
    return pl.pallas_call(
        kern,
        grid=(b, width // wb, l // tq),
        in_specs=[
            pl.BlockSpec((1, tq, wb), lambda i, c, m: (i, m, c0 + c)),
            pl.BlockSpec((1, 8, wb), lambda i, c, m: (i, jnp.maximum(m * r8 - 1, 0), c0 + c)),
            pl.BlockSpec((1, 8, wb), lambda i, c, m: (i, jnp.minimum((m + 1) * r8, l // 8 - 1), c0 + c)),
            pl.BlockSpec((3, wb), lambda i, c, m: (0, c)),
            pl.BlockSpec((1, wb), lambda i, c, m: (0, c)),
        ],
        out_specs=pl.BlockSpec((1, tq, wb), lambda i, c, m: (i, m, c)),
        out_shape=jax.ShapeDtypeStruct((b, l, width), jnp.float32),
        compiler_params=pltpu.CompilerParams(
            dimension_semantics=("arbitrary",) * 3, vmem_limit_bytes=VMEM_LIMIT_BYTES),
        name="short_conv",
    )(p, p, p, w, bias2)


def _dot3(a, b):
    ah = a.astype(jnp.bfloat16)
    bh = b.astype(jnp.bfloat16)
    al = (a - ah.astype(jnp.float32)).astype(jnp.bfloat16)
    bl = (b - bh.astype(jnp.float32)).astype(jnp.bfloat16)
    d = functools.partial(jnp.dot, preferred_element_type=jnp.float32)
    return d(ah, bh) + (d(ah, bl) + d(al, bh))


GDN_PAR = 4


def _gdn_chunk_kernel(qkvc_ref, qkvl_ref, beta_ref, g_ref, u_ref, w_ref, qd_ref, kd_ref, in_ref, gl_ref, qkv_s, *, nc):
    d = pl.program_id(0)
    m = pl.program_id(2)

    @pl.when(m == 0)
    def _():
        qkv_s[...] = qkvc_ref[0]

    @pl.when(m > 0)
    def _():
        qkv_s[...] = qkvl_ref[0]

    cs = GDN_CHUNK
    ii = lax.broadcasted_iota(jnp.int32, (cs, cs), 0)
    jj = lax.broadcasted_iota(jnp.int32, (cs, cs), 1)
    lo = (ii - jj) * (1 - 2 * d)
    incl = lo >= 0
    strict = lo > 0
    tri = jnp.where(incl, 1.0, 0.0).astype(jnp.bfloat16)
    tri3 = jnp.concatenate([tri, tri, tri], axis=1)
    eye = jnp.where(ii == jj, 1.0, 0.0)

    def chunk_pair(cp, carry):
        probs = []
        for c in [GDN_PAR * cp + i for i in range(GDN_PAR)]:
            rows = pl.ds(pl.multiple_of(c * cs, cs), cs)
            g_c = g_ref[0, 0, rows, :]
            b_c = beta_ref[0, 0, rows, :]
            g_hi = g_c.astype(jnp.bfloat16)
            r1 = g_c - g_hi.astype(jnp.float32)
            g_mid = r1.astype(jnp.bfloat16)
            g_lo = (r1 - g_mid.astype(jnp.float32)).astype(jnp.bfloat16)
            gc = jnp.dot(tri3, jnp.concatenate([g_hi, g_mid, g_lo], axis=0),
                         preferred_element_type=jnp.float32)
            tot = jnp.sum(g_c, axis=0, keepdims=True)
            for h in range(C_HEADS):
                probs.append(dict(c=c, h=h, rows=rows, gc=gc[:, h:h + 1], bt=b_c[:, h:h + 1], tot=tot[:, h:h + 1]))
        for pr in probs:
            h, rows = pr["h"], pr["rows"]
            q = qkv_s[rows, h *HEAD_LANES:(h + 1) * HEAD_LANES]
            k = qkv_s[rows, C_W + h * HEAD_LANES:C_W + (h + 1) * HEAD_LANES]
            kb = k * pr["bt"]
            kq = lax.dot_general(jnp.concatenate([kb, q], axis=0).astype(jnp.bfloat16), k.astype(jnp.bfloat16),
                                 _NT, preferred_element_type=jnp.float32)
            gc_row = jnp.broadcast_to(pr["gc"], (cs, HEAD_LANES)).T[:cs, :]
            dm = jnp.where(incl, jnp.exp(pr["gc"] - gc_row), 0.0)
            x = jnp.where(strict, -(kq[:cs] * dm), 0.0)
            in_ref[0, 0, pr["c"], h] = (kq[cs:] * dm).astype(in_ref.dtype)
            pr.update(t=eye + x, pw=x)
        for _ in range(5):
            for pr in probs:
                pr["pw"] = _dot3(pr["pw"], pr["pw"])
            for pr in probs:
                pr["t"] = pr["t"] + _dot3(pr["t"], pr["pw"])
        for pr in probs:
            h, rows = pr["h"], pr["rows"]
            lanes = slice(h * HEAD_LANES, (h + 1) * HEAD_LANES)
            q = qkv_s[rows, h *HEAD_LANES:(h + 1) * HEAD_LANES]
            k = qkv_s[rows, C_W + h * HEAD_LANES:C_W + (h + 1) * HEAD_LANES]
            v = qkv_s[rows, 2 * C_W + h * HEAD_LANES:2 * C_W + (h + 1) * HEAD_LANES]
            eg = jnp.exp(pr["gc"])
            uw = _dot3(pr["t"], jnp.concatenate([v * pr["bt"], k * (pr["bt"] * eg)], axis=1))
            u_ref[0, 0, rows, lanes] = uw[:, :HEAD_LANES]
            w_ref[0, 0, rows, lanes] = uw[:, HEAD_LANES:].astype(w_ref.dtype)
            qd_ref[0, 0, rows, lanes] = (q * eg).astype(qd_ref.dtype)
            kd_ref[0, 0, rows, lanes] = (k * jnp.exp(pr["tot"] - pr["gc"])).astype(kd_ref.dtype)
            gl_ref[0, 0, pr["c"], h:h + 1, :] = jnp.broadcast_to(jnp.exp(pr["tot"]), (1, HEAD_LANES))
        return carry

    lax.fori_loop(0, nc // GDN_PAR, chunk_pair, 0)


def gdn_chunk_prep(qkv_c, qkv_l, beta, g, nc=4):
    b, lc, _ = qkv_c.shape
    l = lc + qkv_l.shape[1]
    cs = GDN_CHUNK
    tq = nc * cs
    assert lc == tq, "the context must fill exactly the first token block"
    nchunks = l // cs
    bf = jnp.bfloat16
    big = lambda dt: jax.ShapeDtypeStruct((2, b, l, C_W), dt)
    bspec = pl.BlockSpec((1, 1, tq, C_W), lambda d, i, m: (d, i, m, 0))
    gspec = pl.BlockSpec((1, 1, tq, C_HEADS), lambda d, i, m: (d, i, m, 0))
    return pl.pallas_call(
        functools.partial(_gdn_chunk_kernel, nc=nc),
        grid=(2, b, l // tq),
        in_specs=[pl.BlockSpec((1, tq, 3 * C_W), lambda d, i, m: (i, 0, 0)),
                  pl.BlockSpec((1, tq, 3 * C_W), lambda d, i, m: (i, jnp.maximum(m - 1, 0), 0)), gspec, gspec],
        out_specs=[bspec, bspec, bspec, bspec,
                   pl.BlockSpec((1, 1, nc, C_HEADS, cs, cs), lambda d, i, m: (d, i, m, 0, 0, 0)),
                   pl.BlockSpec((1, 1, nc, C_HEADS, HEAD_LANES), lambda d, i, m: (d, i, m, 0, 0))],
        out_shape=[big(jnp.float32), big(bf), big(bf), big(bf),
                   jax.ShapeDtypeStruct((2, b, nchunks, C_HEADS, cs, cs), bf),
                   jax.ShapeDtypeStruct((2, b, nchunks, C_HEADS, HEAD_LANES), jnp.float32)],
        scratch_shapes=[pltpu.VMEM((tq, 3 * C_W), jnp.float32)],
        compiler_params=pltpu.CompilerParams(
            dimension_semantics=("arbitrary",) * 3, vmem_limit_bytes=VMEM_LIMIT_BYTES),
        name="gdn_chunk_prep",
    )(qkv_c, qkv_l, beta, g)


def _gdn_scan_kernel(*refs):
    ins, (of_ref, ob_ref, s_ref) = refs[:12], refs[12:]
    step = pl.program_id(1)

    @pl.when(step == 0)
    def _():
        s_ref[...] = jnp.zeros_like(s_ref)

    dot = functools.partial(jnp.dot, preferred_element_type=jnp.float32)
    seqs = [(d, h, slice(h * HEAD_LANES, (h + 1) * HEAD_LANES)) for d in range(2) for h in range(C_HEADS)]
    outs = (of_ref, ob_ref)
    sb, vb = {}, {}
    for d, h, lanes in seqs:
        sb[d, h] = s_ref[d * C_HEADS + h].astype(jnp.bfloat16)
    for d, h, lanes in seqs:
        u_ref, w_ref = ins[6 * d], ins[6 * d + 1]
        vb[d, h] = (u_ref[0, 0, :, lanes] - dot(w_ref[0, 0, :, lanes], sb[d, h])).astype(jnp.bfloat16)
    for d, h, lanes in seqs:
        qd_ref, in_ref = ins[6 * d + 2], ins[6 * d + 4]
        outs[d][0, :, lanes] = dot(qd_ref[0, 0, :, lanes], sb[d, h]) + dot(in_ref[0, 0, 0, h], vb[d, h])
    for d, h, lanes in seqs:
        kd_ref, gl_ref = ins[6 * d + 3], ins[6 * d + 5]
        s_ref[d * C_HEADS + h] = s_ref[d * C_HEADS + h] * gl_ref[0, 0, 0, h:h + 1, :] + lax.dot_general(
            kd_ref[0, 0, :, lanes], vb[d, h], (((0,), (0,)), ((), ())), preferred_element_type=jnp.float32)


def gdn_scan(u, w, qd, kd, intra, gl, n_ctx_chunks):
    _, b, l, _ = u.shape
    cs = GDN_CHUNK
    nchunks = l // cs

    def chunk_of(d, s):
        if d == 0:
            return s
        return jnp.where(s < n_ctx_chunks, n_ctx_chunks - 1 - s, nchunks - 1 + n_ctx_chunks - s)

    in_specs, args = [], []
    for d in range(2):
        big = pl.BlockSpec((1, 1, cs, C_W), lambda i, s, d=d: (d, i, chunk_of(d, s), 0))
        in_specs += [big, big, big, big,
                     pl.BlockSpec((1, 1, 1, C_HEADS, cs, cs), lambda i, s, d=d: (d, i, chunk_of(d, s), 0, 0, 0)),
                     pl.BlockSpec((1, 1, 1, C_HEADS, HEAD_LANES), lambda i, s, d=d: (d, i, chunk_of(d, s), 0, 0))]
        args += [u, w, qd, kd, intra, gl]
    out_specs = [pl.BlockSpec((1, cs, C_W), lambda i, s, d=d: (i, chunk_of(d, s), 0)) for d in range(2)]
    return pl.pallas_call(
        _gdn_scan_kernel,
        grid=(b, nchunks),
        in_specs=in_specs,
        out_specs=out_specs,
        out_shape=[jax.ShapeDtypeStruct((b, l, C_W), jnp.float32)] * 2,
        scratch_shapes=[pltpu.VMEM((2 * C_HEADS, C_DK, C_DV), jnp.float32)],
        compiler_params=pltpu.CompilerParams(
            dimension_semantics=("arbitrary",) * 2, vmem_limit_bytes=VMEM_LIMIT_BYTES),
        name="gdn_scan",
    )(*args)


def _gdn_gate_kernel(of_ref, ob_ref, z_ref, gw_ref, y_ref):
    o = of_ref[0] + ob_ref[0]
    z = z_ref[0]
    for h in range(C_HEADS):
        lanes = slice(h * HEAD_LANES, (h + 1) * HEAD_LANES)
        oh = o[:, lanes]
        zh = z[:, lanes]
        n = oh * lax.rsqrt(jnp.mean(oh * oh, axis=-1, keepdims=True) + EPS) * gw_ref[...]
        y_ref[0, :, lanes] = (n * (zh * jax.nn.sigmoid(zh))).astype(y_ref.dtype)


def gdn_gate(o_f, o_b, row0, p, gnorm_w, tq=256):
    b, l, _ = p.shape
    tq = min(tq, l)
    r0 = row0 // tq
    ospec = pl.BlockSpec((1, tq, C_W), lambda i, m: (i, r0 + m, 0))
    return pl.pallas_call(
        _gdn_gate_kernel,
        grid=(b, l // tq),
        in_specs=[ospec, ospec,
                  pl.BlockSpec((1, tq, C_W), lambda i, m: (i, m, 3)),
                  pl.BlockSpec((1, HEAD_LANES), lambda i, m: (0, 0))],
        out_specs=pl.BlockSpec((1, tq, C_W), lambda i, m: (i, m, 0)),
        out_shape=jax.ShapeDtypeStruct((b, l, C_W), jnp.bfloat16),
        compiler_params=pltpu.CompilerParams(
            dimension_semantics=("arbitrary",) * 2, vmem_limit_bytes=VMEM_LIMIT_BYTES),
        name="gdn_gate",
    )(o_f, o_b, p, gnorm_w.reshape(1, HEAD_LANES))


def gdn_mixer(p, pc, gates_l, gates_c, conv_w, a_log, dt_bias, gnorm_w, with_ctx):
    lc = pc.shape[1]
    conv = functools.partial(short_conv, col0=0, width=3 * C_W, w=conv_w, silu=True,
                             n_l2=2 * C_HEADS, n_scaled=C_HEADS)
    gates = jnp.concatenate([gates_c, gates_l], axis=1)
    gates = gates.reshape(gates.shape[0], gates.shape[1], 4, C_HEADS)
    beta = jax.nn.sigmoid(gates[:, :, :2])
    g = -jnp.exp(a_log) * jax.nn.softplus(gates[:, :, 2:] + dt_bias)
    beta = jnp.moveaxis(beta, 2, 0)
    g = jnp.moveaxis(g, 2, 0)
    u, w, qd, kd, intra, gl = gdn_chunk_prep(conv(pc), conv(p), beta, g)
    o_f, o_b = gdn_scan(u, w, qd, kd, intra, gl, lc // GDN_CHUNK)
    out = gdn_gate(o_f, o_b, lc, p, gnorm_w)
    out_c = gdn_gate(o_f, o_b, 0, pc, gnorm_w) if with_ctx else None
    return out, out_c


FFT_R = 128
FFT_N = FFT_R * FFT_R
SUB = 8
K1_PER_STEP = 2


def stage_a_table():
    idx = np.arange(FFT_R)
    ang = 2.0 * np.pi * np.outer(idx, idx) / FFT_R
    return jnp.asarray(np.stack([np.cos(ang), -np.sin(ang)], axis=1).reshape(2 * FFT_R, FFT_R), jnp.float32)


def _fft_stage_a_kernel(x_ref, l_ref, y_ref):
    l = l_ref[...]
    c = x_ref.shape[-1]
    xs = jnp.swapaxes(x_ref[0], 0, 1)
    ys = jnp.stack([_dot3(l, xs[j]) for j in range(SUB)], axis=0)
    y_ref[0] = jnp.swapaxes(ys, 0, 1).reshape(FFT_R, 2, SUB, c)


def fft_stage_a(x, col_blk, width, stage_a):
    b, l, wtot = x.shape
    n1cnt = l // FFT_R
    x4 = x.reshape(b, n1cnt, FFT_R, wtot)
    return pl.pallas_call(
        _fft_stage_a_kernel,
        grid=(b, FFT_R // SUB),
        in_specs=[pl.BlockSpec((1, n1cnt, SUB, width), lambda i, j: (i, 0, j, col_blk)),
                  pl.BlockSpec((2 * FFT_R, n1cnt), lambda i, j: (0, 0))],
        out_specs=pl.BlockSpec((1, FFT_R, 2, SUB, width), lambda i, j: (i, 0, 0, j, 0)),
        out_shape=jax.ShapeDtypeStruct((b, FFT_R, 2, FFT_R, width), jnp.float32),
        compiler_params=pltpu.CompilerParams(
            dimension_semantics=("arbitrary",) * 2, vmem_limit_bytes=VMEM_LIMIT_BYTES),
        name="fft_stage_a",
    )(x4, stage_a[:, :n1cnt])


def _dot3_presplit(ah, al, b, transpose_lhs=False):
    bh = b.astype(jnp.bfloat16)
    bl = (b - bh.astype(jnp.float32)).astype(jnp.bfloat16)
    dims = (((0,), (0,)), ((), ())) if transpose_lhs else (((1,), (0,)), ((), ()))
    d = functools.partial(lax.dot_general, dimension_numbers=dims, preferred_element_type=jnp.float32)
    return d(ah, bh) + (d(ah, bl) + d(al, bh))


def stage_b_tables():
    r = FFT_R
    k1 = jnp.arange(r, dtype=jnp.int32)[:, None, None]
    k2 = jnp.arange(r, dtype=jnp.int32)[None, :, None]
    n2 = jnp.arange(r, dtype=jnp.int32)[None, None, :]
    th = ((n2 * (r * k2 + k1)) % FFT_N).astype(jnp.float32) * (2.0 * math.pi / FFT_N)
    c, s = jnp.cos(th), jnp.sin(th)
    t = jnp.concatenate([jnp.concatenate([c, s], 2), jnp.concatenate([-s, c], 2)], 1)

    def split(m):
        hi = m.astype(jnp.bfloat16)
        return hi, (m - hi.astype(jnp.float32)).astype(jnp.bfloat16)

    return split(t)


def _fft_mid_kernel(y_ref, h_ref, th_ref, tl_ref, o_ref, *, conv):
    nb, nk, r, c = y_ref.shape[0], y_ref.shape[1], FFT_R, y_ref.shape[-1]
    probs = [(i, k) for k in range(nk) for i in range(nb)]
    xs = [_dot3_presplit(th_ref[k], tl_ref[k], y_ref[i, k].reshape(2 * r, c)) for i, k in probs]
    if not conv:
        for (i, k), x in zip(probs, xs):
            o_ref[i, k] = (x * ((1.0 / FFT_N) / h_ref[i])).reshape(2, r, c)
        return
    ps = [jnp.concatenate([x[:r] * h_ref[0, k, 0] - x[r:] * h_ref[0, k, 1],
                           x[:r] * h_ref[0, k, 1] + x[r:] * h_ref[0, k, 0]], axis=0) for (i, k), x in zip(probs, xs)]
    for (i, k), p in zip(probs, ps):
        o_ref[i, k] = _dot3_presplit(th_ref[k], tl_ref[k], p, transpose_lhs=True).reshape(2, r, c)


def fft_mid(y, h, tables, conv, order=0):
    b, r, _, _, c = y.shape
    hh = h if conv else h.reshape(b, 1, c)
    hspec = (pl.BlockSpec((1, K1_PER_STEP, 2, r, c), lambda k: (order, k, 0, 0, 0)) if conv
             else pl.BlockSpec((b, 1, c), lambda k: (0, 0, 0)))
    blk = pl.BlockSpec((b, K1_PER_STEP, 2, r, c), lambda k: (0, k, 0, 0, 0))
    tspec = pl.BlockSpec((K1_PER_STEP, 2 * r, 2 * r), lambda k: (k, 0, 0))
    return pl.pallas_call(
        functools.partial(_fft_mid_kernel, conv=conv),
        grid=(r // K1_PER_STEP,),
        in_specs=[blk, hspec, tspec, tspec],
        out_specs=blk,
        out_shape=jax.ShapeDtypeStruct(y.shape, jnp.float32),
        compiler_params=pltpu.CompilerParams(
            dimension_semantics=("arbitrary",), vmem_limit_bytes=VMEM_LIMIT_BYTES),
        name="fft_mid",
    )(y, hh, *tables)


def _fft_out_kernel(b_ref, l_ref, xg_ref, xin_ref, bias_ref, o_ref):
    l = l_ref[...]
    c = o_ref.shape[-1]
    bs = jnp.swapaxes(b_ref[0].reshape(2 * FFT_R, SUB, c), 0, 1)
    ys = jnp.stack([_dot3(l, bs[j]) for j in range(SUB)], axis=0)
    y = jnp.swapaxes(ys, 0, 1)
    o_ref[0] = xg_ref[0] * (y + bias_ref[...] * xin_ref[0])


def fft_out_gate(bm, stage_a, xg, xg_blk, xin, xin_blk, bias):
    b, r, _, _, c = bm.shape
    l = xg.shape[1]
    n1cnt = l // r
    view = lambda t: t.reshape(b, n1cnt, r, t.shape[-1])
    lhs = stage_a.T[:n1cnt]
    return pl.pallas_call(
        _fft_out_kernel,
        grid=(b, r // SUB),
        in_specs=[pl.BlockSpec((1, r, 2, SUB, c), lambda i, j: (i, 0, 0, j, 0)),
                  pl.BlockSpec((n1cnt, 2 * r), lambda i, j: (0, 0)),
                  pl.BlockSpec((1, n1cnt, SUB, c), lambda i, j: (i, 0, j, xg_blk)),
                  pl.BlockSpec((1, n1cnt, SUB, c), lambda i, j: (i, 0, j, xin_blk)),
                  pl.BlockSpec((1, c), lambda i, j: (0, 0))],
        out_specs=pl.BlockSpec((1, n1cnt, SUB, c), lambda i, j: (i, 0, j, 0)),
        out_shape=jax.ShapeDtypeStruct((b, n1cnt, r, c), jnp.float32),
        compiler_params=pltpu.CompilerParams(
            dimension_semantics=("arbitrary",) * 2, vmem_limit_bytes=VMEM_LIMIT_BYTES),
        name="fft_out_gate",
    )(bm, lhs, view(xg), view(xin), bias.reshape(1, c)).reshape(b, l, c)


def _direct_conv_kernel(xin_ref, xg_ref, kern_ref, d1_ref, d2_ref, norm_ref, bias_ref, o_ref):
    n = xin_ref.shape[1]
    d1 = d1_ref[...]
    x = xin_ref[0]
    xs = _dot3(d1[:, :n], x)
    hs = _dot3(d1, kern_ref[...]) / norm_ref[...]
    xr, xi, hr, hi = xs[:2 * n], xs[2 * n:], hs[:2 * n], hs[2 * n:]
    p = jnp.concatenate([xr * hr - xi * hi, xr * hi + xi * hr], axis=0)
    y = _dot3(d2_ref[...], p)
    o_ref[0] = xg_ref[0] * (y + bias_ref[...] * x)


def direct_long_conv(xin, xin_blk, xg, xg_blk, kern, norm, bias):
    b, n, _ = xin.shape
    c = kern.shape[1]
    idx = np.arange(2 * n)
    ang = 2.0 * np.pi * np.outer(idx, idx) / (2 * n)
    d1 = jnp.asarray(np.concatenate([np.cos(ang), -np.sin(ang)], axis=0), jnp.float32)
    d2 = jnp.asarray(np.concatenate([np.cos(ang[:n]), -np.sin(ang[:n])], axis=1) / (2 * n), jnp.float32)
    return pl.pallas_call(
        _direct_conv_kernel,
        grid=(b,),
        in_specs=[pl.BlockSpec((1, n, c), lambda i: (i, 0, xin_blk)),
                  pl.BlockSpec((1, n, c), lambda i: (i, 0, xg_blk)),
                  pl.BlockSpec((2 * n, c), lambda i: (0, 0)),
                  pl.BlockSpec((4 * n, 2 * n), lambda i: (0, 0)),
                  pl.BlockSpec((n, 4 * n), lambda i: (0, 0)),
                  pl.BlockSpec((1, c), lambda i: (0, 0)),
                  pl.BlockSpec((1, c), lambda i: (0, 0))],
        out_specs=pl.BlockSpec((1, n, c), lambda i: (i, 0, 0)),
        out_shape=jax.ShapeDtypeStruct((b, n, c), jnp.float32),
        compiler_params=pltpu.CompilerParams(
            dimension_semantics=("arbitrary",), vmem_limit_bytes=VMEM_LIMIT_BYTES),
        name="direct_long_conv",
    )(xin, xg, kern, d1, d2, norm.reshape(1, c), bias.reshape(1, c))


def _hy_filter_kernel(w1_ref, b1_ref, fr_ref, w2_ref, b2_ref, w3_ref, dl_ref, k_ref, s_ref, *, n):
    i = pl.program_id(0)
    tp, c = k_ref.shape[1], k_ref.shape[2]

    @pl.when(i == 0)
    def _():
        s_ref[...] = jnp.zeros_like(s_ref)

    def pos(shape):
        idx = i * tp + lax.broadcasted_iota(jnp.int32, shape, 0)
        t = jnp.where(idx < n, idx, jnp.where(idx == n, 0, 2 * n - idx))
        return idx, t.astype(jnp.float32)

    _, t = pos((tp, LANES))
    lane = lax.broadcasted_iota(jnp.int32, (tp, LANES), 1)
    band = jnp.where(lane <= HY_BANDS, lane, lane - HY_BANDS).astype(jnp.float32)
    ang = 2.0 * math.pi * t * band / n
    feat = jnp.where(lane == 0, t / n,
                     jnp.where(lane <= HY_BANDS, jnp.sin(ang), jnp.where(lane < HY_EMB, jnp.cos(ang), 0.0)))
    hid = jnp.sin(fr_ref[...] * (_dot3(feat, w1_ref[...]) + b1_ref[...]))
    hid = jnp.sin(fr_ref[...] * (_dot3(hid, w2_ref[...]) + b2_ref[...]))
    f = _dot3(hid, w3_ref[...])
    idx, t = pos((tp, c))
    decay = jnp.exp(-(t / n) * dl_ref[...])
    for o in range(HY_ORDER):
        fwd = f[:, (2 * o) * c:(2 * o + 1) * c]
        bwd = f[:, (2 * o + 1) * c:(2 * o + 2) * c]
        val = jnp.where(idx < n, fwd, bwd) * decay
        s_ref[o:o + 1, :] += jnp.sum(jnp.abs(val), axis=0, keepdims=True)
        k_ref[o] = jnp.where(idx == n, 0.0, val)


def hyena_kernels(n, w1, b1, freq, w2, b2, w3):
    c = HY_CH
    tp = min(512, n)
    max_decay = math.log(HY_TARGET) / HY_FAST_DECAY
    min_decay = math.log(HY_TARGET) / HY_SLOW_DECAY
    deltas = jnp.abs(jnp.linspace(min_decay, max_decay, c, dtype=jnp.float32)).reshape(1, c)
    w1p = jnp.zeros((LANES, w1.shape[1]), jnp.float32).at[:w1.shape[0]].set(w1)
    hd = w1.shape[1]
    full = lambda shape: pl.BlockSpec(shape, lambda i: (0,) * len(shape))
    return pl.pallas_call(
        functools.partial(_hy_filter_kernel, n=n),
        grid=(2 * n // tp,),
        in_specs=[full((LANES, hd)), full((1, hd)), full((1, hd)), full((hd, hd)), full((1, hd)),
                  full((hd, HY_ORDER * 2 * c)), full((1, c))],
        out_specs=[pl.BlockSpec((HY_ORDER, tp, c), lambda i: (0, i, 0)), full((HY_ORDER, c))],
        out_shape=[jax.ShapeDtypeStruct((HY_ORDER, 2 * n, c), jnp.float32),
                   jax.ShapeDtypeStruct((HY_ORDER, c), jnp.float32)],
        compiler_params=pltpu.CompilerParams(
            dimension_semantics=("arbitrary",), vmem_limit_bytes=VMEM_LIMIT_BYTES),
        name="hyena_kernels",
    )(w1p, b1.reshape(1, hd), freq.reshape(1, hd), w2, b2.reshape(1, hd), w3, deltas)


def hyena_mixer(p, col0, conv_w, conv_b, filt_args, hy_bias):
    n = p.shape[1]
    uc = short_conv(p, col0, 3 * HY_CH, conv_w, conv_b)
    kerns, norm = hyena_kernels(n, *filt_args)
    if 2 * n != FFT_N:
        v = direct_long_conv(uc, 2, uc, 0, kerns[0], norm[0], hy_bias[0])
        return direct_long_conv(v, 0, uc, 1, kerns[1], norm[1], hy_bias[1])
    stage_a, stage_b = stage_a_table(), stage_b_tables()
    spec = fft_mid(fft_stage_a(kerns, 0, HY_CH, stage_a), norm, stage_b, conv=False)
    v = fft_out_gate(fft_mid(fft_stage_a(uc, 2, HY_CH, stage_a), spec, stage_b, conv=True, order=0),
                     stage_a, uc, 0, uc, 2, hy_bias[0])
    return fft_out_gate(fft_mid(fft_stage_a(v, 0, HY_CH, stage_a), spec, stage_b, conv=True, order=1),
                        stage_a, uc, 1, v, 0, hy_bias[1])


def even_mixer(p, pc, rope_tabs, sink, conv_w, conv_b, fw1, fb1, ffreq, fw2, fb2, fw3, hy_bias, with_ctx):
    q, k, v = qkv_prep(p, 0, A_HEADS, A_KV_HEADS, rope_tabs)
    qc, kc, vc = qkv_prep(pc, 0, A_HEADS, A_KV_HEADS, None)
    o_a = windowed_sink_gqa(q, k, v, kc, vc, sink)
    filt_args = (fw1, fb1, ffreq, fw2, fb2, fw3)
    o_b = hyena_mixer(p, A_Q + 2 * A_KV, conv_w, conv_b, filt_args, hy_bias)
    out_c = None
    if with_ctx:
        o_ac = flash_gqa(qc, kc, vc, sink)
        o_bc = hyena_mixer(pc, A_Q + 2 * A_KV, conv_w, conv_b, filt_args, hy_bias)
        out_c = (o_ac, o_bc)
    return (o_a, o_b), out_c


def odd_mixer(p, pc, gates_l, gates_c, rope_tabs, conv_w, a_log, dt_bias, gnorm_w, qnorm_w, knorm_w, with_ctx):
    o_l, o_c = gdn_mixer(p, pc, gates_l, gates_c, conv_w, a_log, dt_bias, gnorm_w, with_ctx)
    qd, kd, vd = qkv_prep(p, 4 * C_W, D_HEADS, D_KV_HEADS, rope_tabs, qnorm_w, knorm_w)
    qdc, kdc, vdc = qkv_prep(pc, 4 * C_W, D_HEADS, D_KV_HEADS, None, qnorm_w, knorm_w)
    o_d = flash_gqa(qd, jnp.concatenate([kd, kdc], 2), jnp.concatenate([vd, vdc], 2))
    out_c = None
    if with_ctx:
        out_c = (o_c, flash_gqa(qdc, kdc, vdc))
    return (o_l, o_d), out_c


def kernel(x, c, ctx, c_ctx, ada_w, ada_b, ln1_g, ln1_b, ln2_g, ln2_b, peer_wq, peer_k1, peer_k2, peer_u, peer_v, ev_w_in, ev_w_out, ev_sink, ev_conv_w, ev_conv_b, ev_filt_w1, ev_filt_b1, ev_filt_freq, ev_filt_w2, ev_filt_b2, ev_filt_w3, ev_hy_bias, od_w_in, od_w_out, od_conv_w, od_a_log, od_dt_bias, od_gnorm_w, od_qnorm_w, od_knorm_w):
    rope_tabs = rope_tables(x.shape[1])
    bsz = x.shape[0]
    silu_c = jax.nn.silu(c)
    silu_cc = jax.nn.silu(c_ctx)
    for i in range(DEPTH):
        with_ctx = i < DEPTH - 1
        j = i // 2
        mod = (silu_c @ ada_w[i] + ada_b[i])[:, None, :]
        modc = jnp.broadcast_to((silu_cc @ ada_w[i] + ada_b[i])[None, None, :], (bsz, 1, 6 * D_MODEL))
        sh1, sc1, g1, sh2, sc2, g2 = jnp.split(mod, 6, axis=-1)
        sh1c, sc1c, g1c, sh2c, sc2c, g2c = jnp.split(modc, 6, axis=-1)
        if i % 2 == 0:
            p = mod_matmul(x, sh1, sc1, ev_w_in[j])
            pc = mod_matmul(ctx, sh1c, sc1c, ev_w_in[j])
            out, out_c = even_mixer(p, pc, rope_tabs, ev_sink[j], ev_conv_w[j], ev_conv_b[j],
                                    ev_filt_w1[j], ev_filt_b1[j], ev_filt_freq[j], ev_filt_w2[j], ev_filt_b2[j],
                                    ev_filt_w3[j], ev_hy_bias[j], with_ctx)
            w_out = ev_w_out[j]
        else:
            w_in = od_w_in[j]
            w_gate = jnp.pad(w_in[:, 4 * C_W:4 * C_W + C_GATES], ((0, 0), (0, LANES - C_GATES)))
            w_in = jnp.concatenate([w_in[:, :4 * C_W], w_in[:, 4 * C_W + C_GATES:]], axis=1)
            p = mod_matmul(x, sh1, sc1, w_in)
            pc = mod_matmul(ctx, sh1c, sc1c, w_in)
            gates_l = mod_matmul(x, sh1, sc1, w_gate)[..., :C_GATES]
            gates_c = mod_matmul(ctx, sh1c, sc1c, w_gate)[..., :C_GATES]
            out, out_c = odd_mixer(p, pc, gates_l, gates_c, rope_tabs, od_conv_w[j], od_a_log[j], od_dt_bias[j],
                                   od_gnorm_w[j], od_qnorm_w[j], od_knorm_w[j], with_ctx)
            w_out = od_w_out[j]
        u_bf = layer_table_bf16(peer_u, i)
        v_bf = layer_table_bf16(peer_v, i)
        x = proj_residual_ln(out[0], out[1], w_out, x, g1, ln1_g[i], ln1_b[i])
        x = peer_block(x, sh2, sc2, g2, peer_wq[i], peer_k1[i], peer_k2[i], u_bf, v_bf, ln2_g[i], ln2_b[i])
        if with_ctx:
            ctx = proj_residual_ln(out_c[0], out_c[1], w_out, ctx, g1c, ln1_g[i], ln1_b[i])
            ctx = peer_block(ctx, sh2c, sc2c, g2c, peer_wq[i], peer_k1[i], peer_k2[i], u_bf, v_bf,
                             ln2_g[i], ln2_b[i])
    return x
```
